```python
import math
import jax
import jax.numpy as jnp
from jax import lax
import numpy as np

D_MODEL = 2048
BATCH = 1
SEQ = 8192
DEPTH = 2

MIX_W = 512
N_BRANCH = 4
HEAD_DIM = 64
NORM_EPS = 1e-6

RWKV_HEADS = MIX_W // HEAD_DIM
RWKV_W_LORA = 32
RWKV_A_LORA = 32
RWKV_G_LORA = 96
RWKV_DECAY_SCALE = 0.6065306597126334
RWKV_LN_EPS = 64e-5

RET_HEADS = 8
RET_HD = MIX_W // RET_HEADS
RET_CHUNK = 128
ROPE_BASE = 10000.0

LRU_BLOCKS = 8
LRU_BD = MIX_W // LRU_BLOCKS
CONV_W = 4
LRU_C = 8.0

GLA_HEADS = 4
GLA_DK = (MIX_W // 2) // GLA_HEADS
GLA_DV = MIX_W // GLA_HEADS
GLA_LORA = 16
GLA_GATE_NORM = 16.0
GLA_CHUNK = 64

N_GROUPS = 4
EXP_PER_GROUP = 8
N_EXPERTS = N_GROUPS * EXP_PER_GROUP
TOP_K = 2
D_EXPERT = 1024
MOE_BLOCK = 128

A_COLS = 3 * MIX_W + RWKV_W_LORA + RWKV_A_LORA + RWKV_G_LORA
B_COLS = 4 * MIX_W
C_COLS = 2 * MIX_W
D_COLS = 2 * (MIX_W // 2) + MIX_W + GLA_LORA + MIX_W
GATE_COLS = N_BRANCH * D_MODEL
IN_COLS = A_COLS + B_COLS + C_COLS + D_COLS + GATE_COLS

kernel_name = 'hybrid_rwkv7_retnet_rglru_gla_hmoe'


def rmsnorm(x, g):
    xf = x.astype(jnp.float32)
    y = xf * lax.rsqrt(jnp.mean(xf * xf, -1, keepdims=True) + NORM_EPS)
    return (y * g.astype(jnp.float32)).astype(x.dtype)


def head_rmsnorm(y, g):
    h, d = y.shape[-2], y.shape[-1]
    y = y * lax.rsqrt(jnp.mean(y * y, -1, keepdims=True) + NORM_EPS)
    return y * g.astype(jnp.float32).reshape(h, d)


def head_layernorm(y, g, b, eps):
    h, d = y.shape[-2], y.shape[-1]
    mean = jnp.mean(y, -1, keepdims=True)
    yc = y - mean
    yn = yc * lax.rsqrt(jnp.mean(yc * yc, -1, keepdims=True) + eps)
    return yn * g.astype(jnp.float32).reshape(h, d) + b.astype(jnp.float32).reshape(h, d)


def rotary(x, positions):
    d = x.shape[-1]
    inv = 1.0 / (ROPE_BASE ** (jnp.arange(0, d, 2, dtype=jnp.float32) / d))
    ang = positions.astype(jnp.float32)[..., None] * inv
    cos = jnp.cos(ang)[:, :, None, :]
    sin = jnp.sin(ang)[:, :, None, :]
    x1, x2 = x[..., : d // 2], x[..., d // 2:]
    return jnp.concatenate([x1 * cos - x2 * sin, x2 * cos + x1 * sin], -1)


def rwkv7_branch(cols, mu, w0, w_up, a0, a_up, g_up, k_k, k_a, r_k, ln_g, ln_b):
    f32 = jnp.float32
    B_, S_, _ = cols.shape
    H, N = RWKV_HEADS, HEAD_DIM
    prev = jnp.pad(cols, ((0, 0), (1, 0), (0, 0)))[:, :S_]
    z = cols + (prev - cols) * mu
    o1 = 3 * MIX_W + RWKV_W_LORA
    r, k, v, wl, al, gl = jnp.split(z, [MIX_W, 2 * MIX_W, 3 * MIX_W, o1, o1 + RWKV_A_LORA], axis=-1)
    w = jnp.exp(-RWKV_DECAY_SCALE * jax.nn.sigmoid((w0 + jnp.tanh(wl) @ w_up).astype(f32)))
    a = jax.nn.sigmoid((a0 + al @ a_up).astype(f32))
    g = (jax.nn.sigmoid(gl) @ g_up).astype(f32)
    r = r.astype(f32)
    k = k.astype(f32)
    v = v.astype(f32)
    kk = (k * k_k.astype(f32)).reshape(B_, S_, H, N)
    kk = kk / jnp.maximum(jnp.sqrt(jnp.sum(kk * kk, -1, keepdims=True)), 1e-12)
    k = k * (1.0 + (a - 1.0) * k_a.astype(f32))

    def to_steps(t):
        return t.reshape(B_, S_, H, N).transpose(1, 0, 2, 3)

    xs = (to_steps(r), to_steps(w), to_steps(k), to_steps(v), kk.transpose(1, 0, 2, 3), to_steps(a))

    def step(state, inp):
        r_t, w_t, k_t, v_t, kk_t, a_t = inp
        sa = jnp.einsum('bhvk,bhk->bhv', state, -kk_t)
        state = (state * w_t[:, :, None, :]
                 + sa[..., None] * (kk_t * a_t)[:, :, None, :]
                 + v_t[..., None] * k_t[:, :, None, :])
        return state, jnp.einsum('bhvk,bhk->bhv', state, r_t)

    _, y = lax.scan(step, jnp.zeros((B_, H, N, N), f32), xs)
    y = y.transpose(1, 0, 2, 3)
    y = head_layernorm(y, ln_g, ln_b, RWKV_LN_EPS)
    r4, k4, v4 = r.reshape(B_, S_, H, N), k.reshape(B_, S_, H, N), v.reshape(B_, S_, H, N)
    bonus = jnp.sum(r4 * k4 * r_k.astype(f32), -1, keepdims=True) * v4
    y = (y + bonus).reshape(B_, S_, MIX_W) * g
    return y.astype(cols.dtype)


def retention_branch(cols, positions, norm_g):
    f32 = jnp.float32
    B_, S_, _ = cols.shape
    H, d, C = RET_HEADS, RET_HD, RET_CHUNK
    N = S_ // C
    q, k, v, gt = jnp.split(cols, 4, axis=-1)

    def hs(t):
        return t.reshape(B_, S_, H, d).astype(f32)

    q = rotary(hs(q), positions)
    k = rotary(hs(k), positions) * (d ** -0.5)
    v = hs(v)
    log_g = jnp.log(1.0 - jnp.exp(jnp.linspace(math.log(1.0 / 32), math.log(1.0 / 512), H, dtype=f32)))
    idx = jnp.arange(C, dtype=f32)
    diff = idx[:, None] - idx[None, :]
    decay = jnp.where(diff >= 0, jnp.exp(log_g[:, None, None] * jnp.maximum(diff, 0.0)[None]), 0.0)
    qc = q.reshape(B_, N, C, H, d)
    kc = k.reshape(B_, N, C, H, d)
    vc = v.reshape(B_, N, C, H, d)
    scores = jnp.einsum('bnihd,bnjhd->bnhij', qc, kc) * decay
    intra = jnp.einsum('bnhij,bnjhe->bnihe', scores, vc)
    zeta = jnp.exp(log_g[:, None] * (C - 1 - idx)[None])
    kv = jnp.einsum('bnjhd,hj,bnjhe->nbhde', kc, zeta, vc)
    chunk_decay = jnp.exp(log_g * C)[None, :, None, None]

    def step(state, kv_n):
        return state * chunk_decay + kv_n, state

    _, prev = lax.scan(step, jnp.zeros((B_, H, d, d), f32), kv)
    xi = jnp.exp(log_g[:, None] * (idx + 1)[None])
    cross = jnp.einsum('bnihd,nbhde,hi->bnihe', qc, prev, xi)
    o = (intra + cross).reshape(B_, S_, H, d)
    o = head_rmsnorm(o, norm_g).reshape(B_, S_, MIX_W)
    return (o * jax.nn.silu(gt.astype(f32))).astype(cols.dtype)


def rglru_branch(cols, positions, conv_w, conv_b, w_a, b_a, w_x, b_x, lam):
    f32 = jnp.float32
    B_, S_, _ = cols.shape
    G, BD = LRU_BLOCKS, LRU_BD
    xb, gb = jnp.split(cols, 2, axis=-1)
    xc = lax.conv_general_dilated(xb, conv_w[:, None, :].astype(xb.dtype), window_strides=(1,),
                                  padding=[(CONV_W - 1, 0)], dimension_numbers=('NWC', 'WIO', 'NWC'),
                                  feature_group_count=MIX_W) + conv_b
    xh = xc.reshape(B_, S_, G, BD).astype(f32)
    r = jax.nn.sigmoid(jnp.einsum('bsgi,gij->bsgj', xh, w_a.astype(f32)) + b_a.astype(f32).reshape(G, BD))
    i = jax.nn.sigmoid(jnp.einsum('bsgi,gij->bsgj', xh, w_x.astype(f32)) + b_x.astype(f32).reshape(G, BD))
    log_a = -LRU_C * r * jax.nn.softplus(-lam.astype(f32).reshape(G, BD))
    reset = (positions == 0)[:, :, None, None]
    a = jnp.where(reset, 0.0, jnp.exp(log_a))
    mult = jnp.where(reset, 1.0, jnp.sqrt(-jnp.expm1(2.0 * log_a)))
    b = mult * i * xh

    def combine(lhs, rhs):
        a1, b1 = lhs
        a2, b2 = rhs
        return a1 * a2, a2 * b1 + b2

    _, hseq = lax.associative_scan(combine, (a, b), axis=1)
    y = hseq.reshape(B_, S_, MIX_W) * jax.nn.gelu(gb.astype(f32))
    return y.astype(cols.dtype)


def gla_branch(cols, alpha_up, alpha_b, norm_g):
    f32 = jnp.float32
    B_, S_, _ = cols.shape
    H, dk, dv, C = GLA_HEADS, GLA_DK, GLA_DV, GLA_CHUNK
    N = S_ // C
    hk = MIX_W // 2
    q, k, v, al, gt = jnp.split(cols, [hk, 2 * hk, 2 * hk + MIX_W, 2 * hk + MIX_W + GLA_LORA], axis=-1)
    log_a = jax.nn.log_sigmoid((al @ alpha_up + alpha_b).astype(f32)) / GLA_GATE_NORM

    def chunks(t, dd):
        return t.reshape(B_, N, C, H, dd).astype(f32)

    qc = chunks(q, dk) * (dk ** -0.5)
    kc = chunks(k, dk)
    vc = chunks(v, dv)
    bcum = jnp.cumsum(chunks(log_a, dk), axis=2)
    blast = bcum[:, :, -1:]
    q_e = qc * jnp.exp(bcum)
    k_e = kc * jnp.exp(-bcum)
    causal = jnp.tril(jnp.ones((C, C), dtype=bool))
    scores = jnp.where(causal, jnp.einsum('bnihd,bnjhd->bnhij', q_e, k_e), 0.0)
    intra = jnp.einsum('bnhij,bnjhe->bnihe', scores, vc)
    kv = jnp.einsum('bnjhd,bnjhe->nbhde', kc * jnp.exp(blast - bcum), vc)
    dec = jnp.exp(blast[:, :, 0]).transpose(1, 0, 2, 3)

    def step(state, inp):
        kv_n, d_n = inp
        return state * d_n[..., None] + kv_n, state

    _, prev = lax.scan(step, jnp.zeros((B_, H, dk, dv), f32), (kv, dec))
    cross = jnp.einsum('bnihd,nbhde->bnihe', q_e, prev)
    o = (intra + cross).reshape(B_, S_, H, dv)
    o = head_rmsnorm(o, norm_g).reshape(B_, S_, MIX_W)
    return (o * jax.nn.silu(gt.astype(f32))).astype(cols.dtype)


def hier_moe(xn, wg_r, bg_r, we_r, be_r, w_gate, w_up, w_down):
    f32 = jnp.float32
    B_, S_, D = xn.shape
    T = B_ * S_
    xf = xn.reshape(T, D)
    pg = jax.nn.softmax((xf @ wg_r).astype(f32) + bg_r.astype(f32), -1)
    pg_top, g_idx = lax.top_k(pg, 1)
    le = ((xf @ we_r).astype(f32) + be_r.astype(f32)).reshape(T, N_GROUPS, EXP_PER_GROUP)
    sel = jnp.broadcast_to(g_idx[:, :, None], (T, 1, EXP_PER_GROUP))
    le_sel = jnp.take_along_axis(le, sel, axis=1)[:, 0]
    pe_top, e_idx = lax.top_k(jax.nn.softmax(le_sel, -1), TOP_K)
    pe_top = pe_top / jnp.sum(pe_top, -1, keepdims=True)
    gate = (pg_top * pe_top).reshape(-1)
    expert_id = (g_idx * EXP_PER_GROUP + e_idx).reshape(-1).astype(jnp.int32)
    token_id = jnp.repeat(jnp.arange(T, dtype=jnp.int32), TOP_K)
    order = jnp.argsort(expert_id)
    se, st, sg = expert_id[order], token_id[order], gate[order]
    counts = jnp.zeros((N_EXPERTS,), jnp.int32).at[expert_id].add(1)
    starts = jnp.cumsum(counts) - counts
    padded = (counts + MOE_BLOCK - 1) // MOE_BLOCK * MOE_BLOCK
    pad_ends = jnp.cumsum(padded)
    pad_starts = pad_ends - padded
    dest = pad_starts[se] + (jnp.arange(T * TOP_K, dtype=jnp.int32) - starts[se])
    n_blocks = (T * TOP_K + MOE_BLOCK - 1) // MOE_BLOCK + N_EXPERTS
    n_rows = n_blocks * MOE_BLOCK
    row_tok = jnp.full((n_rows,), T, jnp.int32).at[dest].set(st)
    row_gate = jnp.zeros((n_rows,), f32).at[dest].set(sg)
    block_start = jnp.arange(n_blocks, dtype=jnp.int32) * MOE_BLOCK
    block_exp = jnp.minimum(jnp.sum(block_start[:, None] >= pad_ends[None, :], axis=1), N_EXPERTS - 1)
    x_pad = jnp.concatenate([xf, jnp.zeros((1, D), xf.dtype)], 0)
    xb = x_pad[row_tok].reshape(n_blocks, MOE_BLOCK, D)

    def expert_block(args):
        xblk, e = args
        hmid = jax.nn.silu(xblk @ w_gate[e]) * (xblk @ w_up[e])
        return hmid @ w_down[e]

    yb = lax.map(expert_block, (xb, block_exp)).reshape(n_rows, D)
    out = jnp.zeros((T + 1, D), f32).at[row_tok].add(yb.astype(f32) * row_gate[:, None])
    return out[:T].reshape(B_, S_, D).astype(xn.dtype)


def setup_inputs(seed: int = 0) -> dict:
    key = jax.random.key(seed)
    keys = jax.random.split(key, 48)
    f32 = jnp.float32
    L = DEPTH

    def nrm(i, shape, scale):
        return jax.random.normal(keys[i], shape, f32) * scale

    def gain(i, shape):
        return 1.0 + 0.05 * jax.random.normal(keys[i], shape, f32)

    u = jax.random.uniform(keys[27], (L, MIX_W), f32, 0.9, 0.999)
    a_lru = u ** (1.0 / LRU_C)
    return {
        'x': nrm(0, (BATCH, SEQ, D_MODEL), 1.0),
        'positions': jnp.broadcast_to(jnp.arange(SEQ, dtype=jnp.int32)[None], (BATCH, SEQ)),
        'norm_mix_g': gain(1, (L, D_MODEL)),
        'w_in': nrm(2, (L, D_MODEL, IN_COLS), D_MODEL ** -0.5),
        'rwkv_mu': jax.random.uniform(keys[3], (L, A_COLS), f32, 0.2, 0.8),
        'rwkv_w0': nrm(4, (L, MIX_W), 0.5),
        'rwkv_w_up': nrm(5, (L, RWKV_W_LORA, MIX_W), 0.1),
        'rwkv_a0': nrm(6, (L, MIX_W), 0.5),
        'rwkv_a_up': nrm(7, (L, RWKV_A_LORA, MIX_W), 0.1),
        'rwkv_g_up': nrm(8, (L, RWKV_G_LORA, MIX_W), RWKV_G_LORA ** -0.5),
        'rwkv_k_k': 0.85 + nrm(9, (L, MIX_W), 0.05),
        'rwkv_k_a': gain(10, (L, MIX_W)),
        'rwkv_r_k': nrm(11, (L, RWKV_HEADS, HEAD_DIM), 0.1),
        'rwkv_ln_g': gain(12, (L, MIX_W)),
        'rwkv_ln_b': nrm(13, (L, MIX_W), 0.01),
        'rwkv_w_o': nrm(14, (L, MIX_W, D_MODEL), MIX_W ** -0.5),
        'ret_norm_g': gain(15, (L, MIX_W)),
        'ret_w_o': nrm(16, (L, MIX_W, D_MODEL), MIX_W ** -0.5),
        'lru_conv_w': nrm(17, (L, CONV_W, MIX_W), CONV_W ** -0.5),
        'lru_conv_b': nrm(18, (L, MIX_W), 0.01),
        'lru_w_a': nrm(19, (L, LRU_BLOCKS, LRU_BD, LRU_BD), LRU_BD ** -0.5),
        'lru_b_a': nrm(20, (L, MIX_W), 0.01),
        'lru_w_x': nrm(21, (L, LRU_BLOCKS, LRU_BD, LRU_BD), LRU_BD ** -0.5),
        'lru_b_x': nrm(22, (L, MIX_W), 0.01),
        'lru_lambda': jnp.log(a_lru) - jnp.log1p(-a_lru),
        'lru_w_o': nrm(23, (L, MIX_W, D_MODEL), MIX_W ** -0.5),
        'gla_alpha_up': nrm(24, (L, GLA_LORA, MIX_W // 2), GLA_LORA ** -0.5),
        'gla_alpha_b': nrm(25, (L, MIX_W // 2), 0.1),
        'gla_norm_g': gain(26, (L, MIX_W)),
        'gla_w_o': nrm(28, (L, MIX_W, D_MODEL), MIX_W ** -0.5),
        'w_out': nrm(29, (L, D_MODEL, D_MODEL), D_MODEL ** -0.5),
        'norm_ffn_g': gain(30, (L, D_MODEL)),
        'router_group_w': nrm(31, (L, D_MODEL, N_GROUPS), D_MODEL ** -0.5),
        'router_group_b': nrm(32, (L, N_GROUPS), 0.01),
        'router_expert_w': nrm(33, (L, D_MODEL, N_EXPERTS), D_MODEL ** -0.5),
        'router_expert_b': nrm(34, (L, N_EXPERTS), 0.01),
        'moe_w_gate': nrm(35, (L, N_EXPERTS, D_MODEL, D_EXPERT), D_MODEL ** -0.5),
        'moe_w_up': nrm(36, (L, N_EXPERTS, D_MODEL, D_EXPERT), D_MODEL ** -0.5),
        'moe_w_down': nrm(37, (L, N_EXPERTS, D_EXPERT, D_MODEL), D_EXPERT ** -0.5),
        'final_norm_g': gain(38, (D_MODEL,)),
    }


def reference(x, positions, norm_mix_g, w_in, rwkv_mu, rwkv_w0, rwkv_w_up, rwkv_a0, rwkv_a_up,
              rwkv_g_up, rwkv_k_k, rwkv_k_a, rwkv_r_k, rwkv_ln_g, rwkv_ln_b, rwkv_w_o,
              ret_norm_g, ret_w_o, lru_conv_w, lru_conv_b, lru_w_a, lru_b_a, lru_w_x, lru_b_x,
              lru_lambda, lru_w_o, gla_alpha_up, gla_alpha_b, gla_norm_g, gla_w_o, w_out,
              norm_ffn_g, router_group_w, router_group_b, router_expert_w, router_expert_b,
              moe_w_gate, moe_w_up, moe_w_down, final_norm_g):
    B_, S_, _ = x.shape
    splits = [A_COLS, A_COLS + B_COLS, A_COLS + B_COLS + C_COLS, A_COLS + B_COLS + C_COLS + D_COLS]
    h = x
    for l in range(DEPTH):
        xn = rmsnorm(h, norm_mix_g[l])
        proj = xn @ w_in[l]
        a_cols, b_cols, c_cols, d_cols, gate_cols = jnp.split(proj, splits, axis=-1)
        y_a = rwkv7_branch(a_cols, rwkv_mu[l], rwkv_w0[l], rwkv_w_up[l], rwkv_a0[l], rwkv_a_up[l],
                           rwkv_g_up[l], rwkv_k_k[l], rwkv_k_a[l], rwkv_r_k[l], rwkv_ln_g[l], rwkv_ln_b[l])
        y_b = retention_branch(b_cols, positions, ret_norm_g[l])
        y_c = rglru_branch(c_cols, positions, lru_conv_w[l], lru_conv_b[l], lru_w_a[l], lru_b_a[l],
                           lru_w_x[l], lru_b_x[l], lru_lambda[l])
        y_d = gla_branch(d_cols, gla_alpha_up[l], gla_alpha_b[l], gla_norm_g[l])
        gates = jax.nn.sigmoid(gate_cols.astype(jnp.float32)).reshape(B_, S_, N_BRANCH, D_MODEL)
        merged = (gates[:, :, 0] * (y_a @ rwkv_w_o[l]).astype(jnp.float32)
                  + gates[:, :, 1] * (y_b @ ret_w_o[l]).astype(jnp.float32)
                  + gates[:, :, 2] * (y_c @ lru_w_o[l]).astype(jnp.float32)
                  + gates[:, :, 3] * (y_d @ gla_w_o[l]).astype(jnp.float32))
        h = h + (merged.astype(h.dtype) @ w_out[l])
        hn = rmsnorm(h, norm_ffn_g[l])
        h = h + hier_moe(hn, router_group_w[l], router_group_b[l], router_expert_w[l],
                         router_expert_b[l], moe_w_gate[l], moe_w_up[l], moe_w_down[l])
    return rmsnorm(h, final_norm_g)
```

```python
import functools
import math

import jax
import jax.numpy as jnp
from jax import lax
from jax.experimental import pallas as pl
from jax.experimental.pallas import tpu as pltpu

F32 = jnp.float32
BF16 = jnp.bfloat16
HI = lax.Precision.HIGHEST

D_MODEL = 2048
MIX_W = 512
NORM_EPS = 1e-6
HEAD_DIM = 64

RWKV_W_LORA, RWKV_A_LORA, RWKV_G_LORA = 32, 32, 96
RWKV_LORA = RWKV_W_LORA + RWKV_A_LORA + RWKV_G_LORA
RWKV_DECAY_SCALE = 0.6065306597126334
RWKV_LN_EPS = 64e-5
RWKV_CHUNK = 64

RET_HEADS = 8
RET_CHUNK = 128
ROPE_BASE = 10000.0

LRU_BLOCKS = 8
CONV_W = 4
LRU_C = 8.0

GLA_HEADS = 4
GLA_DK = 64
GLA_DV = 128
GLA_LORA = 16
GLA_GATE_NORM = 16.0
GLA_CHUNK = 64

N_GROUPS = 4
EXP_PER_GROUP = 8
N_EXPERTS = N_GROUPS * EXP_PER_GROUP
TOP_K = 2
D_EXPERT = 1024
MOE_BLOCK = 128

A_COLS = 3 * MIX_W + RWKV_LORA
B_COLS = 4 * MIX_W
C_COLS = 2 * MIX_W
D_COLS = 2 * (MIX_W // 2) + MIX_W + GLA_LORA + MIX_W

OFF_BQ, OFF_BK, OFF_BV, OFF_BG = 0, 512, 1024, 1536
OFF_CX, OFF_CG = 2048, 2560
OFF_DV, OFF_DGT = 3072, 3584
OFF_AR, OFF_AK, OFF_AV = 4096, 4608, 5120
OFF_DQ, OFF_DK = 5632, 5888
OFF_AL = 6144
OFF_DAL = 6400
OFF_G = 6656
NP_COLS = OFF_G + 4 * D_MODEL

VMEM_LIMIT = 48 * 1024 * 1024


def _cparams(sem):
    return pltpu.CompilerParams(dimension_semantics=sem, vmem_limit_bytes=VMEM_LIMIT)


def _dot(a, b, prec=None):
    return jnp.dot(a, b, precision=prec, preferred_element_type=F32)


def _dot_nt(a, b, prec=None):
    return lax.dot_general(a, b, (((1,), (1,)), ((), ())), precision=prec, preferred_element_type=F32)


def _dot_tn(a, b, prec=None):
    return lax.dot_general(a, b, (((0,), (0,)), ((), ())), precision=prec, preferred_element_type=F32)


def _sigmoid(x):
    return 1.0 / (1.0 + jnp.exp(-x))


def _block_diag_const(n, blk, value):
    r = lax.broadcasted_iota(jnp.int32, (n, n), 0) // blk
    c = lax.broadcasted_iota(jnp.int32, (n, n), 1) // blk
    return jnp.where(r == c, value, 0.0).astype(F32)


def _rmsnorm_kernel(x_ref, g_ref, o_ref):
    x = x_ref[...]
    ms = jnp.mean(x * x, axis=-1, keepdims=True)
    o_ref[...] = (x * lax.rsqrt(ms + NORM_EPS) * g_ref[...]).astype(o_ref.dtype)


def _rmsnorm(x, g, out_dtype, tb=512):
    t, d = x.shape
    return pl.pallas_call(
        _rmsnorm_kernel,
        grid=(t // tb,),
        in_specs=[pl.BlockSpec((tb, d), lambda i: (i, 0)), pl.BlockSpec((1, d), lambda i: (0, 0))],
        out_specs=pl.BlockSpec((tb, d), lambda i: (i, 0)),
        out_shape=jax.ShapeDtypeStruct((t, d), out_dtype),
        compiler_params=_cparams(("parallel",)),
        name="rmsnorm",
    )(x, g.reshape(1, d))


def _mm_kernel(a_ref, b_ref, o_ref):
    o_ref[...] = _dot(a_ref[...], b_ref[...]).astype(o_ref.dtype)


def _matmul(a, b, tm, tn, out_dtype=F32, name="matmul"):
    m, k = a.shape
    n = b.shape[1]
    return pl.pallas_call(
        _mm_kernel,
        grid=(m // tm, n // tn),
        in_specs=[pl.BlockSpec((tm, k), lambda i, j: (i, 0)), pl.BlockSpec((k, tn), lambda i, j: (0, j))],
        out_specs=pl.BlockSpec((tm, tn), lambda i, j: (i, j)),
        out_shape=jax.ShapeDtypeStruct((m, n), out_dtype),
        compiler_params=_cparams(("parallel", "parallel")),
        name=name,
    )(a, b)


def _mm_res_kernel(a_ref, b_ref, r_ref, o_ref):
    o_ref[...] = r_ref[...] + _dot(a_ref[...], b_ref[...])


def _matmul_residual(a, b, res, tm, tn):
    m, k = a.shape
    n = b.shape[1]
    return pl.pallas_call(
        _mm_res_kernel,
        grid=(m // tm, n // tn),
        in_specs=[pl.BlockSpec((tm, k), lambda i, j: (i, 0)), pl.BlockSpec((k, tn), lambda i, j: (0, j)),
                  pl.BlockSpec((tm, tn), lambda i, j: (i, j))],
        out_specs=pl.BlockSpec((tm, tn), lambda i, j: (i, j)),
        out_shape=jax.ShapeDtypeStruct((m, n), F32),
        compiler_params=_cparams(("parallel", "parallel")),
        name="wout_residual",
    )(a, b, res)


def _shifted(x, tail_ref, sc_ref, width, first):
    tb = x.shape[0]
    sc_ref[0:8, 0:width] = jnp.where(first, 0.0, tail_ref[...])
    sc_ref[8:8 + tb, 0:width] = x
    return sc_ref[7:7 + tb, 0:width]


def _rwkv_prep_kernel(r_ref, k_ref, v_ref, l_ref, rt_ref, kt_ref, vt_ref, lt_ref,
                      mur_ref, muk_ref, muv_ref, mul_ref, wl_ref, b0_ref, kk_ref, ka_ref,
                      ro_ref, lw_ref, ko_ref, vo_ref, kkn_ref, kka_ref, g_ref, sc_ref):
    first = pl.program_id(0) == 0

    def mix(x_ref, t_ref, mu_ref, width):
        x = x_ref[...]
        prev = _shifted(x, t_ref, sc_ref, width, first)
        return x + (prev - x) * mu_ref[...]

    r = mix(r_ref, rt_ref, mur_ref, MIX_W)
    k = mix(k_ref, kt_ref, muk_ref, MIX_W)
    v = mix(v_ref, vt_ref, muv_ref, MIX_W)
    zl = mix(l_ref, lt_ref, mul_ref, 256)
    lane = lax.broadcasted_iota(jnp.int32, zl.shape, 1)
    act = jnp.where(lane < RWKV_W_LORA, jnp.tanh(zl),
                    jnp.where(lane < RWKV_W_LORA + RWKV_A_LORA, zl, _sigmoid(zl)))
    lo = _dot(act, wl_ref[...], HI) + b0_ref[...]
    lw = -RWKV_DECAY_SCALE * _sigmoid(lo[:, 0:MIX_W])
    a = _sigmoid(lo[:, MIX_W:2 * MIX_W])
    g = lo[:, 2 * MIX_W:3 * MIX_W]
    kk = k * kk_ref[...]
    ss = _dot(kk * kk, _block_diag_const(MIX_W, HEAD_DIM, 1.0), HI)
    kkn = kk / jnp.maximum(jnp.sqrt(ss), 1e-12)
    ro_ref[...] = r
    lw_ref[...] = lw
    ko_ref[...] = k * (1.0 + (a - 1.0) * ka_ref[...])
    vo_ref[...] = v
    kkn_ref[...] = kkn
    kka_ref[...] = kkn * a
    g_ref[...] = g


def _rwkv_prep(p, mu, w0, w_up, a0, a_up, g_up, k_k, k_a, tb=256):
    t = p.shape[0]
    nb8 = tb // 8

    def blk(width, off):
        return pl.BlockSpec((tb, width), lambda i: (i, off // width))

    def tail(width, off):
        return pl.BlockSpec((8, width), lambda i: (jnp.maximum(i * nb8 - 1, 0), off // width))

    def vec(width):
        return pl.BlockSpec((1, width), lambda i: (0, 0))

    w_lora = jnp.zeros((256, 3 * MIX_W), F32)
    w_lora = w_lora.at[0:32, 0:MIX_W].set(w_up)
    w_lora = w_lora.at[32:64, MIX_W:2 * MIX_W].set(a_up)
    w_lora = w_lora.at[64:160, 2 * MIX_W:].set(g_up)
    b0 = jnp.concatenate([w0, a0, jnp.zeros((MIX_W,), F32)]).reshape(1, 3 * MIX_W)
    mu_l = jnp.concatenate([mu[3 * MIX_W:], jnp.zeros((256 - RWKV_LORA,), F32)]).reshape(1, 256)
    out = jax.ShapeDtypeStruct((t, MIX_W), F32)
    return pl.pallas_call(
        _rwkv_prep_kernel,
        grid=(t // tb,),
        in_specs=[blk(MIX_W, OFF_AR), blk(MIX_W, OFF_AK), blk(MIX_W, OFF_AV), blk(256, OFF_AL),
                  tail(MIX_W, OFF_AR), tail(MIX_W, OFF_AK), tail(MIX_W, OFF_AV), tail(256, OFF_AL),
                  vec(MIX_W), vec(MIX_W), vec(MIX_W), vec(256),
                  pl.BlockSpec((256, 3 * MIX_W), lambda i: (0, 0)), vec(3 * MIX_W), vec(MIX_W), vec(MIX_W)],
        out_specs=[pl.BlockSpec((tb, MIX_W), lambda i: (i, 0))] * 7,
        out_shape=[out] * 7,
        scratch_shapes=[pltpu.VMEM((tb + 8, MIX_W), F32)],
        compiler_params=_cparams(("parallel",)),
        name="rwkv_prep",
    )(p, p, p, p, p, p, p, p,
      mu[0:MIX_W].reshape(1, -1), mu[MIX_W:2 * MIX_W].reshape(1, -1), mu[2 * MIX_W:3 * MIX_W].reshape(1, -1), mu_l,
      w_lora, b0, k_k.reshape(1, -1), k_a.reshape(1, -1))


def _rwkv_head_chunk(r, lw, k, v, kk, kka, s):
    c = r.shape[0]
    row = lax.broadcasted_iota(jnp.int32, (c, c), 0)
    col = lax.broadcasted_iota(jnp.int32, (c, c), 1)
    incl = row >= col
    strict = row > col
    cum = _dot(incl.astype(F32), lw, HI)
    last = cum[c - 1:c, :]
    pinv = jnp.exp(-cum)
    at = -kk * jnp.exp(cum - lw)
    rt = r * jnp.exp(cum)
    bt = kka * pinv
    kt = k * pinv
    dl = jnp.exp(last - cum)
    lhs = jnp.concatenate([at, rt], axis=0)
    rhs = jnp.concatenate([bt, kt], axis=0)
    sc = _dot_nt(lhs, rhs, HI)
    a_ab = jnp.where(strict, sc[0:c, 0:c], 0.0)
    a_ak = jnp.where(strict, sc[0:c, c:2 * c], 0.0)
    r_b = jnp.where(incl, sc[c:2 * c, 0:c], 0.0)
    r_k = jnp.where(incl, sc[c:2 * c, c:2 * c], 0.0)
    x = jnp.where(row == col, 1.0, 0.0) + a_ab
    pw = a_ab
    for _ in range(int(math.log2(c)) - 1):
        pw = _dot(pw, pw, HI)
        x = x + _dot(x, pw, HI)
    ls = _dot_nt(lhs, s, HI)
    u = _dot(x, ls[0:c] + _dot(a_ak, v, HI), HI)
    uv = jnp.concatenate([u, v], axis=0)
    y = ls[c:2 * c] + _dot(jnp.concatenate([r_b, r_k], axis=1), uv, HI)
    s_new = s * jnp.exp(last) + _dot_tn(uv, jnp.concatenate([kka * dl, k * dl], axis=0), HI)
    return y, s_new


def _rwkv_rec_kernel(r_ref, lw_ref, k_ref, v_ref, kk_ref, kka_ref, g_ref, lng_ref, lnb_ref, rk_ref,
                     o_ref, s_ref):
    @pl.when(pl.program_id(1) == 0)
    def _():
        s_ref[...] = jnp.zeros_like(s_ref)

    r, lw, k, v, kk, kka = (x[...] for x in (r_ref, lw_ref, k_ref, v_ref, kk_ref, kka_ref))
    ys = []
    for hh in range(2):
        sl = slice(hh * HEAD_DIM, (hh + 1) * HEAD_DIM)
        y, s_new = _rwkv_head_chunk(r[:, sl], lw[:, sl], k[:, sl], v[:, sl], kk[:, sl], kka[:, sl], s_ref[hh])
        s_ref[hh] = s_new
        ys.append(y)
    y = jnp.concatenate(ys, axis=1)
    avg = _block_diag_const(2 * HEAD_DIM, HEAD_DIM, 1.0 / HEAD_DIM)
    mean = _dot(y, avg, HI)
    yc = y - mean
    var = _dot(yc * yc, avg, HI)
    yn = yc * lax.rsqrt(var + RWKV_LN_EPS) * lng_ref[...] + lnb_ref[...]
    bonus = _dot(r * k * rk_ref[...], _block_diag_const(2 * HEAD_DIM, HEAD_DIM, 1.0), HI) * v
    o_ref[...] = ((yn + bonus) * g_ref[...]).astype(o_ref.dtype)


def _rwkv_recurrence(r, lw, k, v, kkn, kka, g, ln_g, ln_b, r_k):
    t = r.shape[0]
    c = RWKV_CHUNK
    blk = pl.BlockSpec((c, 128), lambda h, i: (i, h))
    vec = pl.BlockSpec((1, 128), lambda h, i: (0, h))
    return pl.pallas_call(
        _rwkv_rec_kernel,
        grid=(MIX_W // 128, t // c),
        in_specs=[blk] * 7 + [vec] * 3,
        out_specs=blk,
        out_shape=jax.ShapeDtypeStruct((t, MIX_W), BF16),
        scratch_shapes=[pltpu.VMEM((2, HEAD_DIM, HEAD_DIM), F32)],
        compiler_params=_cparams(("parallel", "arbitrary")),
        name="rwkv_recurrence",
    )(r, lw, k, v, kkn, kka, g, ln_g.reshape(1, -1), ln_b.reshape(1, -1), r_k.reshape(1, -1))


def _rope_kernel(pos_ref, inv_ref, cos_ref, sin_ref):
    ang = pos_ref[...].astype(F32) * inv_ref[...]
    lane = lax.broadcasted_iota(jnp.int32, ang.shape, 1)
    cos_ref[...] = jnp.cos(ang)
    sin_ref[...] = jnp.where(lane % HEAD_DIM < HEAD_DIM // 2, -jnp.sin(ang), jnp.sin(ang))


def _rope_table(pos_b, tb=512):
    t = pos_b.shape[0]
    d = HEAD_DIM
    inv = 1.0 / (ROPE_BASE ** (jnp.arange(0, d, 2, dtype=F32) / d))
    inv_b = jnp.tile(inv, 4).reshape(1, 128)
    blk = pl.BlockSpec((tb, 128), lambda i: (i, 0))
    return pl.pallas_call(
        _rope_kernel,
        grid=(t // tb,),
        in_specs=[blk, pl.BlockSpec((1, 128), lambda i: (0, 0))],
        out_specs=[blk, blk],
        out_shape=[jax.ShapeDtypeStruct((t, 128), F32)] * 2,
        compiler_params=_cparams(("parallel",)),
        name="rope_table",
    )(pos_b, inv_b)


def _ret_kernel(q_ref, k_ref, v_ref, g_ref, cos_ref, sin_ref, dec_ref, zeta_ref, xi_ref, cd_ref, ng_ref,
                o_ref, s_ref):
    @pl.when(pl.program_id(1) == 0)
    def _():
        s_ref[...] = jnp.zeros_like(s_ref)

    cos = cos_ref[...]
    sin = sin_ref[...]
    lane = lax.broadcasted_iota(jnp.int32, cos.shape, 1)
    lower_half = lane % HEAD_DIM < HEAD_DIM // 2

    def rope(x):
        swapped = jnp.where(lower_half, pltpu.roll(x, 128 - HEAD_DIM // 2, 1), pltpu.roll(x, HEAD_DIM // 2, 1))
        return x * cos + swapped * sin

    q = rope(q_ref[...])
    k = rope(k_ref[...]) * (HEAD_DIM ** -0.5)
    v = v_ref[...]
    qx = q * xi_ref[0]
    kz = k * zeta_ref[0]
    cd = cd_ref[0]
    outs = []
    for hh in range(2):
        sl = slice(hh * HEAD_DIM, (hh + 1) * HEAD_DIM)
        s = s_ref[hh]
        scores = _dot_nt(q[:, sl], k[:, sl], HI) * dec_ref[hh]
        outs.append(_dot(scores, v[:, sl], HI) + _dot(qx[:, sl], s, HI))
        s_ref[hh] = s * cd[0:HEAD_DIM, sl] + _dot_tn(kz[:, sl], v[:, sl], HI)
    o = jnp.concatenate(outs, axis=1)
    ms = _dot(o * o, _block_diag_const(128, HEAD_DIM, 1.0 / HEAD_DIM), HI)
    on = o * lax.rsqrt(ms + NORM_EPS) * ng_ref[...]
    gt = g_ref[...]
    o_ref[...] = (on * (gt * _sigmoid(gt))).astype(o_ref.dtype)


def _retention(p, cos_t, sin_t, norm_g):
    t = p.shape[0]
    c = RET_CHUNK
    h = RET_HEADS
    log_g = jnp.log(1.0 - jnp.exp(jnp.linspace(math.log(1.0 / 32), math.log(1.0 / 512), h, dtype=F32)))
    idx = jnp.arange(c, dtype=F32)
    diff = idx[:, None] - idx[None, :]
    decay = jnp.where(diff >= 0, jnp.exp(log_g[:, None, None] * jnp.maximum(diff, 0.0)[None]), 0.0)
    zeta = jnp.exp(log_g[:, None] * (c - 1 - idx)[None])
    xi = jnp.exp(log_g[:, None] * (idx + 1)[None])
    cdec = jnp.exp(log_g * c)

    def pair_lanes(x):
        return jnp.repeat(x.reshape(h // 2, 2, c).transpose(0, 2, 1), HEAD_DIM, axis=2)

    zeta_b = pair_lanes(zeta)
    xi_b = pair_lanes(xi)
    cd_b = jnp.broadcast_to(jnp.repeat(cdec.reshape(h // 2, 1, 2), HEAD_DIM, axis=2), (h // 2, HEAD_DIM, 128))

    def blk(off):
        return pl.BlockSpec((c, 128), lambda hp, i: (i, off // 128 + hp))

    tab = pl.BlockSpec((c, 128), lambda hp, i: (i, 0))
    per_pair = pl.BlockSpec((1, c, 128), lambda hp, i: (hp, 0, 0))
    return pl.pallas_call(
        _ret_kernel,
        grid=(h // 2, t // c),
        in_specs=[blk(OFF_BQ), blk(OFF_BK), blk(OFF_BV), blk(OFF_BG), tab, tab,
                  pl.BlockSpec((2, c, c), lambda hp, i: (hp, 0, 0)), per_pair, per_pair,
                  pl.BlockSpec((1, HEAD_DIM, 128), lambda hp, i: (hp, 0, 0)),
                  pl.BlockSpec((1, 128), lambda hp, i: (0, hp))],
        out_specs=pl.BlockSpec((c, 128), lambda hp, i: (i, hp)),
        out_shape=jax.ShapeDtypeStruct((t, MIX_W), BF16),
        scratch_shapes=[pltpu.VMEM((2, HEAD_DIM, HEAD_DIM), F32)],
        compiler_params=_cparams(("parallel", "arbitrary")),
        name="retention",
    )(p, p, p, p, cos_t, sin_t, decay, zeta_b, xi_b, cd_b, norm_g.reshape(1, -1))


def _lru_kernel(x_ref, xt_ref, gb_ref, pos_ref, cw_ref, cb_ref, wa_ref, ba_ref, wx_ref, bx_ref, lam_ref,
                o_ref, sc_ref, a_sc, b_sc, h_ref):
    tb = x_ref.shape[0]
    first = pl.program_id(0) == 0

    @pl.when(first)
    def _():
        h_ref[...] = jnp.zeros_like(h_ref)

    x = x_ref[...]
    sc_ref[0:8, :] = jnp.where(first, 0.0, xt_ref[...])
    sc_ref[8:8 + tb, :] = x
    cw = cw_ref[...]
    xc = cb_ref[...] + x * cw[CONV_W - 1:CONV_W, :]
    for j in range(1, CONV_W):
        xc = xc + sc_ref[8 - j:8 - j + tb, :] * cw[CONV_W - 1 - j:CONV_W - j, :]
    r = _sigmoid(_dot(xc, wa_ref[...], HI) + ba_ref[...])
    ig = _sigmoid(_dot(xc, wx_ref[...], HI) + bx_ref[...])
    nl = -lam_ref[...]
    softplus = jnp.maximum(nl, 0.0) + jnp.log1p(jnp.exp(-jnp.abs(nl)))
    log_a = -LRU_C * r * softplus
    pos = pos_ref[...]
    reset = jnp.concatenate([pos, pos, pos, pos], axis=1) == 0
    a_sc[...] = jnp.where(reset, 0.0, jnp.exp(log_a))
    th = jnp.tanh(log_a)
    b_sc[...] = jnp.where(reset, 1.0, jnp.sqrt(-2.0 * th / (1.0 - th))) * ig * xc

    row = lax.broadcasted_iota(jnp.int32, (8, MIX_W), 0)

    def group(gi, h):
        off = pl.multiple_of(gi * 8, 8)
        a = a_sc[pl.ds(off, 8), :]
        b = b_sc[pl.ds(off, 8), :]
        for d in (1, 2, 4):
            keep = row >= d
            b = jnp.where(keep, a * pltpu.roll(b, d, 0) + b, b)
            a = jnp.where(keep, a * pltpu.roll(a, d, 0), a)
        hs = a * h + b
        b_sc[pl.ds(off, 8), :] = hs
        return hs[7:8, :]

    h_ref[...] = lax.fori_loop(0, tb // 8, group, h_ref[...])
    gb = gb_ref[...]
    gelu = gb * (0.5 * (1.0 + jnp.tanh(math.sqrt(2.0 / math.pi) * (gb + 0.044715 * (gb * gb * gb)))))
    o_ref[...] = (b_sc[...] * gelu).astype(o_ref.dtype)


def _block_diag_weight(w):
    g, n, _ = w.shape
    eye = jnp.eye(g, dtype=w.dtype)
    return (eye[:, None, :, None] * w[:, :, None, :]).reshape(g * n, g * n)


def _rglru(p, pos_b, conv_w, conv_b, w_a, b_a, w_x, b_x, lam, tb=256):
    t = p.shape[0]
    nb8 = tb // 8
    vec = pl.BlockSpec((1, MIX_W), lambda i: (0, 0))
    mat = pl.BlockSpec((MIX_W, MIX_W), lambda i: (0, 0))
    cw8 = jnp.concatenate([conv_w, jnp.zeros((8 - CONV_W, MIX_W), F32)], axis=0)
    return pl.pallas_call(
        _lru_kernel,
        grid=(t // tb,),
        in_specs=[pl.BlockSpec((tb, MIX_W), lambda i: (i, OFF_CX // MIX_W)),
                  pl.BlockSpec((8, MIX_W), lambda i: (jnp.maximum(i * nb8 - 1, 0), OFF_CX // MIX_W)),
                  pl.BlockSpec((tb, MIX_W), lambda i: (i, OFF_CG // MIX_W)),
                  pl.BlockSpec((tb, 128), lambda i: (i, 0)),
                  pl.BlockSpec((8, MIX_W), lambda i: (0, 0)), vec, mat, vec, mat, vec, vec],
        out_specs=pl.BlockSpec((tb, MIX_W), lambda i: (i, 0)),
        out_shape=jax.ShapeDtypeStruct((t, MIX_W), BF16),
        scratch_shapes=[pltpu.VMEM((tb + 8, MIX_W), F32), pltpu.VMEM((tb, MIX_W), F32),
                        pltpu.VMEM((tb, MIX_W), F32), pltpu.VMEM((1, MIX_W), F32)],
        compiler_params=_cparams(("arbitrary",)),
        name="rglru",
    )(p, p, p, pos_b, cw8, conv_b.reshape(1, -1), _block_diag_weight(w_a), b_a.reshape(1, -1),
      _block_diag_weight(w_x), b_x.reshape(1, -1), lam.reshape(1, -1))


def _gla_kernel(v_ref, gt_ref, q_ref, k_ref, al_ref, aup_ref, ab_ref, ng_ref, o_ref, s_ref):
    @pl.when(pl.program_id(0) == 0)
    def _():
        s_ref[...] = jnp.zeros_like(s_ref)

    c = q_ref.shape[0]
    row = lax.broadcasted_iota(jnp.int32, (c, c), 0)
    col = lax.broadcasted_iota(jnp.int32, (c, c), 1)
    causal = row >= col
    pre = _dot(al_ref[...], aup_ref[...], HI) + ab_ref[...]
    log_a = (jnp.minimum(pre, 0.0) - jnp.log1p(jnp.exp(-jnp.abs(pre)))) / GLA_GATE_NORM
    bcum = _dot(causal.astype(F32), log_a, HI)
    blast = bcum[c - 1:c, :]
    q_e = q_ref[...] * (GLA_DK ** -0.5) * jnp.exp(bcum)
    k = k_ref[...]
    k_e = k * jnp.exp(-bcum)
    k_l = k * jnp.exp(blast - bcum)
    dec = jnp.exp(blast)
    v = v_ref[...]
    outs = []
    for h in range(GLA_HEADS):
        sk = slice(h * GLA_DK, (h + 1) * GLA_DK)
        sv = slice(h * GLA_DV, (h + 1) * GLA_DV)
        s = s_ref[h]
        scores = jnp.where(causal, _dot_nt(q_e[:, sk], k_e[:, sk], HI), 0.0)
        o = _dot(scores, v[:, sv], HI) + _dot_nt(q_e[:, sk], s, HI)
        s_ref[h] = s * dec[:, sk] + _dot_tn(v[:, sv], k_l[:, sk], HI)
        outs.append(o * lax.rsqrt(jnp.mean(o * o, axis=-1, keepdims=True) + NORM_EPS))
    on = jnp.concatenate(outs, axis=1) * ng_ref[...]
    gt = gt_ref[...]
    o_ref[...] = (on * (gt * _sigmoid(gt))).astype(o_ref.dtype)


def _gla(p, alpha_up, alpha_b, norm_g):
    t = p.shape[0]
    c = GLA_CHUNK
    hk = MIX_W // 2
    aup = jnp.concatenate([alpha_up, jnp.zeros((128 - GLA_LORA, hk), F32)], axis=0)

    def blk(width, off):
        return pl.BlockSpec((c, width), lambda i: (i, off // width))

    return pl.pallas_call(
        _gla_kernel,
        grid=(t // c,),
        in_specs=[blk(MIX_W, OFF_DV), blk(MIX_W, OFF_DGT), blk(hk, OFF_DQ), blk(hk, OFF_DK), blk(128, OFF_DAL),
                  pl.BlockSpec((128, hk), lambda i: (0, 0)), pl.BlockSpec((1, hk), lambda i: (0, 0)),
                  pl.BlockSpec((1, MIX_W), lambda i: (0, 0))],
        out_specs=pl.BlockSpec((c, MIX_W), lambda i: (i, 0)),
        out_shape=jax.ShapeDtypeStruct((t, MIX_W), BF16),
        scratch_shapes=[pltpu.VMEM((GLA_HEADS, GLA_DV, GLA_DK), F32)],
        compiler_params=_cparams(("arbitrary",)),
        name="gla",
    )(p, p, p, p, p, aup, alpha_b.reshape(1, -1), norm_g.reshape(1, -1))


def _merge_kernel(ya_ref, yb_ref, yc_ref, yd_ref, wa_ref, wb_ref, wc_ref, wd_ref,
                  ga_ref, gb_ref, gc_ref, gd_ref, o_ref):
    acc = _sigmoid(ga_ref[...]) * _dot(ya_ref[...], wa_ref[...])
    acc += _sigmoid(gb_ref[...]) * _dot(yb_ref[...], wb_ref[...])
    acc += _sigmoid(gc_ref[...]) * _dot(yc_ref[...], wc_ref[...])
    acc += _sigmoid(gd_ref[...]) * _dot(yd_ref[...], wd_ref[...])
    o_ref[...] = acc.astype(o_ref.dtype)


def _merge(ys, ws, p, tm=512, tn=512):
    t = p.shape[0]
    yspec = pl.BlockSpec((tm, MIX_W), lambda i, j: (i, 0))
    wspec = pl.BlockSpec((MIX_W, tn), lambda i, j: (0, j))

    def gspec(kk):
        return pl.BlockSpec((tm, tn), lambda i, j: (i, (OFF_G + kk * D_MODEL) // tn + j))

    return pl.pallas_call(
        _merge_kernel,
        grid=(t // tm, D_MODEL // tn),
        in_specs=[yspec] * 4 + [wspec] * 4 + [gspec(kk) for kk in range(4)],
        out_specs=pl.BlockSpec((tm, tn), lambda i, j: (i, j)),
        out_shape=jax.ShapeDtypeStruct((t, D_MODEL), BF16),
        compiler_params=_cparams(("parallel", "parallel")),
        name="merge",
    )(*ys, *ws, p, p, p, p)


def _ffn_norm_kernel(h_ref, g_ref, wr_ref, hn_ref, lg_ref):
    x = h_ref[...]
    ms = jnp.mean(x * x, axis=-1, keepdims=True)
    hn = x * lax.rsqrt(ms + NORM_EPS) * g_ref[...]
    hn_ref[...] = hn
    lg_ref[...] = _dot(hn, wr_ref[...], HI)


def _ffn_norm_router(h, g, w_r, tb=256):
    t, d = h.shape
    return pl.pallas_call(
        _ffn_norm_kernel,
        grid=(t // tb,),
        in_specs=[pl.BlockSpec((tb, d), lambda i: (i, 0)), pl.BlockSpec((1, d), lambda i: (0, 0)),
                  pl.BlockSpec((d, 128), lambda i: (0, 0))],
        out_specs=[pl.BlockSpec((tb, d), lambda i: (i, 0)), pl.BlockSpec((tb, 128), lambda i: (i, 0))],
        out_shape=[jax.ShapeDtypeStruct((t, d), F32), jax.ShapeDtypeStruct((t, 128), F32)],
        compiler_params=_cparams(("parallel",)),
        name="ffn_norm_router",
    )(h, g.reshape(1, d), w_r)


def _row_copy(src_hbm, idx, dst, r, sem):
    return pltpu.make_async_copy(src_hbm.at[pl.ds(idx, 1), :], dst.at[pl.ds(r, 1), :], sem)


def _expert_kernel(tok_ref, exp_ref, nblk_ref, x_hbm, gate_ref, wg_ref, wu_ref, wd_ref, o_ref, xbuf, sem):
    i = pl.program_id(0)
    active = i < nblk_ref[0]

    @pl.when(active)
    def _():
        base = i * MOE_BLOCK

        def start(r, carry):
            _row_copy(x_hbm, tok_ref[base + r], xbuf, r, sem).start()
            return carry

        lax.fori_loop(0, MOE_BLOCK, start, 0)

        def wait(r, carry):
            _row_copy(x_hbm, tok_ref[base + r], xbuf, r, sem).wait()
            return carry

        lax.fori_loop(0, MOE_BLOCK, wait, 0)
        x = xbuf[...].astype(BF16)
        gate = _dot(x, wg_ref[0])
        up = _dot(x, wu_ref[0])
        hmid = (gate * _sigmoid(gate) * up).astype(BF16)
        o_ref[...] = _dot(hmid, wd_ref[0]) * gate_ref[...]

    @pl.when(jnp.logical_not(active))
    def _():
        o_ref[...] = jnp.zeros_like(o_ref)


def _experts(hn, row_tok, block_exp, n_used, row_gate, w_gate, w_up, w_down):
    n_rows = row_tok.shape[0]
    n_blocks = n_rows // MOE_BLOCK
    d = hn.shape[1]
    grid_spec = pltpu.PrefetchScalarGridSpec(
        num_scalar_prefetch=3,
        grid=(n_blocks,),
        in_specs=[pl.BlockSpec(memory_space=pl.ANY),
                  pl.BlockSpec((MOE_BLOCK, 1), lambda i, tok, ex, nb: (i, 0)),
                  pl.BlockSpec((1, d, D_EXPERT), lambda i, tok, ex, nb: (ex[i], 0, 0)),
                  pl.BlockSpec((1, d, D_EXPERT), lambda i, tok, ex, nb: (ex[i], 0, 0)),
                  pl.BlockSpec((1, D_EXPERT, d), lambda i, tok, ex, nb: (ex[i], 0, 0))],
        out_specs=pl.BlockSpec((MOE_BLOCK, d), lambda i, tok, ex, nb: (i, 0)),
        scratch_shapes=[pltpu.VMEM((MOE_BLOCK, d), F32), pltpu.SemaphoreType.DMA(())],
    )
    return pl.pallas_call(
        _expert_kernel,
        grid_spec=grid_spec,
        out_shape=jax.ShapeDtypeStruct((n_rows, d), F32),
        compiler_params=_cparams(("arbitrary",)),
        name="experts",
    )(row_tok, block_exp, n_used, hn, row_gate, w_gate, w_up, w_down)


def _combine_kernel(pos_ref, y_hbm, h_ref, g_ref, o_ref, buf, sem, *, final_norm):
    tb = h_ref.shape[0]
    base = pl.program_id(0) * tb * TOP_K

    def start(r, carry):
        for s in range(TOP_K):
            _row_copy(y_hbm, pos_ref[base + r * TOP_K + s], buf.at[s], r, sem).start()
        return carry

    lax.fori_loop(0, tb, start, 0)

    def wait(r, carry):
        for s in range(TOP_K):
            _row_copy(y_hbm, pos_ref[base + r * TOP_K + s], buf.at[s], r, sem).wait()
        return carry

    lax.fori_loop(0, tb, wait, 0)
    out = h_ref[...] + (buf[0] + buf[1])
    if final_norm:
        ms = jnp.mean(out * out, axis=-1, keepdims=True)
        out = out * lax.rsqrt(ms + NORM_EPS) * g_ref[...]
    o_ref[...] = out


def _combine(h, y_rows, pos, g, final_norm, tb=128):
    t, d = h.shape
    grid_spec = pltpu.PrefetchScalarGridSpec(
        num_scalar_prefetch=1,
        grid=(t // tb,),
        in_specs=[pl.BlockSpec(memory_space=pl.ANY),
                  pl.BlockSpec((tb, d), lambda i, pos: (i, 0)),
                  pl.BlockSpec((1, d), lambda i, pos: (0, 0))],
        out_specs=pl.BlockSpec((tb, d), lambda i, pos: (i, 0)),
        scratch_shapes=[pltpu.VMEM((TOP_K, tb, d), F32), pltpu.SemaphoreType.DMA(())],
    )
    return pl.pallas_call(
        functools.partial(_combine_kernel, final_norm=final_norm),
        grid_spec=grid_spec,
        out_shape=jax.ShapeDtypeStruct((t, d), F32),
        compiler_params=_cparams(("arbitrary",)),
        name="moe_combine",
    )(pos, y_rows, h, g.reshape(1, d))


def _routing(logits, bg, be):
    t = logits.shape[0]
    pg = jax.nn.softmax(logits[:, 0:N_GROUPS] + bg, -1)
    pg_top, g_idx = lax.top_k(pg, 1)
    le = (logits[:, N_GROUPS:N_GROUPS + N_EXPERTS] + be).reshape(t, N_GROUPS, EXP_PER_GROUP)
    le_sel = jnp.take_along_axis(le, jnp.broadcast_to(g_idx[:, :, None], (t, 1, EXP_PER_GROUP)), axis=1)[:, 0]
    pe_top, e_idx = lax.top_k(jax.nn.softmax(le_sel, -1), TOP_K)
    pe_top = pe_top / jnp.sum(pe_top, -1, keepdims=True)
    gate = (pg_top * pe_top).reshape(-1)
    expert_id = (g_idx * EXP_PER_GROUP + e_idx).reshape(-1).astype(jnp.int32)
    token_id = jnp.repeat(jnp.arange(t, dtype=jnp.int32), TOP_K)
    n_assign = t * TOP_K
    order = jnp.argsort(expert_id)
    se, st, sg = expert_id[order], token_id[order], gate[order]
    counts = jnp.zeros((N_EXPERTS,), jnp.int32).at[expert_id].add(1)
    starts = jnp.cumsum(counts) - counts
    padded = (counts + MOE_BLOCK - 1) // MOE_BLOCK * MOE_BLOCK
    pad_ends = jnp.cumsum(padded)
    pad_starts = pad_ends - padded
    dest = pad_starts[se] + (jnp.arange(n_assign, dtype=jnp.int32) - starts[se])
    n_blocks = (n_assign + MOE_BLOCK - 1) // MOE_BLOCK + N_EXPERTS
    n_rows = n_blocks * MOE_BLOCK
    row_tok = jnp.zeros((n_rows,), jnp.int32).at[dest].set(st)
    row_gate = jnp.zeros((n_rows,), F32).at[dest].set(sg)
    block_start = jnp.arange(n_blocks, dtype=jnp.int32) * MOE_BLOCK
    block_exp = jnp.minimum(jnp.sum(block_start[:, None] >= pad_ends[None, :], axis=1), N_EXPERTS - 1).astype(jnp.int32)
    n_used = (pad_ends[-1] // MOE_BLOCK).astype(jnp.int32).reshape(1)
    pos = jnp.zeros((n_assign,), jnp.int32).at[order].set(dest)
    return row_tok, row_gate.reshape(n_rows, 1), block_exp, n_used, pos


def _permute_w_in(w):
    d0 = A_COLS + B_COLS + C_COLS
    g0 = d0 + D_COLS
    b0 = A_COLS
    c0 = A_COLS + B_COLS

    def z(n):
        return jnp.zeros((w.shape[0], n), w.dtype)

    parts = [w[:, b0:b0 + B_COLS], w[:, c0:c0 + C_COLS],
             w[:, d0 + 512:d0 + 1024], w[:, d0 + 1040:d0 + 1552],
             w[:, 0:3 * MIX_W], w[:, d0:d0 + 512],
             w[:, 3 * MIX_W:A_COLS], z(256 - RWKV_LORA),
             w[:, d0 + 1024:d0 + 1040], z(128 - GLA_LORA), z(OFF_G - OFF_DAL - 128),
             w[:, g0:]]
    out = jnp.concatenate(parts, axis=1).astype(BF16)
    assert out.shape[1] == NP_COLS
    return out


def kernel(x, positions, norm_mix_g, w_in, rwkv_mu, rwkv_w0, rwkv_w_up, rwkv_a0, rwkv_a_up, rwkv_g_up, rwkv_k_k, rwkv_k_a, rwkv_r_k, rwkv_ln_g, rwkv_ln_b, rwkv_w_o, ret_norm_g, ret_w_o, lru_conv_w, lru_conv_b, lru_w_a, lru_b_a, lru_w_x, lru_b_x, lru_lambda, lru_w_o, gla_alpha_up, gla_alpha_b, gla_norm_g, gla_w_o, w_out, norm_ffn_g, router_group_w, router_group_b, router_expert_w, router_expert_b, moe_w_gate, moe_w_up, moe_w_down, final_norm_g):
    b_, s_, d = x.shape
    assert b_ == 1 and d == D_MODEL
    depth = w_in.shape[0]
    h = x.reshape(s_, d)
    pos_b = jnp.broadcast_to(positions.reshape(s_, 1), (s_, 128)).astype(jnp.int32)
    cos_t, sin_t = _rope_table(pos_b)
    for l in range(depth):
        xn = _rmsnorm(h, norm_mix_g[l], BF16)
        p = _matmul(xn, _permute_w_in(w_in[l]), 1024, 512, name="in_proj")
        prep = _rwkv_prep(p, rwkv_mu[l], rwkv_w0[l], rwkv_w_up[l], rwkv_a0[l], rwkv_a_up[l], rwkv_g_up[l],
                          rwkv_k_k[l], rwkv_k_a[l])
        y_a = _rwkv_recurrence(*prep, rwkv_ln_g[l], rwkv_ln_b[l], rwkv_r_k[l].reshape(-1))
        y_b = _retention(p, cos_t, sin_t, ret_norm_g[l])
        y_c = _rglru(p, pos_b, lru_conv_w[l], lru_conv_b[l], lru_w_a[l], lru_b_a[l], lru_w_x[l], lru_b_x[l],
                     lru_lambda[l])
        y_d = _gla(p, gla_alpha_up[l], gla_alpha_b[l], gla_norm_g[l])
        merged = _merge((y_a, y_b, y_c, y_d),
                        tuple(w[l].astype(BF16) for w in (rwkv_w_o, ret_w_o, lru_w_o, gla_w_o)), p)
        h = _matmul_residual(merged, w_out[l].astype(BF16), h, 1024, 512)
        w_r = jnp.concatenate([router_group_w[l], router_expert_w[l],
                               jnp.zeros((d, 128 - N_GROUPS - N_EXPERTS), F32)], axis=1)
        hn, logits = _ffn_norm_router(h, norm_ffn_g[l], w_r)
        row_tok, row_gate, block_exp, n_used, pos = _routing(logits, router_group_b[l], router_expert_b[l])
        y_rows = _experts(hn, row_tok, block_exp, n_used, row_gate,
                          moe_w_gate[l].astype(BF16), moe_w_up[l].astype(BF16), moe_w_down[l].astype(BF16))
        last = l == depth - 1
        h = _combine(h, y_rows, pos, final_norm_g if last else norm_ffn_g[l], final_norm=last)
    return h.reshape(b_, s_, d)
```

```python
import functools
import math

import jax
import jax.numpy as jnp
from jax import lax
from jax.experimental import pallas as pl
from jax.experimental.pallas import tpu as pltpu

F32 = jnp.float32
BF16 = jnp.bfloat16
HI = lax.Precision.HIGHEST

D_MODEL = 2048
MIX_W = 512
NORM_EPS = 1e-6
HEAD_DIM = 64
N_HEADS = MIX_W // HEAD_DIM

RWKV_W_LORA, RWKV_A_LORA, RWKV_G_LORA = 32, 32, 96
RWKV_LORA = RWKV_W_LORA + RWKV_A_LORA + RWKV_G_LORA
RWKV_DECAY_SCALE = 0.6065306597126334
RWKV_LN_EPS = 64e-5
RWKV_CHUNK = 64

RET_CHUNK = 128
ROPE_BASE = 10000.0

LRU_BLOCKS = 8
CONV_W = 4
LRU_C = 8.0

GLA_HEADS = 4
GLA_DK = 64
GLA_DV = 128
GLA_LORA = 16
GLA_GATE_NORM = 16.0
GLA_CHUNK = 64

N_GROUPS = 4
EXP_PER_GROUP = 8
N_EXPERTS = N_GROUPS * EXP_PER_GROUP
TOP_K = 2
D_EXPERT = 1024
MOE_BLOCK = 128

A_COLS = 3 * MIX_W + RWKV_LORA
B_COLS = 4 * MIX_W
C_COLS = 2 * MIX_W
D_COLS = 2 * (MIX_W // 2) + MIX_W + GLA_LORA + MIX_W

OFF_BQ, OFF_BK, OFF_BV, OFF_BG = 0, 512, 1024, 1536
OFF_CX, OFF_CG = 2048, 2560
OFF_DV, OFF_DGT = 3072, 3584
OFF_AR, OFF_AK, OFF_AV = 4096, 4608, 5120
OFF_DQ, OFF_DK = 5632, 5888
OFF_AL = 6144
OFF_DAL = 6400
OFF_G = 6656
NP_COLS = OFF_G + 4 * D_MODEL

VMEM_LIMIT = 48 * 1024 * 1024
VMEM_LIMIT_EXPERT = 56 * 1024 * 1024


def _cparams(sem, vmem=VMEM_LIMIT):
    return pltpu.CompilerParams(dimension_semantics=sem, vmem_limit_bytes=vmem)


def _dot(a, b, prec=None):
    return jnp.dot(a, b, precision=prec, preferred_element_type=F32)


def _sigmoid(x):
    return 1.0 / (1.0 + jnp.exp(-x))


def _split(x):
    hi = x.astype(BF16)
    return hi, (x - hi.astype(F32)).astype(BF16)


def _dims(form, batched):
    ca, cb = {"nn": (1, 0), "nt": (1, 1), "tn": (0, 0)}[form]
    if batched:
        return (((ca + 1,), (cb + 1,)), ((0,), (0,)))
    return (((ca,), (cb,)), ((), ()))


def _mm(a, b, mode, form="nn"):
    dims = _dims(form, a.ndim == 3)
    if mode == "hi":
        return lax.dot_general(a, b, dims, precision=HI, preferred_element_type=F32)
    d = functools.partial(lax.dot_general, dimension_numbers=dims, preferred_element_type=F32)
    if mode == "x1":
        return d(a.astype(BF16), b.astype(BF16))
    ah, al = _split(a)
    bh, bl = _split(b)
    return d(ah, bh) + (d(ah, bl) + d(al, bh))


def _mm_exact_rhs(a, b, terms):
    b = b.astype(BF16)
    acc = None
    for _ in range(terms):
        piece = a.astype(BF16)
        part = _dot(piece, b)
        acc = part if acc is None else acc + part
        a = a - piece.astype(F32)
    return acc


def _mm_exact_lhs(a, b, terms):
    a = a.astype(BF16)
    acc = None
    for _ in range(terms):
        piece = b.astype(BF16)
        part = _dot(a, piece)
        acc = part if acc is None else acc + part
        b = b - piece.astype(F32)
    return acc


def _heads(x, width=HEAD_DIM):
    return jnp.stack([x[:, h * width:(h + 1) * width] for h in range(x.shape[1] // width)])


def _block_diag_const(n, blk, value):
    r = lax.broadcasted_iota(jnp.int32, (n, n), 0) // blk
    c = lax.broadcasted_iota(jnp.int32, (n, n), 1) // blk
    return jnp.where(r == c, value, 0.0).astype(F32)


def _rmsnorm_kernel(x_ref, g_ref, o_ref):
    x = x_ref[...]
    ms = jnp.mean(x * x, axis=-1, keepdims=True)
    o_ref[...] = (x * lax.rsqrt(ms + NORM_EPS) * g_ref[...]).astype(o_ref.dtype)


def _rmsnorm(x, g, out_dtype, tb=512):
    t, d = x.shape
    return pl.pallas_call(
        _rmsnorm_kernel,
        grid=(t // tb,),
        in_specs=[pl.BlockSpec((tb, d), lambda i: (i, 0)), pl.BlockSpec((1, d), lambda i: (0, 0))],
        out_specs=pl.BlockSpec((tb, d), lambda i: (i, 0)),
        out_shape=jax.ShapeDtypeStruct((t, d), out_dtype),
        compiler_params=_cparams(("parallel",)),
        name="rmsnorm",
    )(x, g.reshape(1, d))


def _mm_kernel(a_ref, b_ref, o_ref):
    o_ref[...] = _dot(a_ref[...], b_ref[...]).astype(o_ref.dtype)


def _matmul(a, b, tm, tn, out_dtype=F32, name="matmul"):
    m, k = a.shape
    n = b.shape[1]
    return pl.pallas_call(
        _mm_kernel,
        grid=(m // tm, n // tn),
        in_specs=[pl.BlockSpec((tm, k), lambda i, j: (i, 0)), pl.BlockSpec((k, tn), lambda i, j: (0, j))],
        out_specs=pl.BlockSpec((tm, tn), lambda i, j: (i, j)),
        out_shape=jax.ShapeDtypeStruct((m, n), out_dtype),
        compiler_params=_cparams(("parallel", "parallel")),
        name=name,
    )(a, b)


def _mm_res_kernel(a_ref, b_ref, r_ref, o_ref):
    o_ref[...] = r_ref[...] + _dot(a_ref[...], b_ref[...])


def _matmul_residual(a, b, res, tm, tn):
    m, k = a.shape
    n = b.shape[1]
    return pl.pallas_call(
        _mm_res_kernel,
        grid=(m // tm, n // tn),
        in_specs=[pl.BlockSpec((tm, k), lambda i, j: (i, 0)), pl.BlockSpec((k, tn), lambda i, j: (0, j)),
                  pl.BlockSpec((tm, tn), lambda i, j: (i, j))],
        out_specs=pl.BlockSpec((tm, tn), lambda i, j: (i, j)),
        out_shape=jax.ShapeDtypeStruct((m, n), F32),
        compiler_params=_cparams(("parallel", "parallel")),
        name="wout_residual",
    )(a, b, res)


def _shifted(x, tail_ref, sc_ref, width, first):
    tb = x.shape[0]
    sc_ref[0:8, 0:width] = jnp.where(first, 0.0, tail_ref[...])
    sc_ref[8:8 + tb, 0:width] = x
    return sc_ref[7:7 + tb, 0:width]


def _rwkv_prep_kernel(r_ref, k_ref, v_ref, l_ref, rt_ref, kt_ref, vt_ref, lt_ref,
                      mur_ref, muk_ref, muv_ref, mul_ref, wl_ref, b0_ref, kk_ref, ka_ref,
                      ro_ref, lw_ref, ko_ref, vo_ref, kkn_ref, kka_ref, g_ref, sc_ref):
    first = pl.program_id(0) == 0

    def mix(x_ref, t_ref, mu_ref, width):
        x = x_ref[...]
        prev = _shifted(x, t_ref, sc_ref, width, first)
        return x + (prev - x) * mu_ref[...]

    r = mix(r_ref, rt_ref, mur_ref, MIX_W)
    k = mix(k_ref, kt_ref, muk_ref, MIX_W)
    v = mix(v_ref, vt_ref, muv_ref, MIX_W)
    zl = mix(l_ref, lt_ref, mul_ref, 256)
    lane = lax.broadcasted_iota(jnp.int32, zl.shape, 1)
    act = jnp.where(lane < RWKV_W_LORA, jnp.tanh(zl),
                    jnp.where(lane < RWKV_W_LORA + RWKV_A_LORA, zl, _sigmoid(zl)))
    lo = _mm(act, wl_ref[...], "x3") + b0_ref[...]
    lw = -RWKV_DECAY_SCALE * _sigmoid(lo[:, 0:MIX_W])
    a = _sigmoid(lo[:, MIX_W:2 * MIX_W])
    g = lo[:, 2 * MIX_W:3 * MIX_W]
    kk = k * kk_ref[...]
    ss = _mm_exact_rhs(kk * kk, _block_diag_const(MIX_W, HEAD_DIM, 1.0), 2)
    kkn = kk / jnp.maximum(jnp.sqrt(ss), 1e-12)
    ro_ref[...] = r
    lw_ref[...] = lw
    ko_ref[...] = k * (1.0 + (a - 1.0) * ka_ref[...])
    vo_ref[...] = v
    kkn_ref[...] = kkn
    kka_ref[...] = kkn * a
    g_ref[...] = g


def _rwkv_prep(p, mu, w0, w_up, a0, a_up, g_up, k_k, k_a, tb=256):
    t = p.shape[0]
    nb8 = tb // 8

    def blk(width, off):
        return pl.BlockSpec((tb, width), lambda i: (i, off // width))

    def tail(width, off):
        return pl.BlockSpec((8, width), lambda i: (jnp.maximum(i * nb8 - 1, 0), off // width))

    def vec(width):
        return pl.BlockSpec((1, width), lambda i: (0, 0))

    w_lora = jnp.zeros((256, 3 * MIX_W), F32)
    w_lora = w_lora.at[0:32, 0:MIX_W].set(w_up)
    w_lora = w_lora.at[32:64, MIX_W:2 * MIX_W].set(a_up)
    w_lora = w_lora.at[64:160, 2 * MIX_W:].set(g_up)
    b0 = jnp.concatenate([w0, a0, jnp.zeros((MIX_W,), F32)]).reshape(1, 3 * MIX_W)
    mu_l = jnp.concatenate([mu[3 * MIX_W:], jnp.zeros((256 - RWKV_LORA,), F32)]).reshape(1, 256)
    out = jax.ShapeDtypeStruct((t, MIX_W), F32)
    return pl.pallas_call(
        _rwkv_prep_kernel,
        grid=(t // tb,),
        in_specs=[blk(MIX_W, OFF_AR), blk(MIX_W, OFF_AK), blk(MIX_W, OFF_AV), blk(256, OFF_AL),
                  tail(MIX_W, OFF_AR), tail(MIX_W, OFF_AK), tail(MIX_W, OFF_AV), tail(256, OFF_AL),
                  vec(MIX_W), vec(MIX_W), vec(MIX_W), vec(256),
                  pl.BlockSpec((256, 3 * MIX_W), lambda i: (0, 0)), vec(3 * MIX_W), vec(MIX_W), vec(MIX_W)],
        out_specs=[pl.BlockSpec((tb, MIX_W), lambda i: (i, 0))] * 7,
        out_shape=[out] * 7,
        scratch_shapes=[pltpu.VMEM((tb + 8, MIX_W), F32)],
        compiler_params=_cparams(("parallel",)),
        name="rwkv_prep",
    )(p, p, p, p, p, p, p, p,
      mu[0:MIX_W].reshape(1, -1), mu[MIX_W:2 * MIX_W].reshape(1, -1), mu[2 * MIX_W:3 * MIX_W].reshape(1, -1), mu_l,
      w_lora, b0, k_k.reshape(1, -1), k_a.reshape(1, -1))


RW_SC, RW_INV, RW_APPLY, RW_STATE, RW_SCAN = "x3", "x1", "x1", "x1", "x3"


def _rwkv_local_kernel(r_ref, lw_ref, k_ref, v_ref, kk_ref, kka_ref, q_ref, y0_ref, m_ref, n_ref):
    c = r_ref.shape[0]
    row = lax.broadcasted_iota(jnp.int32, (c, c), 0)
    col = lax.broadcasted_iota(jnp.int32, (c, c), 1)
    incl = row >= col
    strict = row > col
    eye = jnp.where(row == col, 1.0, 0.0)
    lw = lw_ref[...]
    k = k_ref[...]
    kka = kka_ref[...]
    cum = _mm_exact_lhs(incl.astype(F32), lw, 3)
    last = cum[c - 1:c, :]
    pinv = jnp.exp(-cum)
    dl = jnp.exp(last - cum)
    at = _heads(-kk_ref[...] * jnp.exp(cum - lw))
    rt = _heads(r_ref[...] * jnp.exp(cum))
    bt = _heads(kka * pinv)
    kt = _heads(k * pinv)
    bl = _heads(kka * dl)
    kl = _heads(k * dl)
    v = _heads(v_ref[...])
    pc = _heads(jnp.exp(last))
    sc = _mm(jnp.concatenate([at, rt], axis=1), jnp.concatenate([bt, kt], axis=1), RW_SC, "nt")
    a_ab = jnp.where(strict, sc[:, 0:c, 0:c], 0.0)
    a_ak = jnp.where(strict, sc[:, 0:c, c:2 * c], 0.0)
    r_b = jnp.where(incl, sc[:, c:2 * c, 0:c], 0.0)
    r_k = jnp.where(incl, sc[:, c:2 * c, c:2 * c], 0.0)
    x = eye + a_ab
    pw = a_ab
    for _ in range(int(math.log2(c)) - 1):
        pw = _mm(pw, pw, RW_INV)
        x = x + _mm(x, pw, RW_INV)
    wu = _mm(x, jnp.concatenate([at, _mm(a_ak, v, RW_APPLY)], axis=2), RW_APPLY)
    lower = jnp.concatenate([jnp.zeros_like(v), v], axis=2)
    qy = _mm(jnp.concatenate([r_b, r_k], axis=2), jnp.concatenate([wu, lower], axis=1), RW_APPLY)
    wb = _mm(wu, bl, RW_STATE, "tn")
    n = wb[:, HEAD_DIM:] + _mm(v, kl, RW_STATE, "tn")
    m = eye * pc + wb[:, 0:HEAD_DIM]
    qm = rt + qy[:, :, 0:HEAD_DIM]
    y0 = qy[:, :, HEAD_DIM:]
    for h in range(N_HEADS):
        sl = slice(h * HEAD_DIM, (h + 1) * HEAD_DIM)
        q_ref[:, sl] = qm[h]
        y0_ref[:, sl] = y0[h]
        m_ref[:, sl] = m[h]
        n_ref[:, sl] = n[h]


def _rwkv_scan_kernel(q_ref, y0_ref, m_ref, n_ref, r_ref, k_ref, v_ref, g_ref, lng_ref, lnb_ref, rk_ref,
                      o_ref, s_ref, y_sc):
    @pl.when(pl.program_id(0) == 0)
    def _():
        s_ref[...] = jnp.zeros_like(s_ref)

    c = RWKV_CHUNK
    s = s_ref[...]
    for j in range(q_ref.shape[0] // c):
        rows = slice(j * c, (j + 1) * c)
        y = _mm(_heads(q_ref[rows, :]), s, RW_SCAN, "nt") + _heads(y0_ref[rows, :])
        s = _mm(s, _heads(m_ref[rows, :]), RW_SCAN) + _heads(n_ref[rows, :])
        for h in range(N_HEADS):
            y_sc[rows, h * HEAD_DIM:(h + 1) * HEAD_DIM] = y[h]
    s_ref[...] = s
    y = y_sc[...]
    avg = _block_diag_const(MIX_W, HEAD_DIM, 1.0 / HEAD_DIM)
    mean = _mm_exact_rhs(y, avg, 2)
    yc = y - mean
    var = _mm_exact_rhs(yc * yc, avg, 2)
    yn = yc * lax.rsqrt(var + RWKV_LN_EPS) * lng_ref[...] + lnb_ref[...]
    v = v_ref[...]
    bonus = _mm_exact_rhs(r_ref[...] * k_ref[...] * rk_ref[...], _block_diag_const(MIX_W, HEAD_DIM, 1.0), 2) * v
    o_ref[...] = ((yn + bonus) * g_ref[...]).astype(o_ref.dtype)


def _rwkv_recurrence(r, lw, k, v, kkn, kka, g, ln_g, ln_b, r_k, tb=256):
    t = r.shape[0]
    c = RWKV_CHUNK
    cblk = pl.BlockSpec((c, MIX_W), lambda i: (i, 0))
    f = jax.ShapeDtypeStruct((t, MIX_W), F32)
    qm, y0, m, n = pl.pallas_call(
        _rwkv_local_kernel,
        grid=(t // c,),
        in_specs=[cblk] * 6,
        out_specs=[cblk] * 4,
        out_shape=[f] * 4,
        compiler_params=_cparams(("parallel",)),
        name="rwkv_local",
    )(r, lw, k, v, kkn, kka)
    blk = pl.BlockSpec((tb, MIX_W), lambda i: (i, 0))
    vec = pl.BlockSpec((1, MIX_W), lambda i: (0, 0))
    return pl.pallas_call(
        _rwkv_scan_kernel,
        grid=(t // tb,),
        in_specs=[blk] * 8 + [vec] * 3,
        out_specs=blk,
        out_shape=jax.ShapeDtypeStruct((t, MIX_W), BF16),
        scratch_shapes=[pltpu.VMEM((N_HEADS, HEAD_DIM, HEAD_DIM), F32), pltpu.VMEM((tb, MIX_W), F32)],
        compiler_params=_cparams(("arbitrary",)),
        name="rwkv_scan",
    )(qm, y0, m, n, r, k, v, g, ln_g.reshape(1, -1), ln_b.reshape(1, -1), r_k.reshape(1, -1))


def _rwkv_branch(p, mu, w0, w_up, a0, a_up, g_up, k_k, k_a, r_k, ln_g, ln_b):
    prep = _rwkv_prep(p, mu, w0, w_up, a0, a_up, g_up, k_k, k_a)
    return _rwkv_recurrence(*prep, ln_g, ln_b, r_k.reshape(-1))


def _rope_kernel(pos_ref, inv_ref, cos_ref, sin_ref):
    ang = pos_ref[...].astype(F32) * inv_ref[...]
    lane = lax.broadcasted_iota(jnp.int32, ang.shape, 1)
    cos_ref[...] = jnp.cos(ang)
    sin_ref[...] = jnp.where(lane % HEAD_DIM < HEAD_DIM // 2, -jnp.sin(ang), jnp.sin(ang))


def _rope_table(pos_b, tb=512):
    t = pos_b.shape[0]
    d = HEAD_DIM
    inv = 1.0 / (ROPE_BASE ** (jnp.arange(0, d, 2, dtype=F32) / d))
    inv_b = jnp.tile(inv, 4).reshape(1, 128)
    blk = pl.BlockSpec((tb, 128), lambda i: (i, 0))
    return pl.pallas_call(
        _rope_kernel,
        grid=(t // tb,),
        in_specs=[blk, pl.BlockSpec((1, 128), lambda i: (0, 0))],
        out_specs=[blk, blk],
        out_shape=[jax.ShapeDtypeStruct((t, 128), F32)] * 2,
        compiler_params=_cparams(("parallel",)),
        name="rope_table",
    )(pos_b, inv_b)


def _ret_kernel(q_ref, k_ref, v_ref, g_ref, cos_ref, sin_ref, dec_ref, zeta_ref, xi_ref, cd_ref, ng_ref,
                o_ref, s_ref, o_sc):
    @pl.when(pl.program_id(0) == 0)
    def _():
        s_ref[...] = jnp.zeros_like(s_ref)

    cos = jnp.concatenate([cos_ref[...]] * 4, axis=1)
    sin = jnp.concatenate([sin_ref[...]] * 4, axis=1)
    lane = lax.broadcasted_iota(jnp.int32, cos.shape, 1)
    lower_half = lane % HEAD_DIM < HEAD_DIM // 2

    def rope(x):
        swapped = jnp.where(lower_half, pltpu.roll(x, MIX_W - HEAD_DIM // 2, 1), pltpu.roll(x, HEAD_DIM // 2, 1))
        return x * cos + swapped * sin

    q = _heads(rope(q_ref[...]))
    k = _heads(rope(k_ref[...]) * (HEAD_DIM ** -0.5))
    v = _heads(v_ref[...])
    s = s_ref[...]
    scores = _mm(q, k, "x1", "nt") * dec_ref[...]
    o = _mm(scores, v, "x1") + _mm(q * xi_ref[...], s, "x1")
    s_ref[...] = s * cd_ref[...] + _mm(k * zeta_ref[...], v, "x1", "tn")
    on = o * lax.rsqrt(jnp.mean(o * o, axis=-1, keepdims=True) + NORM_EPS)
    for h in range(N_HEADS):
        o_sc[:, h * HEAD_DIM:(h + 1) * HEAD_DIM] = on[h]
    gt = g_ref[...]
    o_ref[...] = (o_sc[...] * ng_ref[...] * (gt * _sigmoid(gt))).astype(o_ref.dtype)


def _retention(p, cos_t, sin_t, norm_g):
    t = p.shape[0]
    c = RET_CHUNK
    h = N_HEADS
    log_g = jnp.log(1.0 - jnp.exp(jnp.linspace(math.log(1.0 / 32), math.log(1.0 / 512), h, dtype=F32)))
    idx = jnp.arange(c, dtype=F32)
    diff = idx[:, None] - idx[None, :]
    decay = jnp.where(diff >= 0, jnp.exp(log_g[:, None, None] * jnp.maximum(diff, 0.0)[None]), 0.0)
    zeta = jnp.exp(log_g[:, None] * (c - 1 - idx)[None])
    xi = jnp.exp(log_g[:, None] * (idx + 1)[None])
    cdec = jnp.exp(log_g * c)
    zeta_b = jnp.broadcast_to(zeta[:, :, None], (h, c, HEAD_DIM))
    xi_b = jnp.broadcast_to(xi[:, :, None], (h, c, HEAD_DIM))
    cd_b = jnp.broadcast_to(cdec[:, None, None], (h, HEAD_DIM, HEAD_DIM))

    def blk(off):
        return pl.BlockSpec((c, MIX_W), lambda i: (i, off // MIX_W))

    tab = pl.BlockSpec((c, 128), lambda i: (i, 0))

    def const(shape):
        return pl.BlockSpec(shape, lambda i: (0,) * len(shape))

    return pl.pallas_call(
        _ret_kernel,
        grid=(t // c,),
        in_specs=[blk(OFF_BQ), blk(OFF_BK), blk(OFF_BV), blk(OFF_BG), tab, tab,
                  const((h, c, c)), const((h, c, HEAD_DIM)), const((h, c, HEAD_DIM)),
                  const((h, HEAD_DIM, HEAD_DIM)), const((1, MIX_W))],
        out_specs=pl.BlockSpec((c, MIX_W), lambda i: (i, 0)),
        out_shape=jax.ShapeDtypeStruct((t, MIX_W), BF16),
        scratch_shapes=[pltpu.VMEM((h, HEAD_DIM, HEAD_DIM), F32), pltpu.VMEM((c, MIX_W), F32)],
        compiler_params=_cparams(("arbitrary",)),
        name="retention",
    )(p, p, p, p, cos_t, sin_t, decay, zeta_b, xi_b, cd_b, norm_g.reshape(1, -1))


def _lru_kernel(x_ref, xt_ref, gb_ref, pos_ref, cw_ref, cb_ref, wa_ref, ba_ref, wx_ref, bx_ref, lam_ref,
                o_ref, sc_ref, a_sc, b_sc, h_ref):
    tb = x_ref.shape[0]
    first = pl.program_id(0) == 0

    @pl.when(first)
    def _():
        h_ref[...] = jnp.zeros_like(h_ref)

    x = x_ref[...]
    sc_ref[0:8, :] = jnp.where(first, 0.0, xt_ref[...])
    sc_ref[8:8 + tb, :] = x
    cw = cw_ref[...]
    xc = cb_ref[...] + x * cw[CONV_W - 1:CONV_W, :]
    for j in range(1, CONV_W):
        xc = xc + sc_ref[8 - j:8 - j + tb, :] * cw[CONV_W - 1 - j:CONV_W - j, :]
    r = _sigmoid(_mm(xc, wa_ref[...], "x3") + ba_ref[...])
    ig = _sigmoid(_mm(xc, wx_ref[...], "x3") + bx_ref[...])
    nl = -lam_ref[...]
    softplus = jnp.maximum(nl, 0.0) + jnp.log1p(jnp.exp(-jnp.abs(nl)))
    log_a = -LRU_C * r * softplus
    pos = pos_ref[...]
    reset = jnp.concatenate([pos, pos, pos, pos], axis=1) == 0
    a_sc[...] = jnp.where(reset, 0.0, jnp.exp(log_a))
    th = jnp.tanh(log_a)
    b_sc[...] = jnp.where(reset, 1.0, jnp.sqrt(-2.0 * th / (1.0 - th))) * ig * xc

    row = lax.broadcasted_iota(jnp.int32, (8, MIX_W), 0)

    def group(gi, h):
        off = pl.multiple_of(gi * 8, 8)
        a = a_sc[pl.ds(off, 8), :]
        b = b_sc[pl.ds(off, 8), :]
        for d in (1, 2, 4):
            keep = row >= d
            b = jnp.where(keep, a * pltpu.roll(b, d, 0) + b, b)
            a = jnp.where(keep, a * pltpu.roll(a, d, 0), a)
        hs = a * h + b
        b_sc[pl.ds(off, 8), :] = hs
        return hs[7:8, :]

    h_ref[...] = lax.fori_loop(0, tb // 8, group, h_ref[...])
    gb = gb_ref[...]
    gelu = gb * (0.5 * (1.0 + jnp.tanh(math.sqrt(2.0 / math.pi) * (gb + 0.044715 * (gb * gb * gb)))))
    o_ref[...] = (b_sc[...] * gelu).astype(o_ref.dtype)


def _block_diag_weight(w):
    g, n, _ = w.shape
    eye = jnp.eye(g, dtype=w.dtype)
    return (eye[:, None, :, None] * w[:, :, None, :]).reshape(g * n, g * n)


def _rglru(p, pos_b, conv_w, conv_b, w_a, b_a, w_x, b_x, lam, tb=256):
    t = p.shape[0]
    nb8 = tb // 8
    vec = pl.BlockSpec((1, MIX_W), lambda i: (0, 0))
    mat = pl.BlockSpec((MIX_W, MIX_W), lambda i: (0, 0))
    cw8 = jnp.concatenate([conv_w, jnp.zeros((8 - CONV_W, MIX_W), F32)], axis=0)
    return pl.pallas_call(
        _lru_kernel,
        grid=(t // tb,),
        in_specs=[pl.BlockSpec((tb, MIX_W), lambda i: (i, OFF_CX // MIX_W)),
                  pl.BlockSpec((8, MIX_W), lambda i: (jnp.maximum(i * nb8 - 1, 0), OFF_CX // MIX_W)),
                  pl.BlockSpec((tb, MIX_W), lambda i: (i, OFF_CG // MIX_W)),
                  pl.BlockSpec((tb, 128), lambda i: (i, 0)),
                  pl.BlockSpec((8, MIX_W), lambda i: (0, 0)), vec, mat, vec, mat, vec, vec],
        out_specs=pl.BlockSpec((tb, MIX_W), lambda i: (i, 0)),
        out_shape=jax.ShapeDtypeStruct((t, MIX_W), BF16),
        scratch_shapes=[pltpu.VMEM((tb + 8, MIX_W), F32), pltpu.VMEM((tb, MIX_W), F32),
                        pltpu.VMEM((tb, MIX_W), F32), pltpu.VMEM((1, MIX_W), F32)],
        compiler_params=_cparams(("arbitrary",)),
        name="rglru",
    )(p, p, p, pos_b, cw8, conv_b.reshape(1, -1), _block_diag_weight(w_a), b_a.reshape(1, -1),
      _block_diag_weight(w_x), b_x.reshape(1, -1), lam.reshape(1, -1))


def _gla_kernel(v_ref, gt_ref, q_ref, k_ref, al_ref, aup_ref, ab_ref, ng_ref, o_ref, s_ref):
    @pl.when(pl.program_id(0) == 0)
    def _():
        s_ref[...] = jnp.zeros_like(s_ref)

    c = q_ref.shape[0]
    row = lax.broadcasted_iota(jnp.int32, (c, c), 0)
    col = lax.broadcasted_iota(jnp.int32, (c, c), 1)
    causal = row >= col
    pre = _mm(al_ref[...], aup_ref[...], "x3") + ab_ref[...]
    log_a = (jnp.minimum(pre, 0.0) - jnp.log1p(jnp.exp(-jnp.abs(pre)))) / GLA_GATE_NORM
    bcum = _mm_exact_lhs(causal.astype(F32), log_a, 3)
    blast = bcum[c - 1:c, :]
    k = k_ref[...]
    q_e = _heads(q_ref[...] * (GLA_DK ** -0.5) * jnp.exp(bcum))
    k_e = _heads(k * jnp.exp(-bcum))
    k_l = _heads(k * jnp.exp(blast - bcum))
    dec = _heads(jnp.exp(blast))
    v = _heads(v_ref[...], GLA_DV)
    s = s_ref[...]
    scores = jnp.where(causal, _mm(q_e, k_e, "x1", "nt"), 0.0)
    o = _mm(scores, v, "x1") + _mm(q_e, s, "x1", "nt")
    s_ref[...] = s * dec + _mm(v, k_l, "x1", "tn")
    on = o * lax.rsqrt(jnp.mean(o * o, axis=-1, keepdims=True) + NORM_EPS)
    on = jnp.concatenate([on[h] for h in range(GLA_HEADS)], axis=1) * ng_ref[...]
    gt = gt_ref[...]
    o_ref[...] = (on * (gt * _sigmoid(gt))).astype(o_ref.dtype)


def _gla(p, alpha_up, alpha_b, norm_g):
    t = p.shape[0]
    c = GLA_CHUNK
    hk = MIX_W // 2
    aup = jnp.concatenate([alpha_up, jnp.zeros((128 - GLA_LORA, hk), F32)], axis=0)

    def blk(width, off):
        return pl.BlockSpec((c, width), lambda i: (i, off // width))

    return pl.pallas_call(
        _gla_kernel,
        grid=(t // c,),
        in_specs=[blk(MIX_W, OFF_DV), blk(MIX_W, OFF_DGT), blk(hk, OFF_DQ), blk(hk, OFF_DK), blk(128, OFF_DAL),
                  pl.BlockSpec((128, hk), lambda i: (0, 0)), pl.BlockSpec((1, hk), lambda i: (0, 0)),
                  pl.BlockSpec((1, MIX_W), lambda i: (0, 0))],
        out_specs=pl.BlockSpec((c, MIX_W), lambda i: (i, 0)),
        out_shape=jax.ShapeDtypeStruct((t, MIX_W), BF16),
        scratch_shapes=[pltpu.VMEM((GLA_HEADS, GLA_DV, GLA_DK), F32)],
        compiler_params=_cparams(("arbitrary",)),
        name="gla",
    )(p, p, p, p, p, aup, alpha_b.reshape(1, -1), norm_g.reshape(1, -1))


def _merge_kernel(ya_ref, yb_ref, yc_ref, yd_ref, wa_ref, wb_ref, wc_ref, wd_ref,
                  ga_ref, gb_ref, gc_ref, gd_ref, o_ref):
    acc = _sigmoid(ga_ref[...]) * _dot(ya_ref[...], wa_ref[...])
    acc += _sigmoid(gb_ref[...]) * _dot(yb_ref[...], wb_ref[...])
    acc += _sigmoid(gc_ref[...]) * _dot(yc_ref[...], wc_ref[...])
    acc += _sigmoid(gd_ref[...]) * _dot(yd_ref[...], wd_ref[...])
    o_ref[...] = acc.astype(o_ref.dtype)


def _merge(ys, ws, p, tm=512, tn=512):
    t = p.shape[0]
    yspec = pl.BlockSpec((tm, MIX_W), lambda i, j: (i, 0))
    wspec = pl.BlockSpec((MIX_W, tn), lambda i, j: (0, j))

    def gspec(kk):
        return pl.BlockSpec((tm, tn), lambda i, j: (i, (OFF_G + kk * D_MODEL) // tn + j))

    return pl.pallas_call(
        _merge_kernel,
        grid=(t // tm, D_MODEL // tn),
        in_specs=[yspec] * 4 + [wspec] * 4 + [gspec(kk) for kk in range(4)],
        out_specs=pl.BlockSpec((tm, tn), lambda i, j: (i, j)),
        out_shape=jax.ShapeDtypeStruct((t, D_MODEL), BF16),
        compiler_params=_cparams(("parallel", "parallel")),
        name="merge",
    )(*ys, *ws, p, p, p, p)


def _ffn_norm_kernel(h_ref, g_ref, wr_ref, hn_ref, lg_ref):
    x = h_ref[...]
    ms = jnp.mean(x * x, axis=-1, keepdims=True)
    hn = x * lax.rsqrt(ms + NORM_EPS) * g_ref[...]
    hn_ref[...] = hn
    lg_ref[...] = _mm(hn, wr_ref[...], "x3")


def _ffn_norm_router(h, g, w_r, tb=256):
    t, d = h.shape
    return pl.pallas_call(
        _ffn_norm_kernel,
        grid=(t // tb,),
        in_specs=[pl.BlockSpec((tb, d), lambda i: (i, 0)), pl.BlockSpec((1, d), lambda i: (0, 0)),
                  pl.BlockSpec((d, 128), lambda i: (0, 0))],
        out_specs=[pl.BlockSpec((tb, d), lambda i: (i, 0)), pl.BlockSpec((tb, 128), lambda i: (i, 0))],
        out_shape=[jax.ShapeDtypeStruct((t, d), F32), jax.ShapeDtypeStruct((t, 128), F32)],
        compiler_params=_cparams(("parallel",)),
        name="ffn_norm_router",
    )(h, g.reshape(1, d), w_r)


def _row_copy(src_hbm, idx, dst, r, sem):
    return pltpu.make_async_copy(src_hbm.at[pl.ds(idx, 1), :], dst.at[pl.ds(r, 1), :], sem)


def _expert_changed(i, exp_ref):
    return jnp.logical_or(i == 0, exp_ref[i] != exp_ref[jnp.maximum(i - 1, 0)])


def _expert_up_kernel(tok_ref, exp_ref, nblk_ref, x_hbm, wg_ref, wu_ref, o_ref, xbuf, wg_bf, wu_bf, sem):
    i = pl.program_id(0)
    active = i < nblk_ref[0]

    @pl.when(active)
    def _():
        base = i * MOE_BLOCK

        def start(r, carry):
            _row_copy(x_hbm, tok_ref[base + r], xbuf, r, sem).start()
            return carry

        lax.fori_loop(0, MOE_BLOCK, start, 0)

        @pl.when(_expert_changed(i, exp_ref))
        def _():
            wg_bf[...] = wg_ref[0, 0].astype(BF16)
            wu_bf[...] = wu_ref[0, 0].astype(BF16)

        def wait(r, carry):
            _row_copy(x_hbm, tok_ref[base + r], xbuf, r, sem).wait()
            return carry

        lax.fori_loop(0, MOE_BLOCK, wait, 0)
        x = xbuf[...].astype(BF16)
        gate = _dot(x, wg_bf[...])
        up = _dot(x, wu_bf[...])
        o_ref[...] = (gate * _sigmoid(gate) * up).astype(o_ref.dtype)

    @pl.when(jnp.logical_not(active))
    def _():
        o_ref[...] = jnp.zeros_like(o_ref)


def _expert_down_kernel(exp_ref, nblk_ref, hm_ref, gate_ref, wd_ref, o_ref, wd_bf):
    i = pl.program_id(0)
    active = i < nblk_ref[0]

    @pl.when(active)
    def _():
        @pl.when(_expert_changed(i, exp_ref))
        def _():
            wd_bf[...] = wd_ref[0, 0].astype(BF16)

        o_ref[...] = _dot(hm_ref[...], wd_bf[...]) * gate_ref[...]

    @pl.when(jnp.logical_not(active))
    def _():
        o_ref[...] = jnp.zeros_like(o_ref)


def _experts(hn, row_tok, block_exp, n_used, row_gate, w_gate, w_up, w_down, layer):
    n_rows = row_tok.shape[0]
    n_blocks = n_rows // MOE_BLOCK
    d = hn.shape[1]
    hmid = pl.pallas_call(
        _expert_up_kernel,
        grid_spec=pltpu.PrefetchScalarGridSpec(
            num_scalar_prefetch=3,
            grid=(n_blocks,),
            in_specs=[pl.BlockSpec(memory_space=pl.ANY),
                      pl.BlockSpec((1, 1, d, D_EXPERT), lambda i, tok, ex, nb: (layer, ex[i], 0, 0)),
                      pl.BlockSpec((1, 1, d, D_EXPERT), lambda i, tok, ex, nb: (layer, ex[i], 0, 0))],
            out_specs=pl.BlockSpec((MOE_BLOCK, D_EXPERT), lambda i, tok, ex, nb: (i, 0)),
            scratch_shapes=[pltpu.VMEM((MOE_BLOCK, d), F32), pltpu.VMEM((d, D_EXPERT), BF16),
                            pltpu.VMEM((d, D_EXPERT), BF16), pltpu.SemaphoreType.DMA(())],
        ),
        out_shape=jax.ShapeDtypeStruct((n_rows, D_EXPERT), BF16),
        compiler_params=_cparams(("arbitrary",), VMEM_LIMIT_EXPERT),
        name="expert_up",
    )(row_tok, block_exp, n_used, hn, w_gate, w_up)
    return pl.pallas_call(
        _expert_down_kernel,
        grid_spec=pltpu.PrefetchScalarGridSpec(
            num_scalar_prefetch=2,
            grid=(n_blocks,),
            in_specs=[pl.BlockSpec((MOE_BLOCK, D_EXPERT), lambda i, ex, nb: (i, 0)),
                      pl.BlockSpec((MOE_BLOCK, 1), lambda i, ex, nb: (i, 0)),
                      pl.BlockSpec((1, 1, D_EXPERT, d), lambda i, ex, nb: (layer, ex[i], 0, 0))],
            out_specs=pl.BlockSpec((MOE_BLOCK, d), lambda i, ex, nb: (i, 0)),
            scratch_shapes=[pltpu.VMEM((D_EXPERT, d), BF16)],
        ),
        out_shape=jax.ShapeDtypeStruct((n_rows, d), F32),
        compiler_params=_cparams(("arbitrary",)),
        name="expert_down",
    )(block_exp, n_used, hmid, row_gate, w_down)


def _combine_kernel(pos_ref, y_hbm, h_ref, g_ref, o_ref, buf, sem, *, final_norm):
    tb = h_ref.shape[0]
    base = pl.program_id(0) * tb * TOP_K

    def start(r, carry):
        for s in range(TOP_K):
            _row_copy(y_hbm, pos_ref[base + r * TOP_K + s], buf.at[s], r, sem).start()
        return carry

    lax.fori_loop(0, tb, start, 0)

    def wait(r, carry):
        for s in range(TOP_K):
            _row_copy(y_hbm, pos_ref[base + r * TOP_K + s], buf.at[s], r, sem).wait()
        return carry

    lax.fori_loop(0, tb, wait, 0)
    out = h_ref[...] + (buf[0] + buf[1])
    if final_norm:
        ms = jnp.mean(out * out, axis=-1, keepdims=True)
        out = out * lax.rsqrt(ms + NORM_EPS) * g_ref[...]
    o_ref[...] = out


def _combine(h, y_rows, pos, g, final_norm, tb=128):
    t, d = h.shape
    grid_spec = pltpu.PrefetchScalarGridSpec(
        num_scalar_prefetch=1,
        grid=(t // tb,),
        in_specs=[pl.BlockSpec(memory_space=pl.ANY),
                  pl.BlockSpec((tb, d), lambda i, pos: (i, 0)),
                  pl.BlockSpec((1, d), lambda i, pos: (0, 0))],
        out_specs=pl.BlockSpec((tb, d), lambda i, pos: (i, 0)),
        scratch_shapes=[pltpu.VMEM((TOP_K, tb, d), F32), pltpu.SemaphoreType.DMA(())],
    )
    return pl.pallas_call(
        functools.partial(_combine_kernel, final_norm=final_norm),
        grid_spec=grid_spec,
        out_shape=jax.ShapeDtypeStruct((t, d), F32),
        compiler_params=_cparams(("arbitrary",)),
        name="moe_combine",
    )(pos, y_rows, h, g.reshape(1, d))


def _routing(logits, bg, be):
    t = logits.shape[0]
    pg = jax.nn.softmax(logits[:, 0:N_GROUPS] + bg, -1)
    pg_top, g_idx = lax.top_k(pg, 1)
    le = (logits[:, N_GROUPS:N_GROUPS + N_EXPERTS] + be).reshape(t, N_GROUPS, EXP_PER_GROUP)
    le_sel = jnp.take_along_axis(le, jnp.broadcast_to(g_idx[:, :, None], (t, 1, EXP_PER_GROUP)), axis=1)[:, 0]
    pe_top, e_idx = lax.top_k(jax.nn.softmax(le_sel, -1), TOP_K)
    pe_top = pe_top / jnp.sum(pe_top, -1, keepdims=True)
    gate = (pg_top * pe_top).reshape(-1)
    expert_id = (g_idx * EXP_PER_GROUP + e_idx).reshape(-1).astype(jnp.int32)
    n_assign = t * TOP_K
    order = jnp.argsort(expert_id).astype(jnp.int32)
    st = order // TOP_K
    sg = gate[order]
    se = expert_id[order]
    counts = jnp.sum((expert_id[:, None] == jnp.arange(N_EXPERTS, dtype=jnp.int32)[None, :]).astype(jnp.int32), axis=0)
    starts = jnp.cumsum(counts) - counts
    padded = (counts + MOE_BLOCK - 1) // MOE_BLOCK * MOE_BLOCK
    pad_ends = jnp.cumsum(padded)
    pad_starts = pad_ends - padded
    dest = pad_starts[se] + (jnp.arange(n_assign, dtype=jnp.int32) - starts[se])
    n_blocks = (n_assign + MOE_BLOCK - 1) // MOE_BLOCK + N_EXPERTS
    n_rows = n_blocks * MOE_BLOCK
    block_start = jnp.arange(n_blocks, dtype=jnp.int32) * MOE_BLOCK
    block_exp = jnp.minimum(jnp.sum(block_start[:, None] >= pad_ends[None, :], axis=1), N_EXPERTS - 1).astype(jnp.int32)
    rows = jnp.arange(n_rows, dtype=jnp.int32)
    row_exp = jnp.repeat(block_exp, MOE_BLOCK)
    local = rows - pad_starts[row_exp]
    valid = local < counts[row_exp]
    src = jnp.clip(starts[row_exp] + local, 0, n_assign - 1)
    row_tok = jnp.where(valid, st[src], 0)
    row_gate = jnp.where(valid, sg[src], 0.0)
    n_used = (pad_ends[-1] // MOE_BLOCK).astype(jnp.int32).reshape(1)
    pos = dest[jnp.argsort(order)]
    return row_tok, row_gate.reshape(n_rows, 1), block_exp, n_used, pos


def _moe_layer(h, norm_g, wg_r, bg_r, we_r, be_r, w_gate, w_up, w_down, layer, out_g, final_norm):
    d = h.shape[1]
    w_r = jnp.concatenate([wg_r, we_r, jnp.zeros((d, 128 - N_GROUPS - N_EXPERTS), F32)], axis=1)
    hn, logits = _ffn_norm_router(h, norm_g, w_r)
    row_tok, row_gate, block_exp, n_used, pos = _routing(logits, bg_r, be_r)
    y_rows = _experts(hn, row_tok, block_exp, n_used, row_gate, w_gate, w_up, w_down, layer)
    return _combine(h, y_rows, pos, out_g, final_norm)


def _permute_w_in(w):
    d0 = A_COLS + B_COLS + C_COLS
    g0 = d0 + D_COLS
    b0 = A_COLS
    c0 = A_COLS + B_COLS

    def z(n):
        return jnp.zeros((w.shape[0], n), w.dtype)

    parts = [w[:, b0:b0 + B_COLS], w[:, c0:c0 + C_COLS],
             w[:, d0 + 512:d0 + 1024], w[:, d0 + 1040:d0 + 1552],
             w[:, 0:3 * MIX_W], w[:, d0:d0 + 512],
             w[:, 3 * MIX_W:A_COLS], z(256 - RWKV_LORA),
             w[:, d0 + 1024:d0 + 1040], z(128 - GLA_LORA), z(OFF_G - OFF_DAL - 128),
             w[:, g0:]]
    out = jnp.concatenate(parts, axis=1).astype(BF16)
    assert out.shape[1] == NP_COLS
    return out


def kernel(x, positions, norm_mix_g, w_in, rwkv_mu, rwkv_w0, rwkv_w_up, rwkv_a0, rwkv_a_up, rwkv_g_up, rwkv_k_k, rwkv_k_a, rwkv_r_k, rwkv_ln_g, rwkv_ln_b, rwkv_w_o, ret_norm_g, ret_w_o, lru_conv_w, lru_conv_b, lru_w_a, lru_b_a, lru_w_x, lru_b_x, lru_lambda, lru_w_o, gla_alpha_up, gla_alpha_b, gla_norm_g, gla_w_o, w_out, norm_ffn_g, router_group_w, router_group_b, router_expert_w, router_expert_b, moe_w_gate, moe_w_up, moe_w_down, final_norm_g):
    b_, s_, d = x.shape
    assert b_ == 1 and d == D_MODEL
    depth = w_in.shape[0]
    h = x.reshape(s_, d)
    pos_b = jnp.broadcast_to(positions.reshape(s_, 1), (s_, 128)).astype(jnp.int32)
    cos_t, sin_t = _rope_table(pos_b)
    for l in range(depth):
        xn = _rmsnorm(h, norm_mix_g[l], BF16)
        p = _matmul(xn, _permute_w_in(w_in[l]), 1024, 512, name="in_proj")
        y_a = _rwkv_branch(p, rwkv_mu[l], rwkv_w0[l], rwkv_w_up[l], rwkv_a0[l], rwkv_a_up[l], rwkv_g_up[l],
                           rwkv_k_k[l], rwkv_k_a[l], rwkv_r_k[l], rwkv_ln_g[l], rwkv_ln_b[l])
        y_b = _retention(p, cos_t, sin_t, ret_norm_g[l])
        y_c = _rglru(p, pos_b, lru_conv_w[l], lru_conv_b[l], lru_w_a[l], lru_b_a[l], lru_w_x[l], lru_b_x[l],
                     lru_lambda[l])
        y_d = _gla(p, gla_alpha_up[l], gla_alpha_b[l], gla_norm_g[l])
        merged = _merge((y_a, y_b, y_c, y_d),
                        tuple(w[l].astype(BF16) for w in (rwkv_w_o, ret_w_o, lru_w_o, gla_w_o)), p)
        h = _matmul_residual(merged, w_out[l].astype(BF16), h, 1024, 512)
        last = l == depth - 1
        h = _moe_layer(h, norm_ffn_g[l], router_group_w[l], router_group_b[l], router_expert_w[l],
                       router_expert_b[l], moe_w_gate, moe_w_up, moe_w_down, l,
                       final_norm_g if last else norm_ffn_g[l], last)
    return h.reshape(b_, s_, d)
```

```python
import functools
import math

import jax
import jax.numpy as jnp
from jax import lax
from jax.experimental import pallas as pl
from jax.experimental.pallas import tpu as pltpu

F32 = jnp.float32
BF16 = jnp.bfloat16
HI = lax.Precision.HIGHEST

D_MODEL = 2048
MIX_W = 512
NORM_EPS = 1e-6
HEAD_DIM = 64
N_HEADS = MIX_W // HEAD_DIM

RWKV_W_LORA, RWKV_A_LORA, RWKV_G_LORA = 32, 32, 96
RWKV_LORA = RWKV_W_LORA + RWKV_A_LORA + RWKV_G_LORA
RWKV_DECAY_SCALE = 0.6065306597126334
RWKV_LN_EPS = 64e-5
RWKV_CHUNK = 64

RET_CHUNK = 128
ROPE_BASE = 10000.0

LRU_BLOCKS = 8
CONV_W = 4
LRU_C = 8.0

GLA_HEADS = 4
GLA_DK = 64
GLA_DV = 128
GLA_LORA = 16
GLA_GATE_NORM = 16.0
GLA_CHUNK = 64

N_GROUPS = 4
EXP_PER_GROUP = 8
N_EXPERTS = N_GROUPS * EXP_PER_GROUP
TOP_K = 2
D_EXPERT = 1024
MOE_BLOCK = 128

A_COLS = 3 * MIX_W + RWKV_LORA
B_COLS = 4 * MIX_W
C_COLS = 2 * MIX_W
D_COLS = 2 * (MIX_W // 2) + MIX_W + GLA_LORA + MIX_W

OFF_BQ, OFF_BK, OFF_BV, OFF_BG = 0, 512, 1024, 1536
OFF_CX, OFF_CG = 2048, 2560
OFF_DV, OFF_DGT = 3072, 3584
OFF_AR, OFF_AK, OFF_AV = 4096, 4608, 5120
OFF_DQ, OFF_DK = 5632, 5888
OFF_AL = 6144
OFF_DAL = 6400
OFF_G = 6656
NP_COLS = OFF_G + 4 * D_MODEL

VMEM_LIMIT = 48 * 1024 * 1024
VMEM_LIMIT_EXPERT = 56 * 1024 * 1024


def _cparams(sem, vmem=VMEM_LIMIT):
    return pltpu.CompilerParams(dimension_semantics=sem, vmem_limit_bytes=vmem)


def _dot(a, b, prec=None):
    return jnp.dot(a, b, precision=prec, preferred_element_type=F32)


def _sigmoid(x):
    return 1.0 / (1.0 + jnp.exp(-x))


def _split(x):
    hi = x.astype(BF16)
    return hi, (x - hi.astype(F32)).astype(BF16)


def _dims(form, batched):
    ca, cb = {"nn": (1, 0), "nt": (1, 1), "tn": (0, 0)}[form]
    if batched:
        return (((ca + 1,), (cb + 1,)), ((0,), (0,)))
    return (((ca,), (cb,)), ((), ()))


def _mm(a, b, mode, form="nn"):
    dims = _dims(form, a.ndim == 3)
    if mode == "hi":
        return lax.dot_general(a, b, dims, precision=HI, preferred_element_type=F32)
    d = functools.partial(lax.dot_general, dimension_numbers=dims, preferred_element_type=F32)
    if mode == "x1":
        return d(a.astype(BF16), b.astype(BF16))
    ah, al = _split(a)
    bh, bl = _split(b)
    return d(ah, bh) + (d(ah, bl) + d(al, bh))


def _mm_exact_rhs(a, b, terms):
    b = b.astype(BF16)
    acc = None
    for _ in range(terms):
        piece = a.astype(BF16)
        part = _dot(piece, b)
        acc = part if acc is None else acc + part
        a = a - piece.astype(F32)
    return acc


def _mm_exact_lhs(a, b, terms):
    a = a.astype(BF16)
    acc = None
    for _ in range(terms):
        piece = b.astype(BF16)
        part = _dot(a, piece)
        acc = part if acc is None else acc + part
        b = b - piece.astype(F32)
    return acc


def _heads(x, width=HEAD_DIM):
    return jnp.stack([x[:, h * width:(h + 1) * width] for h in range(x.shape[1] // width)])


def _block_diag_const(n, blk, value):
    r = lax.broadcasted_iota(jnp.int32, (n, n), 0) // blk
    c = lax.broadcasted_iota(jnp.int32, (n, n), 1) // blk
    return jnp.where(r == c, value, 0.0).astype(F32)


def _rmsnorm_kernel(x_ref, g_ref, o_ref):
    x = x_ref[...]
    ms = jnp.mean(x * x, axis=-1, keepdims=True)
    o_ref[...] = (x * lax.rsqrt(ms + NORM_EPS) * g_ref[...]).astype(o_ref.dtype)


def _rmsnorm(x, g, out_dtype, tb=512):
    t, d = x.shape
    return pl.pallas_call(
        _rmsnorm_kernel,
        grid=(t // tb,),
        in_specs=[pl.BlockSpec((tb, d), lambda i: (i, 0)), pl.BlockSpec((1, d), lambda i: (0, 0))],
        out_specs=pl.BlockSpec((tb, d), lambda i: (i, 0)),
        out_shape=jax.ShapeDtypeStruct((t, d), out_dtype),
        compiler_params=_cparams(("parallel",)),
        name="rmsnorm",
    )(x, g.reshape(1, d))


def _mm_kernel(a_ref, b_ref, o_ref):
    o_ref[...] = _dot(a_ref[...], b_ref[...]).astype(o_ref.dtype)


def _matmul(a, b, tm, tn, out_dtype=F32, name="matmul"):
    m, k = a.shape
    n = b.shape[1]
    return pl.pallas_call(
        _mm_kernel,
        grid=(m // tm, n // tn),
        in_specs=[pl.BlockSpec((tm, k), lambda i, j: (i, 0)), pl.BlockSpec((k, tn), lambda i, j: (0, j))],
        out_specs=pl.BlockSpec((tm, tn), lambda i, j: (i, j)),
        out_shape=jax.ShapeDtypeStruct((m, n), out_dtype),
        compiler_params=_cparams(("parallel", "parallel")),
        name=name,
    )(a, b)


def _mm_res_kernel(a_ref, b_ref, r_ref, o_ref):
    o_ref[...] = r_ref[...] + _dot(a_ref[...], b_ref[...])


def _matmul_residual(a, b, res, tm, tn):
    m, k = a.shape
    n = b.shape[1]
    return pl.pallas_call(
        _mm_res_kernel,
        grid=(m // tm, n // tn),
        in_specs=[pl.BlockSpec((tm, k), lambda i, j: (i, 0)), pl.BlockSpec((k, tn), lambda i, j: (0, j)),
                  pl.BlockSpec((tm, tn), lambda i, j: (i, j))],
        out_specs=pl.BlockSpec((tm, tn), lambda i, j: (i, j)),
        out_shape=jax.ShapeDtypeStruct((m, n), F32),
        compiler_params=_cparams(("parallel", "parallel")),
        name="wout_residual",
    )(a, b, res)


def _shifted(x, tail_ref, sc_ref, width, first):
    tb = x.shape[0]
    sc_ref[0:8, 0:width] = jnp.where(first, 0.0, tail_ref[...])
    sc_ref[8:8 + tb, 0:width] = x
    return sc_ref[7:7 + tb, 0:width]


def _rwkv_prep_kernel(r_ref, k_ref, v_ref, l_ref, rt_ref, kt_ref, vt_ref, lt_ref,
                      mur_ref, muk_ref, muv_ref, mul_ref, wl_ref, b0_ref, kk_ref, ka_ref,
                      ro_ref, lw_ref, ko_ref, vo_ref, kkn_ref, kka_ref, g_ref, sc_ref):
    first = pl.program_id(0) == 0

    def mix(x_ref, t_ref, mu_ref, width):
        x = x_ref[...]
        prev = _shifted(x, t_ref, sc_ref, width, first)
        return x + (prev - x) * mu_ref[...]

    r = mix(r_ref, rt_ref, mur_ref, MIX_W)
    k = mix(k_ref, kt_ref, muk_ref, MIX_W)
    v = mix(v_ref, vt_ref, muv_ref, MIX_W)
    zl = mix(l_ref, lt_ref, mul_ref, 256)
    lane = lax.broadcasted_iota(jnp.int32, zl.shape, 1)
    act = jnp.where(lane < RWKV_W_LORA, jnp.tanh(zl),
                    jnp.where(lane < RWKV_W_LORA + RWKV_A_LORA, zl, _sigmoid(zl)))
    lo = _mm(act, wl_ref[...], "x3") + b0_ref[...]
    lw = -RWKV_DECAY_SCALE * _sigmoid(lo[:, 0:MIX_W])
    a = _sigmoid(lo[:, MIX_W:2 * MIX_W])
    g = lo[:, 2 * MIX_W:3 * MIX_W]
    kk = k * kk_ref[...]
    ss = _mm_exact_rhs(kk * kk, _block_diag_const(MIX_W, HEAD_DIM, 1.0), 2)
    kkn = kk / jnp.maximum(jnp.sqrt(ss), 1e-12)
    ro_ref[...] = r
    lw_ref[...] = lw
    ko_ref[...] = k * (1.0 + (a - 1.0) * ka_ref[...])
    vo_ref[...] = v
    kkn_ref[...] = kkn
    kka_ref[...] = kkn * a
    g_ref[...] = g


def _rwkv_prep(p, mu, w0, w_up, a0, a_up, g_up, k_k, k_a, tb=256):
    t = p.shape[0]
    nb8 = tb // 8

    def blk(width, off):
        return pl.BlockSpec((tb, width), lambda i: (i, off // width))

    def tail(width, off):
        return pl.BlockSpec((8, width), lambda i: (jnp.maximum(i * nb8 - 1, 0), off // width))

    def vec(width):
        return pl.BlockSpec((1, width), lambda i: (0, 0))

    w_lora = jnp.zeros((256, 3 * MIX_W), F32)
    w_lora = w_lora.at[0:32, 0:MIX_W].set(w_up)
    w_lora = w_lora.at[32:64, MIX_W:2 * MIX_W].set(a_up)
    w_lora = w_lora.at[64:160, 2 * MIX_W:].set(g_up)
    b0 = jnp.concatenate([w0, a0, jnp.zeros((MIX_W,), F32)]).reshape(1, 3 * MIX_W)
    mu_l = jnp.concatenate([mu[3 * MIX_W:], jnp.zeros((256 - RWKV_LORA,), F32)]).reshape(1, 256)
    out = jax.ShapeDtypeStruct((t, MIX_W), F32)
    return pl.pallas_call(
        _rwkv_prep_kernel,
        grid=(t // tb,),
        in_specs=[blk(MIX_W, OFF_AR), blk(MIX_W, OFF_AK), blk(MIX_W, OFF_AV), blk(256, OFF_AL),
                  tail(MIX_W, OFF_AR), tail(MIX_W, OFF_AK), tail(MIX_W, OFF_AV), tail(256, OFF_AL),
                  vec(MIX_W), vec(MIX_W), vec(MIX_W), vec(256),
                  pl.BlockSpec((256, 3 * MIX_W), lambda i: (0, 0)), vec(3 * MIX_W), vec(MIX_W), vec(MIX_W)],
        out_specs=[pl.BlockSpec((tb, MIX_W), lambda i: (i, 0))] * 7,
        out_shape=[out] * 7,
        scratch_shapes=[pltpu.VMEM((tb + 8, MIX_W), F32)],
        compiler_params=_cparams(("parallel",)),
        name="rwkv_prep",
    )(p, p, p, p, p, p, p, p,
      mu[0:MIX_W].reshape(1, -1), mu[MIX_W:2 * MIX_W].reshape(1, -1), mu[2 * MIX_W:3 * MIX_W].reshape(1, -1), mu_l,
      w_lora, b0, k_k.reshape(1, -1), k_a.reshape(1, -1))


RW_SC, RW_INV, RW_APPLY, RW_STATE, RW_SCAN = "x3", "x1", "x1", "x1", "x3"


def _rwkv_local_kernel(r_ref, lw_ref, k_ref, v_ref, kk_ref, kka_ref, q_ref, y0_ref, m_ref, n_ref):
    c = r_ref.shape[0]
    row = lax.broadcasted_iota(jnp.int32, (c, c), 0)
    col = lax.broadcasted_iota(jnp.int32, (c, c), 1)
    incl = row >= col
    strict = row > col
    eye = jnp.where(row == col, 1.0, 0.0)
    lw = lw_ref[...]
    k = k_ref[...]
    kka = kka_ref[...]
    cum = _mm_exact_lhs(incl.astype(F32), lw, 3)
    last = cum[c - 1:c, :]
    pinv = jnp.exp(-cum)
    dl = jnp.exp(last - cum)
    at = _heads(-kk_ref[...] * jnp.exp(cum - lw))
    rt = _heads(r_ref[...] * jnp.exp(cum))
    bt = _heads(kka * pinv)
    kt = _heads(k * pinv)
    bl = _heads(kka * dl)
    kl = _heads(k * dl)
    v = _heads(v_ref[...])
    pc = _heads(jnp.exp(last))
    sc = _mm(jnp.concatenate([at, rt], axis=1), jnp.concatenate([bt, kt], axis=1), RW_SC, "nt")
    a_ab = jnp.where(strict, sc[:, 0:c, 0:c], 0.0)
    a_ak = jnp.where(strict, sc[:, 0:c, c:2 * c], 0.0)
    r_b = jnp.where(incl, sc[:, c:2 * c, 0:c], 0.0)
    r_k = jnp.where(incl, sc[:, c:2 * c, c:2 * c], 0.0)
    x = eye + a_ab
    pw = a_ab
    for _ in range(int(math.log2(c)) - 1):
        pw = _mm(pw, pw, RW_INV)
        x = x + _mm(x, pw, RW_INV)
    wu = _mm(x, jnp.concatenate([at, _mm(a_ak, v, RW_APPLY)], axis=2), RW_APPLY)
    lower = jnp.concatenate([jnp.zeros_like(v), v], axis=2)
    qy = _mm(jnp.concatenate([r_b, r_k], axis=2), jnp.concatenate([wu, lower], axis=1), RW_APPLY)
    wb = _mm(wu, bl, RW_STATE, "tn")
    n = wb[:, HEAD_DIM:] + _mm(v, kl, RW_STATE, "tn")
    m = eye * pc + wb[:, 0:HEAD_DIM]
    qm = rt + qy[:, :, 0:HEAD_DIM]
    y0 = qy[:, :, HEAD_DIM:]
    for h in range(N_HEADS):
        sl = slice(h * HEAD_DIM, (h + 1) * HEAD_DIM)
        q_ref[:, sl] = qm[h]
        y0_ref[:, sl] = y0[h]
        m_ref[:, sl] = m[h]
        n_ref[:, sl] = n[h]


def _rwkv_scan_kernel(q_ref, y0_ref, m_ref, n_ref, r_ref, k_ref, v_ref, g_ref, lng_ref, lnb_ref, rk_ref,
                      o_ref, s_ref, y_sc):
    @pl.when(pl.program_id(0) == 0)
    def _():
        s_ref[...] = jnp.zeros_like(s_ref)

    c = RWKV_CHUNK
    s = s_ref[...]
    for j in range(q_ref.shape[0] // c):
        rows = slice(j * c, (j + 1) * c)
        y = _mm(_heads(q_ref[rows, :]), s, RW_SCAN, "nt") + _heads(y0_ref[rows, :])
        s = _mm(s, _heads(m_ref[rows, :]), RW_SCAN) + _heads(n_ref[rows, :])
        for h in range(N_HEADS):
            y_sc[rows, h * HEAD_DIM:(h + 1) * HEAD_DIM] = y[h]
    s_ref[...] = s
    y = y_sc[...]
    avg = _block_diag_const(MIX_W, HEAD_DIM, 1.0 / HEAD_DIM)
    mean = _mm_exact_rhs(y, avg, 2)
    yc = y - mean
    var = _mm_exact_rhs(yc * yc, avg, 2)
    yn = yc * lax.rsqrt(var + RWKV_LN_EPS) * lng_ref[...] + lnb_ref[...]
    v = v_ref[...]
    bonus = _mm_exact_rhs(r_ref[...] * k_ref[...] * rk_ref[...], _block_diag_const(MIX_W, HEAD_DIM, 1.0), 2) * v
    o_ref[...] = ((yn + bonus) * g_ref[...]).astype(o_ref.dtype)


def _rwkv_recurrence(r, lw, k, v, kkn, kka, g, ln_g, ln_b, r_k, tb=256):
    t = r.shape[0]
    c = RWKV_CHUNK
    cblk = pl.BlockSpec((c, MIX_W), lambda i: (i, 0))
    f = jax.ShapeDtypeStruct((t, MIX_W), F32)
    qm, y0, m, n = pl.pallas_call(
        _rwkv_local_kernel,
        grid=(t // c,),
        in_specs=[cblk] * 6,
        out_specs=[cblk] * 4,
        out_shape=[f] * 4,
        compiler_params=_cparams(("parallel",)),
        name="rwkv_local",
    )(r, lw, k, v, kkn, kka)
    blk = pl.BlockSpec((tb, MIX_W), lambda i: (i, 0))
    vec = pl.BlockSpec((1, MIX_W), lambda i: (0, 0))
    return pl.pallas_call(
        _rwkv_scan_kernel,
        grid=(t // tb,),
        in_specs=[blk] * 8 + [vec] * 3,
        out_specs=blk,
        out_shape=jax.ShapeDtypeStruct((t, MIX_W), BF16),
        scratch_shapes=[pltpu.VMEM((N_HEADS, HEAD_DIM, HEAD_DIM), F32), pltpu.VMEM((tb, MIX_W), F32)],
        compiler_params=_cparams(("arbitrary",)),
        name="rwkv_scan",
    )(qm, y0, m, n, r, k, v, g, ln_g.reshape(1, -1), ln_b.reshape(1, -1), r_k.reshape(1, -1))


def _rwkv_branch(p, mu, w0, w_up, a0, a_up, g_up, k_k, k_a, r_k, ln_g, ln_b):
    prep = _rwkv_prep(p, mu, w0, w_up, a0, a_up, g_up, k_k, k_a)
    return _rwkv_recurrence(*prep, ln_g, ln_b, r_k.reshape(-1))


def _rope_kernel(pos_ref, inv_ref, cos_ref, sin_ref):
    ang = pos_ref[...].astype(F32) * inv_ref[...]
    lane = lax.broadcasted_iota(jnp.int32, ang.shape, 1)
    cos_ref[...] = jnp.cos(ang)
    sin_ref[...] = jnp.where(lane % HEAD_DIM < HEAD_DIM // 2, -jnp.sin(ang), jnp.sin(ang))


def _rope_table(pos_b, tb=512):
    t = pos_b.shape[0]
    d = HEAD_DIM
    inv = 1.0 / (ROPE_BASE ** (jnp.arange(0, d, 2, dtype=F32) / d))
    inv_b = jnp.tile(inv, 4).reshape(1, 128)
    blk = pl.BlockSpec((tb, 128), lambda i: (i, 0))
    return pl.pallas_call(
        _rope_kernel,
        grid=(t // tb,),
        in_specs=[blk, pl.BlockSpec((1, 128), lambda i: (0, 0))],
        out_specs=[blk, blk],
        out_shape=[jax.ShapeDtypeStruct((t, 128), F32)] * 2,
        compiler_params=_cparams(("parallel",)),
        name="rope_table",
    )(pos_b, inv_b)


def _ret_kernel(q_ref, k_ref, v_ref, g_ref, cos_ref, sin_ref, dec_ref, zeta_ref, xi_ref, cd_ref, ng_ref,
                o_ref, s_ref, o_sc):
    @pl.when(pl.program_id(0) == 0)
    def _():
        s_ref[...] = jnp.zeros_like(s_ref)

    cos = jnp.concatenate([cos_ref[...]] * 4, axis=1)
    sin = jnp.concatenate([sin_ref[...]] * 4, axis=1)
    lane = lax.broadcasted_iota(jnp.int32, cos.shape, 1)
    lower_half = lane % HEAD_DIM < HEAD_DIM // 2

    def rope(x):
        swapped = jnp.where(lower_half, pltpu.roll(x, MIX_W - HEAD_DIM // 2, 1), pltpu.roll(x, HEAD_DIM // 2, 1))
        return x * cos + swapped * sin

    q = _heads(rope(q_ref[...]))
    k = _heads(rope(k_ref[...]) * (HEAD_DIM ** -0.5))
    v = _heads(v_ref[...])
    s = s_ref[...]
    scores = _mm(q, k, "x1", "nt") * dec_ref[...]
    o = _mm(scores, v, "x1") + _mm(q * xi_ref[...], s, "x1")
    s_ref[...] = s * cd_ref[...] + _mm(k * zeta_ref[...], v, "x1", "tn")
    on = o * lax.rsqrt(jnp.mean(o * o, axis=-1, keepdims=True) + NORM_EPS)
    for h in range(N_HEADS):
        o_sc[:, h * HEAD_DIM:(h + 1) * HEAD_DIM] = on[h]
    gt = g_ref[...]
    o_ref[...] = (o_sc[...] * ng_ref[...] * (gt * _sigmoid(gt))).astype(o_ref.dtype)


def _retention(p, cos_t, sin_t, norm_g):
    t = p.shape[0]
    c = RET_CHUNK
    h = N_HEADS
    log_g = jnp.log(1.0 - jnp.exp(jnp.linspace(math.log(1.0 / 32), math.log(1.0 / 512), h, dtype=F32)))
    idx = jnp.arange(c, dtype=F32)
    diff = idx[:, None] - idx[None, :]
    decay = jnp.where(diff >= 0, jnp.exp(log_g[:, None, None] * jnp.maximum(diff, 0.0)[None]), 0.0)
    zeta = jnp.exp(log_g[:, None] * (c - 1 - idx)[None])
    xi = jnp.exp(log_g[:, None] * (idx + 1)[None])
    cdec = jnp.exp(log_g * c)
    zeta_b = jnp.broadcast_to(zeta[:, :, None], (h, c, HEAD_DIM))
    xi_b = jnp.broadcast_to(xi[:, :, None], (h, c, HEAD_DIM))
    cd_b = jnp.broadcast_to(cdec[:, None, None], (h, HEAD_DIM, HEAD_DIM))

    def blk(off):
        return pl.BlockSpec((c, MIX_W), lambda i: (i, off // MIX_W))

    tab = pl.BlockSpec((c, 128), lambda i: (i, 0))

    def const(shape):
        return pl.BlockSpec(shape, lambda i: (0,) * len(shape))

    return pl.pallas_call(
        _ret_kernel,
        grid=(t // c,),
        in_specs=[blk(OFF_BQ), blk(OFF_BK), blk(OFF_BV), blk(OFF_BG), tab, tab,
                  const((h, c, c)), const((h, c, HEAD_DIM)), const((h, c, HEAD_DIM)),
                  const((h, HEAD_DIM, HEAD_DIM)), const((1, MIX_W))],
        out_specs=pl.BlockSpec((c, MIX_W), lambda i: (i, 0)),
        out_shape=jax.ShapeDtypeStruct((t, MIX_W), BF16),
        scratch_shapes=[pltpu.VMEM((h, HEAD_DIM, HEAD_DIM), F32), pltpu.VMEM((c, MIX_W), F32)],
        compiler_params=_cparams(("arbitrary",)),
        name="retention",
    )(p, p, p, p, cos_t, sin_t, decay, zeta_b, xi_b, cd_b, norm_g.reshape(1, -1))


def _lru_kernel(x_ref, xt_ref, gb_ref, pos_ref, cw_ref, cb_ref, wa_ref, ba_ref, wx_ref, bx_ref, lam_ref,
                o_ref, sc_ref, a_sc, b_sc, h_ref):
    tb = x_ref.shape[0]
    first = pl.program_id(0) == 0

    @pl.when(first)
    def _():
        h_ref[...] = jnp.zeros_like(h_ref)

    x = x_ref[...]
    sc_ref[0:8, :] = jnp.where(first, 0.0, xt_ref[...])
    sc_ref[8:8 + tb, :] = x
    cw = cw_ref[...]
    xc = cb_ref[...] + x * cw[CONV_W - 1:CONV_W, :]
    for j in range(1, CONV_W):
        xc = xc + sc_ref[8 - j:8 - j + tb, :] * cw[CONV_W - 1 - j:CONV_W - j, :]
    r = _sigmoid(_mm(xc, wa_ref[...], "x3") + ba_ref[...])
    ig = _sigmoid(_mm(xc, wx_ref[...], "x3") + bx_ref[...])
    nl = -lam_ref[...]
    softplus = jnp.maximum(nl, 0.0) + jnp.log1p(jnp.exp(-jnp.abs(nl)))
    log_a = -LRU_C * r * softplus
    pos = pos_ref[...]
    reset = jnp.concatenate([pos, pos, pos, pos], axis=1) == 0
    a_sc[...] = jnp.where(reset, 0.0, jnp.exp(log_a))
    th = jnp.tanh(log_a)
    b_sc[...] = jnp.where(reset, 1.0, jnp.sqrt(-2.0 * th / (1.0 - th))) * ig * xc

    row = lax.broadcasted_iota(jnp.int32, (8, MIX_W), 0)

    def group(gi, h):
        off = pl.multiple_of(gi * 8, 8)
        a = a_sc[pl.ds(off, 8), :]
        b = b_sc[pl.ds(off, 8), :]
        for d in (1, 2, 4):
            keep = row >= d
            b = jnp.where(keep, a * pltpu.roll(b, d, 0) + b, b)
            a = jnp.where(keep, a * pltpu.roll(a, d, 0), a)
        hs = a * h + b
        b_sc[pl.ds(off, 8), :] = hs
        return hs[7:8, :]

    h_ref[...] = lax.fori_loop(0, tb // 8, group, h_ref[...])
    gb = gb_ref[...]
    gelu = gb * (0.5 * (1.0 + jnp.tanh(math.sqrt(2.0 / math.pi) * (gb + 0.044715 * (gb * gb * gb)))))
    o_ref[...] = (b_sc[...] * gelu).astype(o_ref.dtype)


def _block_diag_weight(w):
    g, n, _ = w.shape
    eye = jnp.eye(g, dtype=w.dtype)
    return (eye[:, None, :, None] * w[:, :, None, :]).reshape(g * n, g * n)


def _rglru(p, pos_b, conv_w, conv_b, w_a, b_a, w_x, b_x, lam, tb=256):
    t = p.shape[0]
    nb8 = tb // 8
    vec = pl.BlockSpec((1, MIX_W), lambda i: (0, 0))
    mat = pl.BlockSpec((MIX_W, MIX_W), lambda i: (0, 0))
    cw8 = jnp.concatenate([conv_w, jnp.zeros((8 - CONV_W, MIX_W), F32)], axis=0)
    return pl.pallas_call(
        _lru_kernel,
        grid=(t // tb,),
        in_specs=[pl.BlockSpec((tb, MIX_W), lambda i: (i, OFF_CX // MIX_W)),
                  pl.BlockSpec((8, MIX_W), lambda i: (jnp.maximum(i * nb8 - 1, 0), OFF_CX // MIX_W)),
                  pl.BlockSpec((tb, MIX_W), lambda i: (i, OFF_CG // MIX_W)),
                  pl.BlockSpec((tb, 128), lambda i: (i, 0)),
                  pl.BlockSpec((8, MIX_W), lambda i: (0, 0)), vec, mat, vec, mat, vec, vec],
        out_specs=pl.BlockSpec((tb, MIX_W), lambda i: (i, 0)),
        out_shape=jax.ShapeDtypeStruct((t, MIX_W), BF16),
        scratch_shapes=[pltpu.VMEM((tb + 8, MIX_W), F32), pltpu.VMEM((tb, MIX_W), F32),
                        pltpu.VMEM((tb, MIX_W), F32), pltpu.VMEM((1, MIX_W), F32)],
        compiler_params=_cparams(("arbitrary",)),
        name="rglru",
    )(p, p, p, pos_b, cw8, conv_b.reshape(1, -1), _block_diag_weight(w_a), b_a.reshape(1, -1),
      _block_diag_weight(w_x), b_x.reshape(1, -1), lam.reshape(1, -1))


def _gla_kernel(v_ref, gt_ref, q_ref, k_ref, al_ref, aup_ref, ab_ref, ng_ref, o_ref, s_ref):
    @pl.when(pl.program_id(0) == 0)
    def _():
        s_ref[...] = jnp.zeros_like(s_ref)

    c = q_ref.shape[0]
    row = lax.broadcasted_iota(jnp.int32, (c, c), 0)
    col = lax.broadcasted_iota(jnp.int32, (c, c), 1)
    causal = row >= col
    pre = _mm(al_ref[...], aup_ref[...], "x3") + ab_ref[...]
    log_a = (jnp.minimum(pre, 0.0) - jnp.log1p(jnp.exp(-jnp.abs(pre)))) / GLA_GATE_NORM
    bcum = _mm_exact_lhs(causal.astype(F32), log_a, 3)
    blast = bcum[c - 1:c, :]
    k = k_ref[...]
    q_e = _heads(q_ref[...] * (GLA_DK ** -0.5) * jnp.exp(bcum))
    k_e = _heads(k * jnp.exp(-bcum))
    k_l = _heads(k * jnp.exp(blast - bcum))
    dec = _heads(jnp.exp(blast))
    v = _heads(v_ref[...], GLA_DV)
    s = s_ref[...]
    scores = jnp.where(causal, _mm(q_e, k_e, "x1", "nt"), 0.0)
    o = _mm(scores, v, "x1") + _mm(q_e, s, "x1", "nt")
    s_ref[...] = s * dec + _mm(v, k_l, "x1", "tn")
    on = o * lax.rsqrt(jnp.mean(o * o, axis=-1, keepdims=True) + NORM_EPS)
    on = jnp.concatenate([on[h] for h in range(GLA_HEADS)], axis=1) * ng_ref[...]
    gt = gt_ref[...]
    o_ref[...] = (on * (gt * _sigmoid(gt))).astype(o_ref.dtype)


def _gla(p, alpha_up, alpha_b, norm_g):
    t = p.shape[0]
    c = GLA_CHUNK
    hk = MIX_W // 2
    aup = jnp.concatenate([alpha_up, jnp.zeros((128 - GLA_LORA, hk), F32)], axis=0)

    def blk(width, off):
        return pl.BlockSpec((c, width), lambda i: (i, off // width))

    return pl.pallas_call(
        _gla_kernel,
        grid=(t // c,),
        in_specs=[blk(MIX_W, OFF_DV), blk(MIX_W, OFF_DGT), blk(hk, OFF_DQ), blk(hk, OFF_DK), blk(128, OFF_DAL),
                  pl.BlockSpec((128, hk), lambda i: (0, 0)), pl.BlockSpec((1, hk), lambda i: (0, 0)),
                  pl.BlockSpec((1, MIX_W), lambda i: (0, 0))],
        out_specs=pl.BlockSpec((c, MIX_W), lambda i: (i, 0)),
        out_shape=jax.ShapeDtypeStruct((t, MIX_W), BF16),
        scratch_shapes=[pltpu.VMEM((GLA_HEADS, GLA_DV, GLA_DK), F32)],
        compiler_params=_cparams(("arbitrary",)),
        name="gla",
    )(p, p, p, p, p, aup, alpha_b.reshape(1, -1), norm_g.reshape(1, -1))


def _merge_kernel(ya_ref, yb_ref, yc_ref, yd_ref, wa_ref, wb_ref, wc_ref, wd_ref,
                  ga_ref, gb_ref, gc_ref, gd_ref, o_ref):
    acc = _sigmoid(ga_ref[...]) * _dot(ya_ref[...], wa_ref[...])
    acc += _sigmoid(gb_ref[...]) * _dot(yb_ref[...], wb_ref[...])
    acc += _sigmoid(gc_ref[...]) * _dot(yc_ref[...], wc_ref[...])
    acc += _sigmoid(gd_ref[...]) * _dot(yd_ref[...], wd_ref[...])
    o_ref[...] = acc.astype(o_ref.dtype)


def _merge(ys, ws, p, tm=512, tn=512):
    t = p.shape[0]
    yspec = pl.BlockSpec((tm, MIX_W), lambda i, j: (i, 0))
    wspec = pl.BlockSpec((MIX_W, tn), lambda i, j: (0, j))

    def gspec(kk):
        return pl.BlockSpec((tm, tn), lambda i, j: (i, (OFF_G + kk * D_MODEL) // tn + j))

    return pl.pallas_call(
        _merge_kernel,
        grid=(t // tm, D_MODEL // tn),
        in_specs=[yspec] * 4 + [wspec] * 4 + [gspec(kk) for kk in range(4)],
        out_specs=pl.BlockSpec((tm, tn), lambda i, j: (i, j)),
        out_shape=jax.ShapeDtypeStruct((t, D_MODEL), BF16),
        compiler_params=_cparams(("parallel", "parallel")),
        name="merge",
    )(*ys, *ws, p, p, p, p)


def _ffn_norm_kernel(h_ref, g_ref, wr_ref, br_ref, hn_ref, rt_ref):
    x = h_ref[...]
    ms = jnp.mean(x * x, axis=-1, keepdims=True)
    hn = x * lax.rsqrt(ms + NORM_EPS) * g_ref[...]
    hn_ref[...] = hn
    z = _mm(hn, wr_ref[...], "x3") + br_ref[...]
    lane = lax.broadcasted_iota(jnp.int32, z.shape, 1)
    neg = jnp.float32(-1e30)

    def first_argmax(v):
        m = jnp.max(v, axis=-1, keepdims=True)
        return m, jnp.min(jnp.where(v == m, lane, 128), axis=-1, keepdims=True)

    is_group = lane < N_GROUPS
    zg = jnp.where(is_group, z, neg)
    mg, g_idx = first_argmax(zg)
    pg_top = 1.0 / jnp.sum(jnp.where(is_group, jnp.exp(zg - mg), 0.0), axis=-1, keepdims=True)
    lo = N_GROUPS + g_idx * EXP_PER_GROUP
    in_group = jnp.logical_and(lane >= lo, lane < lo + EXP_PER_GROUP)
    ze = jnp.where(in_group, z, neg)
    m1, i1 = first_argmax(ze)
    se = jnp.sum(jnp.where(in_group, jnp.exp(ze - m1), 0.0), axis=-1, keepdims=True)
    m2, i2 = first_argmax(jnp.where(lane == i1, neg, ze))
    p1 = 1.0 / se
    p2 = jnp.exp(m2 - m1) / se
    tot = p1 + p2
    rt_ref[...] = jnp.where(lane == 0, pg_top * (p1 / tot),
                            jnp.where(lane == 1, pg_top * (p2 / tot),
                                      jnp.where(lane == 2, (i1 - N_GROUPS).astype(F32),
                                                jnp.where(lane == 3, (i2 - N_GROUPS).astype(F32), 0.0))))


def _ffn_norm_router(h, g, w_r, b_r, tb=256):
    t, d = h.shape
    return pl.pallas_call(
        _ffn_norm_kernel,
        grid=(t // tb,),
        in_specs=[pl.BlockSpec((tb, d), lambda i: (i, 0)), pl.BlockSpec((1, d), lambda i: (0, 0)),
                  pl.BlockSpec((d, 128), lambda i: (0, 0)), pl.BlockSpec((1, 128), lambda i: (0, 0))],
        out_specs=[pl.BlockSpec((tb, d), lambda i: (i, 0)), pl.BlockSpec((tb, 128), lambda i: (i, 0))],
        out_shape=[jax.ShapeDtypeStruct((t, d), F32), jax.ShapeDtypeStruct((t, 128), F32)],
        compiler_params=_cparams(("parallel",)),
        name="ffn_norm_router",
    )(h, g.reshape(1, d), w_r, b_r)


def _row_copy(src_hbm, idx, dst, r, sem):
    return pltpu.make_async_copy(src_hbm.at[pl.ds(idx, 1), :], dst.at[pl.ds(r, 1), :], sem)


def _expert_changed(i, exp_ref):
    return jnp.logical_or(i == 0, exp_ref[i] != exp_ref[jnp.maximum(i - 1, 0)])


def _gather_rows(src_hbm, idx_ref, base, n, dst, sem):
    def start(r, carry):
        _row_copy(src_hbm, idx_ref[base + r], dst, r, sem).start()
        return carry

    lax.fori_loop(0, n, start, 0, unroll=8)


def _gather_wait(src_hbm, dst, sem):
    pltpu.make_async_copy(src_hbm.at[pl.ds(0, dst.shape[0]), :], dst, sem).wait()


def _expert_up_kernel(tok_ref, exp_ref, nblk_ref, x_hbm, wg_ref, wu_ref, o_ref, xbuf, wg_bf, wu_bf, sem):
    i = pl.program_id(0)
    n = nblk_ref[0]
    slot = i % 2

    @pl.when(jnp.logical_and(i == 0, n > 0))
    def _():
        _gather_rows(x_hbm, tok_ref, 0, MOE_BLOCK, xbuf.at[0], sem.at[0])

    @pl.when(i + 1 < n)
    def _():
        _gather_rows(x_hbm, tok_ref, (i + 1) * MOE_BLOCK, MOE_BLOCK, xbuf.at[1 - slot], sem.at[1 - slot])

    @pl.when(i < n)
    def _():
        @pl.when(_expert_changed(i, exp_ref))
        def _():
            wg_bf[...] = wg_ref[0, 0].astype(BF16)
            wu_bf[...] = wu_ref[0, 0].astype(BF16)

        _gather_wait(x_hbm, xbuf.at[slot], sem.at[slot])
        x = xbuf[slot].astype(BF16)
        gate = _dot(x, wg_bf[...])
        up = _dot(x, wu_bf[...])
        o_ref[...] = (gate * _sigmoid(gate) * up).astype(o_ref.dtype)

    @pl.when(i >= n)
    def _():
        o_ref[...] = jnp.zeros_like(o_ref)


def _expert_down_kernel(exp_ref, nblk_ref, hm_ref, gate_ref, wd_ref, o_ref, wd_bf):
    i = pl.program_id(0)
    active = i < nblk_ref[0]

    @pl.when(active)
    def _():
        @pl.when(_expert_changed(i, exp_ref))
        def _():
            wd_bf[...] = wd_ref[0, 0].astype(BF16)

        o_ref[...] = _dot(hm_ref[...], wd_bf[...]) * gate_ref[...]

    @pl.when(jnp.logical_not(active))
    def _():
        o_ref[...] = jnp.zeros_like(o_ref)


def _experts(hn, row_tok, block_exp, n_used, row_gate, w_gate, w_up, w_down, layer):
    n_rows = row_tok.shape[0]
    n_blocks = n_rows // MOE_BLOCK
    d = hn.shape[1]
    hmid = pl.pallas_call(
        _expert_up_kernel,
        grid_spec=pltpu.PrefetchScalarGridSpec(
            num_scalar_prefetch=3,
            grid=(n_blocks,),
            in_specs=[pl.BlockSpec(memory_space=pl.ANY),
                      pl.BlockSpec((1, 1, d, D_EXPERT), lambda i, tok, ex, nb: (layer, ex[i], 0, 0)),
                      pl.BlockSpec((1, 1, d, D_EXPERT), lambda i, tok, ex, nb: (layer, ex[i], 0, 0))],
            out_specs=pl.BlockSpec((MOE_BLOCK, D_EXPERT), lambda i, tok, ex, nb: (i, 0)),
            scratch_shapes=[pltpu.VMEM((2, MOE_BLOCK, d), F32), pltpu.VMEM((d, D_EXPERT), BF16),
                            pltpu.VMEM((d, D_EXPERT), BF16), pltpu.SemaphoreType.DMA((2,))],
        ),
        out_shape=jax.ShapeDtypeStruct((n_rows, D_EXPERT), BF16),
        compiler_params=_cparams(("arbitrary",), VMEM_LIMIT_EXPERT),
        name="expert_up",
    )(row_tok, block_exp, n_used, hn, w_gate, w_up)
    return pl.pallas_call(
        _expert_down_kernel,
        grid_spec=pltpu.PrefetchScalarGridSpec(
            num_scalar_prefetch=2,
            grid=(n_blocks,),
            in_specs=[pl.BlockSpec((MOE_BLOCK, D_EXPERT), lambda i, ex, nb: (i, 0)),
                      pl.BlockSpec((MOE_BLOCK, 1), lambda i, ex, nb: (i, 0)),
                      pl.BlockSpec((1, 1, D_EXPERT, d), lambda i, ex, nb: (layer, ex[i], 0, 0))],
            out_specs=pl.BlockSpec((MOE_BLOCK, d), lambda i, ex, nb: (i, 0)),
            scratch_shapes=[pltpu.VMEM((D_EXPERT, d), BF16)],
        ),
        out_shape=jax.ShapeDtypeStruct((n_rows, d), F32),
        compiler_params=_cparams(("arbitrary",)),
        name="expert_down",
    )(block_exp, n_used, hmid, row_gate, w_down)


def _combine_kernel(pos_ref, y_hbm, h_ref, g_ref, o_ref, buf, sem, *, final_norm):
    tb = h_ref.shape[0]
    i = pl.program_id(0)
    slot = i % 2

    def fetch(blk, sl):
        for s in range(TOP_K):
            _gather_rows(y_hbm, pos_ref, (s * pl.num_programs(0) + blk) * tb, tb, buf.at[sl, s], sem.at[sl])

    @pl.when(i == 0)
    def _():
        fetch(0, 0)

    @pl.when(i + 1 < pl.num_programs(0))
    def _():
        fetch(i + 1, 1 - slot)

    for s in range(TOP_K):
        _gather_wait(y_hbm, buf.at[slot, s], sem.at[slot])
    out = h_ref[...] + (buf[slot, 0] + buf[slot, 1])
    if final_norm:
        ms = jnp.mean(out * out, axis=-1, keepdims=True)
        out = out * lax.rsqrt(ms + NORM_EPS) * g_ref[...]
    o_ref[...] = out


def _combine(h, y_rows, pos, g, final_norm, tb=128):
    t, d = h.shape
    grid_spec = pltpu.PrefetchScalarGridSpec(
        num_scalar_prefetch=1,
        grid=(t // tb,),
        in_specs=[pl.BlockSpec(memory_space=pl.ANY),
                  pl.BlockSpec((tb, d), lambda i, pos: (i, 0)),
                  pl.BlockSpec((1, d), lambda i, pos: (0, 0))],
        out_specs=pl.BlockSpec((tb, d), lambda i, pos: (i, 0)),
        scratch_shapes=[pltpu.VMEM((2, TOP_K, tb, d), F32), pltpu.SemaphoreType.DMA((2,))],
    )
    return pl.pallas_call(
        functools.partial(_combine_kernel, final_norm=final_norm),
        grid_spec=grid_spec,
        out_shape=jax.ShapeDtypeStruct((t, d), F32),
        compiler_params=_cparams(("arbitrary",)),
        name="moe_combine",
    )(pos, y_rows, h, g.reshape(1, d))


def _routing(route):
    t = route.shape[0]
    gate = route[:, 0:TOP_K].reshape(-1)
    expert_id = route[:, TOP_K:2 * TOP_K].astype(jnp.int32).reshape(-1)
    n_assign = t * TOP_K
    order = jnp.argsort(expert_id).astype(jnp.int32)
    st = order // TOP_K
    sg = gate[order]
    se = expert_id[order]
    counts = jnp.sum((expert_id[:, None] == jnp.arange(N_EXPERTS, dtype=jnp.int32)[None, :]).astype(jnp.int32), axis=0)
    starts = jnp.cumsum(counts) - counts
    padded = (counts + MOE_BLOCK - 1) // MOE_BLOCK * MOE_BLOCK
    pad_ends = jnp.cumsum(padded)
    pad_starts = pad_ends - padded
    dest = pad_starts[se] + (jnp.arange(n_assign, dtype=jnp.int32) - starts[se])
    n_blocks = (n_assign + MOE_BLOCK - 1) // MOE_BLOCK + N_EXPERTS
    n_rows = n_blocks * MOE_BLOCK
    block_start = jnp.arange(n_blocks, dtype=jnp.int32) * MOE_BLOCK
    block_exp = jnp.minimum(jnp.sum(block_start[:, None] >= pad_ends[None, :], axis=1), N_EXPERTS - 1).astype(jnp.int32)
    local = (block_start - pad_starts[block_exp])[:, None] + jnp.arange(MOE_BLOCK, dtype=jnp.int32)[None, :]
    valid = (local < counts[block_exp][:, None]).reshape(-1)
    src = jnp.clip(starts[block_exp][:, None] + local, 0, n_assign - 1).reshape(-1)
    row_tok = jnp.where(valid, st[src], 0)
    row_gate = jnp.where(valid, sg[src], 0.0)
    n_used = (pad_ends[-1] // MOE_BLOCK).astype(jnp.int32).reshape(1)
    pos = dest[jnp.argsort(order)].reshape(t, TOP_K).T.reshape(-1)
    return row_tok, row_gate.reshape(n_rows, 1), block_exp, n_used, pos


def _moe_layer(h, norm_g, wg_r, bg_r, we_r, be_r, w_gate, w_up, w_down, layer, out_g, final_norm):
    d = h.shape[1]
    w_r = jnp.concatenate([wg_r, we_r, jnp.zeros((d, 128 - N_GROUPS - N_EXPERTS), F32)], axis=1)
    b_r = jnp.concatenate([bg_r, be_r, jnp.zeros((128 - N_GROUPS - N_EXPERTS,), F32)]).reshape(1, 128)
    hn, route = _ffn_norm_router(h, norm_g, w_r, b_r)
    row_tok, row_gate, block_exp, n_used, pos = _routing(route)
    y_rows = _experts(hn, row_tok, block_exp, n_used, row_gate, w_gate, w_up, w_down, layer)
    return _combine(h, y_rows, pos, out_g, final_norm)


def _permute_w_in(w):
    d0 = A_COLS + B_COLS + C_COLS
    g0 = d0 + D_COLS
    b0 = A_COLS
    c0 = A_COLS + B_COLS

    def z(n):
        return jnp.zeros((w.shape[0], n), w.dtype)

    parts = [w[:, b0:b0 + B_COLS], w[:, c0:c0 + C_COLS],
             w[:, d0 + 512:d0 + 1024], w[:, d0 + 1040:d0 + 1552],
             w[:, 0:3 * MIX_W], w[:, d0:d0 + 512],
             w[:, 3 * MIX_W:A_COLS], z(256 - RWKV_LORA),
             w[:, d0 + 1024:d0 + 1040], z(128 - GLA_LORA), z(OFF_G - OFF_DAL - 128),
             w[:, g0:]]
    out = jnp.concatenate(parts, axis=1).astype(BF16)
    assert out.shape[1] == NP_COLS
    return out


def kernel(x, positions, norm_mix_g, w_in, rwkv_mu, rwkv_w0, rwkv_w_up, rwkv_a0, rwkv_a_up, rwkv_g_up, rwkv_k_k, rwkv_k_a, rwkv_r_k, rwkv_ln_g, rwkv_ln_b, rwkv_w_o, ret_norm_g, ret_w_o, lru_conv_w, lru_conv_b, lru_w_a, lru_b_a, lru_w_x, lru_b_x, lru_lambda, lru_w_o, gla_alpha_up, gla_alpha_b, gla_norm_g, gla_w_o, w_out, norm_ffn_g, router_group_w, router_group_b, router_expert_w, router_expert_b, moe_w_gate, moe_w_up, moe_w_down, final_norm_g):
    b_, s_, d = x.shape
    assert b_ == 1 and d == D_MODEL
    depth = w_in.shape[0]
    h = x.reshape(s_, d)
    pos_b = jnp.broadcast_to(positions.reshape(s_, 1), (s_, 128)).astype(jnp.int32)
    cos_t, sin_t = _rope_table(pos_b)
    for l in range(depth):
        xn = _rmsnorm(h, norm_mix_g[l], BF16)
        p = _matmul(xn, _permute_w_in(w_in[l]), 1024, 512, name="in_proj")
        y_a = _rwkv_branch(p, rwkv_mu[l], rwkv_w0[l], rwkv_w_up[l], rwkv_a0[l], rwkv_a_up[l], rwkv_g_up[l],
                           rwkv_k_k[l], rwkv_k_a[l], rwkv_r_k[l], rwkv_ln_g[l], rwkv_ln_b[l])
        y_b = _retention(p, cos_t, sin_t, ret_norm_g[l])
        y_c = _rglru(p, pos_b, lru_conv_w[l], lru_conv_b[l], lru_w_a[l], lru_b_a[l], lru_w_x[l], lru_b_x[l],
                     lru_lambda[l])
        y_d = _gla(p, gla_alpha_up[l], gla_alpha_b[l], gla_norm_g[l])
        merged = _merge((y_a, y_b, y_c, y_d),
                        tuple(w[l].astype(BF16) for w in (rwkv_w_o, ret_w_o, lru_w_o, gla_w_o)), p)
        h = _matmul_residual(merged, w_out[l].astype(BF16), h, 1024, 512)
        last = l == depth - 1
        h = _moe_layer(h, norm_ffn_g[l], router_group_w[l], router_group_b[l], router_expert_w[l],
                       router_expert_b[l], moe_w_gate, moe_w_up, moe_w_down, l,
                       final_norm_g if last else norm_ffn_g[l], last)
    return h.reshape(b_, s_, d)
```

```python
import functools
import math

import jax
import jax.numpy as jnp
from jax import lax
from jax.experimental import pallas as pl
from jax.experimental.pallas import tpu as pltpu

F32 = jnp.float32
BF16 = jnp.bfloat16
HI = lax.Precision.HIGHEST

D_MODEL = 2048
MIX_W = 512
NORM_EPS = 1e-6
HEAD_DIM = 64
N_HEADS = MIX_W // HEAD_DIM

RWKV_W_LORA, RWKV_A_LORA, RWKV_G_LORA = 32, 32, 96
RWKV_LORA = RWKV_W_LORA + RWKV_A_LORA + RWKV_G_LORA
RWKV_DECAY_SCALE = 0.6065306597126334
RWKV_LN_EPS = 64e-5
RWKV_CHUNK = 64

RET_CHUNK = 128
ROPE_BASE = 10000.0

LRU_BLOCKS = 8
CONV_W = 4
LRU_C = 8.0

GLA_HEADS = 4
GLA_DK = 64
GLA_DV = 128
GLA_LORA = 16
GLA_GATE_NORM = 16.0
GLA_CHUNK = 64

N_GROUPS = 4
EXP_PER_GROUP = 8
N_EXPERTS = N_GROUPS * EXP_PER_GROUP
TOP_K = 2
D_EXPERT = 1024
MOE_BLOCK = 128

A_COLS = 3 * MIX_W + RWKV_LORA
B_COLS = 4 * MIX_W
C_COLS = 2 * MIX_W
D_COLS = 2 * (MIX_W // 2) + MIX_W + GLA_LORA + MIX_W

OFF_BQ, OFF_BK, OFF_BV, OFF_BG = 0, 512, 1024, 1536
OFF_CX, OFF_CG = 2048, 2560
OFF_DV, OFF_DGT = 3072, 3584
OFF_AR, OFF_AK, OFF_AV = 4096, 4608, 5120
OFF_DQ, OFF_DK = 5632, 5888
OFF_AL = 6144
OFF_DAL = 6400
OFF_G = 6656
NP_COLS = OFF_G + 4 * D_MODEL

VMEM_LIMIT = 48 * 1024 * 1024
VMEM_LIMIT_EXPERT = 56 * 1024 * 1024


def _cparams(sem, vmem=VMEM_LIMIT):
    return pltpu.CompilerParams(dimension_semantics=sem, vmem_limit_bytes=vmem)


def _dot(a, b, prec=None):
    return jnp.dot(a, b, precision=prec, preferred_element_type=F32)


def _sigmoid(x):
    return 1.0 / (1.0 + jnp.exp(-x))


def _split(x):
    hi = x.astype(BF16)
    return hi, (x - hi.astype(F32)).astype(BF16)


def _dims(form, batched):
    ca, cb = {"nn": (1, 0), "nt": (1, 1), "tn": (0, 0)}[form]
    if batched:
        return (((ca + 1,), (cb + 1,)), ((0,), (0,)))
    return (((ca,), (cb,)), ((), ()))


def _mm(a, b, mode, form="nn"):
    dims = _dims(form, a.ndim == 3)
    if mode == "hi":
        return lax.dot_general(a, b, dims, precision=HI, preferred_element_type=F32)
    d = functools.partial(lax.dot_general, dimension_numbers=dims, preferred_element_type=F32)
    if mode == "x1":
        return d(a.astype(BF16), b.astype(BF16))
    ah, al = _split(a)
    bh, bl = _split(b)
    return d(ah, bh) + (d(ah, bl) + d(al, bh))


def _mm_exact_rhs(a, b, terms):
    b = b.astype(BF16)
    acc = None
    for _ in range(terms):
        piece = a.astype(BF16)
        part = _dot(piece, b)
        acc = part if acc is None else acc + part
        a = a - piece.astype(F32)
    return acc


def _mm_exact_lhs(a, b, terms):
    a = a.astype(BF16)
    acc = None
    for _ in range(terms):
        piece = b.astype(BF16)
        part = _dot(a, piece)
        acc = part if acc is None else acc + part
        b = b - piece.astype(F32)
    return acc


def _heads(x, width=HEAD_DIM):
    return jnp.stack([x[:, h * width:(h + 1) * width] for h in range(x.shape[1] // width)])


def _block_diag_const(n, blk, value):
    r = lax.broadcasted_iota(jnp.int32, (n, n), 0) // blk
    c = lax.broadcasted_iota(jnp.int32, (n, n), 1) // blk
    return jnp.where(r == c, value, 0.0).astype(F32)


def _rmsnorm_kernel(x_ref, g_ref, o_ref):
    x = x_ref[...]
    ms = jnp.mean(x * x, axis=-1, keepdims=True)
    o_ref[...] = (x * lax.rsqrt(ms + NORM_EPS) * g_ref[...]).astype(o_ref.dtype)


def _rmsnorm(x, g, out_dtype, tb=512):
    t, d = x.shape
    return pl.pallas_call(
        _rmsnorm_kernel,
        grid=(t // tb,),
        in_specs=[pl.BlockSpec((tb, d), lambda i: (i, 0)), pl.BlockSpec((1, d), lambda i: (0, 0))],
        out_specs=pl.BlockSpec((tb, d), lambda i: (i, 0)),
        out_shape=jax.ShapeDtypeStruct((t, d), out_dtype),
        compiler_params=_cparams(("parallel",)),
        name="rmsnorm",
    )(x, g.reshape(1, d))


def _mm_kernel(a_ref, b_ref, o_ref):
    o_ref[...] = _dot(a_ref[...], b_ref[...]).astype(o_ref.dtype)


def _matmul(a, b, tm, tn, out_dtype=F32, name="matmul"):
    m, k = a.shape
    n = b.shape[1]
    return pl.pallas_call(
        _mm_kernel,
        grid=(m // tm, n // tn),
        in_specs=[pl.BlockSpec((tm, k), lambda i, j: (i, 0)), pl.BlockSpec((k, tn), lambda i, j: (0, j))],
        out_specs=pl.BlockSpec((tm, tn), lambda i, j: (i, j)),
        out_shape=jax.ShapeDtypeStruct((m, n), out_dtype),
        compiler_params=_cparams(("parallel", "parallel")),
        name=name,
    )(a, b)


def _mm_res_kernel(a_ref, b_ref, r_ref, o_ref):
    o_ref[...] = r_ref[...] + _dot(a_ref[...], b_ref[...])


def _matmul_residual(a, b, res, tm, tn):
    m, k = a.shape
    n = b.shape[1]
    return pl.pallas_call(
        _mm_res_kernel,
        grid=(m // tm, n // tn),
        in_specs=[pl.BlockSpec((tm, k), lambda i, j: (i, 0)), pl.BlockSpec((k, tn), lambda i, j: (0, j)),
                  pl.BlockSpec((tm, tn), lambda i, j: (i, j))],
        out_specs=pl.BlockSpec((tm, tn), lambda i, j: (i, j)),
        out_shape=jax.ShapeDtypeStruct((m, n), F32),
        compiler_params=_cparams(("parallel", "parallel")),
        name="wout_residual",
    )(a, b, res)


def _shifted(x, tail_ref, sc_ref, width, first):
    tb = x.shape[0]
    sc_ref[0:8, 0:width] = jnp.where(first, 0.0, tail_ref[...])
    sc_ref[8:8 + tb, 0:width] = x
    return sc_ref[7:7 + tb, 0:width]


def _rwkv_prep_kernel(r_ref, k_ref, v_ref, l_ref, rt_ref, kt_ref, vt_ref, lt_ref,
                      mur_ref, muk_ref, muv_ref, mul_ref, wl_ref, b0_ref, kk_ref, ka_ref,
                      ro_ref, lw_ref, ko_ref, vo_ref, kkn_ref, kka_ref, g_ref, sc_ref):
    first = pl.program_id(0) == 0

    def mix(x_ref, t_ref, mu_ref, width):
        x = x_ref[...]
        prev = _shifted(x, t_ref, sc_ref, width, first)
        return x + (prev - x) * mu_ref[...]

    r = mix(r_ref, rt_ref, mur_ref, MIX_W)
    k = mix(k_ref, kt_ref, muk_ref, MIX_W)
    v = mix(v_ref, vt_ref, muv_ref, MIX_W)
    zl = mix(l_ref, lt_ref, mul_ref, 256)
    lane = lax.broadcasted_iota(jnp.int32, zl.shape, 1)
    act = jnp.where(lane < RWKV_W_LORA, jnp.tanh(zl),
                    jnp.where(lane < RWKV_W_LORA + RWKV_A_LORA, zl, _sigmoid(zl)))
    lo = _mm(act, wl_ref[...], "x3") + b0_ref[...]
    lw = -RWKV_DECAY_SCALE * _sigmoid(lo[:, 0:MIX_W])
    a = _sigmoid(lo[:, MIX_W:2 * MIX_W])
    g = lo[:, 2 * MIX_W:3 * MIX_W]
    kk = k * kk_ref[...]
    ss = _mm_exact_rhs(kk * kk, _block_diag_const(MIX_W, HEAD_DIM, 1.0), 2)
    kkn = kk / jnp.maximum(jnp.sqrt(ss), 1e-12)
    ro_ref[...] = r
    lw_ref[...] = lw
    ko_ref[...] = k * (1.0 + (a - 1.0) * ka_ref[...])
    vo_ref[...] = v
    kkn_ref[...] = kkn
    kka_ref[...] = kkn * a
    g_ref[...] = g


def _rwkv_prep(p, mu, w0, w_up, a0, a_up, g_up, k_k, k_a, tb=256):
    t = p.shape[0]
    nb8 = tb // 8

    def blk(width, off):
        return pl.BlockSpec((tb, width), lambda i: (i, off // width))

    def tail(width, off):
        return pl.BlockSpec((8, width), lambda i: (jnp.maximum(i * nb8 - 1, 0), off // width))

    def vec(width):
        return pl.BlockSpec((1, width), lambda i: (0, 0))

    w_lora = jnp.zeros((256, 3 * MIX_W), F32)
    w_lora = w_lora.at[0:32, 0:MIX_W].set(w_up)
    w_lora = w_lora.at[32:64, MIX_W:2 * MIX_W].set(a_up)
    w_lora = w_lora.at[64:160, 2 * MIX_W:].set(g_up)
    b0 = jnp.concatenate([w0, a0, jnp.zeros((MIX_W,), F32)]).reshape(1, 3 * MIX_W)
    mu_l = jnp.concatenate([mu[3 * MIX_W:], jnp.zeros((256 - RWKV_LORA,), F32)]).reshape(1, 256)
    out = jax.ShapeDtypeStruct((t, MIX_W), F32)
    return pl.pallas_call(
        _rwkv_prep_kernel,
        grid=(t // tb,),
        in_specs=[blk(MIX_W, OFF_AR), blk(MIX_W, OFF_AK), blk(MIX_W, OFF_AV), blk(256, OFF_AL),
                  tail(MIX_W, OFF_AR), tail(MIX_W, OFF_AK), tail(MIX_W, OFF_AV), tail(256, OFF_AL),
                  vec(MIX_W), vec(MIX_W), vec(MIX_W), vec(256),
                  pl.BlockSpec((256, 3 * MIX_W), lambda i: (0, 0)), vec(3 * MIX_W), vec(MIX_W), vec(MIX_W)],
        out_specs=[pl.BlockSpec((tb, MIX_W), lambda i: (i, 0))] * 7,
        out_shape=[out] * 7,
        scratch_shapes=[pltpu.VMEM((tb + 8, MIX_W), F32)],
        compiler_params=_cparams(("parallel",)),
        name="rwkv_prep",
    )(p, p, p, p, p, p, p, p,
      mu[0:MIX_W].reshape(1, -1), mu[MIX_W:2 * MIX_W].reshape(1, -1), mu[2 * MIX_W:3 * MIX_W].reshape(1, -1), mu_l,
      w_lora, b0, k_k.reshape(1, -1), k_a.reshape(1, -1))


RW_SC, RW_INV, RW_APPLY, RW_STATE, RW_SCAN = "x1", "x1", "x1", "x1", "x3"


def _rwkv_local_kernel(r_ref, lw_ref, k_ref, v_ref, kk_ref, kka_ref, q_ref, y0_ref, m_ref, n_ref):
    c = r_ref.shape[0]
    row = lax.broadcasted_iota(jnp.int32, (c, c), 0)
    col = lax.broadcasted_iota(jnp.int32, (c, c), 1)
    incl = row >= col
    strict = row > col
    eye = jnp.where(row == col, 1.0, 0.0)
    lw = lw_ref[...]
    k = k_ref[...]
    kka = kka_ref[...]
    cum = _mm_exact_lhs(incl.astype(F32), lw, 3)
    last = cum[c - 1:c, :]
    pinv = jnp.exp(-cum)
    dl = jnp.exp(last - cum)
    at = _heads(-kk_ref[...] * jnp.exp(cum - lw))
    rt = _heads(r_ref[...] * jnp.exp(cum))
    bt = _heads(kka * pinv)
    kt = _heads(k * pinv)
    bl = _heads(kka * dl)
    kl = _heads(k * dl)
    v = _heads(v_ref[...])
    pc = _heads(jnp.exp(last))
    sc = _mm(jnp.concatenate([at, rt], axis=1), jnp.concatenate([bt, kt], axis=1), RW_SC, "nt")
    a_ab = jnp.where(strict, sc[:, 0:c, 0:c], 0.0)
    a_ak = jnp.where(strict, sc[:, 0:c, c:2 * c], 0.0)
    r_b = jnp.where(incl, sc[:, c:2 * c, 0:c], 0.0)
    r_k = jnp.where(incl, sc[:, c:2 * c, c:2 * c], 0.0)
    x = eye + a_ab
    pw = a_ab
    for _ in range(int(math.log2(c)) - 1):
        pw = _mm(pw, pw, RW_INV)
        x = x + _mm(x, pw, RW_INV)
    wu = _mm(x, jnp.concatenate([at, _mm(a_ak, v, RW_APPLY)], axis=2), RW_APPLY)
    lower = jnp.concatenate([jnp.zeros_like(v), v], axis=2)
    qy = _mm(jnp.concatenate([r_b, r_k], axis=2), jnp.concatenate([wu, lower], axis=1), RW_APPLY)
    wb = _mm(wu, bl, RW_STATE, "tn")
    n = wb[:, HEAD_DIM:] + _mm(v, kl, RW_STATE, "tn")
    m = eye * pc + wb[:, 0:HEAD_DIM]
    qm = rt + qy[:, :, 0:HEAD_DIM]
    y0 = qy[:, :, HEAD_DIM:]
    for h in range(N_HEADS):
        sl = slice(h * HEAD_DIM, (h + 1) * HEAD_DIM)
        q_ref[:, sl] = qm[h]
        y0_ref[:, sl] = y0[h]
        m_ref[:, sl] = m[h]
        n_ref[:, sl] = n[h]


def _rwkv_scan_kernel(q_ref, y0_ref, m_ref, n_ref, r_ref, k_ref, v_ref, g_ref, lng_ref, lnb_ref, rk_ref,
                      o_ref, s_ref, y_sc):
    @pl.when(pl.program_id(0) == 0)
    def _():
        s_ref[...] = jnp.zeros_like(s_ref)

    c = RWKV_CHUNK
    s = s_ref[...]
    for j in range(q_ref.shape[0] // c):
        rows = slice(j * c, (j + 1) * c)
        y = _mm(_heads(q_ref[rows, :]), s, RW_SCAN, "nt") + _heads(y0_ref[rows, :])
        s = _mm(s, _heads(m_ref[rows, :]), RW_SCAN) + _heads(n_ref[rows, :])
        for h in range(N_HEADS):
            y_sc[rows, h * HEAD_DIM:(h + 1) * HEAD_DIM] = y[h]
    s_ref[...] = s
    y = y_sc[...]
    avg = _block_diag_const(MIX_W, HEAD_DIM, 1.0 / HEAD_DIM)
    mean = _mm_exact_rhs(y, avg, 2)
    yc = y - mean
    var = _mm_exact_rhs(yc * yc, avg, 2)
    yn = yc * lax.rsqrt(var + RWKV_LN_EPS) * lng_ref[...] + lnb_ref[...]
    v = v_ref[...]
    bonus = _mm_exact_rhs(r_ref[...] * k_ref[...] * rk_ref[...], _block_diag_const(MIX_W, HEAD_DIM, 1.0), 2) * v
    o_ref[...] = ((yn + bonus) * g_ref[...]).astype(o_ref.dtype)


def _rwkv_recurrence(r, lw, k, v, kkn, kka, g, ln_g, ln_b, r_k, tb=256):
    t = r.shape[0]
    c = RWKV_CHUNK
    cblk = pl.BlockSpec((c, MIX_W), lambda i: (i, 0))
    f = jax.ShapeDtypeStruct((t, MIX_W), F32)
    qm, y0, m, n = pl.pallas_call(
        _rwkv_local_kernel,
        grid=(t // c,),
        in_specs=[cblk] * 6,
        out_specs=[cblk] * 4,
        out_shape=[f] * 4,
        compiler_params=_cparams(("parallel",)),
        name="rwkv_local",
    )(r, lw, k, v, kkn, kka)
    blk = pl.BlockSpec((tb, MIX_W), lambda i: (i, 0))
    vec = pl.BlockSpec((1, MIX_W), lambda i: (0, 0))
    return pl.pallas_call(
        _rwkv_scan_kernel,
        grid=(t // tb,),
        in_specs=[blk] * 8 + [vec] * 3,
        out_specs=blk,
        out_shape=jax.ShapeDtypeStruct((t, MIX_W), BF16),
        scratch_shapes=[pltpu.VMEM((N_HEADS, HEAD_DIM, HEAD_DIM), F32), pltpu.VMEM((tb, MIX_W), F32)],
        compiler_params=_cparams(("arbitrary",)),
        name="rwkv_scan",
    )(qm, y0, m, n, r, k, v, g, ln_g.reshape(1, -1), ln_b.reshape(1, -1), r_k.reshape(1, -1))


def _rwkv_branch(p, mu, w0, w_up, a0, a_up, g_up, k_k, k_a, r_k, ln_g, ln_b):
    prep = _rwkv_prep(p, mu, w0, w_up, a0, a_up, g_up, k_k, k_a)
    return _rwkv_recurrence(*prep, ln_g, ln_b, r_k.reshape(-1))


def _rope_kernel(pos_ref, inv_ref, cos_ref, sin_ref):
    ang = pos_ref[...].astype(F32) * inv_ref[...]
    lane = lax.broadcasted_iota(jnp.int32, ang.shape, 1)
    cos_ref[...] = jnp.cos(ang)
    sin_ref[...] = jnp.where(lane % HEAD_DIM < HEAD_DIM // 2, -jnp.sin(ang), jnp.sin(ang))


def _rope_table(pos_b, tb=512):
    t = pos_b.shape[0]
    d = HEAD_DIM
    inv = 1.0 / (ROPE_BASE ** (jnp.arange(0, d, 2, dtype=F32) / d))
    inv_b = jnp.tile(inv, 4).reshape(1, 128)
    blk = pl.BlockSpec((tb, 128), lambda i: (i, 0))
    return pl.pallas_call(
        _rope_kernel,
        grid=(t // tb,),
        in_specs=[blk, pl.BlockSpec((1, 128), lambda i: (0, 0))],
        out_specs=[blk, blk],
        out_shape=[jax.ShapeDtypeStruct((t, 128), F32)] * 2,
        compiler_params=_cparams(("parallel",)),
        name="rope_table",
    )(pos_b, inv_b)


def _ret_kernel(q_ref, k_ref, v_ref, g_ref, cos_ref, sin_ref, dec_ref, zeta_ref, xi_ref, cd_ref, ng_ref,
                o_ref, s_ref, o_sc):
    @pl.when(pl.program_id(0) == 0)
    def _():
        s_ref[...] = jnp.zeros_like(s_ref)

    cos = jnp.concatenate([cos_ref[...]] * 4, axis=1)
    sin = jnp.concatenate([sin_ref[...]] * 4, axis=1)
    lane = lax.broadcasted_iota(jnp.int32, cos.shape, 1)
    lower_half = lane % HEAD_DIM < HEAD_DIM // 2

    def rope(x):
        swapped = jnp.where(lower_half, pltpu.roll(x, MIX_W - HEAD_DIM // 2, 1), pltpu.roll(x, HEAD_DIM // 2, 1))
        return x * cos + swapped * sin

    q = _heads(rope(q_ref[...]))
    k = _heads(rope(k_ref[...]) * (HEAD_DIM ** -0.5))
    v = _heads(v_ref[...])
    s = s_ref[...]
    scores = _mm(q, k, "x1", "nt") * dec_ref[...]
    o = _mm(scores, v, "x1") + _mm(q * xi_ref[...], s, "x1")
    s_ref[...] = s * cd_ref[...] + _mm(k * zeta_ref[...], v, "x1", "tn")
    on = o * lax.rsqrt(jnp.mean(o * o, axis=-1, keepdims=True) + NORM_EPS)
    for h in range(N_HEADS):
        o_sc[:, h * HEAD_DIM:(h + 1) * HEAD_DIM] = on[h]
    gt = g_ref[...]
    o_ref[...] = (o_sc[...] * ng_ref[...] * (gt * _sigmoid(gt))).astype(o_ref.dtype)


def _retention(p, cos_t, sin_t, norm_g):
    t = p.shape[0]
    c = RET_CHUNK
    h = N_HEADS
    log_g = jnp.log(1.0 - jnp.exp(jnp.linspace(math.log(1.0 / 32), math.log(1.0 / 512), h, dtype=F32)))
    idx = jnp.arange(c, dtype=F32)
    diff = idx[:, None] - idx[None, :]
    decay = jnp.where(diff >= 0, jnp.exp(log_g[:, None, None] * jnp.maximum(diff, 0.0)[None]), 0.0)
    zeta = jnp.exp(log_g[:, None] * (c - 1 - idx)[None])
    xi = jnp.exp(log_g[:, None] * (idx + 1)[None])
    cdec = jnp.exp(log_g * c)
    zeta_b = jnp.broadcast_to(zeta[:, :, None], (h, c, HEAD_DIM))
    xi_b = jnp.broadcast_to(xi[:, :, None], (h, c, HEAD_DIM))
    cd_b = jnp.broadcast_to(cdec[:, None, None], (h, HEAD_DIM, HEAD_DIM))

    def blk(off):
        return pl.BlockSpec((c, MIX_W), lambda i: (i, off // MIX_W))

    tab = pl.BlockSpec((c, 128), lambda i: (i, 0))

    def const(shape):
        return pl.BlockSpec(shape, lambda i: (0,) * len(shape))

    return pl.pallas_call(
        _ret_kernel,
        grid=(t // c,),
        in_specs=[blk(OFF_BQ), blk(OFF_BK), blk(OFF_BV), blk(OFF_BG), tab, tab,
                  const((h, c, c)), const((h, c, HEAD_DIM)), const((h, c, HEAD_DIM)),
                  const((h, HEAD_DIM, HEAD_DIM)), const((1, MIX_W))],
        out_specs=pl.BlockSpec((c, MIX_W), lambda i: (i, 0)),
        out_shape=jax.ShapeDtypeStruct((t, MIX_W), BF16),
        scratch_shapes=[pltpu.VMEM((h, HEAD_DIM, HEAD_DIM), F32), pltpu.VMEM((c, MIX_W), F32)],
        compiler_params=_cparams(("arbitrary",)),
        name="retention",
    )(p, p, p, p, cos_t, sin_t, decay, zeta_b, xi_b, cd_b, norm_g.reshape(1, -1))


def _lru_kernel(x_ref, xt_ref, gb_ref, pos_ref, cw_ref, cb_ref, wa_ref, ba_ref, wx_ref, bx_ref, lam_ref,
                o_ref, sc_ref, a_sc, b_sc, h_ref):
    tb = x_ref.shape[0]
    first = pl.program_id(0) == 0

    @pl.when(first)
    def _():
        h_ref[...] = jnp.zeros_like(h_ref)

    x = x_ref[...]
    sc_ref[0:8, :] = jnp.where(first, 0.0, xt_ref[...])
    sc_ref[8:8 + tb, :] = x
    cw = cw_ref[...]
    xc = cb_ref[...] + x * cw[CONV_W - 1:CONV_W, :]
    for j in range(1, CONV_W):
        xc = xc + sc_ref[8 - j:8 - j + tb, :] * cw[CONV_W - 1 - j:CONV_W - j, :]
    r = _sigmoid(_mm(xc, wa_ref[...], "x3") + ba_ref[...])
    ig = _sigmoid(_mm(xc, wx_ref[...], "x3") + bx_ref[...])
    nl = -lam_ref[...]
    softplus = jnp.maximum(nl, 0.0) + jnp.log1p(jnp.exp(-jnp.abs(nl)))
    log_a = -LRU_C * r * softplus
    pos = pos_ref[...]
    reset = jnp.concatenate([pos, pos, pos, pos], axis=1) == 0
    a_sc[...] = jnp.where(reset, 0.0, jnp.exp(log_a))
    th = jnp.tanh(log_a)
    b_sc[...] = jnp.where(reset, 1.0, jnp.sqrt(-2.0 * th / (1.0 - th))) * ig * xc

    row = lax.broadcasted_iota(jnp.int32, (8, MIX_W), 0)

    def group(gi, h):
        off = pl.multiple_of(gi * 8, 8)
        a = a_sc[pl.ds(off, 8), :]
        b = b_sc[pl.ds(off, 8), :]
        for d in (1, 2, 4):
            keep = row >= d
            b = jnp.where(keep, a * pltpu.roll(b, d, 0) + b, b)
            a = jnp.where(keep, a * pltpu.roll(a, d, 0), a)
        hs = a * h + b
        b_sc[pl.ds(off, 8), :] = hs
        return hs[7:8, :]

    h_ref[...] = lax.fori_loop(0, tb // 8, group, h_ref[...])
    gb = gb_ref[...]
    gelu = gb * (0.5 * (1.0 + jnp.tanh(math.sqrt(2.0 / math.pi) * (gb + 0.044715 * (gb * gb * gb)))))
    o_ref[...] = (b_sc[...] * gelu).astype(o_ref.dtype)


def _block_diag_weight(w):
    g, n, _ = w.shape
    eye = jnp.eye(g, dtype=w.dtype)
    return (eye[:, None, :, None] * w[:, :, None, :]).reshape(g * n, g * n)


def _rglru(p, pos_b, conv_w, conv_b, w_a, b_a, w_x, b_x, lam, tb=256):
    t = p.shape[0]
    nb8 = tb // 8
    vec = pl.BlockSpec((1, MIX_W), lambda i: (0, 0))
    mat = pl.BlockSpec((MIX_W, MIX_W), lambda i: (0, 0))
    cw8 = jnp.concatenate([conv_w, jnp.zeros((8 - CONV_W, MIX_W), F32)], axis=0)
    return pl.pallas_call(
        _lru_kernel,
        grid=(t // tb,),
        in_specs=[pl.BlockSpec((tb, MIX_W), lambda i: (i, OFF_CX // MIX_W)),
                  pl.BlockSpec((8, MIX_W), lambda i: (jnp.maximum(i * nb8 - 1, 0), OFF_CX // MIX_W)),
                  pl.BlockSpec((tb, MIX_W), lambda i: (i, OFF_CG // MIX_W)),
                  pl.BlockSpec((tb, 128), lambda i: (i, 0)),
                  pl.BlockSpec((8, MIX_W), lambda i: (0, 0)), vec, mat, vec, mat, vec, vec],
        out_specs=pl.BlockSpec((tb, MIX_W), lambda i: (i, 0)),
        out_shape=jax.ShapeDtypeStruct((t, MIX_W), BF16),
        scratch_shapes=[pltpu.VMEM((tb + 8, MIX_W), F32), pltpu.VMEM((tb, MIX_W), F32),
                        pltpu.VMEM((tb, MIX_W), F32), pltpu.VMEM((1, MIX_W), F32)],
        compiler_params=_cparams(("arbitrary",)),
        name="rglru",
    )(p, p, p, pos_b, cw8, conv_b.reshape(1, -1), _block_diag_weight(w_a), b_a.reshape(1, -1),
      _block_diag_weight(w_x), b_x.reshape(1, -1), lam.reshape(1, -1))


def _gla_kernel(v_ref, gt_ref, q_ref, k_ref, al_ref, aup_ref, ab_ref, ng_ref, o_ref, s_ref):
    @pl.when(pl.program_id(0) == 0)
    def _():
        s_ref[...] = jnp.zeros_like(s_ref)

    c = q_ref.shape[0]
    row = lax.broadcasted_iota(jnp.int32, (c, c), 0)
    col = lax.broadcasted_iota(jnp.int32, (c, c), 1)
    causal = row >= col
    pre = _mm(al_ref[...], aup_ref[...], "x3") + ab_ref[...]
    log_a = (jnp.minimum(pre, 0.0) - jnp.log1p(jnp.exp(-jnp.abs(pre)))) / GLA_GATE_NORM
    bcum = _mm_exact_lhs(causal.astype(F32), log_a, 3)
    blast = bcum[c - 1:c, :]
    k = k_ref[...]
    q_e = _heads(q_ref[...] * (GLA_DK ** -0.5) * jnp.exp(bcum))
    k_e = _heads(k * jnp.exp(-bcum))
    k_l = _heads(k * jnp.exp(blast - bcum))
    dec = _heads(jnp.exp(blast))
    v = _heads(v_ref[...], GLA_DV)
    s = s_ref[...]
    scores = jnp.where(causal, _mm(q_e, k_e, "x1", "nt"), 0.0)
    o = _mm(scores, v, "x1") + _mm(q_e, s, "x1", "nt")
    s_ref[...] = s * dec + _mm(v, k_l, "x1", "tn")
    on = o * lax.rsqrt(jnp.mean(o * o, axis=-1, keepdims=True) + NORM_EPS)
    on = jnp.concatenate([on[h] for h in range(GLA_HEADS)], axis=1) * ng_ref[...]
    gt = gt_ref[...]
    o_ref[...] = (on * (gt * _sigmoid(gt))).astype(o_ref.dtype)


def _gla(p, alpha_up, alpha_b, norm_g):
    t = p.shape[0]
    c = GLA_CHUNK
    hk = MIX_W // 2
    aup = jnp.concatenate([alpha_up, jnp.zeros((128 - GLA_LORA, hk), F32)], axis=0)

    def blk(width, off):
        return pl.BlockSpec((c, width), lambda i: (i, off // width))

    return pl.pallas_call(
        _gla_kernel,
        grid=(t // c,),
        in_specs=[blk(MIX_W, OFF_DV), blk(MIX_W, OFF_DGT), blk(hk, OFF_DQ), blk(hk, OFF_DK), blk(128, OFF_DAL),
                  pl.BlockSpec((128, hk), lambda i: (0, 0)), pl.BlockSpec((1, hk), lambda i: (0, 0)),
                  pl.BlockSpec((1, MIX_W), lambda i: (0, 0))],
        out_specs=pl.BlockSpec((c, MIX_W), lambda i: (i, 0)),
        out_shape=jax.ShapeDtypeStruct((t, MIX_W), BF16),
        scratch_shapes=[pltpu.VMEM((GLA_HEADS, GLA_DV, GLA_DK), F32)],
        compiler_params=_cparams(("arbitrary",)),
        name="gla",
    )(p, p, p, p, p, aup, alpha_b.reshape(1, -1), norm_g.reshape(1, -1))


def _merge_kernel(ya_ref, yb_ref, yc_ref, yd_ref, wa_ref, wb_ref, wc_ref, wd_ref,
                  ga_ref, gb_ref, gc_ref, gd_ref, o_ref):
    acc = _sigmoid(ga_ref[...]) * _dot(ya_ref[...], wa_ref[...])
    acc += _sigmoid(gb_ref[...]) * _dot(yb_ref[...], wb_ref[...])
    acc += _sigmoid(gc_ref[...]) * _dot(yc_ref[...], wc_ref[...])
    acc += _sigmoid(gd_ref[...]) * _dot(yd_ref[...], wd_ref[...])
    o_ref[...] = acc.astype(o_ref.dtype)


def _merge(ys, ws, p, tm=512, tn=512):
    t = p.shape[0]
    yspec = pl.BlockSpec((tm, MIX_W), lambda i, j: (i, 0))
    wspec = pl.BlockSpec((MIX_W, tn), lambda i, j: (0, j))

    def gspec(kk):
        return pl.BlockSpec((tm, tn), lambda i, j: (i, (OFF_G + kk * D_MODEL) // tn + j))

    return pl.pallas_call(
        _merge_kernel,
        grid=(t // tm, D_MODEL // tn),
        in_specs=[yspec] * 4 + [wspec] * 4 + [gspec(kk) for kk in range(4)],
        out_specs=pl.BlockSpec((tm, tn), lambda i, j: (i, j)),
        out_shape=jax.ShapeDtypeStruct((t, D_MODEL), BF16),
        compiler_params=_cparams(("parallel", "parallel")),
        name="merge",
    )(*ys, *ws, p, p, p, p)


def _ffn_norm_kernel(h_ref, g_ref, wr_ref, br_ref, hn_ref, rt_ref):
    x = h_ref[...]
    ms = jnp.mean(x * x, axis=-1, keepdims=True)
    hn = x * lax.rsqrt(ms + NORM_EPS) * g_ref[...]
    hn_ref[...] = hn
    z = _mm(hn, wr_ref[...], "x3") + br_ref[...]
    lane = lax.broadcasted_iota(jnp.int32, z.shape, 1)
    neg = jnp.float32(-1e30)

    def first_argmax(v):
        m = jnp.max(v, axis=-1, keepdims=True)
        return m, jnp.min(jnp.where(v == m, lane, 128), axis=-1, keepdims=True)

    is_group = lane < N_GROUPS
    zg = jnp.where(is_group, z, neg)
    mg, g_idx = first_argmax(zg)
    pg_top = 1.0 / jnp.sum(jnp.where(is_group, jnp.exp(zg - mg), 0.0), axis=-1, keepdims=True)
    lo = N_GROUPS + g_idx * EXP_PER_GROUP
    in_group = jnp.logical_and(lane >= lo, lane < lo + EXP_PER_GROUP)
    ze = jnp.where(in_group, z, neg)
    m1, i1 = first_argmax(ze)
    se = jnp.sum(jnp.where(in_group, jnp.exp(ze - m1), 0.0), axis=-1, keepdims=True)
    m2, i2 = first_argmax(jnp.where(lane == i1, neg, ze))
    p1 = 1.0 / se
    p2 = jnp.exp(m2 - m1) / se
    tot = p1 + p2
    rt_ref[...] = jnp.where(lane == 0, pg_top * (p1 / tot),
                            jnp.where(lane == 1, pg_top * (p2 / tot),
                                      jnp.where(lane == 2, (i1 - N_GROUPS).astype(F32),
                                                jnp.where(lane == 3, (i2 - N_GROUPS).astype(F32), 0.0))))


def _ffn_norm_router(h, g, w_r, b_r, tb=256):
    t, d = h.shape
    return pl.pallas_call(
        _ffn_norm_kernel,
        grid=(t // tb,),
        in_specs=[pl.BlockSpec((tb, d), lambda i: (i, 0)), pl.BlockSpec((1, d), lambda i: (0, 0)),
                  pl.BlockSpec((d, 128), lambda i: (0, 0)), pl.BlockSpec((1, 128), lambda i: (0, 0))],
        out_specs=[pl.BlockSpec((tb, d), lambda i: (i, 0)), pl.BlockSpec((tb, 128), lambda i: (i, 0))],
        out_shape=[jax.ShapeDtypeStruct((t, d), F32), jax.ShapeDtypeStruct((t, 128), F32)],
        compiler_params=_cparams(("parallel",)),
        name="ffn_norm_router",
    )(h, g.reshape(1, d), w_r, b_r)


def _row_copy(src_hbm, idx, dst, r, sem):
    return pltpu.make_async_copy(src_hbm.at[pl.ds(idx, 1), :], dst.at[pl.ds(r, 1), :], sem)


def _expert_changed(i, exp_ref):
    return jnp.logical_or(i == 0, exp_ref[i] != exp_ref[jnp.maximum(i - 1, 0)])


def _gather_rows(src_hbm, idx_ref, base, n, dst, sem):
    def start(r, carry):
        _row_copy(src_hbm, idx_ref[base + r], dst, r, sem).start()
        return carry

    lax.fori_loop(0, n, start, 0, unroll=8)


def _gather_wait(src_hbm, dst, sem):
    pltpu.make_async_copy(src_hbm.at[pl.ds(0, dst.shape[0]), :], dst, sem).wait()


def _expert_kernel(tok_ref, exp_ref, nxt_ref, nblk_ref, x_hbm, wg_hbm, wu_hbm, wd_hbm, gate_ref, o_ref,
                   xbuf, stage_g, stage_u, stage_d, wg_bf, wu_bf, wd_bf, xsem, wsem, *, layer):
    i = pl.program_id(0)
    n = nblk_ref[0]
    slot = i % 2

    def weight_copies(e):
        return (pltpu.make_async_copy(wg_hbm.at[layer, e], stage_g, wsem.at[0]),
                pltpu.make_async_copy(wu_hbm.at[layer, e], stage_u, wsem.at[1]),
                pltpu.make_async_copy(wd_hbm.at[layer, e], stage_d, wsem.at[2]))

    @pl.when(jnp.logical_and(i == 0, n > 0))
    def _():
        _gather_rows(x_hbm, tok_ref, 0, MOE_BLOCK, xbuf.at[0], xsem.at[0])
        for cp in weight_copies(exp_ref[0]):
            cp.start()

    @pl.when(i + 1 < n)
    def _():
        _gather_rows(x_hbm, tok_ref, (i + 1) * MOE_BLOCK, MOE_BLOCK, xbuf.at[1 - slot], xsem.at[1 - slot])

    @pl.when(i < n)
    def _():
        @pl.when(_expert_changed(i, exp_ref))
        def _():
            for cp in weight_copies(exp_ref[i]):
                cp.wait()
            wg_bf[...] = stage_g[...].astype(BF16)
            wu_bf[...] = stage_u[...].astype(BF16)
            wd_bf[...] = stage_d[...].astype(BF16)
            nxt = nxt_ref[i]

            @pl.when(nxt >= 0)
            def _():
                for cp in weight_copies(nxt):
                    cp.start()

        _gather_wait(x_hbm, xbuf.at[slot], xsem.at[slot])
        x = xbuf[slot].astype(BF16)
        gate = _dot(x, wg_bf[...])
        up = _dot(x, wu_bf[...])
        hmid = (gate * _sigmoid(gate) * up).astype(BF16)
        o_ref[...] = _dot(hmid, wd_bf[...]) * gate_ref[...]

    @pl.when(i >= n)
    def _():
        o_ref[...] = jnp.zeros_like(o_ref)


def _experts(hn, row_tok, block_exp, next_exp, n_used, row_gate, w_gate, w_up, w_down, layer):
    n_rows = row_tok.shape[0]
    n_blocks = n_rows // MOE_BLOCK
    d = hn.shape[1]
    hbm = pl.BlockSpec(memory_space=pl.ANY)
    return pl.pallas_call(
        functools.partial(_expert_kernel, layer=layer),
        grid_spec=pltpu.PrefetchScalarGridSpec(
            num_scalar_prefetch=4,
            grid=(n_blocks,),
            in_specs=[hbm, hbm, hbm, hbm, pl.BlockSpec((MOE_BLOCK, 1), lambda i, *_: (i, 0))],
            out_specs=pl.BlockSpec((MOE_BLOCK, d), lambda i, *_: (i, 0)),
            scratch_shapes=[pltpu.VMEM((2, MOE_BLOCK, d), F32),
                            pltpu.VMEM((d, D_EXPERT), F32), pltpu.VMEM((d, D_EXPERT), F32),
                            pltpu.VMEM((D_EXPERT, d), F32),
                            pltpu.VMEM((d, D_EXPERT), BF16), pltpu.VMEM((d, D_EXPERT), BF16),
                            pltpu.VMEM((D_EXPERT, d), BF16),
                            pltpu.SemaphoreType.DMA((2,)), pltpu.SemaphoreType.DMA((3,))],
        ),
        out_shape=jax.ShapeDtypeStruct((n_rows, d), F32),
        compiler_params=_cparams(("arbitrary",), VMEM_LIMIT_EXPERT),
        name="experts",
    )(row_tok, block_exp, next_exp, n_used, hn, w_gate, w_up, w_down, row_gate)


def _combine_kernel(pos_ref, y_hbm, h_ref, g_ref, o_ref, buf, sem, *, final_norm):
    tb = h_ref.shape[0]
    i = pl.program_id(0)
    slot = i % 2

    def fetch(blk, sl):
        for s in range(TOP_K):
            _gather_rows(y_hbm, pos_ref, (s * pl.num_programs(0) + blk) * tb, tb, buf.at[sl, s], sem.at[sl])

    @pl.when(i == 0)
    def _():
        fetch(0, 0)

    @pl.when(i + 1 < pl.num_programs(0))
    def _():
        fetch(i + 1, 1 - slot)

    for s in range(TOP_K):
        _gather_wait(y_hbm, buf.at[slot, s], sem.at[slot])
    out = h_ref[...] + (buf[slot, 0] + buf[slot, 1])
    if final_norm:
        ms = jnp.mean(out * out, axis=-1, keepdims=True)
        out = out * lax.rsqrt(ms + NORM_EPS) * g_ref[...]
    o_ref[...] = out


def _combine(h, y_rows, pos, g, final_norm, tb=128):
    t, d = h.shape
    grid_spec = pltpu.PrefetchScalarGridSpec(
        num_scalar_prefetch=1,
        grid=(t // tb,),
        in_specs=[pl.BlockSpec(memory_space=pl.ANY),
                  pl.BlockSpec((tb, d), lambda i, pos: (i, 0)),
                  pl.BlockSpec((1, d), lambda i, pos: (0, 0))],
        out_specs=pl.BlockSpec((tb, d), lambda i, pos: (i, 0)),
        scratch_shapes=[pltpu.VMEM((2, TOP_K, tb, d), F32), pltpu.SemaphoreType.DMA((2,))],
    )
    return pl.pallas_call(
        functools.partial(_combine_kernel, final_norm=final_norm),
        grid_spec=grid_spec,
        out_shape=jax.ShapeDtypeStruct((t, d), F32),
        compiler_params=_cparams(("arbitrary",)),
        name="moe_combine",
    )(pos, y_rows, h, g.reshape(1, d))


def _routing(route):
    t = route.shape[0]
    gate = route[:, 0:TOP_K].reshape(-1)
    expert_id = route[:, TOP_K:2 * TOP_K].astype(jnp.int32).reshape(-1)
    n_assign = t * TOP_K
    order = jnp.argsort(expert_id).astype(jnp.int32)
    st = order // TOP_K
    sg = gate[order]
    se = expert_id[order]
    counts = jnp.sum((expert_id[:, None] == jnp.arange(N_EXPERTS, dtype=jnp.int32)[None, :]).astype(jnp.int32), axis=0)
    starts = jnp.cumsum(counts) - counts
    padded = (counts + MOE_BLOCK - 1) // MOE_BLOCK * MOE_BLOCK
    pad_ends = jnp.cumsum(padded)
    pad_starts = pad_ends - padded
    dest = pad_starts[se] + (jnp.arange(n_assign, dtype=jnp.int32) - starts[se])
    n_blocks = (n_assign + MOE_BLOCK - 1) // MOE_BLOCK + N_EXPERTS
    n_rows = n_blocks * MOE_BLOCK
    block_start = jnp.arange(n_blocks, dtype=jnp.int32) * MOE_BLOCK
    block_exp = jnp.minimum(jnp.sum(block_start[:, None] >= pad_ends[None, :], axis=1), N_EXPERTS - 1).astype(jnp.int32)
    local = (block_start - pad_starts[block_exp])[:, None] + jnp.arange(MOE_BLOCK, dtype=jnp.int32)[None, :]
    valid = (local < counts[block_exp][:, None]).reshape(-1)
    src = jnp.clip(starts[block_exp][:, None] + local, 0, n_assign - 1).reshape(-1)
    row_tok = jnp.where(valid, st[src], 0)
    row_gate = jnp.where(valid, sg[src], 0.0)
    n_used = (pad_ends[-1] // MOE_BLOCK).astype(jnp.int32).reshape(1)
    ids = jnp.arange(N_EXPERTS, dtype=jnp.int32)
    first_used_from = lax.cummin(jnp.where(counts > 0, ids, N_EXPERTS)[::-1])[::-1]
    next_used = jnp.concatenate([first_used_from[1:], jnp.full((1,), N_EXPERTS, jnp.int32)])
    next_exp = jnp.where(next_used < N_EXPERTS, next_used, -1)[block_exp]
    pos = dest[jnp.argsort(order)].reshape(t, TOP_K).T.reshape(-1)
    return row_tok, row_gate.reshape(n_rows, 1), block_exp, next_exp, n_used, pos


def _moe_layer(h, norm_g, wg_r, bg_r, we_r, be_r, w_gate, w_up, w_down, layer, out_g, final_norm):
    d = h.shape[1]
    w_r = jnp.concatenate([wg_r, we_r, jnp.zeros((d, 128 - N_GROUPS - N_EXPERTS), F32)], axis=1)
    b_r = jnp.concatenate([bg_r, be_r, jnp.zeros((128 - N_GROUPS - N_EXPERTS,), F32)]).reshape(1, 128)
    hn, route = _ffn_norm_router(h, norm_g, w_r, b_r)
    row_tok, row_gate, block_exp, next_exp, n_used, pos = _routing(route)
    y_rows = _experts(hn, row_tok, block_exp, next_exp, n_used, row_gate, w_gate, w_up, w_down, layer)
    return _combine(h, y_rows, pos, out_g, final_norm)


def _permute_w_in(w):
    d0 = A_COLS + B_COLS + C_COLS
    g0 = d0 + D_COLS
    b0 = A_COLS
    c0 = A_COLS + B_COLS

    def z(n):
        return jnp.zeros((w.shape[0], n), w.dtype)

    parts = [w[:, b0:b0 + B_COLS], w[:, c0:c0 + C_COLS],
             w[:, d0 + 512:d0 + 1024], w[:, d0 + 1040:d0 + 1552],
             w[:, 0:3 * MIX_W], w[:, d0:d0 + 512],
             w[:, 3 * MIX_W:A_COLS], z(256 - RWKV_LORA),
             w[:, d0 + 1024:d0 + 1040], z(128 - GLA_LORA), z(OFF_G - OFF_DAL - 128),
             w[:, g0:]]
    out = jnp.concatenate(parts, axis=1).astype(BF16)
    assert out.shape[1] == NP_COLS
    return out


def kernel(x, positions, norm_mix_g, w_in, rwkv_mu, rwkv_w0, rwkv_w_up, rwkv_a0, rwkv_a_up, rwkv_g_up, rwkv_k_k, rwkv_k_a, rwkv_r_k, rwkv_ln_g, rwkv_ln_b, rwkv_w_o, ret_norm_g, ret_w_o, lru_conv_w, lru_conv_b, lru_w_a, lru_b_a, lru_w_x, lru_b_x, lru_lambda, lru_w_o, gla_alpha_up, gla_alpha_b, gla_norm_g, gla_w_o, w_out, norm_ffn_g, router_group_w, router_group_b, router_expert_w, router_expert_b, moe_w_gate, moe_w_up, moe_w_down, final_norm_g):
    b_, s_, d = x.shape
    assert b_ == 1 and d == D_MODEL
    depth = w_in.shape[0]
    h = x.reshape(s_, d)
    pos_b = jnp.broadcast_to(positions.reshape(s_, 1), (s_, 128)).astype(jnp.int32)
    cos_t, sin_t = _rope_table(pos_b)
    for l in range(depth):
        xn = _rmsnorm(h, norm_mix_g[l], BF16)
        p = _matmul(xn, _permute_w_in(w_in[l]), 2048, 512, name="in_proj")
        y_a = _rwkv_branch(p, rwkv_mu[l], rwkv_w0[l], rwkv_w_up[l], rwkv_a0[l], rwkv_a_up[l], rwkv_g_up[l],
                           rwkv_k_k[l], rwkv_k_a[l], rwkv_r_k[l], rwkv_ln_g[l], rwkv_ln_b[l])
        y_b = _retention(p, cos_t, sin_t, ret_norm_g[l])
        y_c = _rglru(p, pos_b, lru_conv_w[l], lru_conv_b[l], lru_w_a[l], lru_b_a[l], lru_w_x[l], lru_b_x[l],
                     lru_lambda[l])
        y_d = _gla(p, gla_alpha_up[l], gla_alpha_b[l], gla_norm_g[l])
        merged = _merge((y_a, y_b, y_c, y_d),
                        tuple(w[l].astype(BF16) for w in (rwkv_w_o, ret_w_o, lru_w_o, gla_w_o)), p)
        h = _matmul_residual(merged, w_out[l].astype(BF16), h, 1024, 512)
        last = l == depth - 1
        h = _moe_layer(h, norm_ffn_g[l], router_group_w[l], router_group_b[l], router_expert_w[l],
                       router_expert_b[l], moe_w_gate, moe_w_up, moe_w_down, l,
                       final_norm_g if last else norm_ffn_g[l], last)
    return h.reshape(b_, s_, d)
```

```python
import functools
import math

import jax
import jax.numpy as jnp
from jax import lax
from jax.experimental import pallas as pl
from jax.experimental.pallas import tpu as pltpu

F32 = jnp.float32
BF16 = jnp.bfloat16
HI = lax.Precision.HIGHEST

D_MODEL = 2048
MIX_W = 512
NORM_EPS = 1e-6
HEAD_DIM = 64
N_HEADS = MIX_W // HEAD_DIM

RWKV_W_LORA, RWKV_A_LORA, RWKV_G_LORA = 32, 32, 96
RWKV_LORA = RWKV_W_LORA + RWKV_A_LORA + RWKV_G_LORA
RWKV_DECAY_SCALE = 0.6065306597126334
RWKV_LN_EPS = 64e-5
RWKV_CHUNK = 64

RET_CHUNK = 128
ROPE_BASE = 10000.0

LRU_BLOCKS = 8
CONV_W = 4
LRU_C = 8.0

GLA_HEADS = 4
GLA_DK = 64
GLA_DV = 128
GLA_LORA = 16
GLA_GATE_NORM = 16.0
GLA_CHUNK = 64

N_GROUPS = 4
EXP_PER_GROUP = 8
N_EXPERTS = N_GROUPS * EXP_PER_GROUP
TOP_K = 2
D_EXPERT = 1024
MOE_BLOCK = 128
WEIGHT_CHUNKS = 4

A_COLS = 3 * MIX_W + RWKV_LORA
B_COLS = 4 * MIX_W
C_COLS = 2 * MIX_W
D_COLS = 2 * (MIX_W // 2) + MIX_W + GLA_LORA + MIX_W

OFF_BQ, OFF_BK, OFF_BV, OFF_BG = 0, 512, 1024, 1536
OFF_CX, OFF_CG = 2048, 2560
OFF_DV, OFF_DGT = 3072, 3584
OFF_AR, OFF_AK, OFF_AV = 4096, 4608, 5120
OFF_DQ, OFF_DK = 5632, 5888
OFF_AL = 6144
OFF_DAL = 6400
OFF_G = 6656
NP_COLS = OFF_G + 4 * D_MODEL

VMEM_LIMIT = 48 * 1024 * 1024
VMEM_LIMIT_EXPERT = 56 * 1024 * 1024


def _cparams(sem, vmem=VMEM_LIMIT):
    return pltpu.CompilerParams(dimension_semantics=sem, vmem_limit_bytes=vmem)


def _dot(a, b, prec=None):
    return jnp.dot(a, b, precision=prec, preferred_element_type=F32)


def _sigmoid(x):
    return 1.0 / (1.0 + jnp.exp(-x))


def _split(x):
    hi = x.astype(BF16)
    return hi, (x - hi.astype(F32)).astype(BF16)


def _dims(form, batched):
    ca, cb = {"nn": (1, 0), "nt": (1, 1), "tn": (0, 0)}[form]
    if batched:
        return (((ca + 1,), (cb + 1,)), ((0,), (0,)))
    return (((ca,), (cb,)), ((), ()))


def _mm(a, b, mode, form="nn"):
    dims = _dims(form, a.ndim == 3)
    if mode == "hi":
        return lax.dot_general(a, b, dims, precision=HI, preferred_element_type=F32)
    d = functools.partial(lax.dot_general, dimension_numbers=dims, preferred_element_type=F32)
    if mode == "x1":
        return d(a.astype(BF16), b.astype(BF16))
    ah, al = _split(a)
    bh, bl = _split(b)
    return d(ah, bh) + (d(ah, bl) + d(al, bh))


def _mm_exact_rhs(a, b, terms):
    b = b.astype(BF16)
    acc = None
    for _ in range(terms):
        piece = a.astype(BF16)
        part = _dot(piece, b)
        acc = part if acc is None else acc + part
        a = a - piece.astype(F32)
    return acc


def _mm_exact_lhs(a, b, terms):
    a = a.astype(BF16)
    acc = None
    for _ in range(terms):
        piece = b.astype(BF16)
        part = _dot(a, piece)
        acc = part if acc is None else acc + part
        b = b - piece.astype(F32)
    return acc


def _heads(x, width=HEAD_DIM):
    return jnp.stack([x[:, h * width:(h + 1) * width] for h in range(x.shape[1] // width)])


def _block_diag_const(n, blk, value):
    r = lax.broadcasted_iota(jnp.int32, (n, n), 0) // blk
    c = lax.broadcasted_iota(jnp.int32, (n, n), 1) // blk
    return jnp.where(r == c, value, 0.0).astype(F32)


def _rmsnorm_kernel(x_ref, g_ref, o_ref):
    x = x_ref[...]
    ms = jnp.mean(x * x, axis=-1, keepdims=True)
    o_ref[...] = (x * lax.rsqrt(ms + NORM_EPS) * g_ref[...]).astype(o_ref.dtype)


def _rmsnorm(x, g, out_dtype, tb=512):
    t, d = x.shape
    return pl.pallas_call(
        _rmsnorm_kernel,
        grid=(t // tb,),
        in_specs=[pl.BlockSpec((tb, d), lambda i: (i, 0)), pl.BlockSpec((1, d), lambda i: (0, 0))],
        out_specs=pl.BlockSpec((tb, d), lambda i: (i, 0)),
        out_shape=jax.ShapeDtypeStruct((t, d), out_dtype),
        compiler_params=_cparams(("parallel",)),
        name="rmsnorm",
    )(x, g.reshape(1, d))


def _mm_kernel(a_ref, b_ref, o_ref):
    o_ref[...] = _dot(a_ref[...], b_ref[...]).astype(o_ref.dtype)


def _matmul(a, b, tm, tn, out_dtype=F32, name="matmul"):
    m, k = a.shape
    n = b.shape[1]
    return pl.pallas_call(
        _mm_kernel,
        grid=(m // tm, n // tn),
        in_specs=[pl.BlockSpec((tm, k), lambda i, j: (i, 0)), pl.BlockSpec((k, tn), lambda i, j: (0, j))],
        out_specs=pl.BlockSpec((tm, tn), lambda i, j: (i, j)),
        out_shape=jax.ShapeDtypeStruct((m, n), out_dtype),
        compiler_params=_cparams(("parallel", "parallel")),
        name=name,
    )(a, b)


def _mm_res_kernel(a_ref, b_ref, r_ref, o_ref):
    o_ref[...] = r_ref[...] + _dot(a_ref[...], b_ref[...].astype(BF16))


def _matmul_residual(a, b, layer, res, tm, tn):
    m, k = a.shape
    n = b.shape[2]
    return pl.pallas_call(
        _mm_res_kernel,
        grid=(m // tm, n // tn),
        in_specs=[pl.BlockSpec((tm, k), lambda i, j: (i, 0)), pl.BlockSpec((None, k, tn), lambda i, j: (layer, 0, j)),
                  pl.BlockSpec((tm, tn), lambda i, j: (i, j))],
        out_specs=pl.BlockSpec((tm, tn), lambda i, j: (i, j)),
        out_shape=jax.ShapeDtypeStruct((m, n), F32),
        compiler_params=_cparams(("parallel", "parallel")),
        name="wout_residual",
    )(a, b, res)


def _shifted(x, tail_ref, sc_ref, width, first):
    tb = x.shape[0]
    sc_ref[0:8, 0:width] = jnp.where(first, 0.0, tail_ref[...])
    sc_ref[8:8 + tb, 0:width] = x
    return sc_ref[7:7 + tb, 0:width]


def _rwkv_prep_kernel(r_ref, k_ref, v_ref, l_ref, rt_ref, kt_ref, vt_ref, lt_ref,
                      mur_ref, muk_ref, muv_ref, mul_ref, wl_ref, b0_ref, kk_ref, ka_ref,
                      ro_ref, lw_ref, ko_ref, vo_ref, kkn_ref, kka_ref, g_ref, sc_ref):
    first = pl.program_id(0) == 0

    def mix(x_ref, t_ref, mu_ref, width):
        x = x_ref[...]
        prev = _shifted(x, t_ref, sc_ref, width, first)
        return x + (prev - x) * mu_ref[...]

    r = mix(r_ref, rt_ref, mur_ref, MIX_W)
    k = mix(k_ref, kt_ref, muk_ref, MIX_W)
    v = mix(v_ref, vt_ref, muv_ref, MIX_W)
    zl = mix(l_ref, lt_ref, mul_ref, 256)
    lane = lax.broadcasted_iota(jnp.int32, zl.shape, 1)
    act = jnp.where(lane < RWKV_W_LORA, jnp.tanh(zl),
                    jnp.where(lane < RWKV_W_LORA + RWKV_A_LORA, zl, _sigmoid(zl)))
    lo = _mm(act, wl_ref[...], "x3") + b0_ref[...]
    lw = -RWKV_DECAY_SCALE * _sigmoid(lo[:, 0:MIX_W])
    a = _sigmoid(lo[:, MIX_W:2 * MIX_W])
    g = lo[:, 2 * MIX_W:3 * MIX_W]
    kk = k * kk_ref[...]
    ss = _mm_exact_rhs(kk * kk, _block_diag_const(MIX_W, HEAD_DIM, 1.0), 2)
    kkn = kk / jnp.maximum(jnp.sqrt(ss), 1e-12)
    ro_ref[...] = r
    lw_ref[...] = lw
    ko_ref[...] = k * (1.0 + (a - 1.0) * ka_ref[...])
    vo_ref[...] = v
    kkn_ref[...] = kkn
    kka_ref[...] = kkn * a
    g_ref[...] = g


def _rwkv_prep(p, mu, w0, w_up, a0, a_up, g_up, k_k, k_a, tb=256):
    t = p.shape[0]
    nb8 = tb // 8

    def blk(width, off):
        return pl.BlockSpec((tb, width), lambda i: (i, off // width))

    def tail(width, off):
        return pl.BlockSpec((8, width), lambda i: (jnp.maximum(i * nb8 - 1, 0), off // width))

    def vec(width):
        return pl.BlockSpec((1, width), lambda i: (0, 0))

    w_lora = jnp.zeros((256, 3 * MIX_W), F32)
    w_lora = w_lora.at[0:32, 0:MIX_W].set(w_up)
    w_lora = w_lora.at[32:64, MIX_W:2 * MIX_W].set(a_up)
    w_lora = w_lora.at[64:160, 2 * MIX_W:].set(g_up)
    b0 = jnp.concatenate([w0, a0, jnp.zeros((MIX_W,), F32)]).reshape(1, 3 * MIX_W)
    mu_l = jnp.concatenate([mu[3 * MIX_W:], jnp.zeros((256 - RWKV_LORA,), F32)]).reshape(1, 256)
    out = jax.ShapeDtypeStruct((t, MIX_W), F32)
    return pl.pallas_call(
        _rwkv_prep_kernel,
        grid=(t // tb,),
        in_specs=[blk(MIX_W, OFF_AR), blk(MIX_W, OFF_AK), blk(MIX_W, OFF_AV), blk(256, OFF_AL),
                  tail(MIX_W, OFF_AR), tail(MIX_W, OFF_AK), tail(MIX_W, OFF_AV), tail(256, OFF_AL),
                  vec(MIX_W), vec(MIX_W), vec(MIX_W), vec(256),
                  pl.BlockSpec((256, 3 * MIX_W), lambda i: (0, 0)), vec(3 * MIX_W), vec(MIX_W), vec(MIX_W)],
        out_specs=[pl.BlockSpec((tb, MIX_W), lambda i: (i, 0))] * 7,
        out_shape=[out] * 7,
        scratch_shapes=[pltpu.VMEM((tb + 8, MIX_W), F32)],
        compiler_params=_cparams(("parallel",)),
        name="rwkv_prep",
    )(p, p, p, p, p, p, p, p,
      mu[0:MIX_W].reshape(1, -1), mu[MIX_W:2 * MIX_W].reshape(1, -1), mu[2 * MIX_W:3 * MIX_W].reshape(1, -1), mu_l,
      w_lora, b0, k_k.reshape(1, -1), k_a.reshape(1, -1))


RW_SC, RW_INV, RW_APPLY, RW_STATE, RW_SCAN = "x1", "x1", "x1", "x1", "x3"


def _rwkv_local_kernel(r_ref, lw_ref, k_ref, v_ref, kk_ref, kka_ref, q_ref, y0_ref, m_ref, n_ref):
    c = r_ref.shape[0]
    row = lax.broadcasted_iota(jnp.int32, (c, c), 0)
    col = lax.broadcasted_iota(jnp.int32, (c, c), 1)
    incl = row >= col
    strict = row > col
    eye = jnp.where(row == col, 1.0, 0.0)
    lw = lw_ref[...]
    k = k_ref[...]
    kka = kka_ref[...]
    cum = _mm_exact_lhs(incl.astype(F32), lw, 3)
    last = cum[c - 1:c, :]
    pinv = jnp.exp(-cum)
    dl = jnp.exp(last - cum)
    at = _heads(-kk_ref[...] * jnp.exp(cum - lw))
    rt = _heads(r_ref[...] * jnp.exp(cum))
    bt = _heads(kka * pinv)
    kt = _heads(k * pinv)
    bl = _heads(kka * dl)
    kl = _heads(k * dl)
    v = _heads(v_ref[...])
    pc = _heads(jnp.exp(last))
    sc = _mm(jnp.concatenate([at, rt], axis=1), jnp.concatenate([bt, kt], axis=1), RW_SC, "nt")
    a_ab = jnp.where(strict, sc[:, 0:c, 0:c], 0.0)
    a_ak = jnp.where(strict, sc[:, 0:c, c:2 * c], 0.0)
    r_b = jnp.where(incl, sc[:, c:2 * c, 0:c], 0.0)
    r_k = jnp.where(incl, sc[:, c:2 * c, c:2 * c], 0.0)
    x = eye + a_ab
    pw = a_ab
    for _ in range(int(math.log2(c)) - 1):
        pw = _mm(pw, pw, RW_INV)
        x = x + _mm(x, pw, RW_INV)
    wu = _mm(x, jnp.concatenate([at, _mm(a_ak, v, RW_APPLY)], axis=2), RW_APPLY)
    lower = jnp.concatenate([jnp.zeros_like(v), v], axis=2)
    qy = _mm(jnp.concatenate([r_b, r_k], axis=2), jnp.concatenate([wu, lower], axis=1), RW_APPLY)
    wb = _mm(wu, bl, RW_STATE, "tn")
    n = wb[:, HEAD_DIM:] + _mm(v, kl, RW_STATE, "tn")
    m = eye * pc + wb[:, 0:HEAD_DIM]
    qm = rt + qy[:, :, 0:HEAD_DIM]
    y0 = qy[:, :, HEAD_DIM:]
    for h in range(N_HEADS):
        sl = slice(h * HEAD_DIM, (h + 1) * HEAD_DIM)
        q_ref[:, sl] = qm[h]
        y0_ref[:, sl] = y0[h]
        m_ref[:, sl] = m[h]
        n_ref[:, sl] = n[h]


def _rwkv_scan_kernel(q_ref, y0_ref, m_ref, n_ref, r_ref, k_ref, v_ref, g_ref, lng_ref, lnb_ref, rk_ref,
                      o_ref, s_ref, y_sc):
    @pl.when(pl.program_id(0) == 0)
    def _():
        s_ref[...] = jnp.zeros_like(s_ref)

    c = RWKV_CHUNK
    s = s_ref[...]
    for j in range(q_ref.shape[0] // c):
        rows = slice(j * c, (j + 1) * c)
        y = _mm(_heads(q_ref[rows, :]), s, RW_SCAN, "nt") + _heads(y0_ref[rows, :])
        s = _mm(s, _heads(m_ref[rows, :]), RW_SCAN) + _heads(n_ref[rows, :])
        for h in range(N_HEADS):
            y_sc[rows, h * HEAD_DIM:(h + 1) * HEAD_DIM] = y[h]
    s_ref[...] = s
    y = y_sc[...]
    avg = _block_diag_const(MIX_W, HEAD_DIM, 1.0 / HEAD_DIM)
    mean = _mm_exact_rhs(y, avg, 2)
    yc = y - mean
    var = _mm_exact_rhs(yc * yc, avg, 2)
    yn = yc * lax.rsqrt(var + RWKV_LN_EPS) * lng_ref[...] + lnb_ref[...]
    v = v_ref[...]
    bonus = _mm_exact_rhs(r_ref[...] * k_ref[...] * rk_ref[...], _block_diag_const(MIX_W, HEAD_DIM, 1.0), 2) * v
    o_ref[...] = ((yn + bonus) * g_ref[...]).astype(o_ref.dtype)


def _rwkv_recurrence(r, lw, k, v, kkn, kka, g, ln_g, ln_b, r_k, tb=256):
    t = r.shape[0]
    c = RWKV_CHUNK
    cblk = pl.BlockSpec((c, MIX_W), lambda i: (i, 0))
    f = jax.ShapeDtypeStruct((t, MIX_W), F32)
    qm, y0, m, n = pl.pallas_call(
        _rwkv_local_kernel,
        grid=(t // c,),
        in_specs=[cblk] * 6,
        out_specs=[cblk] * 4,
        out_shape=[f] * 4,
        compiler_params=_cparams(("parallel",)),
        name="rwkv_local",
    )(r, lw, k, v, kkn, kka)
    blk = pl.BlockSpec((tb, MIX_W), lambda i: (i, 0))
    vec = pl.BlockSpec((1, MIX_W), lambda i: (0, 0))
    return pl.pallas_call(
        _rwkv_scan_kernel,
        grid=(t // tb,),
        in_specs=[blk] * 8 + [vec] * 3,
        out_specs=blk,
        out_shape=jax.ShapeDtypeStruct((t, MIX_W), BF16),
        scratch_shapes=[pltpu.VMEM((N_HEADS, HEAD_DIM, HEAD_DIM), F32), pltpu.VMEM((tb, MIX_W), F32)],
        compiler_params=_cparams(("arbitrary",)),
        name="rwkv_scan",
    )(qm, y0, m, n, r, k, v, g, ln_g.reshape(1, -1), ln_b.reshape(1, -1), r_k.reshape(1, -1))


def _rwkv_branch(p, mu, w0, w_up, a0, a_up, g_up, k_k, k_a, r_k, ln_g, ln_b):
    prep = _rwkv_prep(p, mu, w0, w_up, a0, a_up, g_up, k_k, k_a)
    return _rwkv_recurrence(*prep, ln_g, ln_b, r_k.reshape(-1))


def _rope_kernel(pos_ref, inv_ref, cos_ref, sin_ref):
    ang = pos_ref[...].astype(F32) * inv_ref[...]
    lane = lax.broadcasted_iota(jnp.int32, ang.shape, 1)
    cos_ref[...] = jnp.cos(ang)
    sin_ref[...] = jnp.where(lane % HEAD_DIM < HEAD_DIM // 2, -jnp.sin(ang), jnp.sin(ang))


def _rope_table(pos_b, tb=512):
    t = pos_b.shape[0]
    d = HEAD_DIM
    inv = 1.0 / (ROPE_BASE ** (jnp.arange(0, d, 2, dtype=F32) / d))
    inv_b = jnp.tile(inv, 4).reshape(1, 128)
    blk = pl.BlockSpec((tb, 128), lambda i: (i, 0))
    return pl.pallas_call(
        _rope_kernel,
        grid=(t // tb,),
        in_specs=[blk, pl.BlockSpec((1, 128), lambda i: (0, 0))],
        out_specs=[blk, blk],
        out_shape=[jax.ShapeDtypeStruct((t, 128), F32)] * 2,
        compiler_params=_cparams(("parallel",)),
        name="rope_table",
    )(pos_b, inv_b)


def _ret_kernel(q_ref, k_ref, v_ref, g_ref, cos_ref, sin_ref, dec_ref, zeta_ref, xi_ref, cd_ref, ng_ref,
                o_ref, s_ref, o_sc):
    @pl.when(pl.program_id(0) == 0)
    def _():
        s_ref[...] = jnp.zeros_like(s_ref)

    cos = jnp.concatenate([cos_ref[...]] * 4, axis=1)
    sin = jnp.concatenate([sin_ref[...]] * 4, axis=1)
    lane = lax.broadcasted_iota(jnp.int32, cos.shape, 1)
    lower_half = lane % HEAD_DIM < HEAD_DIM // 2

    def rope(x):
        swapped = jnp.where(lower_half, pltpu.roll(x, MIX_W - HEAD_DIM // 2, 1), pltpu.roll(x, HEAD_DIM // 2, 1))
        return x * cos + swapped * sin

    q = _heads(rope(q_ref[...]))
    k = _heads(rope(k_ref[...]) * (HEAD_DIM ** -0.5))
    v = _heads(v_ref[...])
    s = s_ref[...]
    scores = _mm(q, k, "x1", "nt") * dec_ref[...]
    o = _mm(scores, v, "x1") + _mm(q * xi_ref[...], s, "x1")
    s_ref[...] = s * cd_ref[...] + _mm(k * zeta_ref[...], v, "x1", "tn")
    on = o * lax.rsqrt(jnp.mean(o * o, axis=-1, keepdims=True) + NORM_EPS)
    for h in range(N_HEADS):
        o_sc[:, h * HEAD_DIM:(h + 1) * HEAD_DIM] = on[h]
    gt = g_ref[...]
    o_ref[...] = (o_sc[...] * ng_ref[...] * (gt * _sigmoid(gt))).astype(o_ref.dtype)


def _retention(p, cos_t, sin_t, norm_g):
    t = p.shape[0]
    c = RET_CHUNK
    h = N_HEADS
    log_g = jnp.log(1.0 - jnp.exp(jnp.linspace(math.log(1.0 / 32), math.log(1.0 / 512), h, dtype=F32)))
    idx = jnp.arange(c, dtype=F32)
    diff = idx[:, None] - idx[None, :]
    decay = jnp.where(diff >= 0, jnp.exp(log_g[:, None, None] * jnp.maximum(diff, 0.0)[None]), 0.0)
    zeta = jnp.exp(log_g[:, None] * (c - 1 - idx)[None])
    xi = jnp.exp(log_g[:, None] * (idx + 1)[None])
    cdec = jnp.exp(log_g * c)
    zeta_b = jnp.broadcast_to(zeta[:, :, None], (h, c, HEAD_DIM))
    xi_b = jnp.broadcast_to(xi[:, :, None], (h, c, HEAD_DIM))
    cd_b = jnp.broadcast_to(cdec[:, None, None], (h, HEAD_DIM, HEAD_DIM))

    def blk(off):
        return pl.BlockSpec((c, MIX_W), lambda i: (i, off // MIX_W))

    tab = pl.BlockSpec((c, 128), lambda i: (i, 0))

    def const(shape):
        return pl.BlockSpec(shape, lambda i: (0,) * len(shape))

    return pl.pallas_call(
        _ret_kernel,
        grid=(t // c,),
        in_specs=[blk(OFF_BQ), blk(OFF_BK), blk(OFF_BV), blk(OFF_BG), tab, tab,
                  const((h, c, c)), const((h, c, HEAD_DIM)), const((h, c, HEAD_DIM)),
                  const((h, HEAD_DIM, HEAD_DIM)), const((1, MIX_W))],
        out_specs=pl.BlockSpec((c, MIX_W), lambda i: (i, 0)),
        out_shape=jax.ShapeDtypeStruct((t, MIX_W), BF16),
        scratch_shapes=[pltpu.VMEM((h, HEAD_DIM, HEAD_DIM), F32), pltpu.VMEM((c, MIX_W), F32)],
        compiler_params=_cparams(("arbitrary",)),
        name="retention",
    )(p, p, p, p, cos_t, sin_t, decay, zeta_b, xi_b, cd_b, norm_g.reshape(1, -1))


def _lru_kernel(x_ref, xt_ref, gb_ref, pos_ref, cw_ref, cb_ref, wa_ref, ba_ref, wx_ref, bx_ref, lam_ref,
                o_ref, sc_ref, a_sc, b_sc, h_ref):
    tb = x_ref.shape[0]
    first = pl.program_id(0) == 0

    @pl.when(first)
    def _():
        h_ref[...] = jnp.zeros_like(h_ref)

    x = x_ref[...]
    sc_ref[0:8, :] = jnp.where(first, 0.0, xt_ref[...])
    sc_ref[8:8 + tb, :] = x
    cw = cw_ref[...]
    xc = cb_ref[...] + x * cw[CONV_W - 1:CONV_W, :]
    for j in range(1, CONV_W):
        xc = xc + sc_ref[8 - j:8 - j + tb, :] * cw[CONV_W - 1 - j:CONV_W - j, :]
    r = _sigmoid(_mm(xc, wa_ref[...], "x3") + ba_ref[...])
    ig = _sigmoid(_mm(xc, wx_ref[...], "x3") + bx_ref[...])
    nl = -lam_ref[...]
    softplus = jnp.maximum(nl, 0.0) + jnp.log1p(jnp.exp(-jnp.abs(nl)))
    log_a = -LRU_C * r * softplus
    pos = pos_ref[...]
    reset = jnp.concatenate([pos, pos, pos, pos], axis=1) == 0
    a_sc[...] = jnp.where(reset, 0.0, jnp.exp(log_a))
    th = jnp.tanh(log_a)
    b_sc[...] = jnp.where(reset, 1.0, jnp.sqrt(-2.0 * th / (1.0 - th))) * ig * xc

    row = lax.broadcasted_iota(jnp.int32, (8, MIX_W), 0)

    def group(gi, h):
        off = pl.multiple_of(gi * 8, 8)
        a = a_sc[pl.ds(off, 8), :]
        b = b_sc[pl.ds(off, 8), :]
        for d in (1, 2, 4):
            keep = row >= d
            b = jnp.where(keep, a * pltpu.roll(b, d, 0) + b, b)
            a = jnp.where(keep, a * pltpu.roll(a, d, 0), a)
        hs = a * h + b
        b_sc[pl.ds(off, 8), :] = hs
        return hs[7:8, :]

    h_ref[...] = lax.fori_loop(0, tb // 8, group, h_ref[...])
    gb = gb_ref[...]
    gelu = gb * (0.5 * (1.0 + jnp.tanh(math.sqrt(2.0 / math.pi) * (gb + 0.044715 * (gb * gb * gb)))))
    o_ref[...] = (b_sc[...] * gelu).astype(o_ref.dtype)


def _block_diag_weight(w):
    g, n, _ = w.shape
    eye = jnp.eye(g, dtype=w.dtype)
    return (eye[:, None, :, None] * w[:, :, None, :]).reshape(g * n, g * n)


def _rglru(p, pos_b, conv_w, conv_b, w_a, b_a, w_x, b_x, lam, tb=256):
    t = p.shape[0]
    nb8 = tb // 8
    vec = pl.BlockSpec((1, MIX_W), lambda i: (0, 0))
    mat = pl.BlockSpec((MIX_W, MIX_W), lambda i: (0, 0))
    cw8 = jnp.concatenate([conv_w, jnp.zeros((8 - CONV_W, MIX_W), F32)], axis=0)
    return pl.pallas_call(
        _lru_kernel,
        grid=(t // tb,),
        in_specs=[pl.BlockSpec((tb, MIX_W), lambda i: (i, OFF_CX // MIX_W)),
                  pl.BlockSpec((8, MIX_W), lambda i: (jnp.maximum(i * nb8 - 1, 0), OFF_CX // MIX_W)),
                  pl.BlockSpec((tb, MIX_W), lambda i: (i, OFF_CG // MIX_W)),
                  pl.BlockSpec((tb, 128), lambda i: (i, 0)),
                  pl.BlockSpec((8, MIX_W), lambda i: (0, 0)), vec, mat, vec, mat, vec, vec],
        out_specs=pl.BlockSpec((tb, MIX_W), lambda i: (i, 0)),
        out_shape=jax.ShapeDtypeStruct((t, MIX_W), BF16),
        scratch_shapes=[pltpu.VMEM((tb + 8, MIX_W), F32), pltpu.VMEM((tb, MIX_W), F32),
                        pltpu.VMEM((tb, MIX_W), F32), pltpu.VMEM((1, MIX_W), F32)],
        compiler_params=_cparams(("arbitrary",)),
        name="rglru",
    )(p, p, p, pos_b, cw8, conv_b.reshape(1, -1), _block_diag_weight(w_a), b_a.reshape(1, -1),
      _block_diag_weight(w_x), b_x.reshape(1, -1), lam.reshape(1, -1))


def _gla_kernel(v_ref, gt_ref, q_ref, k_ref, al_ref, aup_ref, ab_ref, ng_ref, o_ref, s_ref):
    @pl.when(pl.program_id(0) == 0)
    def _():
        s_ref[...] = jnp.zeros_like(s_ref)

    c = q_ref.shape[0]
    row = lax.broadcasted_iota(jnp.int32, (c, c), 0)
    col = lax.broadcasted_iota(jnp.int32, (c, c), 1)
    causal = row >= col
    pre = _mm(al_ref[...], aup_ref[...], "x3") + ab_ref[...]
    log_a = (jnp.minimum(pre, 0.0) - jnp.log1p(jnp.exp(-jnp.abs(pre)))) / GLA_GATE_NORM
    bcum = _mm_exact_lhs(causal.astype(F32), log_a, 3)
    blast = bcum[c - 1:c, :]
    k = k_ref[...]
    q_e = _heads(q_ref[...] * (GLA_DK ** -0.5) * jnp.exp(bcum))
    k_e = _heads(k * jnp.exp(-bcum))
    k_l = _heads(k * jnp.exp(blast - bcum))
    dec = _heads(jnp.exp(blast))
    v = _heads(v_ref[...], GLA_DV)
    s = s_ref[...]
    scores = jnp.where(causal, _mm(q_e, k_e, "x1", "nt"), 0.0)
    o = _mm(scores, v, "x1") + _mm(q_e, s, "x1", "nt")
    s_ref[...] = s * dec + _mm(v, k_l, "x1", "tn")
    on = o * lax.rsqrt(jnp.mean(o * o, axis=-1, keepdims=True) + NORM_EPS)
    on = jnp.concatenate([on[h] for h in range(GLA_HEADS)], axis=1) * ng_ref[...]
    gt = gt_ref[...]
    o_ref[...] = (on * (gt * _sigmoid(gt))).astype(o_ref.dtype)


def _gla(p, alpha_up, alpha_b, norm_g):
    t = p.shape[0]
    c = GLA_CHUNK
    hk = MIX_W // 2
    aup = jnp.concatenate([alpha_up, jnp.zeros((128 - GLA_LORA, hk), F32)], axis=0)

    def blk(width, off):
        return pl.BlockSpec((c, width), lambda i: (i, off // width))

    return pl.pallas_call(
        _gla_kernel,
        grid=(t // c,),
        in_specs=[blk(MIX_W, OFF_DV), blk(MIX_W, OFF_DGT), blk(hk, OFF_DQ), blk(hk, OFF_DK), blk(128, OFF_DAL),
                  pl.BlockSpec((128, hk), lambda i: (0, 0)), pl.BlockSpec((1, hk), lambda i: (0, 0)),
                  pl.BlockSpec((1, MIX_W), lambda i: (0, 0))],
        out_specs=pl.BlockSpec((c, MIX_W), lambda i: (i, 0)),
        out_shape=jax.ShapeDtypeStruct((t, MIX_W), BF16),
        scratch_shapes=[pltpu.VMEM((GLA_HEADS, GLA_DV, GLA_DK), F32)],
        compiler_params=_cparams(("arbitrary",)),
        name="gla",
    )(p, p, p, p, p, aup, alpha_b.reshape(1, -1), norm_g.reshape(1, -1))


def _merge_kernel(ya_ref, yb_ref, yc_ref, yd_ref, wa_ref, wb_ref, wc_ref, wd_ref,
                  ga_ref, gb_ref, gc_ref, gd_ref, o_ref):
    acc = _sigmoid(ga_ref[...]) * _dot(ya_ref[...], wa_ref[0].astype(BF16))
    acc += _sigmoid(gb_ref[...]) * _dot(yb_ref[...], wb_ref[0].astype(BF16))
    acc += _sigmoid(gc_ref[...]) * _dot(yc_ref[...], wc_ref[0].astype(BF16))
    acc += _sigmoid(gd_ref[...]) * _dot(yd_ref[...], wd_ref[0].astype(BF16))
    o_ref[...] = acc.astype(o_ref.dtype)


def _merge(ys, ws, layer, p, tm=512, tn=512):
    t = p.shape[0]
    yspec = pl.BlockSpec((tm, MIX_W), lambda i, j: (i, 0))
    wspec = pl.BlockSpec((1, MIX_W, tn), lambda i, j: (layer, 0, j))

    def gspec(kk):
        return pl.BlockSpec((tm, tn), lambda i, j: (i, (OFF_G + kk * D_MODEL) // tn + j))

    return pl.pallas_call(
        _merge_kernel,
        grid=(t // tm, D_MODEL // tn),
        in_specs=[yspec] * 4 + [wspec] * 4 + [gspec(kk) for kk in range(4)],
        out_specs=pl.BlockSpec((tm, tn), lambda i, j: (i, j)),
        out_shape=jax.ShapeDtypeStruct((t, D_MODEL), BF16),
        compiler_params=_cparams(("parallel", "parallel")),
        name="merge",
    )(*ys, *ws, p, p, p, p)


def _ffn_norm_kernel(h_ref, g_ref, wr_ref, br_ref, hn_ref, rt_ref):
    x = h_ref[...]
    ms = jnp.mean(x * x, axis=-1, keepdims=True)
    hn = x * lax.rsqrt(ms + NORM_EPS) * g_ref[...]
    hn_ref[...] = hn
    z = _mm(hn, wr_ref[...], "x3") + br_ref[...]
    lane = lax.broadcasted_iota(jnp.int32, z.shape, 1)
    neg = jnp.float32(-1e30)

    def first_argmax(v):
        m = jnp.max(v, axis=-1, keepdims=True)
        return m, jnp.min(jnp.where(v == m, lane, 128), axis=-1, keepdims=True)

    is_group = lane < N_GROUPS
    zg = jnp.where(is_group, z, neg)
    mg, g_idx = first_argmax(zg)
    pg_top = 1.0 / jnp.sum(jnp.where(is_group, jnp.exp(zg - mg), 0.0), axis=-1, keepdims=True)
    lo = N_GROUPS + g_idx * EXP_PER_GROUP
    in_group = jnp.logical_and(lane >= lo, lane < lo + EXP_PER_GROUP)
    ze = jnp.where(in_group, z, neg)
    m1, i1 = first_argmax(ze)
    se = jnp.sum(jnp.where(in_group, jnp.exp(ze - m1), 0.0), axis=-1, keepdims=True)
    m2, i2 = first_argmax(jnp.where(lane == i1, neg, ze))
    p1 = 1.0 / se
    p2 = jnp.exp(m2 - m1) / se
    tot = p1 + p2
    rt_ref[...] = jnp.where(lane == 0, pg_top * (p1 / tot),
                            jnp.where(lane == 1, pg_top * (p2 / tot),
                                      jnp.where(lane == 2, (i1 - N_GROUPS).astype(F32),
                                                jnp.where(lane == 3, (i2 - N_GROUPS).astype(F32), 0.0))))


def _ffn_norm_router(h, g, w_r, b_r, tb=256):
    t, d = h.shape
    return pl.pallas_call(
        _ffn_norm_kernel,
        grid=(t // tb,),
        in_specs=[pl.BlockSpec((tb, d), lambda i: (i, 0)), pl.BlockSpec((1, d), lambda i: (0, 0)),
                  pl.BlockSpec((d, 128), lambda i: (0, 0)), pl.BlockSpec((1, 128), lambda i: (0, 0))],
        out_specs=[pl.BlockSpec((tb, d), lambda i: (i, 0)), pl.BlockSpec((tb, 128), lambda i: (i, 0))],
        out_shape=[jax.ShapeDtypeStruct((t, d), F32), jax.ShapeDtypeStruct((t, 128), F32)],
        compiler_params=_cparams(("parallel",)),
        name="ffn_norm_router",
    )(h, g.reshape(1, d), w_r, b_r)


def _row_copy(src_hbm, idx, dst, r, sem):
    return pltpu.make_async_copy(src_hbm.at[pl.ds(idx, 1), :], dst.at[pl.ds(r, 1), :], sem)


def _expert_changed(i, exp_ref):
    return jnp.logical_or(i == 0, exp_ref[i] != exp_ref[jnp.maximum(i - 1, 0)])


def _gather_rows(src_hbm, idx_ref, base, n, dst, sem):
    def start(r, carry):
        _row_copy(src_hbm, idx_ref[base + r], dst, r, sem).start()
        return carry

    lax.fori_loop(0, n, start, 0, unroll=8)


def _gather_wait(src_hbm, dst, sem):
    pltpu.make_async_copy(src_hbm.at[pl.ds(0, dst.shape[0]), :], dst, sem).wait()


def _expert_kernel(tok_ref, exp_ref, nxt_ref, nblk_ref, x_hbm, wg_hbm, wu_hbm, wd_hbm, gate_ref, o_ref,
                   xbuf, stage_g, stage_u, stage_d, wg_bf, wu_bf, wd_bf, xsem, wsem, *, layer):
    i = pl.program_id(0)
    n = nblk_ref[0]
    slot = i % 2

    stages = (stage_g, stage_u, stage_d)
    caches = (wg_bf, wu_bf, wd_bf)

    def chunk_rows(m, c):
        rows = stages[m].shape[0] // WEIGHT_CHUNKS
        return pl.ds(c * rows, rows)

    def weight_copy(e, m, c):
        src = (wg_hbm, wu_hbm, wd_hbm)[m]
        return pltpu.make_async_copy(src.at[layer, e, chunk_rows(m, c), :], stages[m].at[chunk_rows(m, c), :],
                                     wsem.at[m, c])

    def start_weights(e):
        for c in range(WEIGHT_CHUNKS):
            for m in range(3):
                weight_copy(e, m, c).start()

    @pl.when(jnp.logical_and(i == 0, n > 0))
    def _():
        _gather_rows(x_hbm, tok_ref, 0, MOE_BLOCK, xbuf.at[0], xsem.at[0])
        start_weights(exp_ref[0])

    @pl.when(i + 1 < n)
    def _():
        _gather_rows(x_hbm, tok_ref, (i + 1) * MOE_BLOCK, MOE_BLOCK, xbuf.at[1 - slot], xsem.at[1 - slot])

    @pl.when(i < n)
    def _():
        @pl.when(_expert_changed(i, exp_ref))
        def _():
            nxt = nxt_ref[i]
            for c in range(WEIGHT_CHUNKS):
                for m in range(3):
                    weight_copy(exp_ref[i], m, c).wait()
                    caches[m][chunk_rows(m, c), :] = stages[m][chunk_rows(m, c), :].astype(BF16)

                @pl.when(nxt >= 0)
                def _():
                    for m in range(3):
                        weight_copy(nxt, m, c).start()

        _gather_wait(x_hbm, xbuf.at[slot], xsem.at[slot])
        x = xbuf[slot].astype(BF16)
        gate = _dot(x, wg_bf[...])
        up = _dot(x, wu_bf[...])
        hmid = (gate * _sigmoid(gate) * up).astype(BF16)
        o_ref[...] = _dot(hmid, wd_bf[...]) * gate_ref[...]

    @pl.when(i >= n)
    def _():
        o_ref[...] = jnp.zeros_like(o_ref)


def _experts(hn, row_tok, block_exp, next_exp, n_used, row_gate, w_gate, w_up, w_down, layer):
    n_rows = row_tok.shape[0]
    n_blocks = n_rows // MOE_BLOCK
    d = hn.shape[1]
    hbm = pl.BlockSpec(memory_space=pl.ANY)
    return pl.pallas_call(
        functools.partial(_expert_kernel, layer=layer),
        grid_spec=pltpu.PrefetchScalarGridSpec(
            num_scalar_prefetch=4,
            grid=(n_blocks,),
            in_specs=[hbm, hbm, hbm, hbm, pl.BlockSpec((MOE_BLOCK, 1), lambda i, *_: (i, 0))],
            out_specs=pl.BlockSpec((MOE_BLOCK, d), lambda i, *_: (i, 0)),
            scratch_shapes=[pltpu.VMEM((2, MOE_BLOCK, d), F32),
                            pltpu.VMEM((d, D_EXPERT), F32), pltpu.VMEM((d, D_EXPERT), F32),
                            pltpu.VMEM((D_EXPERT, d), F32),
                            pltpu.VMEM((d, D_EXPERT), BF16), pltpu.VMEM((d, D_EXPERT), BF16),
                            pltpu.VMEM((D_EXPERT, d), BF16),
                            pltpu.SemaphoreType.DMA((2,)), pltpu.SemaphoreType.DMA((3, WEIGHT_CHUNKS))],
        ),
        out_shape=jax.ShapeDtypeStruct((n_rows, d), F32),
        compiler_params=_cparams(("arbitrary",), VMEM_LIMIT_EXPERT),
        name="experts",
    )(row_tok, block_exp, next_exp, n_used, hn, w_gate, w_up, w_down, row_gate)


def _combine_kernel(pos_ref, y_hbm, h_ref, g_ref, o_ref, buf, sem, *, final_norm):
    tb = h_ref.shape[0]
    i = pl.program_id(0)
    slot = i % 2

    def fetch(blk, sl):
        for s in range(TOP_K):
            _gather_rows(y_hbm, pos_ref, (s * pl.num_programs(0) + blk) * tb, tb, buf.at[sl, s], sem.at[sl])

    @pl.when(i == 0)
    def _():
        fetch(0, 0)

    @pl.when(i + 1 < pl.num_programs(0))
    def _():
        fetch(i + 1, 1 - slot)

    for s in range(TOP_K):
        _gather_wait(y_hbm, buf.at[slot, s], sem.at[slot])
    out = h_ref[...] + (buf[slot, 0] + buf[slot, 1])
    if final_norm:
        ms = jnp.mean(out * out, axis=-1, keepdims=True)
        out = out * lax.rsqrt(ms + NORM_EPS) * g_ref[...]
    o_ref[...] = out


def _combine(h, y_rows, pos, g, final_norm, tb=128):
    t, d = h.shape
    grid_spec = pltpu.PrefetchScalarGridSpec(
        num_scalar_prefetch=1,
        grid=(t // tb,),
        in_specs=[pl.BlockSpec(memory_space=pl.ANY),
                  pl.BlockSpec((tb, d), lambda i, pos: (i, 0)),
                  pl.BlockSpec((1, d), lambda i, pos: (0, 0))],
        out_specs=pl.BlockSpec((tb, d), lambda i, pos: (i, 0)),
        scratch_shapes=[pltpu.VMEM((2, TOP_K, tb, d), F32), pltpu.SemaphoreType.DMA((2,))],
    )
    return pl.pallas_call(
        functools.partial(_combine_kernel, final_norm=final_norm),
        grid_spec=grid_spec,
        out_shape=jax.ShapeDtypeStruct((t, d), F32),
        compiler_params=_cparams(("arbitrary",)),
        name="moe_combine",
    )(pos, y_rows, h, g.reshape(1, d))


def _routing(route):
    t = route.shape[0]
    gate = route[:, 0:TOP_K].reshape(-1)
    expert_id = route[:, TOP_K:2 * TOP_K].astype(jnp.int32).reshape(-1)
    n_assign = t * TOP_K
    order = jnp.argsort(expert_id).astype(jnp.int32)
    st = order // TOP_K
    sg = gate[order]
    se = expert_id[order]
    counts = jnp.sum((expert_id[:, None] == jnp.arange(N_EXPERTS, dtype=jnp.int32)[None, :]).astype(jnp.int32), axis=0)
    starts = jnp.cumsum(counts) - counts
    padded = (counts + MOE_BLOCK - 1) // MOE_BLOCK * MOE_BLOCK
    pad_ends = jnp.cumsum(padded)
    pad_starts = pad_ends - padded
    dest = pad_starts[se] + (jnp.arange(n_assign, dtype=jnp.int32) - starts[se])
    n_blocks = (n_assign + MOE_BLOCK - 1) // MOE_BLOCK + N_EXPERTS
    n_rows = n_blocks * MOE_BLOCK
    block_start = jnp.arange(n_blocks, dtype=jnp.int32) * MOE_BLOCK
    block_exp = jnp.minimum(jnp.sum(block_start[:, None] >= pad_ends[None, :], axis=1), N_EXPERTS - 1).astype(jnp.int32)
    local = (block_start - pad_starts[block_exp])[:, None] + jnp.arange(MOE_BLOCK, dtype=jnp.int32)[None, :]
    valid = (local < counts[block_exp][:, None]).reshape(-1)
    src = jnp.clip(starts[block_exp][:, None] + local, 0, n_assign - 1).reshape(-1)
    row_tok = jnp.where(valid, st[src], 0)
    row_gate = jnp.where(valid, sg[src], 0.0)
    n_used = (pad_ends[-1] // MOE_BLOCK).astype(jnp.int32).reshape(1)
    ids = jnp.arange(N_EXPERTS, dtype=jnp.int32)
    first_used_from = lax.cummin(jnp.where(counts > 0, ids, N_EXPERTS)[::-1])[::-1]
    next_used = jnp.concatenate([first_used_from[1:], jnp.full((1,), N_EXPERTS, jnp.int32)])
    next_exp = jnp.where(next_used < N_EXPERTS, next_used, -1)[block_exp]
    pos = dest[jnp.argsort(order)].reshape(t, TOP_K).T.reshape(-1)
    return row_tok, row_gate.reshape(n_rows, 1), block_exp, next_exp, n_used, pos


def _moe_layer(h, norm_g, wg_r, bg_r, we_r, be_r, w_gate, w_up, w_down, layer, out_g, final_norm):
    d = h.shape[1]
    w_r = jnp.concatenate([wg_r, we_r, jnp.zeros((d, 128 - N_GROUPS - N_EXPERTS), F32)], axis=1)
    b_r = jnp.concatenate([bg_r, be_r, jnp.zeros((128 - N_GROUPS - N_EXPERTS,), F32)]).reshape(1, 128)
    hn, route = _ffn_norm_router(h, norm_g, w_r, b_r)
    row_tok, row_gate, block_exp, next_exp, n_used, pos = _routing(route)
    y_rows = _experts(hn, row_tok, block_exp, next_exp, n_used, row_gate, w_gate, w_up, w_down, layer)
    return _combine(h, y_rows, pos, out_g, final_norm)


_D0 = A_COLS + B_COLS + C_COLS
_W_IN_PIECES = (
    (OFF_BQ, A_COLS, B_COLS), (OFF_CX, A_COLS + B_COLS, C_COLS),
    (OFF_DV, _D0 + 512, 512), (OFF_DGT, _D0 + 1040, 512),
    (OFF_AR, 0, 3 * MIX_W), (OFF_DQ, _D0, 512),
    (OFF_AL, 3 * MIX_W, RWKV_LORA), (OFF_DAL, _D0 + 1024, GLA_LORA),
    (OFF_G, _D0 + D_COLS, 4 * D_MODEL),
)


def _relayout_kernel(w_ref, o_ref):
    o_ref[...] = jnp.zeros_like(o_ref)
    for dst, src, width in _W_IN_PIECES:
        o_ref[:, dst:dst + width] = w_ref[0, :, src:src + width].astype(o_ref.dtype)


def _permute_w_in(w_in, layer, tr=128):
    _, d, n_in = w_in.shape
    return pl.pallas_call(
        _relayout_kernel,
        grid=(d // tr,),
        in_specs=[pl.BlockSpec((1, tr, n_in), lambda i: (layer, i, 0))],
        out_specs=pl.BlockSpec((tr, NP_COLS), lambda i: (i, 0)),
        out_shape=jax.ShapeDtypeStruct((d, NP_COLS), BF16),
        compiler_params=_cparams(("parallel",)),
        name="w_in_relayout",
    )(w_in)


def kernel(x, positions, norm_mix_g, w_in, rwkv_mu, rwkv_w0, rwkv_w_up, rwkv_a0, rwkv_a_up, rwkv_g_up, rwkv_k_k, rwkv_k_a, rwkv_r_k, rwkv_ln_g, rwkv_ln_b, rwkv_w_o, ret_norm_g, ret_w_o, lru_conv_w, lru_conv_b, lru_w_a, lru_b_a, lru_w_x, lru_b_x, lru_lambda, lru_w_o, gla_alpha_up, gla_alpha_b, gla_norm_g, gla_w_o, w_out, norm_ffn_g, router_group_w, router_group_b, router_expert_w, router_expert_b, moe_w_gate, moe_w_up, moe_w_down, final_norm_g):
    b_, s_, d = x.shape
    assert b_ == 1 and d == D_MODEL
    depth = w_in.shape[0]
    h = x.reshape(s_, d)
    pos_b = jnp.broadcast_to(positions.reshape(s_, 1), (s_, 128)).astype(jnp.int32)
    cos_t, sin_t = _rope_table(pos_b)
    for l in range(depth):
        xn = _rmsnorm(h, norm_mix_g[l], BF16)
        p = _matmul(xn, _permute_w_in(w_in, l), 2048, 512, name="in_proj")
        y_a = _rwkv_branch(p, rwkv_mu[l], rwkv_w0[l], rwkv_w_up[l], rwkv_a0[l], rwkv_a_up[l], rwkv_g_up[l],
                           rwkv_k_k[l], rwkv_k_a[l], rwkv_r_k[l], rwkv_ln_g[l], rwkv_ln_b[l])
        y_b = _retention(p, cos_t, sin_t, ret_norm_g[l])
        y_c = _rglru(p, pos_b, lru_conv_w[l], lru_conv_b[l], lru_w_a[l], lru_b_a[l], lru_w_x[l], lru_b_x[l],
                     lru_lambda[l])
        y_d = _gla(p, gla_alpha_up[l], gla_alpha_b[l], gla_norm_g[l])
        merged = _merge((y_a, y_b, y_c, y_d), (rwkv_w_o, ret_w_o, lru_w_o, gla_w_o), l, p)
        h = _matmul_residual(merged, w_out, l, h, 1024, 512)
        last = l == depth - 1
        h = _moe_layer(h, norm_ffn_g[l], router_group_w[l], router_group_b[l], router_expert_w[l],
                       router_expert_b[l], moe_w_gate, moe_w_up, moe_w_down, l,
                       final_norm_g if last else norm_ffn_g[l], last)
    return h.reshape(b_, s_, d)
```

```python
import functools
import math

import jax
import jax.numpy as jnp
import numpy as np
from jax import lax
from jax.experimental import pallas as pl
from jax.experimental.pallas import tpu as pltpu

F32 = jnp.float32
BF16 = jnp.bfloat16
HI = lax.Precision.HIGHEST

D_MODEL = 2048
MIX_W = 512
NORM_EPS = 1e-6
HEAD_DIM = 64
N_HEADS = MIX_W // HEAD_DIM

RWKV_W_LORA, RWKV_A_LORA, RWKV_G_LORA = 32, 32, 96
RWKV_LORA = RWKV_W_LORA + RWKV_A_LORA + RWKV_G_LORA
RWKV_DECAY_SCALE = 0.6065306597126334
RWKV_LN_EPS = 64e-5
RWKV_CHUNK = 64
RWKV_LOCAL_CHUNKS = 2

RET_CHUNK = 128
ROPE_BASE = 10000.0

LRU_BLOCKS = 8
CONV_W = 4
LRU_C = 8.0

GLA_HEADS = 4
GLA_DK = 64
GLA_DV = 128
GLA_LORA = 16
GLA_GATE_NORM = 16.0
GLA_CHUNK = 64
GLA_STEP_CHUNKS = 2

N_GROUPS = 4
EXP_PER_GROUP = 8
N_EXPERTS = N_GROUPS * EXP_PER_GROUP
TOP_K = 2
D_EXPERT = 1024
MOE_BLOCK = 128
WEIGHT_CHUNKS = 4

A_COLS = 3 * MIX_W + RWKV_LORA
B_COLS = 4 * MIX_W
C_COLS = 2 * MIX_W
D_COLS = 2 * (MIX_W // 2) + MIX_W + GLA_LORA + MIX_W

OFF_BQ, OFF_BK, OFF_BV, OFF_BG = 0, 512, 1024, 1536
OFF_CX, OFF_CG = 2048, 2560
OFF_DV, OFF_DGT = 3072, 3584
OFF_AR, OFF_AK, OFF_AV = 4096, 4608, 5120
OFF_DQ, OFF_DK = 5632, 5888
OFF_AL = 6144
OFF_DAL = 6400
OFF_G = 6656
NP_COLS = OFF_G + 4 * D_MODEL

VMEM_LIMIT = 48 * 1024 * 1024
VMEM_LIMIT_EXPERT = 56 * 1024 * 1024


def _cparams(sem, vmem=VMEM_LIMIT):
    return pltpu.CompilerParams(dimension_semantics=sem, vmem_limit_bytes=vmem)


def _dot(a, b, prec=None):
    return jnp.dot(a, b, precision=prec, preferred_element_type=F32)


def _sigmoid(x):
    return 1.0 / (1.0 + jnp.exp(-x))


def _split(x):
    hi = x.astype(BF16)
    return hi, (x - hi.astype(F32)).astype(BF16)


def _dims(form, batched):
    ca, cb = {"nn": (1, 0), "nt": (1, 1), "tn": (0, 0)}[form]
    if batched:
        return (((ca + 1,), (cb + 1,)), ((0,), (0,)))
    return (((ca,), (cb,)), ((), ()))


def _mm(a, b, mode, form="nn"):
    dims = _dims(form, a.ndim == 3)
    if mode == "hi":
        return lax.dot_general(a, b, dims, precision=HI, preferred_element_type=F32)
    d = functools.partial(lax.dot_general, dimension_numbers=dims, preferred_element_type=F32)
    if mode == "x1":
        return d(a.astype(BF16), b.astype(BF16))
    ah, al = _split(a)
    bh, bl = _split(b)
    return d(ah, bh) + (d(ah, bl) + d(al, bh))


def _mm_exact_rhs(a, b, terms):
    b = b.astype(BF16)
    acc = None
    for _ in range(terms):
        piece = a.astype(BF16)
        part = _dot(piece, b)
        acc = part if acc is None else acc + part
        a = a - piece.astype(F32)
    return acc


def _mm_exact_lhs(a, b, terms):
    a = a.astype(BF16)
    acc = None
    for _ in range(terms):
        piece = b.astype(BF16)
        part = _dot(a, piece)
        acc = part if acc is None else acc + part
        b = b - piece.astype(F32)
    return acc


def _heads(x, width=HEAD_DIM):
    return jnp.stack([x[:, h * width:(h + 1) * width] for h in range(x.shape[1] // width)])


def _block_diag_const(n, blk, value):
    r = lax.broadcasted_iota(jnp.int32, (n, n), 0) // blk
    c = lax.broadcasted_iota(jnp.int32, (n, n), 1) // blk
    return jnp.where(r == c, value, 0.0).astype(F32)


def _rmsnorm_kernel(x_ref, g_ref, o_ref):
    x = x_ref[...]
    ms = jnp.mean(x * x, axis=-1, keepdims=True)
    o_ref[...] = (x * lax.rsqrt(ms + NORM_EPS) * g_ref[...]).astype(o_ref.dtype)


def _rmsnorm(x, g, out_dtype, tb=512):
    t, d = x.shape
    return pl.pallas_call(
        _rmsnorm_kernel,
        grid=(t // tb,),
        in_specs=[pl.BlockSpec((tb, d), lambda i: (i, 0)), pl.BlockSpec((1, d), lambda i: (0, 0))],
        out_specs=pl.BlockSpec((tb, d), lambda i: (i, 0)),
        out_shape=jax.ShapeDtypeStruct((t, d), out_dtype),
        compiler_params=_cparams(("parallel",)),
        name="rmsnorm",
    )(x, g.reshape(1, d))


def _mm_nt_kernel(a_ref, b_ref, o_ref):
    o_ref[...] = _mm(a_ref[...], b_ref[...], "x1", "nt").astype(o_ref.dtype)


def _matmul_nt(a, b_t, tm, tn, name):
    m, k = a.shape
    n = b_t.shape[0]
    return pl.pallas_call(
        _mm_nt_kernel,
        grid=(m // tm, n // tn),
        in_specs=[pl.BlockSpec((tm, k), lambda i, j: (i, 0)), pl.BlockSpec((tn, k), lambda i, j: (j, 0))],
        out_specs=pl.BlockSpec((tm, tn), lambda i, j: (i, j)),
        out_shape=jax.ShapeDtypeStruct((m, n), F32),
        compiler_params=_cparams(("parallel", "parallel")),
        name=name,
    )(a, b_t)


def _mm_res_kernel(a_ref, b_ref, r_ref, o_ref):
    o_ref[...] = r_ref[...] + _dot(a_ref[...], b_ref[...].astype(BF16))


def _matmul_residual(a, b, layer, res, tm, tn):
    m, k = a.shape
    n = b.shape[2]
    return pl.pallas_call(
        _mm_res_kernel,
        grid=(m // tm, n // tn),
        in_specs=[pl.BlockSpec((tm, k), lambda i, j: (i, 0)), pl.BlockSpec((None, k, tn), lambda i, j: (layer, 0, j)),
                  pl.BlockSpec((tm, tn), lambda i, j: (i, j))],
        out_specs=pl.BlockSpec((tm, tn), lambda i, j: (i, j)),
        out_shape=jax.ShapeDtypeStruct((m, n), F32),
        compiler_params=_cparams(("parallel", "parallel")),
        name="wout_residual",
    )(a, b, res)


def _shifted(x, tail_ref, sc_ref, width, first):
    tb = x.shape[0]
    sc_ref[0:8, 0:width] = jnp.where(first, 0.0, tail_ref[...])
    sc_ref[8:8 + tb, 0:width] = x
    return sc_ref[7:7 + tb, 0:width]


def _rwkv_prep_kernel(r_ref, k_ref, v_ref, l_ref, rt_ref, kt_ref, vt_ref, lt_ref,
                      mur_ref, muk_ref, muv_ref, mul_ref, wl_ref, b0_ref, kk_ref, ka_ref,
                      ro_ref, lw_ref, ko_ref, vo_ref, kkn_ref, kka_ref, g_ref, sc_ref):
    first = pl.program_id(0) == 0

    def mix(x_ref, t_ref, mu_ref, width):
        x = x_ref[...]
        prev = _shifted(x, t_ref, sc_ref, width, first)
        return x + (prev - x) * mu_ref[...]

    r = mix(r_ref, rt_ref, mur_ref, MIX_W)
    k = mix(k_ref, kt_ref, muk_ref, MIX_W)
    v = mix(v_ref, vt_ref, muv_ref, MIX_W)
    zl = mix(l_ref, lt_ref, mul_ref, 256)
    lane = lax.broadcasted_iota(jnp.int32, zl.shape, 1)
    act = jnp.where(lane < RWKV_W_LORA, jnp.tanh(zl),
                    jnp.where(lane < RWKV_W_LORA + RWKV_A_LORA, zl, _sigmoid(zl)))
    lo = _mm(act, wl_ref[...], "x3") + b0_ref[...]
    lw = -RWKV_DECAY_SCALE * _sigmoid(lo[:, 0:MIX_W])
    a = _sigmoid(lo[:, MIX_W:2 * MIX_W])
    g = lo[:, 2 * MIX_W:3 * MIX_W]
    kk = k * kk_ref[...]
    ss = _mm_exact_rhs(kk * kk, _block_diag_const(MIX_W, HEAD_DIM, 1.0), 2)
    kkn = kk / jnp.maximum(jnp.sqrt(ss), 1e-12)
    ro_ref[...] = r
    lw_ref[...] = lw
    ko_ref[...] = k * (1.0 + (a - 1.0) * ka_ref[...])
    vo_ref[...] = v
    kkn_ref[...] = kkn
    kka_ref[...] = kkn * a
    g_ref[...] = g


def _rwkv_prep(p, mu, w0, w_up, a0, a_up, g_up, k_k, k_a, tb=256):
    t = p.shape[0]
    nb8 = tb // 8

    def blk(width, off):
        return pl.BlockSpec((tb, width), lambda i: (i, off // width))

    def tail(width, off):
        return pl.BlockSpec((8, width), lambda i: (jnp.maximum(i * nb8 - 1, 0), off // width))

    def vec(width):
        return pl.BlockSpec((1, width), lambda i: (0, 0))

    w_lora = jnp.zeros((256, 3 * MIX_W), F32)
    w_lora = w_lora.at[0:32, 0:MIX_W].set(w_up)
    w_lora = w_lora.at[32:64, MIX_W:2 * MIX_W].set(a_up)
    w_lora = w_lora.at[64:160, 2 * MIX_W:].set(g_up)
    b0 = jnp.concatenate([w0, a0, jnp.zeros((MIX_W,), F32)]).reshape(1, 3 * MIX_W)
    mu_l = jnp.concatenate([mu[3 * MIX_W:], jnp.zeros((256 - RWKV_LORA,), F32)]).reshape(1, 256)
    out = jax.ShapeDtypeStruct((t, MIX_W), F32)
    return pl.pallas_call(
        _rwkv_prep_kernel,
        grid=(t // tb,),
        in_specs=[blk(MIX_W, OFF_AR), blk(MIX_W, OFF_AK), blk(MIX_W, OFF_AV), blk(256, OFF_AL),
                  tail(MIX_W, OFF_AR), tail(MIX_W, OFF_AK), tail(MIX_W, OFF_AV), tail(256, OFF_AL),
                  vec(MIX_W), vec(MIX_W), vec(MIX_W), vec(256),
                  pl.BlockSpec((256, 3 * MIX_W), lambda i: (0, 0)), vec(3 * MIX_W), vec(MIX_W), vec(MIX_W)],
        out_specs=[pl.BlockSpec((tb, MIX_W), lambda i: (i, 0))] * 7,
        out_shape=[out] * 7,
        scratch_shapes=[pltpu.VMEM((tb + 8, MIX_W), F32)],
        compiler_params=_cparams(("parallel",)),
        name="rwkv_prep",
    )(p, p, p, p, p, p, p, p,
      mu[0:MIX_W].reshape(1, -1), mu[MIX_W:2 * MIX_W].reshape(1, -1), mu[2 * MIX_W:3 * MIX_W].reshape(1, -1), mu_l,
      w_lora, b0, k_k.reshape(1, -1), k_a.reshape(1, -1))


RW_SC, RW_INV, RW_APPLY, RW_STATE, RW_SCAN = "x1", "x1", "x1", "x1", "x3"


def _rwkv_local_chunk(r, lw, k, v, kk, kka):
    c = r.shape[0]
    row = lax.broadcasted_iota(jnp.int32, (c, c), 0)
    col = lax.broadcasted_iota(jnp.int32, (c, c), 1)
    incl = row >= col
    strict = row > col
    eye = jnp.where(row == col, 1.0, 0.0)
    cum = _mm_exact_lhs(incl.astype(F32), lw, 3)
    last = cum[c - 1:c, :]
    pinv = jnp.exp(-cum)
    dl = jnp.exp(last - cum)
    at = _heads(-kk * jnp.exp(cum - lw))
    rt = _heads(r * jnp.exp(cum))
    bt = _heads(kka * pinv)
    kt = _heads(k * pinv)
    bl = _heads(kka * dl)
    kl = _heads(k * dl)
    v = _heads(v)
    pc = _heads(jnp.exp(last))
    sc = _mm(jnp.concatenate([at, rt], axis=1), jnp.concatenate([bt, kt], axis=1), RW_SC, "nt")
    a_ab = jnp.where(strict, sc[:, 0:c, 0:c], 0.0)
    a_ak = jnp.where(strict, sc[:, 0:c, c:2 * c], 0.0)
    r_b = jnp.where(incl, sc[:, c:2 * c, 0:c], 0.0)
    r_k = jnp.where(incl, sc[:, c:2 * c, c:2 * c], 0.0)
    x = eye + a_ab
    pw = a_ab
    for _ in range(int(math.log2(c)) - 1):
        pw = _mm(pw, pw, RW_INV)
        x = x + _mm(x, pw, RW_INV)
    wu = _mm(x, jnp.concatenate([at, _mm(a_ak, v, RW_APPLY)], axis=2), RW_APPLY)
    lower = jnp.concatenate([jnp.zeros_like(v), v], axis=2)
    qy = _mm(jnp.concatenate([r_b, r_k], axis=2), jnp.concatenate([wu, lower], axis=1), RW_APPLY)
    wb = _mm(wu, bl, RW_STATE, "tn")
    n = wb[:, HEAD_DIM:] + _mm(v, kl, RW_STATE, "tn")
    m = eye * pc + wb[:, 0:HEAD_DIM]
    return rt + qy[:, :, 0:HEAD_DIM], qy[:, :, HEAD_DIM:], m, n


def _rwkv_local_kernel(r_ref, lw_ref, k_ref, v_ref, kk_ref, kka_ref, q_ref, y0_ref, m_ref, n_ref):
    c = RWKV_CHUNK
    outs = [_rwkv_local_chunk(*(ref[j * c:(j + 1) * c, :] for ref in (r_ref, lw_ref, k_ref, v_ref, kk_ref, kka_ref)))
            for j in range(r_ref.shape[0] // c)]
    for j, res in enumerate(outs):
        for o_ref, val in zip((q_ref, y0_ref, m_ref, n_ref), res):
            for h in range(N_HEADS):
                o_ref[j * c:(j + 1) * c, h * HEAD_DIM:(h + 1) * HEAD_DIM] = val[h]


def _rwkv_scan_kernel(q_ref, y0_ref, m_ref, n_ref, r_ref, k_ref, v_ref, g_ref, lng_ref, lnb_ref, rk_ref,
                      o_ref, s_ref, y_sc):
    @pl.when(pl.program_id(0) == 0)
    def _():
        s_ref[...] = jnp.zeros_like(s_ref)

    c = RWKV_CHUNK
    s = s_ref[...]
    for j in range(q_ref.shape[0] // c):
        rows = slice(j * c, (j + 1) * c)
        y = _mm(_heads(q_ref[rows, :]), s, RW_SCAN, "nt") + _heads(y0_ref[rows, :])
        s = _mm(s, _heads(m_ref[rows, :]), RW_SCAN) + _heads(n_ref[rows, :])
        for h in range(N_HEADS):
            y_sc[rows, h * HEAD_DIM:(h + 1) * HEAD_DIM] = y[h]
    s_ref[...] = s
    y = y_sc[...]
    avg = _block_diag_const(MIX_W, HEAD_DIM, 1.0 / HEAD_DIM)
    mean = _mm_exact_rhs(y, avg, 2)
    yc = y - mean
    var = _mm_exact_rhs(yc * yc, avg, 2)
    yn = yc * lax.rsqrt(var + RWKV_LN_EPS) * lng_ref[...] + lnb_ref[...]
    v = v_ref[...]
    bonus = _mm_exact_rhs(r_ref[...] * k_ref[...] * rk_ref[...], _block_diag_const(MIX_W, HEAD_DIM, 1.0), 2) * v
    o_ref[...] = ((yn + bonus) * g_ref[...]).astype(o_ref.dtype)


def _rwkv_recurrence(r, lw, k, v, kkn, kka, g, ln_g, ln_b, r_k, tb=256):
    t = r.shape[0]
    c = RWKV_CHUNK
    cblk = pl.BlockSpec((RWKV_LOCAL_CHUNKS * c, MIX_W), lambda i: (i, 0))
    f = jax.ShapeDtypeStruct((t, MIX_W), F32)
    qm, y0, m, n = pl.pallas_call(
        _rwkv_local_kernel,
        grid=(t // (RWKV_LOCAL_CHUNKS * c),),
        in_specs=[cblk] * 6,
        out_specs=[cblk] * 4,
        out_shape=[f] * 4,
        compiler_params=_cparams(("parallel",)),
        name="rwkv_local",
    )(r, lw, k, v, kkn, kka)
    blk = pl.BlockSpec((tb, MIX_W), lambda i: (i, 0))
    vec = pl.BlockSpec((1, MIX_W), lambda i: (0, 0))
    return pl.pallas_call(
        _rwkv_scan_kernel,
        grid=(t // tb,),
        in_specs=[blk] * 8 + [vec] * 3,
        out_specs=blk,
        out_shape=jax.ShapeDtypeStruct((t, MIX_W), BF16),
        scratch_shapes=[pltpu.VMEM((N_HEADS, HEAD_DIM, HEAD_DIM), F32), pltpu.VMEM((tb, MIX_W), F32)],
        compiler_params=_cparams(("arbitrary",)),
        name="rwkv_scan",
    )(qm, y0, m, n, r, k, v, g, ln_g.reshape(1, -1), ln_b.reshape(1, -1), r_k.reshape(1, -1))


def _rwkv_branch(p, mu, w0, w_up, a0, a_up, g_up, k_k, k_a, r_k, ln_g, ln_b):
    prep = _rwkv_prep(p, mu, w0, w_up, a0, a_up, g_up, k_k, k_a)
    return _rwkv_recurrence(*prep, ln_g, ln_b, r_k.reshape(-1))


def _rope_kernel(pos_ref, inv_ref, cos_ref, sin_ref):
    ang = pos_ref[...].astype(F32) * inv_ref[...]
    lane = lax.broadcasted_iota(jnp.int32, ang.shape, 1)
    cos_ref[...] = jnp.cos(ang)
    sin_ref[...] = jnp.where(lane % HEAD_DIM < HEAD_DIM // 2, -jnp.sin(ang), jnp.sin(ang))


def _rope_table(pos_b, tb=512):
    t = pos_b.shape[0]
    d = HEAD_DIM
    inv = 1.0 / (ROPE_BASE ** (jnp.arange(0, d, 2, dtype=F32) / d))
    inv_b = jnp.tile(inv, 4).reshape(1, 128)
    blk = pl.BlockSpec((tb, 128), lambda i: (i, 0))
    return pl.pallas_call(
        _rope_kernel,
        grid=(t // tb,),
        in_specs=[blk, pl.BlockSpec((1, 128), lambda i: (0, 0))],
        out_specs=[blk, blk],
        out_shape=[jax.ShapeDtypeStruct((t, 128), F32)] * 2,
        compiler_params=_cparams(("parallel",)),
        name="rope_table",
    )(pos_b, inv_b)


def _ret_kernel(q_ref, k_ref, v_ref, g_ref, cos_ref, sin_ref, dec_ref, zeta_ref, xi_ref, cd_ref, ng_ref,
                o_ref, s_ref, o_sc):
    @pl.when(pl.program_id(0) == 0)
    def _():
        s_ref[...] = jnp.zeros_like(s_ref)

    cos = jnp.concatenate([cos_ref[...]] * 4, axis=1)
    sin = jnp.concatenate([sin_ref[...]] * 4, axis=1)
    lane = lax.broadcasted_iota(jnp.int32, cos.shape, 1)
    lower_half = lane % HEAD_DIM < HEAD_DIM // 2

    def rope(x):
        swapped = jnp.where(lower_half, pltpu.roll(x, MIX_W - HEAD_DIM // 2, 1), pltpu.roll(x, HEAD_DIM // 2, 1))
        return x * cos + swapped * sin

    q = _heads(rope(q_ref[...]))
    k = _heads(rope(k_ref[...]) * (HEAD_DIM ** -0.5))
    v = _heads(v_ref[...])
    s = s_ref[...]
    scores = _mm(q, k, "x1", "nt") * dec_ref[...]
    o = _mm(scores, v, "x1") + _mm(q * xi_ref[...], s, "x1")
    s_ref[...] = s * cd_ref[...] + _mm(k * zeta_ref[...], v, "x1", "tn")
    on = o * lax.rsqrt(jnp.mean(o * o, axis=-1, keepdims=True) + NORM_EPS)
    for h in range(N_HEADS):
        o_sc[:, h * HEAD_DIM:(h + 1) * HEAD_DIM] = on[h]
    gt = g_ref[...]
    o_ref[...] = (o_sc[...] * ng_ref[...] * (gt * _sigmoid(gt))).astype(o_ref.dtype)


def _retention(p, cos_t, sin_t, norm_g):
    t = p.shape[0]
    c = RET_CHUNK
    h = N_HEADS
    log_g = jnp.log(1.0 - jnp.exp(jnp.linspace(math.log(1.0 / 32), math.log(1.0 / 512), h, dtype=F32)))
    idx = jnp.arange(c, dtype=F32)
    diff = idx[:, None] - idx[None, :]
    decay = jnp.where(diff >= 0, jnp.exp(log_g[:, None, None] * jnp.maximum(diff, 0.0)[None]), 0.0)
    zeta = jnp.exp(log_g[:, None] * (c - 1 - idx)[None])
    xi = jnp.exp(log_g[:, None] * (idx + 1)[None])
    cdec = jnp.exp(log_g * c)
    zeta_b = jnp.broadcast_to(zeta[:, :, None], (h, c, HEAD_DIM))
    xi_b = jnp.broadcast_to(xi[:, :, None], (h, c, HEAD_DIM))
    cd_b = jnp.broadcast_to(cdec[:, None, None], (h, HEAD_DIM, HEAD_DIM))

    def blk(off):
        return pl.BlockSpec((c, MIX_W), lambda i: (i, off // MIX_W))

    tab = pl.BlockSpec((c, 128), lambda i: (i, 0))

    def const(shape):
        return pl.BlockSpec(shape, lambda i: (0,) * len(shape))

    return pl.pallas_call(
        _ret_kernel,
        grid=(t // c,),
        in_specs=[blk(OFF_BQ), blk(OFF_BK), blk(OFF_BV), blk(OFF_BG), tab, tab,
                  const((h, c, c)), const((h, c, HEAD_DIM)), const((h, c, HEAD_DIM)),
                  const((h, HEAD_DIM, HEAD_DIM)), const((1, MIX_W))],
        out_specs=pl.BlockSpec((c, MIX_W), lambda i: (i, 0)),
        out_shape=jax.ShapeDtypeStruct((t, MIX_W), BF16),
        scratch_shapes=[pltpu.VMEM((h, HEAD_DIM, HEAD_DIM), F32), pltpu.VMEM((c, MIX_W), F32)],
        compiler_params=_cparams(("arbitrary",)),
        name="retention",
    )(p, p, p, p, cos_t, sin_t, decay, zeta_b, xi_b, cd_b, norm_g.reshape(1, -1))


def _lru_kernel(x_ref, xt_ref, gb_ref, pos_ref, cw_ref, cb_ref, wa_ref, ba_ref, wx_ref, bx_ref, lam_ref,
                o_ref, sc_ref, a_sc, b_sc, h_ref):
    tb = x_ref.shape[0]
    first = pl.program_id(0) == 0

    @pl.when(first)
    def _():
        h_ref[...] = jnp.zeros_like(h_ref)

    x = x_ref[...]
    sc_ref[0:8, :] = jnp.where(first, 0.0, xt_ref[...])
    sc_ref[8:8 + tb, :] = x
    cw = cw_ref[...]
    xc = cb_ref[...] + x * cw[CONV_W - 1:CONV_W, :]
    for j in range(1, CONV_W):
        xc = xc + sc_ref[8 - j:8 - j + tb, :] * cw[CONV_W - 1 - j:CONV_W - j, :]
    r = _sigmoid(_mm(xc, wa_ref[...], "x3") + ba_ref[...])
    ig = _sigmoid(_mm(xc, wx_ref[...], "x3") + bx_ref[...])
    nl = -lam_ref[...]
    softplus = jnp.maximum(nl, 0.0) + jnp.log1p(jnp.exp(-jnp.abs(nl)))
    log_a = -LRU_C * r * softplus
    pos = pos_ref[...]
    reset = jnp.concatenate([pos, pos, pos, pos], axis=1) == 0
    a_sc[...] = jnp.where(reset, 0.0, jnp.exp(log_a))
    th = jnp.tanh(log_a)
    b_sc[...] = jnp.where(reset, 1.0, jnp.sqrt(-2.0 * th / (1.0 - th))) * ig * xc

    row = lax.broadcasted_iota(jnp.int32, (8, MIX_W), 0)

    def group(gi, h):
        off = pl.multiple_of(gi * 8, 8)
        a = a_sc[pl.ds(off, 8), :]
        b = b_sc[pl.ds(off, 8), :]
        for d in (1, 2, 4):
            keep = row >= d
            b = jnp.where(keep, a * pltpu.roll(b, d, 0) + b, b)
            a = jnp.where(keep, a * pltpu.roll(a, d, 0), a)
        hs = a * h + b
        b_sc[pl.ds(off, 8), :] = hs
        return hs[7:8, :]

    h_ref[...] = lax.fori_loop(0, tb // 8, group, h_ref[...])
    gb = gb_ref[...]
    gelu = gb * (0.5 * (1.0 + jnp.tanh(math.sqrt(2.0 / math.pi) * (gb + 0.044715 * (gb * gb * gb)))))
    o_ref[...] = (b_sc[...] * gelu).astype(o_ref.dtype)


def _block_diag_weight(w):
    g, n, _ = w.shape
    eye = jnp.eye(g, dtype=w.dtype)
    return (eye[:, None, :, None] * w[:, :, None, :]).reshape(g * n, g * n)


def _rglru(p, pos_b, conv_w, conv_b, w_a, b_a, w_x, b_x, lam, tb=256):
    t = p.shape[0]
    nb8 = tb // 8
    vec = pl.BlockSpec((1, MIX_W), lambda i: (0, 0))
    mat = pl.BlockSpec((MIX_W, MIX_W), lambda i: (0, 0))
    cw8 = jnp.concatenate([conv_w, jnp.zeros((8 - CONV_W, MIX_W), F32)], axis=0)
    return pl.pallas_call(
        _lru_kernel,
        grid=(t // tb,),
        in_specs=[pl.BlockSpec((tb, MIX_W), lambda i: (i, OFF_CX // MIX_W)),
                  pl.BlockSpec((8, MIX_W), lambda i: (jnp.maximum(i * nb8 - 1, 0), OFF_CX // MIX_W)),
                  pl.BlockSpec((tb, MIX_W), lambda i: (i, OFF_CG // MIX_W)),
                  pl.BlockSpec((tb, 128), lambda i: (i, 0)),
                  pl.BlockSpec((8, MIX_W), lambda i: (0, 0)), vec, mat, vec, mat, vec, vec],
        out_specs=pl.BlockSpec((tb, MIX_W), lambda i: (i, 0)),
        out_shape=jax.ShapeDtypeStruct((t, MIX_W), BF16),
        scratch_shapes=[pltpu.VMEM((tb + 8, MIX_W), F32), pltpu.VMEM((tb, MIX_W), F32),
                        pltpu.VMEM((tb, MIX_W), F32), pltpu.VMEM((1, MIX_W), F32)],
        compiler_params=_cparams(("arbitrary",)),
        name="rglru",
    )(p, p, p, pos_b, cw8, conv_b.reshape(1, -1), _block_diag_weight(w_a), b_a.reshape(1, -1),
      _block_diag_weight(w_x), b_x.reshape(1, -1), lam.reshape(1, -1))


def _gla_kernel(v_ref, gt_ref, q_ref, k_ref, al_ref, aup_ref, ab_ref, ng_ref, o_ref, s_ref):
    @pl.when(pl.program_id(0) == 0)
    def _():
        s_ref[...] = jnp.zeros_like(s_ref)

    c = GLA_CHUNK
    row = lax.broadcasted_iota(jnp.int32, (c, c), 0)
    col = lax.broadcasted_iota(jnp.int32, (c, c), 1)
    causal = row >= col
    pre = _mm(al_ref[...], aup_ref[...], "x3") + ab_ref[...]
    log_a = (jnp.minimum(pre, 0.0) - jnp.log1p(jnp.exp(-jnp.abs(pre)))) / GLA_GATE_NORM
    s = s_ref[...]
    outs = []
    for j in range(q_ref.shape[0] // c):
        rows = slice(j * c, (j + 1) * c)
        bcum = _mm_exact_lhs(causal.astype(F32), log_a[rows], 3)
        blast = bcum[c - 1:c, :]
        k = k_ref[rows, :]
        q_e = _heads(q_ref[rows, :] * (GLA_DK ** -0.5) * jnp.exp(bcum))
        k_e = _heads(k * jnp.exp(-bcum))
        k_l = _heads(k * jnp.exp(blast - bcum))
        dec = _heads(jnp.exp(blast))
        v = _heads(v_ref[rows, :], GLA_DV)
        scores = jnp.where(causal, _mm(q_e, k_e, "x1", "nt"), 0.0)
        o = _mm(scores, v, "x1") + _mm(q_e, s, "x1", "nt")
        s = s * dec + _mm(v, k_l, "x1", "tn")
        on = o * lax.rsqrt(jnp.mean(o * o, axis=-1, keepdims=True) + NORM_EPS)
        outs.append(jnp.concatenate([on[h] for h in range(GLA_HEADS)], axis=1))
    s_ref[...] = s
    on = jnp.concatenate(outs, axis=0) * ng_ref[...]
    gt = gt_ref[...]
    o_ref[...] = (on * (gt * _sigmoid(gt))).astype(o_ref.dtype)


def _gla(p, alpha_up, alpha_b, norm_g):
    t = p.shape[0]
    c = GLA_STEP_CHUNKS * GLA_CHUNK
    hk = MIX_W // 2
    aup = jnp.concatenate([alpha_up, jnp.zeros((128 - GLA_LORA, hk), F32)], axis=0)

    def blk(width, off):
        return pl.BlockSpec((c, width), lambda i: (i, off // width))

    return pl.pallas_call(
        _gla_kernel,
        grid=(t // c,),
        in_specs=[blk(MIX_W, OFF_DV), blk(MIX_W, OFF_DGT), blk(hk, OFF_DQ), blk(hk, OFF_DK), blk(128, OFF_DAL),
                  pl.BlockSpec((128, hk), lambda i: (0, 0)), pl.BlockSpec((1, hk), lambda i: (0, 0)),
                  pl.BlockSpec((1, MIX_W), lambda i: (0, 0))],
        out_specs=pl.BlockSpec((c, MIX_W), lambda i: (i, 0)),
        out_shape=jax.ShapeDtypeStruct((t, MIX_W), BF16),
        scratch_shapes=[pltpu.VMEM((GLA_HEADS, GLA_DV, GLA_DK), F32)],
        compiler_params=_cparams(("arbitrary",)),
        name="gla",
    )(p, p, p, p, p, aup, alpha_b.reshape(1, -1), norm_g.reshape(1, -1))


def _merge_kernel(ya_ref, yb_ref, yc_ref, yd_ref, wa_ref, wb_ref, wc_ref, wd_ref,
                  ga_ref, gb_ref, gc_ref, gd_ref, o_ref):
    acc = _sigmoid(ga_ref[...]) * _dot(ya_ref[...], wa_ref[0].astype(BF16))
    acc += _sigmoid(gb_ref[...]) * _dot(yb_ref[...], wb_ref[0].astype(BF16))
    acc += _sigmoid(gc_ref[...]) * _dot(yc_ref[...], wc_ref[0].astype(BF16))
    acc += _sigmoid(gd_ref[...]) * _dot(yd_ref[...], wd_ref[0].astype(BF16))
    o_ref[...] = acc.astype(o_ref.dtype)


def _merge(ys, ws, layer, p, tm=512, tn=512):
    t = p.shape[0]
    yspec = pl.BlockSpec((tm, MIX_W), lambda i, j: (i, 0))
    wspec = pl.BlockSpec((1, MIX_W, tn), lambda i, j: (layer, 0, j))

    def gspec(kk):
        return pl.BlockSpec((tm, tn), lambda i, j: (i, (OFF_G + kk * D_MODEL) // tn + j))

    return pl.pallas_call(
        _merge_kernel,
        grid=(t // tm, D_MODEL // tn),
        in_specs=[yspec] * 4 + [wspec] * 4 + [gspec(kk) for kk in range(4)],
        out_specs=pl.BlockSpec((tm, tn), lambda i, j: (i, j)),
        out_shape=jax.ShapeDtypeStruct((t, D_MODEL), BF16),
        compiler_params=_cparams(("parallel", "parallel")),
        name="merge",
    )(*ys, *ws, p, p, p, p)


def _ffn_norm_kernel(h_ref, g_ref, wr_ref, br_ref, hn_ref, rt_ref):
    x = h_ref[...]
    ms = jnp.mean(x * x, axis=-1, keepdims=True)
    hn = x * lax.rsqrt(ms + NORM_EPS) * g_ref[...]
    hn_ref[...] = hn
    z = _mm(hn, wr_ref[...], "x3") + br_ref[...]
    lane = lax.broadcasted_iota(jnp.int32, z.shape, 1)
    neg = jnp.float32(-1e30)

    def first_argmax(v):
        m = jnp.max(v, axis=-1, keepdims=True)
        return m, jnp.min(jnp.where(v == m, lane, 128), axis=-1, keepdims=True)

    is_group = lane < N_GROUPS
    zg = jnp.where(is_group, z, neg)
    mg, g_idx = first_argmax(zg)
    pg_top = 1.0 / jnp.sum(jnp.where(is_group, jnp.exp(zg - mg), 0.0), axis=-1, keepdims=True)
    lo = N_GROUPS + g_idx * EXP_PER_GROUP
    in_group = jnp.logical_and(lane >= lo, lane < lo + EXP_PER_GROUP)
    ze = jnp.where(in_group, z, neg)
    m1, i1 = first_argmax(ze)
    se = jnp.sum(jnp.where(in_group, jnp.exp(ze - m1), 0.0), axis=-1, keepdims=True)
    m2, i2 = first_argmax(jnp.where(lane == i1, neg, ze))
    p1 = 1.0 / se
    p2 = jnp.exp(m2 - m1) / se
    tot = p1 + p2
    rt_ref[...] = jnp.where(lane == 0, pg_top * (p1 / tot),
                            jnp.where(lane == 1, pg_top * (p2 / tot),
                                      jnp.where(lane == 2, (i1 - N_GROUPS).astype(F32),
                                                jnp.where(lane == 3, (i2 - N_GROUPS).astype(F32), 0.0))))


def _ffn_norm_router(h, g, w_r, b_r, tb=256):
    t, d = h.shape
    return pl.pallas_call(
        _ffn_norm_kernel,
        grid=(t // tb,),
        in_specs=[pl.BlockSpec((tb, d), lambda i: (i, 0)), pl.BlockSpec((1, d), lambda i: (0, 0)),
                  pl.BlockSpec((d, 128), lambda i: (0, 0)), pl.BlockSpec((1, 128), lambda i: (0, 0))],
        out_specs=[pl.BlockSpec((tb, d), lambda i: (i, 0)), pl.BlockSpec((tb, 128), lambda i: (i, 0))],
        out_shape=[jax.ShapeDtypeStruct((t, d), F32), jax.ShapeDtypeStruct((t, 128), F32)],
        compiler_params=_cparams(("parallel",)),
        name="ffn_norm_router",
    )(h, g.reshape(1, d), w_r, b_r)


def _row_copy(src_hbm, idx, dst, r, sem):
    return pltpu.make_async_copy(src_hbm.at[pl.ds(idx, 1), :], dst.at[pl.ds(r, 1), :], sem)


def _expert_changed(i, exp_ref):
    return jnp.logical_or(i == 0, exp_ref[i] != exp_ref[jnp.maximum(i - 1, 0)])


def _gather_rows(src_hbm, row_of, n, dst, sem):
    def start(r, carry):
        _row_copy(src_hbm, row_of(r), dst, r, sem).start()
        return carry

    lax.fori_loop(0, n, start, 0, unroll=8)


def _gather_wait(src_hbm, dst, sem):
    pltpu.make_async_copy(src_hbm.at[pl.ds(0, dst.shape[0]), :], dst, sem).wait()


def _expert_kernel(st_ref, exp_ref, base_ref, nxt_ref, nblk_ref, x_hbm, wg_hbm, wu_hbm, wd_hbm, o_ref,
                   xbuf, stage_g, stage_u, stage_d, wg_bf, wu_bf, wd_bf, xsem, wsem, *, layer):
    i = pl.program_id(0)
    n = nblk_ref[0]
    slot = i % 2

    stages = (stage_g, stage_u, stage_d)
    caches = (wg_bf, wu_bf, wd_bf)

    def chunk_rows(m, c):
        rows = stages[m].shape[0] // WEIGHT_CHUNKS
        return pl.ds(c * rows, rows)

    def weight_copy(e, m, c):
        src = (wg_hbm, wu_hbm, wd_hbm)[m]
        return pltpu.make_async_copy(src.at[layer, e, chunk_rows(m, c), :], stages[m].at[chunk_rows(m, c), :],
                                     wsem.at[m, c])

    def start_weights(e):
        for c in range(WEIGHT_CHUNKS):
            for m in range(3):
                weight_copy(e, m, c).start()

    last = st_ref.shape[0] - 1

    def fetch(blk, sl):
        base = base_ref[blk]
        _gather_rows(x_hbm, lambda r: st_ref[jnp.minimum(base + r, last)], MOE_BLOCK, xbuf.at[sl], xsem.at[sl])

    @pl.when(jnp.logical_and(i == 0, n > 0))
    def _():
        fetch(0, 0)
        start_weights(exp_ref[0])

    @pl.when(i + 1 < n)
    def _():
        fetch(i + 1, 1 - slot)

    @pl.when(i < n)
    def _():
        @pl.when(_expert_changed(i, exp_ref))
        def _():
            nxt = nxt_ref[i]
            for c in range(WEIGHT_CHUNKS):
                for m in range(3):
                    weight_copy(exp_ref[i], m, c).wait()
                    caches[m][chunk_rows(m, c), :] = stages[m][chunk_rows(m, c), :].astype(BF16)

                @pl.when(nxt >= 0)
                def _():
                    for m in range(3):
                        weight_copy(nxt, m, c).start()

        _gather_wait(x_hbm, xbuf.at[slot], xsem.at[slot])
        x = xbuf[slot].astype(BF16)
        gate = _dot(x, wg_bf[...])
        up = _dot(x, wu_bf[...])
        hmid = (gate * _sigmoid(gate) * up).astype(BF16)
        o_ref[...] = _dot(hmid, wd_bf[...])

    @pl.when(i >= n)
    def _():
        o_ref[...] = jnp.zeros_like(o_ref)


def _experts(hn, st, plan, w_gate, w_up, w_down, layer):
    block_exp, src_base, next_exp, n_used, _ = plan
    n_blocks = block_exp.shape[0]
    d = hn.shape[1]
    hbm = pl.BlockSpec(memory_space=pl.ANY)
    return pl.pallas_call(
        functools.partial(_expert_kernel, layer=layer),
        grid_spec=pltpu.PrefetchScalarGridSpec(
            num_scalar_prefetch=5,
            grid=(n_blocks,),
            in_specs=[hbm, hbm, hbm, hbm],
            out_specs=pl.BlockSpec((MOE_BLOCK, d), lambda i, *_: (i, 0)),
            scratch_shapes=[pltpu.VMEM((2, MOE_BLOCK, d), F32),
                            pltpu.VMEM((d, D_EXPERT), F32), pltpu.VMEM((d, D_EXPERT), F32),
                            pltpu.VMEM((D_EXPERT, d), F32),
                            pltpu.VMEM((d, D_EXPERT), BF16), pltpu.VMEM((d, D_EXPERT), BF16),
                            pltpu.VMEM((D_EXPERT, d), BF16),
                            pltpu.SemaphoreType.DMA((2,)), pltpu.SemaphoreType.DMA((3, WEIGHT_CHUNKS))],
        ),
        out_shape=jax.ShapeDtypeStruct((n_blocks * MOE_BLOCK, d), F32),
        compiler_params=_cparams(("arbitrary",), VMEM_LIMIT_EXPERT),
        name="experts",
    )(st, block_exp, src_base, next_exp, n_used, hn, w_gate, w_up, w_down)


def _combine_kernel(inv_ref, eid_ref, shift_ref, y_hbm, h_ref, rt_ref, g_ref, o_ref, buf, sem, *, final_norm):
    tb = h_ref.shape[0]
    i = pl.program_id(0)
    slot = i % 2

    def fetch(blk, sl):
        for s in range(TOP_K):
            base = (s * pl.num_programs(0) + blk) * tb
            _gather_rows(y_hbm, lambda r: inv_ref[base + r] + shift_ref[eid_ref[base + r]], tb,
                         buf.at[sl, s], sem.at[sl])

    @pl.when(i == 0)
    def _():
        fetch(0, 0)

    @pl.when(i + 1 < pl.num_programs(0))
    def _():
        fetch(i + 1, 1 - slot)

    for s in range(TOP_K):
        _gather_wait(y_hbm, buf.at[slot, s], sem.at[slot])
    rt = rt_ref[...]
    out = h_ref[...] + (rt[:, 0:1] * buf[slot, 0] + rt[:, 1:2] * buf[slot, 1])
    if final_norm:
        ms = jnp.mean(out * out, axis=-1, keepdims=True)
        out = out * lax.rsqrt(ms + NORM_EPS) * g_ref[...]
    o_ref[...] = out


def _combine(h, y_rows, inv_sm, eid_sm, shift, route, g, final_norm, tb=128):
    t, d = h.shape
    grid_spec = pltpu.PrefetchScalarGridSpec(
        num_scalar_prefetch=3,
        grid=(t // tb,),
        in_specs=[pl.BlockSpec(memory_space=pl.ANY),
                  pl.BlockSpec((tb, d), lambda i, *_: (i, 0)),
                  pl.BlockSpec((tb, 128), lambda i, *_: (i, 0)),
                  pl.BlockSpec((1, d), lambda i, *_: (0, 0))],
        out_specs=pl.BlockSpec((tb, d), lambda i, *_: (i, 0)),
        scratch_shapes=[pltpu.VMEM((2, TOP_K, tb, d), F32), pltpu.SemaphoreType.DMA((2,))],
    )
    return pl.pallas_call(
        functools.partial(_combine_kernel, final_norm=final_norm),
        grid_spec=grid_spec,
        out_shape=jax.ShapeDtypeStruct((t, d), F32),
        compiler_params=_cparams(("arbitrary",)),
        name="moe_combine",
    )(inv_sm, eid_sm, shift, y_rows, h, route, g.reshape(1, d))


def _plan_kernel(counts_ref, bexp_ref, base_ref, nxt_ref, nused_ref, shift_ref):
    n_blocks = bexp_ref.shape[0]

    def per_expert(e, carry):
        start, pad_start = carry
        nb = (counts_ref[e] + MOE_BLOCK - 1) // MOE_BLOCK
        shift_ref[e] = pad_start - start

        def fill(j, c):
            blk = pad_start // MOE_BLOCK + j
            bexp_ref[blk] = e
            base_ref[blk] = start + j * MOE_BLOCK
            return c

        lax.fori_loop(0, nb, fill, 0)
        return start + counts_ref[e], pad_start + nb * MOE_BLOCK

    _, pad_end = lax.fori_loop(0, N_EXPERTS, per_expert, (jnp.int32(0), jnp.int32(0)))
    n_used = pad_end // MOE_BLOCK
    nused_ref[0] = n_used

    def tail(blk, c):
        bexp_ref[blk] = N_EXPERTS - 1
        base_ref[blk] = 0
        nxt_ref[blk] = -1
        return c

    lax.fori_loop(n_used, n_blocks, tail, 0)

    def backward(k, carry):
        nxt, cur = carry
        blk = n_used - 1 - k
        e = bexp_ref[blk]
        nxt = jnp.where(e != cur, cur, nxt)
        nxt_ref[blk] = nxt
        return nxt, e

    lax.fori_loop(0, n_used, backward, (jnp.int32(-1), jnp.int32(-1)))


def _plan(counts, n_blocks):
    smem = pl.BlockSpec(memory_space=pltpu.SMEM)
    i32 = lambda n: jax.ShapeDtypeStruct((n,), jnp.int32)
    return pl.pallas_call(
        _plan_kernel,
        in_specs=[smem],
        out_specs=[smem] * 5,
        out_shape=[i32(n_blocks), i32(n_blocks), i32(n_blocks), i32(1), i32(N_EXPERTS)],
        name="moe_plan",
    )(counts)


def _moe_layer(h, norm_g, wg_r, bg_r, we_r, be_r, w_gate, w_up, w_down, layer, out_g, final_norm):
    t, d = h.shape
    w_r = jnp.concatenate([wg_r, we_r, jnp.zeros((d, 128 - N_GROUPS - N_EXPERTS), F32)], axis=1)
    b_r = jnp.concatenate([bg_r, be_r, jnp.zeros((128 - N_GROUPS - N_EXPERTS,), F32)]).reshape(1, 128)
    hn, route = _ffn_norm_router(h, norm_g, w_r, b_r)
    expert_id = route[:, TOP_K:2 * TOP_K].astype(jnp.int32)
    flat = expert_id.reshape(-1)
    order = jnp.argsort(flat).astype(jnp.int32)
    st = order // TOP_K
    inv_sm = jnp.argsort(order).astype(jnp.int32).reshape(t, TOP_K).T.reshape(-1)
    eid_sm = expert_id.T.reshape(-1)
    counts = jnp.sum((flat[:, None] == jnp.arange(N_EXPERTS, dtype=jnp.int32)[None, :]).astype(jnp.int32), axis=0)
    plan = _plan(counts, (t * TOP_K + MOE_BLOCK - 1) // MOE_BLOCK + N_EXPERTS)
    y_rows = _experts(hn, st, plan, w_gate, w_up, w_down, layer)
    return _combine(h, y_rows, inv_sm, eid_sm, plan[4], route, out_g, final_norm)


_D0 = A_COLS + B_COLS + C_COLS
_W_IN_PIECES = (
    (OFF_BQ, A_COLS, B_COLS), (OFF_CX, A_COLS + B_COLS, C_COLS),
    (OFF_DV, _D0 + 512, 512), (OFF_DGT, _D0 + 1040, 512),
    (OFF_AR, 0, 3 * MIX_W), (OFF_DQ, _D0, 512),
    (OFF_AL, 3 * MIX_W, RWKV_LORA), (OFF_DAL, _D0 + 1024, GLA_LORA),
    (OFF_G, _D0 + D_COLS, 4 * D_MODEL),
)


RELAYOUT_ROWS = 512
_SPECIAL_BLOCK = OFF_AL // RELAYOUT_ROWS


def _w_in_source_rows():
    rows = np.zeros((NP_COLS // RELAYOUT_ROWS,), np.int32)
    for dst, src, width in _W_IN_PIECES:
        if width % RELAYOUT_ROWS == 0:
            for k in range(width // RELAYOUT_ROWS):
                rows[dst // RELAYOUT_ROWS + k] = src + k * RELAYOUT_ROWS
    return rows


def _relayout_kernel(src_ref, w_hbm, extra_hbm, o_ref, stage, sem, *, layer):
    j = pl.program_id(0)
    slot = j % 2

    def start(blk, sl):
        @pl.when(blk == _SPECIAL_BLOCK)
        def _():
            pltpu.make_async_copy(extra_hbm, stage.at[sl], sem.at[sl]).start()

        @pl.when(blk != _SPECIAL_BLOCK)
        def _():
            rows = pl.ds(pl.multiple_of(src_ref[blk], 16), RELAYOUT_ROWS)
            pltpu.make_async_copy(w_hbm.at[layer, rows, :], stage.at[sl], sem.at[sl]).start()

    @pl.when(j == 0)
    def _():
        start(0, 0)

    @pl.when(j + 1 < pl.num_programs(0))
    def _():
        start(j + 1, 1 - slot)

    pltpu.make_async_copy(extra_hbm, stage.at[slot], sem.at[slot]).wait()
    o_ref[...] = stage[slot].astype(o_ref.dtype)


def _permute_w_in(w_in, layer):
    w_t = jnp.swapaxes(w_in, 1, 2)
    d = w_t.shape[2]
    zeros = lambda n: jnp.zeros((n, d), F32)
    extra = jnp.concatenate([w_t[layer, 3 * MIX_W:A_COLS], zeros(OFF_DAL - OFF_AL - RWKV_LORA),
                             w_t[layer, _D0 + 1024:_D0 + 1024 + GLA_LORA],
                             zeros(OFF_G - OFF_DAL - GLA_LORA)], axis=0)
    hbm = pl.BlockSpec(memory_space=pl.ANY)
    return pl.pallas_call(
        functools.partial(_relayout_kernel, layer=layer),
        grid_spec=pltpu.PrefetchScalarGridSpec(
            num_scalar_prefetch=1,
            grid=(NP_COLS // RELAYOUT_ROWS,),
            in_specs=[hbm, hbm],
            out_specs=pl.BlockSpec((RELAYOUT_ROWS, d), lambda j, *_: (j, 0)),
            scratch_shapes=[pltpu.VMEM((2, RELAYOUT_ROWS, d), F32), pltpu.SemaphoreType.DMA((2,))],
        ),
        out_shape=jax.ShapeDtypeStruct((NP_COLS, d), BF16),
        compiler_params=_cparams(("arbitrary",)),
        name="w_in_relayout",
    )(jnp.asarray(_w_in_source_rows()), w_t, extra)


def kernel(x, positions, norm_mix_g, w_in, rwkv_mu, rwkv_w0, rwkv_w_up, rwkv_a0, rwkv_a_up, rwkv_g_up, rwkv_k_k, rwkv_k_a, rwkv_r_k, rwkv_ln_g, rwkv_ln_b, rwkv_w_o, ret_norm_g, ret_w_o, lru_conv_w, lru_conv_b, lru_w_a, lru_b_a, lru_w_x, lru_b_x, lru_lambda, lru_w_o, gla_alpha_up, gla_alpha_b, gla_norm_g, gla_w_o, w_out, norm_ffn_g, router_group_w, router_group_b, router_expert_w, router_expert_b, moe_w_gate, moe_w_up, moe_w_down, final_norm_g):
    b_, s_, d = x.shape
    assert b_ == 1 and d == D_MODEL
    depth = w_in.shape[0]
    h = x.reshape(s_, d)
    pos_b = jnp.broadcast_to(positions.reshape(s_, 1), (s_, 128)).astype(jnp.int32)
    cos_t, sin_t = _rope_table(pos_b)
    for l in range(depth):
        xn = _rmsnorm(h, norm_mix_g[l], BF16)
        p = _matmul_nt(xn, _permute_w_in(w_in, l), 2048, 512, name="in_proj")
        y_a = _rwkv_branch(p, rwkv_mu[l], rwkv_w0[l], rwkv_w_up[l], rwkv_a0[l], rwkv_a_up[l], rwkv_g_up[l],
                           rwkv_k_k[l], rwkv_k_a[l], rwkv_r_k[l], rwkv_ln_g[l], rwkv_ln_b[l])
        y_b = _retention(p, cos_t, sin_t, ret_norm_g[l])
        y_c = _rglru(p, pos_b, lru_conv_w[l], lru_conv_b[l], lru_w_a[l], lru_b_a[l], lru_w_x[l], lru_b_x[l],
                     lru_lambda[l])
        y_d = _gla(p, gla_alpha_up[l], gla_alpha_b[l], gla_norm_g[l])
        merged = _merge((y_a, y_b, y_c, y_d), (rwkv_w_o, ret_w_o, lru_w_o, gla_w_o), l, p)
        h = _matmul_residual(merged, w_out, l, h, 1024, 512)
        last = l == depth - 1
        h = _moe_layer(h, norm_ffn_g[l], router_group_w[l], router_group_b[l], router_expert_w[l],
                       router_expert_b[l], moe_w_gate, moe_w_up, moe_w_down, l,
                       final_norm_g if last else norm_ffn_g[l], last)
    return h.reshape(b_, s_, d)
```

```python
import functools
import math

import jax
import jax.numpy as jnp
import numpy as np
from jax import lax
from jax.experimental import pallas as pl
from jax.experimental.pallas import tpu as pltpu

F32 = jnp.float32
BF16 = jnp.bfloat16
HI = lax.Precision.HIGHEST

D_MODEL = 2048
MIX_W = 512
NORM_EPS = 1e-6
HEAD_DIM = 64
N_HEADS = MIX_W // HEAD_DIM

RWKV_W_LORA, RWKV_A_LORA, RWKV_G_LORA = 32, 32, 96
RWKV_LORA = RWKV_W_LORA + RWKV_A_LORA + RWKV_G_LORA
RWKV_DECAY_SCALE = 0.6065306597126334
RWKV_LN_EPS = 64e-5
RWKV_CHUNK = 64
RWKV_LOCAL_CHUNKS = 2

RET_CHUNK = 128
ROPE_BASE = 10000.0

LRU_BLOCKS = 8
CONV_W = 4
LRU_C = 8.0

GLA_HEADS = 4
GLA_DK = 64
GLA_DV = 128
GLA_LORA = 16
GLA_GATE_NORM = 16.0
GLA_CHUNK = 64
GLA_STEP_CHUNKS = 2

N_GROUPS = 4
EXP_PER_GROUP = 8
N_EXPERTS = N_GROUPS * EXP_PER_GROUP
TOP_K = 2
D_EXPERT = 1024
MOE_BLOCK = 128
WEIGHT_CHUNKS = 4

A_COLS = 3 * MIX_W + RWKV_LORA
B_COLS = 4 * MIX_W
C_COLS = 2 * MIX_W
D_COLS = 2 * (MIX_W // 2) + MIX_W + GLA_LORA + MIX_W

OFF_BQ, OFF_BK, OFF_BV, OFF_BG = 0, 512, 1024, 1536
OFF_CX, OFF_CG = 2048, 2560
OFF_DV, OFF_DGT = 3072, 3584
OFF_AR, OFF_AK, OFF_AV = 4096, 4608, 5120
OFF_DQ, OFF_DK = 5632, 5888
OFF_AL = 6144
OFF_DAL = 6400
OFF_G = 6656
NP_COLS = OFF_G + 4 * D_MODEL

VMEM_LIMIT = 48 * 1024 * 1024
VMEM_LIMIT_EXPERT = 56 * 1024 * 1024


def _cparams(sem, vmem=VMEM_LIMIT):
    return pltpu.CompilerParams(dimension_semantics=sem, vmem_limit_bytes=vmem)


def _dot(a, b, prec=None):
    return jnp.dot(a, b, precision=prec, preferred_element_type=F32)


def _sigmoid(x):
    return 1.0 / (1.0 + jnp.exp(-x))


def _split(x):
    hi = x.astype(BF16)
    return hi, (x - hi.astype(F32)).astype(BF16)


def _dims(form, batched):
    ca, cb = {"nn": (1, 0), "nt": (1, 1), "tn": (0, 0)}[form]
    if batched:
        return (((ca + 1,), (cb + 1,)), ((0,), (0,)))
    return (((ca,), (cb,)), ((), ()))


def _mm(a, b, mode, form="nn"):
    dims = _dims(form, a.ndim == 3)
    if mode == "hi":
        return lax.dot_general(a, b, dims, precision=HI, preferred_element_type=F32)
    d = functools.partial(lax.dot_general, dimension_numbers=dims, preferred_element_type=F32)
    if mode == "x1":
        return d(a.astype(BF16), b.astype(BF16))
    ah, al = _split(a)
    bh, bl = _split(b)
    return d(ah, bh) + (d(ah, bl) + d(al, bh))


def _mm_exact_rhs(a, b, terms):
    b = b.astype(BF16)
    acc = None
    for _ in range(terms):
        piece = a.astype(BF16)
        part = _dot(piece, b)
        acc = part if acc is None else acc + part
        a = a - piece.astype(F32)
    return acc


def _mm_exact_lhs(a, b, terms):
    a = a.astype(BF16)
    acc = None
    for _ in range(terms):
        piece = b.astype(BF16)
        part = _dot(a, piece)
        acc = part if acc is None else acc + part
        b = b - piece.astype(F32)
    return acc


def _heads(x, width=HEAD_DIM):
    return jnp.stack([x[:, h * width:(h + 1) * width] for h in range(x.shape[1] // width)])


def _block_diag_const(n, blk, value):
    r = lax.broadcasted_iota(jnp.int32, (n, n), 0) // blk
    c = lax.broadcasted_iota(jnp.int32, (n, n), 1) // blk
    return jnp.where(r == c, value, 0.0).astype(F32)


def _rmsnorm_kernel(x_ref, g_ref, o_ref):
    x = x_ref[...]
    ms = jnp.mean(x * x, axis=-1, keepdims=True)
    o_ref[...] = (x * lax.rsqrt(ms + NORM_EPS) * g_ref[...]).astype(o_ref.dtype)


def _rmsnorm(x, g, out_dtype, tb=512):
    t, d = x.shape
    return pl.pallas_call(
        _rmsnorm_kernel,
        grid=(t // tb,),
        in_specs=[pl.BlockSpec((tb, d), lambda i: (i, 0)), pl.BlockSpec((1, d), lambda i: (0, 0))],
        out_specs=pl.BlockSpec((tb, d), lambda i: (i, 0)),
        out_shape=jax.ShapeDtypeStruct((t, d), out_dtype),
        compiler_params=_cparams(("parallel",)),
        name="rmsnorm",
    )(x, g.reshape(1, d))


def _mm_nt_kernel(a_ref, b_ref, o_ref):
    o_ref[...] = _mm(a_ref[...], b_ref[...], "x1", "nt").astype(o_ref.dtype)


def _matmul_nt(a, b_t, tm, tn, name):
    m, k = a.shape
    n = b_t.shape[0]
    return pl.pallas_call(
        _mm_nt_kernel,
        grid=(m // tm, n // tn),
        in_specs=[pl.BlockSpec((tm, k), lambda i, j: (i, 0)), pl.BlockSpec((tn, k), lambda i, j: (j, 0))],
        out_specs=pl.BlockSpec((tm, tn), lambda i, j: (i, j)),
        out_shape=jax.ShapeDtypeStruct((m, n), F32),
        compiler_params=_cparams(("parallel", "parallel")),
        name=name,
    )(a, b_t)


def _mm_res_kernel(a_ref, b_ref, r_ref, o_ref, b_bf):
    @pl.when(pl.program_id(1) == 0)
    def _():
        b_bf[...] = b_ref[...].astype(BF16)

    o_ref[...] = r_ref[...] + _dot(a_ref[...], b_bf[...])


def _matmul_residual(a, b, layer, res, tm, tn):
    m, k = a.shape
    n = b.shape[2]
    return pl.pallas_call(
        _mm_res_kernel,
        grid=(n // tn, m // tm),
        in_specs=[pl.BlockSpec((tm, k), lambda j, i: (i, 0)), pl.BlockSpec((None, k, tn), lambda j, i: (layer, 0, j)),
                  pl.BlockSpec((tm, tn), lambda j, i: (i, j))],
        out_specs=pl.BlockSpec((tm, tn), lambda j, i: (i, j)),
        out_shape=jax.ShapeDtypeStruct((m, n), F32),
        scratch_shapes=[pltpu.VMEM((k, tn), BF16)],
        compiler_params=_cparams(("parallel", "arbitrary")),
        name="wout_residual",
    )(a, b, res)


def _shifted(x, tail_ref, sc_ref, width, first):
    tb = x.shape[0]
    sc_ref[0:8, 0:width] = jnp.where(first, 0.0, tail_ref[...])
    sc_ref[8:8 + tb, 0:width] = x
    return sc_ref[7:7 + tb, 0:width]


def _rwkv_prep_kernel(r_ref, k_ref, v_ref, l_ref, rt_ref, kt_ref, vt_ref, lt_ref,
                      mur_ref, muk_ref, muv_ref, mul_ref, wl_ref, b0_ref, kk_ref, ka_ref,
                      ro_ref, lw_ref, ko_ref, vo_ref, kkn_ref, kka_ref, g_ref, sc_ref):
    first = pl.program_id(0) == 0

    def mix(x_ref, t_ref, mu_ref, width):
        x = x_ref[...]
        prev = _shifted(x, t_ref, sc_ref, width, first)
        return x + (prev - x) * mu_ref[...]

    r = mix(r_ref, rt_ref, mur_ref, MIX_W)
    k = mix(k_ref, kt_ref, muk_ref, MIX_W)
    v = mix(v_ref, vt_ref, muv_ref, MIX_W)
    zl = mix(l_ref, lt_ref, mul_ref, 256)
    lane = lax.broadcasted_iota(jnp.int32, zl.shape, 1)
    act = jnp.where(lane < RWKV_W_LORA, jnp.tanh(zl),
                    jnp.where(lane < RWKV_W_LORA + RWKV_A_LORA, zl, _sigmoid(zl)))
    lo = _mm(act, wl_ref[...], "x3") + b0_ref[...]
    lw = -RWKV_DECAY_SCALE * _sigmoid(lo[:, 0:MIX_W])
    a = _sigmoid(lo[:, MIX_W:2 * MIX_W])
    g = lo[:, 2 * MIX_W:3 * MIX_W]
    kk = k * kk_ref[...]
    ss = _mm_exact_rhs(kk * kk, _block_diag_const(MIX_W, HEAD_DIM, 1.0), 2)
    kkn = kk / jnp.maximum(jnp.sqrt(ss), 1e-12)
    ro_ref[...] = r
    lw_ref[...] = lw
    ko_ref[...] = k * (1.0 + (a - 1.0) * ka_ref[...])
    vo_ref[...] = v
    kkn_ref[...] = kkn
    kka_ref[...] = kkn * a
    g_ref[...] = g


def _rwkv_prep(p, mu, w0, w_up, a0, a_up, g_up, k_k, k_a, tb=256):
    t = p.shape[0]
    nb8 = tb // 8

    def blk(width, off):
        return pl.BlockSpec((tb, width), lambda i: (i, off // width))

    def tail(width, off):
        return pl.BlockSpec((8, width), lambda i: (jnp.maximum(i * nb8 - 1, 0), off // width))

    def vec(width):
        return pl.BlockSpec((1, width), lambda i: (0, 0))

    w_lora = jnp.zeros((256, 3 * MIX_W), F32)
    w_lora = w_lora.at[0:32, 0:MIX_W].set(w_up)
    w_lora = w_lora.at[32:64, MIX_W:2 * MIX_W].set(a_up)
    w_lora = w_lora.at[64:160, 2 * MIX_W:].set(g_up)
    b0 = jnp.concatenate([w0, a0, jnp.zeros((MIX_W,), F32)]).reshape(1, 3 * MIX_W)
    mu_l = jnp.concatenate([mu[3 * MIX_W:], jnp.zeros((256 - RWKV_LORA,), F32)]).reshape(1, 256)
    out = jax.ShapeDtypeStruct((t, MIX_W), F32)
    return pl.pallas_call(
        _rwkv_prep_kernel,
        grid=(t // tb,),
        in_specs=[blk(MIX_W, OFF_AR), blk(MIX_W, OFF_AK), blk(MIX_W, OFF_AV), blk(256, OFF_AL),
                  tail(MIX_W, OFF_AR), tail(MIX_W, OFF_AK), tail(MIX_W, OFF_AV), tail(256, OFF_AL),
                  vec(MIX_W), vec(MIX_W), vec(MIX_W), vec(256),
                  pl.BlockSpec((256, 3 * MIX_W), lambda i: (0, 0)), vec(3 * MIX_W), vec(MIX_W), vec(MIX_W)],
        out_specs=[pl.BlockSpec((tb, MIX_W), lambda i: (i, 0))] * 7,
        out_shape=[out] * 7,
        scratch_shapes=[pltpu.VMEM((tb + 8, MIX_W), F32)],
        compiler_params=_cparams(("parallel",)),
        name="rwkv_prep",
    )(p, p, p, p, p, p, p, p,
      mu[0:MIX_W].reshape(1, -1), mu[MIX_W:2 * MIX_W].reshape(1, -1), mu[2 * MIX_W:3 * MIX_W].reshape(1, -1), mu_l,
      w_lora, b0, k_k.reshape(1, -1), k_a.reshape(1, -1))


RW_SC, RW_INV, RW_APPLY, RW_STATE, RW_SCAN = "x1", "x1", "x1", "x1", "x3"


def _rwkv_local_chunk(r, lw, k, v, kk, kka):
    c = r.shape[0]
    row = lax.broadcasted_iota(jnp.int32, (c, c), 0)
    col = lax.broadcasted_iota(jnp.int32, (c, c), 1)
    incl = row >= col
    strict = row > col
    eye = jnp.where(row == col, 1.0, 0.0)
    cum = _mm_exact_lhs(incl.astype(F32), lw, 3)
    last = cum[c - 1:c, :]
    pinv = jnp.exp(-cum)
    dl = jnp.exp(last - cum)
    at = _heads(-kk * jnp.exp(cum - lw))
    rt = _heads(r * jnp.exp(cum))
    bt = _heads(kka * pinv)
    kt = _heads(k * pinv)
    bl = _heads(kka * dl)
    kl = _heads(k * dl)
    v = _heads(v)
    pc = _heads(jnp.exp(last))
    sc = _mm(jnp.concatenate([at, rt], axis=1), jnp.concatenate([bt, kt], axis=1), RW_SC, "nt")
    a_ab = jnp.where(strict, sc[:, 0:c, 0:c], 0.0)
    a_ak = jnp.where(strict, sc[:, 0:c, c:2 * c], 0.0)
    r_b = jnp.where(incl, sc[:, c:2 * c, 0:c], 0.0)
    r_k = jnp.where(incl, sc[:, c:2 * c, c:2 * c], 0.0)
    x = eye + a_ab
    pw = a_ab
    for _ in range(int(math.log2(c)) - 1):
        pw = _mm(pw, pw, RW_INV)
        x = x + _mm(x, pw, RW_INV)
    wu = _mm(x, jnp.concatenate([at, _mm(a_ak, v, RW_APPLY)], axis=2), RW_APPLY)
    lower = jnp.concatenate([jnp.zeros_like(v), v], axis=2)
    qy = _mm(jnp.concatenate([r_b, r_k], axis=2), jnp.concatenate([wu, lower], axis=1), RW_APPLY)
    wb = _mm(wu, bl, RW_STATE, "tn")
    n = wb[:, HEAD_DIM:] + _mm(v, kl, RW_STATE, "tn")
    m = eye * pc + wb[:, 0:HEAD_DIM]
    return rt + qy[:, :, 0:HEAD_DIM], qy[:, :, HEAD_DIM:], m, n


def _rwkv_local_kernel(r_ref, lw_ref, k_ref, v_ref, kk_ref, kka_ref, q_ref, y0_ref, m_ref, n_ref):
    c = RWKV_CHUNK
    outs = [_rwkv_local_chunk(*(ref[j * c:(j + 1) * c, :] for ref in (r_ref, lw_ref, k_ref, v_ref, kk_ref, kka_ref)))
            for j in range(r_ref.shape[0] // c)]
    for j, res in enumerate(outs):
        for o_ref, val in zip((q_ref, y0_ref, m_ref, n_ref), res):
            for h in range(N_HEADS):
                o_ref[j * c:(j + 1) * c, h * HEAD_DIM:(h + 1) * HEAD_DIM] = val[h]


def _rwkv_scan_kernel(q_ref, y0_ref, m_ref, n_ref, r_ref, k_ref, v_ref, g_ref, lng_ref, lnb_ref, rk_ref,
                      o_ref, s_ref, y_sc):
    @pl.when(pl.program_id(0) == 0)
    def _():
        s_ref[...] = jnp.zeros_like(s_ref)

    c = RWKV_CHUNK
    s = s_ref[...]
    for j in range(q_ref.shape[0] // c):
        rows = slice(j * c, (j + 1) * c)
        y = _mm(_heads(q_ref[rows, :]), s, RW_SCAN, "nt") + _heads(y0_ref[rows, :])
        s = _mm(s, _heads(m_ref[rows, :]), RW_SCAN) + _heads(n_ref[rows, :])
        for h in range(N_HEADS):
            y_sc[rows, h * HEAD_DIM:(h + 1) * HEAD_DIM] = y[h]
    s_ref[...] = s
    y = y_sc[...]
    avg = _block_diag_const(MIX_W, HEAD_DIM, 1.0 / HEAD_DIM)
    mean = _mm_exact_rhs(y, avg, 2)
    yc = y - mean
    var = _mm_exact_rhs(yc * yc, avg, 2)
    yn = yc * lax.rsqrt(var + RWKV_LN_EPS) * lng_ref[...] + lnb_ref[...]
    v = v_ref[...]
    bonus = _mm_exact_rhs(r_ref[...] * k_ref[...] * rk_ref[...], _block_diag_const(MIX_W, HEAD_DIM, 1.0), 2) * v
    o_ref[...] = ((yn + bonus) * g_ref[...]).astype(o_ref.dtype)


def _rwkv_recurrence(r, lw, k, v, kkn, kka, g, ln_g, ln_b, r_k, tb=256):
    t = r.shape[0]
    c = RWKV_CHUNK
    cblk = pl.BlockSpec((RWKV_LOCAL_CHUNKS * c, MIX_W), lambda i: (i, 0))
    f = jax.ShapeDtypeStruct((t, MIX_W), F32)
    qm, y0, m, n = pl.pallas_call(
        _rwkv_local_kernel,
        grid=(t // (RWKV_LOCAL_CHUNKS * c),),
        in_specs=[cblk] * 6,
        out_specs=[cblk] * 4,
        out_shape=[f] * 4,
        compiler_params=_cparams(("parallel",)),
        name="rwkv_local",
    )(r, lw, k, v, kkn, kka)
    blk = pl.BlockSpec((tb, MIX_W), lambda i: (i, 0))
    vec = pl.BlockSpec((1, MIX_W), lambda i: (0, 0))
    return pl.pallas_call(
        _rwkv_scan_kernel,
        grid=(t // tb,),
        in_specs=[blk] * 8 + [vec] * 3,
        out_specs=blk,
        out_shape=jax.ShapeDtypeStruct((t, MIX_W), BF16),
        scratch_shapes=[pltpu.VMEM((N_HEADS, HEAD_DIM, HEAD_DIM), F32), pltpu.VMEM((tb, MIX_W), F32)],
        compiler_params=_cparams(("arbitrary",)),
        name="rwkv_scan",
    )(qm, y0, m, n, r, k, v, g, ln_g.reshape(1, -1), ln_b.reshape(1, -1), r_k.reshape(1, -1))


def _rwkv_branch(p, mu, w0, w_up, a0, a_up, g_up, k_k, k_a, r_k, ln_g, ln_b):
    prep = _rwkv_prep(p, mu, w0, w_up, a0, a_up, g_up, k_k, k_a)
    return _rwkv_recurrence(*prep, ln_g, ln_b, r_k.reshape(-1))


def _rope_kernel(pos_ref, inv_ref, cos_ref, sin_ref):
    ang = pos_ref[...].astype(F32) * inv_ref[...]
    lane = lax.broadcasted_iota(jnp.int32, ang.shape, 1)
    cos_ref[...] = jnp.cos(ang)
    sin_ref[...] = jnp.where(lane % HEAD_DIM < HEAD_DIM // 2, -jnp.sin(ang), jnp.sin(ang))


def _rope_table(pos_b, tb=512):
    t = pos_b.shape[0]
    d = HEAD_DIM
    inv = 1.0 / (ROPE_BASE ** (jnp.arange(0, d, 2, dtype=F32) / d))
    inv_b = jnp.tile(inv, 4).reshape(1, 128)
    blk = pl.BlockSpec((tb, 128), lambda i: (i, 0))
    return pl.pallas_call(
        _rope_kernel,
        grid=(t // tb,),
        in_specs=[blk, pl.BlockSpec((1, 128), lambda i: (0, 0))],
        out_specs=[blk, blk],
        out_shape=[jax.ShapeDtypeStruct((t, 128), F32)] * 2,
        compiler_params=_cparams(("parallel",)),
        name="rope_table",
    )(pos_b, inv_b)


def _ret_kernel(q_ref, k_ref, v_ref, g_ref, cos_ref, sin_ref, dec_ref, zeta_ref, xi_ref, cd_ref, ng_ref,
                o_ref, s_ref, o_sc):
    @pl.when(pl.program_id(0) == 0)
    def _():
        s_ref[...] = jnp.zeros_like(s_ref)

    cos = jnp.concatenate([cos_ref[...]] * 4, axis=1)
    sin = jnp.concatenate([sin_ref[...]] * 4, axis=1)
    lane = lax.broadcasted_iota(jnp.int32, cos.shape, 1)
    lower_half = lane % HEAD_DIM < HEAD_DIM // 2

    def rope(x):
        swapped = jnp.where(lower_half, pltpu.roll(x, MIX_W - HEAD_DIM // 2, 1), pltpu.roll(x, HEAD_DIM // 2, 1))
        return x * cos + swapped * sin

    q = _heads(rope(q_ref[...]))
    k = _heads(rope(k_ref[...]) * (HEAD_DIM ** -0.5))
    v = _heads(v_ref[...])
    s = s_ref[...]
    scores = _mm(q, k, "x1", "nt") * dec_ref[...]
    o = _mm(scores, v, "x1") + _mm(q * xi_ref[...], s, "x1")
    s_ref[...] = s * cd_ref[...] + _mm(k * zeta_ref[...], v, "x1", "tn")
    on = o * lax.rsqrt(jnp.mean(o * o, axis=-1, keepdims=True) + NORM_EPS)
    for h in range(N_HEADS):
        o_sc[:, h * HEAD_DIM:(h + 1) * HEAD_DIM] = on[h]
    gt = g_ref[...]
    o_ref[...] = (o_sc[...] * ng_ref[...] * (gt * _sigmoid(gt))).astype(o_ref.dtype)


def _retention(p, cos_t, sin_t, norm_g):
    t = p.shape[0]
    c = RET_CHUNK
    h = N_HEADS
    log_g = jnp.log(1.0 - jnp.exp(jnp.linspace(math.log(1.0 / 32), math.log(1.0 / 512), h, dtype=F32)))
    idx = jnp.arange(c, dtype=F32)
    diff = idx[:, None] - idx[None, :]
    decay = jnp.where(diff >= 0, jnp.exp(log_g[:, None, None] * jnp.maximum(diff, 0.0)[None]), 0.0)
    zeta = jnp.exp(log_g[:, None] * (c - 1 - idx)[None])
    xi = jnp.exp(log_g[:, None] * (idx + 1)[None])
    cdec = jnp.exp(log_g * c)
    zeta_b = jnp.broadcast_to(zeta[:, :, None], (h, c, HEAD_DIM))
    xi_b = jnp.broadcast_to(xi[:, :, None], (h, c, HEAD_DIM))
    cd_b = jnp.broadcast_to(cdec[:, None, None], (h, HEAD_DIM, HEAD_DIM))

    def blk(off):
        return pl.BlockSpec((c, MIX_W), lambda i: (i, off // MIX_W))

    tab = pl.BlockSpec((c, 128), lambda i: (i, 0))

    def const(shape):
        return pl.BlockSpec(shape, lambda i: (0,) * len(shape))

    return pl.pallas_call(
        _ret_kernel,
        grid=(t // c,),
        in_specs=[blk(OFF_BQ), blk(OFF_BK), blk(OFF_BV), blk(OFF_BG), tab, tab,
                  const((h, c, c)), const((h, c, HEAD_DIM)), const((h, c, HEAD_DIM)),
                  const((h, HEAD_DIM, HEAD_DIM)), const((1, MIX_W))],
        out_specs=pl.BlockSpec((c, MIX_W), lambda i: (i, 0)),
        out_shape=jax.ShapeDtypeStruct((t, MIX_W), BF16),
        scratch_shapes=[pltpu.VMEM((h, HEAD_DIM, HEAD_DIM), F32), pltpu.VMEM((c, MIX_W), F32)],
        compiler_params=_cparams(("arbitrary",)),
        name="retention",
    )(p, p, p, p, cos_t, sin_t, decay, zeta_b, xi_b, cd_b, norm_g.reshape(1, -1))


def _lru_kernel(x_ref, xt_ref, gb_ref, pos_ref, cw_ref, cb_ref, wa_ref, ba_ref, wx_ref, bx_ref, lam_ref,
                o_ref, sc_ref, a_sc, b_sc, h_ref):
    tb = x_ref.shape[0]
    first = pl.program_id(0) == 0

    @pl.when(first)
    def _():
        h_ref[...] = jnp.zeros_like(h_ref)

    x = x_ref[...]
    sc_ref[0:8, :] = jnp.where(first, 0.0, xt_ref[...])
    sc_ref[8:8 + tb, :] = x
    cw = cw_ref[...]
    xc = cb_ref[...] + x * cw[CONV_W - 1:CONV_W, :]
    for j in range(1, CONV_W):
        xc = xc + sc_ref[8 - j:8 - j + tb, :] * cw[CONV_W - 1 - j:CONV_W - j, :]
    r = _sigmoid(_mm(xc, wa_ref[...], "x3") + ba_ref[...])
    ig = _sigmoid(_mm(xc, wx_ref[...], "x3") + bx_ref[...])
    nl = -lam_ref[...]
    softplus = jnp.maximum(nl, 0.0) + jnp.log1p(jnp.exp(-jnp.abs(nl)))
    log_a = -LRU_C * r * softplus
    pos = pos_ref[...]
    reset = jnp.concatenate([pos, pos, pos, pos], axis=1) == 0
    a_sc[...] = jnp.where(reset, 0.0, jnp.exp(log_a))
    th = jnp.tanh(log_a)
    b_sc[...] = jnp.where(reset, 1.0, jnp.sqrt(-2.0 * th / (1.0 - th))) * ig * xc

    row = lax.broadcasted_iota(jnp.int32, (8, MIX_W), 0)

    def group(gi, h):
        off = pl.multiple_of(gi * 8, 8)
        a = a_sc[pl.ds(off, 8), :]
        b = b_sc[pl.ds(off, 8), :]
        for d in (1, 2, 4):
            keep = row >= d
            b = jnp.where(keep, a * pltpu.roll(b, d, 0) + b, b)
            a = jnp.where(keep, a * pltpu.roll(a, d, 0), a)
        hs = a * h + b
        b_sc[pl.ds(off, 8), :] = hs
        return hs[7:8, :]

    h_ref[...] = lax.fori_loop(0, tb // 8, group, h_ref[...])
    gb = gb_ref[...]
    gelu = gb * (0.5 * (1.0 + jnp.tanh(math.sqrt(2.0 / math.pi) * (gb + 0.044715 * (gb * gb * gb)))))
    o_ref[...] = (b_sc[...] * gelu).astype(o_ref.dtype)


def _block_diag_weight(w):
    g, n, _ = w.shape
    eye = jnp.eye(g, dtype=w.dtype)
    return (eye[:, None, :, None] * w[:, :, None, :]).reshape(g * n, g * n)


def _rglru(p, pos_b, conv_w, conv_b, w_a, b_a, w_x, b_x, lam, tb=256):
    t = p.shape[0]
    nb8 = tb // 8
    vec = pl.BlockSpec((1, MIX_W), lambda i: (0, 0))
    mat = pl.BlockSpec((MIX_W, MIX_W), lambda i: (0, 0))
    cw8 = jnp.concatenate([conv_w, jnp.zeros((8 - CONV_W, MIX_W), F32)], axis=0)
    return pl.pallas_call(
        _lru_kernel,
        grid=(t // tb,),
        in_specs=[pl.BlockSpec((tb, MIX_W), lambda i: (i, OFF_CX // MIX_W)),
                  pl.BlockSpec((8, MIX_W), lambda i: (jnp.maximum(i * nb8 - 1, 0), OFF_CX // MIX_W)),
                  pl.BlockSpec((tb, MIX_W), lambda i: (i, OFF_CG // MIX_W)),
                  pl.BlockSpec((tb, 128), lambda i: (i, 0)),
                  pl.BlockSpec((8, MIX_W), lambda i: (0, 0)), vec, mat, vec, mat, vec, vec],
        out_specs=pl.BlockSpec((tb, MIX_W), lambda i: (i, 0)),
        out_shape=jax.ShapeDtypeStruct((t, MIX_W), BF16),
        scratch_shapes=[pltpu.VMEM((tb + 8, MIX_W), F32), pltpu.VMEM((tb, MIX_W), F32),
                        pltpu.VMEM((tb, MIX_W), F32), pltpu.VMEM((1, MIX_W), F32)],
        compiler_params=_cparams(("arbitrary",)),
        name="rglru",
    )(p, p, p, pos_b, cw8, conv_b.reshape(1, -1), _block_diag_weight(w_a), b_a.reshape(1, -1),
      _block_diag_weight(w_x), b_x.reshape(1, -1), lam.reshape(1, -1))


def _gla_kernel(v_ref, gt_ref, q_ref, k_ref, al_ref, aup_ref, ab_ref, ng_ref, o_ref, s_ref):
    @pl.when(pl.program_id(0) == 0)
    def _():
        s_ref[...] = jnp.zeros_like(s_ref)

    c = GLA_CHUNK
    row = lax.broadcasted_iota(jnp.int32, (c, c), 0)
    col = lax.broadcasted_iota(jnp.int32, (c, c), 1)
    causal = row >= col
    pre = _mm(al_ref[...], aup_ref[...], "x3") + ab_ref[...]
    log_a = (jnp.minimum(pre, 0.0) - jnp.log1p(jnp.exp(-jnp.abs(pre)))) / GLA_GATE_NORM
    s = s_ref[...]
    outs = []
    for j in range(q_ref.shape[0] // c):
        rows = slice(j * c, (j + 1) * c)
        bcum = _mm_exact_lhs(causal.astype(F32), log_a[rows], 3)
        blast = bcum[c - 1:c, :]
        k = k_ref[rows, :]
        q_e = _heads(q_ref[rows, :] * (GLA_DK ** -0.5) * jnp.exp(bcum))
        k_e = _heads(k * jnp.exp(-bcum))
        k_l = _heads(k * jnp.exp(blast - bcum))
        dec = _heads(jnp.exp(blast))
        v = _heads(v_ref[rows, :], GLA_DV)
        scores = jnp.where(causal, _mm(q_e, k_e, "x1", "nt"), 0.0)
        o = _mm(scores, v, "x1") + _mm(q_e, s, "x1", "nt")
        s = s * dec + _mm(v, k_l, "x1", "tn")
        on = o * lax.rsqrt(jnp.mean(o * o, axis=-1, keepdims=True) + NORM_EPS)
        outs.append(jnp.concatenate([on[h] for h in range(GLA_HEADS)], axis=1))
    s_ref[...] = s
    on = jnp.concatenate(outs, axis=0) * ng_ref[...]
    gt = gt_ref[...]
    o_ref[...] = (on * (gt * _sigmoid(gt))).astype(o_ref.dtype)


def _gla(p, alpha_up, alpha_b, norm_g):
    t = p.shape[0]
    c = GLA_STEP_CHUNKS * GLA_CHUNK
    hk = MIX_W // 2
    aup = jnp.concatenate([alpha_up, jnp.zeros((128 - GLA_LORA, hk), F32)], axis=0)

    def blk(width, off):
        return pl.BlockSpec((c, width), lambda i: (i, off // width))

    return pl.pallas_call(
        _gla_kernel,
        grid=(t // c,),
        in_specs=[blk(MIX_W, OFF_DV), blk(MIX_W, OFF_DGT), blk(hk, OFF_DQ), blk(hk, OFF_DK), blk(128, OFF_DAL),
                  pl.BlockSpec((128, hk), lambda i: (0, 0)), pl.BlockSpec((1, hk), lambda i: (0, 0)),
                  pl.BlockSpec((1, MIX_W), lambda i: (0, 0))],
        out_specs=pl.BlockSpec((c, MIX_W), lambda i: (i, 0)),
        out_shape=jax.ShapeDtypeStruct((t, MIX_W), BF16),
        scratch_shapes=[pltpu.VMEM((GLA_HEADS, GLA_DV, GLA_DK), F32)],
        compiler_params=_cparams(("arbitrary",)),
        name="gla",
    )(p, p, p, p, p, aup, alpha_b.reshape(1, -1), norm_g.reshape(1, -1))


def _merge_kernel(ya_ref, yb_ref, yc_ref, yd_ref, wa_ref, wb_ref, wc_ref, wd_ref,
                  ga_ref, gb_ref, gc_ref, gd_ref, o_ref, w_bf):
    @pl.when(pl.program_id(1) == 0)
    def _():
        for kk, w_ref in enumerate((wa_ref, wb_ref, wc_ref, wd_ref)):
            w_bf[kk] = w_ref[0].astype(BF16)

    acc = _sigmoid(ga_ref[...]) * _dot(ya_ref[...], w_bf[0])
    acc += _sigmoid(gb_ref[...]) * _dot(yb_ref[...], w_bf[1])
    acc += _sigmoid(gc_ref[...]) * _dot(yc_ref[...], w_bf[2])
    acc += _sigmoid(gd_ref[...]) * _dot(yd_ref[...], w_bf[3])
    o_ref[...] = acc.astype(o_ref.dtype)


def _merge(ys, ws, layer, p, tm=512, tn=512):
    t = p.shape[0]
    yspec = pl.BlockSpec((tm, MIX_W), lambda j, i: (i, 0))
    wspec = pl.BlockSpec((1, MIX_W, tn), lambda j, i: (layer, 0, j))

    def gspec(kk):
        return pl.BlockSpec((tm, tn), lambda j, i: (i, (OFF_G + kk * D_MODEL) // tn + j))

    return pl.pallas_call(
        _merge_kernel,
        grid=(D_MODEL // tn, t // tm),
        in_specs=[yspec] * 4 + [wspec] * 4 + [gspec(kk) for kk in range(4)],
        out_specs=pl.BlockSpec((tm, tn), lambda j, i: (i, j)),
        out_shape=jax.ShapeDtypeStruct((t, D_MODEL), BF16),
        scratch_shapes=[pltpu.VMEM((4, MIX_W, tn), BF16)],
        compiler_params=_cparams(("parallel", "arbitrary")),
        name="merge",
    )(*ys, *ws, p, p, p, p)


def _ffn_norm_kernel(h_ref, g_ref, wr_ref, br_ref, hn_ref, rt_ref):
    x = h_ref[...]
    ms = jnp.mean(x * x, axis=-1, keepdims=True)
    hn = x * lax.rsqrt(ms + NORM_EPS) * g_ref[...]
    hn_ref[...] = hn
    z = _mm(hn, wr_ref[...], "x3") + br_ref[...]
    lane = lax.broadcasted_iota(jnp.int32, z.shape, 1)
    neg = jnp.float32(-1e30)

    def first_argmax(v):
        m = jnp.max(v, axis=-1, keepdims=True)
        return m, jnp.min(jnp.where(v == m, lane, 128), axis=-1, keepdims=True)

    is_group = lane < N_GROUPS
    zg = jnp.where(is_group, z, neg)
    mg, g_idx = first_argmax(zg)
    pg_top = 1.0 / jnp.sum(jnp.where(is_group, jnp.exp(zg - mg), 0.0), axis=-1, keepdims=True)
    lo = N_GROUPS + g_idx * EXP_PER_GROUP
    in_group = jnp.logical_and(lane >= lo, lane < lo + EXP_PER_GROUP)
    ze = jnp.where(in_group, z, neg)
    m1, i1 = first_argmax(ze)
    se = jnp.sum(jnp.where(in_group, jnp.exp(ze - m1), 0.0), axis=-1, keepdims=True)
    m2, i2 = first_argmax(jnp.where(lane == i1, neg, ze))
    p1 = 1.0 / se
    p2 = jnp.exp(m2 - m1) / se
    tot = p1 + p2
    rt_ref[...] = jnp.where(lane == 0, pg_top * (p1 / tot),
                            jnp.where(lane == 1, pg_top * (p2 / tot),
                                      jnp.where(lane == 2, (i1 - N_GROUPS).astype(F32),
                                                jnp.where(lane == 3, (i2 - N_GROUPS).astype(F32), 0.0))))


def _ffn_norm_router(h, g, w_r, b_r, tb=256):
    t, d = h.shape
    return pl.pallas_call(
        _ffn_norm_kernel,
        grid=(t // tb,),
        in_specs=[pl.BlockSpec((tb, d), lambda i: (i, 0)), pl.BlockSpec((1, d), lambda i: (0, 0)),
                  pl.BlockSpec((d, 128), lambda i: (0, 0)), pl.BlockSpec((1, 128), lambda i: (0, 0))],
        out_specs=[pl.BlockSpec((tb, d), lambda i: (i, 0)), pl.BlockSpec((tb, 128), lambda i: (i, 0))],
        out_shape=[jax.ShapeDtypeStruct((t, d), F32), jax.ShapeDtypeStruct((t, 128), F32)],
        compiler_params=_cparams(("parallel",)),
        name="ffn_norm_router",
    )(h, g.reshape(1, d), w_r, b_r)


def _row_copy(src_hbm, idx, dst, r, sem):
    return pltpu.make_async_copy(src_hbm.at[pl.ds(idx, 1), :], dst.at[pl.ds(r, 1), :], sem)


def _expert_changed(i, exp_ref):
    return jnp.logical_or(i == 0, exp_ref[i] != exp_ref[jnp.maximum(i - 1, 0)])


def _gather_rows(src_hbm, row_of, n, dst, sem):
    def start(r, carry):
        _row_copy(src_hbm, row_of(r), dst, r, sem).start()
        return carry

    lax.fori_loop(0, n, start, 0, unroll=8)


def _gather_wait(src_hbm, dst, sem):
    pltpu.make_async_copy(src_hbm.at[pl.ds(0, dst.shape[0]), :], dst, sem).wait()


def _expert_kernel(st_ref, exp_ref, base_ref, nxt_ref, nblk_ref, x_hbm, wg_hbm, wu_hbm, wd_hbm, o_ref,
                   xbuf, stage_g, stage_u, stage_d, wg_bf, wu_bf, wd_bf, xsem, wsem, *, layer):
    i = pl.program_id(0)
    n = nblk_ref[0]
    slot = i % 2

    stages = (stage_g, stage_u, stage_d)
    caches = (wg_bf, wu_bf, wd_bf)

    def chunk_rows(m, c):
        rows = stages[m].shape[0] // WEIGHT_CHUNKS
        return pl.ds(c * rows, rows)

    def weight_copy(e, m, c):
        src = (wg_hbm, wu_hbm, wd_hbm)[m]
        return pltpu.make_async_copy(src.at[layer, e, chunk_rows(m, c), :], stages[m].at[chunk_rows(m, c), :],
                                     wsem.at[m, c])

    def start_weights(e):
        for c in range(WEIGHT_CHUNKS):
            for m in range(3):
                weight_copy(e, m, c).start(priority=1)

    last = st_ref.shape[0] - 1

    def fetch(blk, sl):
        base = base_ref[blk]
        _gather_rows(x_hbm, lambda r: st_ref[jnp.minimum(base + r, last)], MOE_BLOCK, xbuf.at[sl], xsem.at[sl])

    @pl.when(jnp.logical_and(i == 0, n > 0))
    def _():
        fetch(0, 0)
        start_weights(exp_ref[0])

    @pl.when(i + 1 < n)
    def _():
        fetch(i + 1, 1 - slot)

    @pl.when(i < n)
    def _():
        @pl.when(_expert_changed(i, exp_ref))
        def _():
            nxt = nxt_ref[i]
            for c in range(WEIGHT_CHUNKS):
                for m in range(3):
                    weight_copy(exp_ref[i], m, c).wait()
                    caches[m][chunk_rows(m, c), :] = stages[m][chunk_rows(m, c), :].astype(BF16)

                @pl.when(nxt >= 0)
                def _():
                    for m in range(3):
                        weight_copy(nxt, m, c).start(priority=1)

        _gather_wait(x_hbm, xbuf.at[slot], xsem.at[slot])
        x = xbuf[slot].astype(BF16)
        gate = _dot(x, wg_bf[...])
        up = _dot(x, wu_bf[...])
        hmid = (gate * _sigmoid(gate) * up).astype(BF16)
        o_ref[...] = _dot(hmid, wd_bf[...])

    @pl.when(i >= n)
    def _():
        o_ref[...] = jnp.zeros_like(o_ref)


def _experts(hn, st, plan, w_gate, w_up, w_down, layer):
    block_exp, src_base, next_exp, n_used, _ = plan
    n_blocks = block_exp.shape[0]
    d = hn.shape[1]
    hbm = pl.BlockSpec(memory_space=pl.ANY)
    return pl.pallas_call(
        functools.partial(_expert_kernel, layer=layer),
        grid_spec=pltpu.PrefetchScalarGridSpec(
            num_scalar_prefetch=5,
            grid=(n_blocks,),
            in_specs=[hbm, hbm, hbm, hbm],
            out_specs=pl.BlockSpec((MOE_BLOCK, d), lambda i, *_: (i, 0)),
            scratch_shapes=[pltpu.VMEM((2, MOE_BLOCK, d), F32),
                            pltpu.VMEM((d, D_EXPERT), F32), pltpu.VMEM((d, D_EXPERT), F32),
                            pltpu.VMEM((D_EXPERT, d), F32),
                            pltpu.VMEM((d, D_EXPERT), BF16), pltpu.VMEM((d, D_EXPERT), BF16),
                            pltpu.VMEM((D_EXPERT, d), BF16),
                            pltpu.SemaphoreType.DMA((2,)), pltpu.SemaphoreType.DMA((3, WEIGHT_CHUNKS))],
        ),
        out_shape=jax.ShapeDtypeStruct((n_blocks * MOE_BLOCK, d), F32),
        compiler_params=_cparams(("arbitrary",), VMEM_LIMIT_EXPERT),
        name="experts",
    )(st, block_exp, src_base, next_exp, n_used, hn, w_gate, w_up, w_down)


def _combine_kernel(inv_ref, eid_ref, shift_ref, y_hbm, h_ref, rt_ref, g_ref, o_ref, buf, sem, *, final_norm):
    tb = h_ref.shape[0]
    i = pl.program_id(0)
    slot = i % 2

    def fetch(blk, sl):
        for s in range(TOP_K):
            base = (s * pl.num_programs(0) + blk) * tb
            _gather_rows(y_hbm, lambda r: inv_ref[base + r] + shift_ref[eid_ref[base + r]], tb,
                         buf.at[sl, s], sem.at[sl])

    @pl.when(i == 0)
    def _():
        fetch(0, 0)

    @pl.when(i + 1 < pl.num_programs(0))
    def _():
        fetch(i + 1, 1 - slot)

    for s in range(TOP_K):
        _gather_wait(y_hbm, buf.at[slot, s], sem.at[slot])
    rt = rt_ref[...]
    out = h_ref[...] + (rt[:, 0:1] * buf[slot, 0] + rt[:, 1:2] * buf[slot, 1])
    if final_norm:
        ms = jnp.mean(out * out, axis=-1, keepdims=True)
        out = out * lax.rsqrt(ms + NORM_EPS) * g_ref[...]
    o_ref[...] = out


def _combine(h, y_rows, inv_sm, eid_sm, shift, route, g, final_norm, tb=128):
    t, d = h.shape
    grid_spec = pltpu.PrefetchScalarGridSpec(
        num_scalar_prefetch=3,
        grid=(t // tb,),
        in_specs=[pl.BlockSpec(memory_space=pl.ANY),
                  pl.BlockSpec((tb, d), lambda i, *_: (i, 0)),
                  pl.BlockSpec((tb, 128), lambda i, *_: (i, 0)),
                  pl.BlockSpec((1, d), lambda i, *_: (0, 0))],
        out_specs=pl.BlockSpec((tb, d), lambda i, *_: (i, 0)),
        scratch_shapes=[pltpu.VMEM((2, TOP_K, tb, d), F32), pltpu.SemaphoreType.DMA((2,))],
    )
    return pl.pallas_call(
        functools.partial(_combine_kernel, final_norm=final_norm),
        grid_spec=grid_spec,
        out_shape=jax.ShapeDtypeStruct((t, d), F32),
        compiler_params=_cparams(("arbitrary",)),
        name="moe_combine",
    )(inv_sm, eid_sm, shift, y_rows, h, route, g.reshape(1, d))


def _plan_kernel(counts_ref, bexp_ref, base_ref, nxt_ref, nused_ref, shift_ref):
    n_blocks = bexp_ref.shape[0]

    def per_expert(e, carry):
        start, pad_start = carry
        nb = (counts_ref[e] + MOE_BLOCK - 1) // MOE_BLOCK
        shift_ref[e] = pad_start - start

        def fill(j, c):
            blk = pad_start // MOE_BLOCK + j
            bexp_ref[blk] = e
            base_ref[blk] = start + j * MOE_BLOCK
            return c

        lax.fori_loop(0, nb, fill, 0)
        return start + counts_ref[e], pad_start + nb * MOE_BLOCK

    _, pad_end = lax.fori_loop(0, N_EXPERTS, per_expert, (jnp.int32(0), jnp.int32(0)))
    n_used = pad_end // MOE_BLOCK
    nused_ref[0] = n_used

    def tail(blk, c):
        bexp_ref[blk] = N_EXPERTS - 1
        base_ref[blk] = 0
        nxt_ref[blk] = -1
        return c

    lax.fori_loop(n_used, n_blocks, tail, 0)

    def backward(k, carry):
        nxt, cur = carry
        blk = n_used - 1 - k
        e = bexp_ref[blk]
        nxt = jnp.where(e != cur, cur, nxt)
        nxt_ref[blk] = nxt
        return nxt, e

    lax.fori_loop(0, n_used, backward, (jnp.int32(-1), jnp.int32(-1)))


def _plan(counts, n_blocks):
    smem = pl.BlockSpec(memory_space=pltpu.SMEM)
    i32 = lambda n: jax.ShapeDtypeStruct((n,), jnp.int32)
    return pl.pallas_call(
        _plan_kernel,
        in_specs=[smem],
        out_specs=[smem] * 5,
        out_shape=[i32(n_blocks), i32(n_blocks), i32(n_blocks), i32(1), i32(N_EXPERTS)],
        name="moe_plan",
    )(counts)


def _moe_layer(h, norm_g, wg_r, bg_r, we_r, be_r, w_gate, w_up, w_down, layer, out_g, final_norm):
    t, d = h.shape
    w_r = jnp.concatenate([wg_r, we_r, jnp.zeros((d, 128 - N_GROUPS - N_EXPERTS), F32)], axis=1)
    b_r = jnp.concatenate([bg_r, be_r, jnp.zeros((128 - N_GROUPS - N_EXPERTS,), F32)]).reshape(1, 128)
    hn, route = _ffn_norm_router(h, norm_g, w_r, b_r)
    expert_id = route[:, TOP_K:2 * TOP_K].astype(jnp.int32)
    flat = expert_id.reshape(-1)
    order = jnp.argsort(flat).astype(jnp.int32)
    st = order // TOP_K
    inv_sm = jnp.argsort(order).astype(jnp.int32).reshape(t, TOP_K).T.reshape(-1)
    eid_sm = expert_id.T.reshape(-1)
    counts = jnp.sum((flat[:, None] == jnp.arange(N_EXPERTS, dtype=jnp.int32)[None, :]).astype(jnp.int32), axis=0)
    plan = _plan(counts, (t * TOP_K + MOE_BLOCK - 1) // MOE_BLOCK + N_EXPERTS)
    y_rows = _experts(hn, st, plan, w_gate, w_up, w_down, layer)
    return _combine(h, y_rows, inv_sm, eid_sm, plan[4], route, out_g, final_norm)


_D0 = A_COLS + B_COLS + C_COLS
_W_IN_PIECES = (
    (OFF_BQ, A_COLS, B_COLS), (OFF_CX, A_COLS + B_COLS, C_COLS),
    (OFF_DV, _D0 + 512, 512), (OFF_DGT, _D0 + 1040, 512),
    (OFF_AR, 0, 3 * MIX_W), (OFF_DQ, _D0, 512),
    (OFF_AL, 3 * MIX_W, RWKV_LORA), (OFF_DAL, _D0 + 1024, GLA_LORA),
    (OFF_G, _D0 + D_COLS, 4 * D_MODEL),
)


RELAYOUT_ROWS = 512
_SPECIAL_BLOCK = OFF_AL // RELAYOUT_ROWS


def _w_in_source_rows():
    rows = np.zeros((NP_COLS // RELAYOUT_ROWS,), np.int32)
    for dst, src, width in _W_IN_PIECES:
        if width % RELAYOUT_ROWS == 0:
            for k in range(width // RELAYOUT_ROWS):
                rows[dst // RELAYOUT_ROWS + k] = src + k * RELAYOUT_ROWS
    return rows


def _relayout_kernel(src_ref, w_hbm, extra_hbm, o_ref, stage, sem, *, layer):
    j = pl.program_id(0)
    slot = j % 2

    def start(blk, sl):
        @pl.when(blk == _SPECIAL_BLOCK)
        def _():
            pltpu.make_async_copy(extra_hbm, stage.at[sl], sem.at[sl]).start()

        @pl.when(blk != _SPECIAL_BLOCK)
        def _():
            rows = pl.ds(pl.multiple_of(src_ref[blk], 16), RELAYOUT_ROWS)
            pltpu.make_async_copy(w_hbm.at[layer, rows, :], stage.at[sl], sem.at[sl]).start()

    @pl.when(j == 0)
    def _():
        start(0, 0)

    @pl.when(j + 1 < pl.num_programs(0))
    def _():
        start(j + 1, 1 - slot)

    pltpu.make_async_copy(extra_hbm, stage.at[slot], sem.at[slot]).wait()
    o_ref[...] = stage[slot].astype(o_ref.dtype)


def _permute_w_in(w_in, layer):
    w_t = jnp.swapaxes(w_in, 1, 2)
    d = w_t.shape[2]
    zeros = lambda n: jnp.zeros((n, d), F32)
    extra = jnp.concatenate([w_t[layer, 3 * MIX_W:A_COLS], zeros(OFF_DAL - OFF_AL - RWKV_LORA),
                             w_t[layer, _D0 + 1024:_D0 + 1024 + GLA_LORA],
                             zeros(OFF_G - OFF_DAL - GLA_LORA)], axis=0)
    hbm = pl.BlockSpec(memory_space=pl.ANY)
    return pl.pallas_call(
        functools.partial(_relayout_kernel, layer=layer),
        grid_spec=pltpu.PrefetchScalarGridSpec(
            num_scalar_prefetch=1,
            grid=(NP_COLS // RELAYOUT_ROWS,),
            in_specs=[hbm, hbm],
            out_specs=pl.BlockSpec((RELAYOUT_ROWS, d), lambda j, *_: (j, 0)),
            scratch_shapes=[pltpu.VMEM((2, RELAYOUT_ROWS, d), F32), pltpu.SemaphoreType.DMA((2,))],
        ),
        out_shape=jax.ShapeDtypeStruct((NP_COLS, d), BF16),
        compiler_params=_cparams(("arbitrary",)),
        name="w_in_relayout",
    )(jnp.asarray(_w_in_source_rows()), w_t, extra)


def kernel(x, positions, norm_mix_g, w_in, rwkv_mu, rwkv_w0, rwkv_w_up, rwkv_a0, rwkv_a_up, rwkv_g_up, rwkv_k_k, rwkv_k_a, rwkv_r_k, rwkv_ln_g, rwkv_ln_b, rwkv_w_o, ret_norm_g, ret_w_o, lru_conv_w, lru_conv_b, lru_w_a, lru_b_a, lru_w_x, lru_b_x, lru_lambda, lru_w_o, gla_alpha_up, gla_alpha_b, gla_norm_g, gla_w_o, w_out, norm_ffn_g, router_group_w, router_group_b, router_expert_w, router_expert_b, moe_w_gate, moe_w_up, moe_w_down, final_norm_g):
    b_, s_, d = x.shape
    assert b_ == 1 and d == D_MODEL
    depth = w_in.shape[0]
    h = x.reshape(s_, d)
    pos_b = jnp.broadcast_to(positions.reshape(s_, 1), (s_, 128)).astype(jnp.int32)
    cos_t, sin_t = _rope_table(pos_b)
    for l in range(depth):
        xn = _rmsnorm(h, norm_mix_g[l], BF16)
        p = _matmul_nt(xn, _permute_w_in(w_in, l), 2048, 512, name="in_proj")
        y_a = _rwkv_branch(p, rwkv_mu[l], rwkv_w0[l], rwkv_w_up[l], rwkv_a0[l], rwkv_a_up[l], rwkv_g_up[l],
                           rwkv_k_k[l], rwkv_k_a[l], rwkv_r_k[l], rwkv_ln_g[l], rwkv_ln_b[l])
        y_b = _retention(p, cos_t, sin_t, ret_norm_g[l])
        y_c = _rglru(p, pos_b, lru_conv_w[l], lru_conv_b[l], lru_w_a[l], lru_b_a[l], lru_w_x[l], lru_b_x[l],
                     lru_lambda[l])
        y_d = _gla(p, gla_alpha_up[l], gla_alpha_b[l], gla_norm_g[l])
        merged = _merge((y_a, y_b, y_c, y_d), (rwkv_w_o, ret_w_o, lru_w_o, gla_w_o), l, p)
        h = _matmul_residual(merged, w_out, l, h, 1024, 512)
        last = l == depth - 1
        h = _moe_layer(h, norm_ffn_g[l], router_group_w[l], router_group_b[l], router_expert_w[l],
                       router_expert_b[l], moe_w_gate, moe_w_up, moe_w_down, l,
                       final_norm_g if last else norm_ffn_g[l], last)
    return h.reshape(b_, s_, d)
```

```python
import functools
import math

import jax
import jax.numpy as jnp
import numpy as np
from jax import lax
from jax.experimental import pallas as pl
from jax.experimental.pallas import tpu as pltpu

F32 = jnp.float32
BF16 = jnp.bfloat16
HI = lax.Precision.HIGHEST

D_MODEL = 2048
MIX_W = 512
NORM_EPS = 1e-6
HEAD_DIM = 64
N_HEADS = MIX_W // HEAD_DIM

RWKV_W_LORA, RWKV_A_LORA, RWKV_G_LORA = 32, 32, 96
RWKV_LORA = RWKV_W_LORA + RWKV_A_LORA + RWKV_G_LORA
RWKV_DECAY_SCALE = 0.6065306597126334
RWKV_LN_EPS = 64e-5
RWKV_CHUNK = 64
RWKV_LOCAL_CHUNKS = 4

RET_CHUNK = 128
ROPE_BASE = 10000.0

LRU_BLOCKS = 8
CONV_W = 4
LRU_C = 8.0

GLA_HEADS = 4
GLA_DK = 64
GLA_DV = 128
GLA_LORA = 16
GLA_GATE_NORM = 16.0
GLA_CHUNK = 64
GLA_STEP_CHUNKS = 2

N_GROUPS = 4
EXP_PER_GROUP = 8
N_EXPERTS = N_GROUPS * EXP_PER_GROUP
TOP_K = 2
D_EXPERT = 1024
MOE_BLOCK = 128
WEIGHT_CHUNKS = 4

A_COLS = 3 * MIX_W + RWKV_LORA
B_COLS = 4 * MIX_W
C_COLS = 2 * MIX_W
D_COLS = 2 * (MIX_W // 2) + MIX_W + GLA_LORA + MIX_W

OFF_BQ, OFF_BK, OFF_BV, OFF_BG = 0, 512, 1024, 1536
OFF_CX, OFF_CG = 2048, 2560
OFF_DV, OFF_DGT = 3072, 3584
OFF_AR, OFF_AK, OFF_AV = 4096, 4608, 5120
OFF_DQ, OFF_DK = 5632, 5888
OFF_AL = 6144
OFF_DAL = 6400
OFF_G = 6656
NP_COLS = OFF_G + 4 * D_MODEL

VMEM_LIMIT = 48 * 1024 * 1024
VMEM_LIMIT_EXPERT = 56 * 1024 * 1024


def _cparams(sem, vmem=VMEM_LIMIT):
    return pltpu.CompilerParams(dimension_semantics=sem, vmem_limit_bytes=vmem)


def _dot(a, b, prec=None):
    return jnp.dot(a, b, precision=prec, preferred_element_type=F32)


def _sigmoid(x):
    return 1.0 / (1.0 + jnp.exp(-x))


def _split(x):
    hi = x.astype(BF16)
    return hi, (x - hi.astype(F32)).astype(BF16)


def _dims(form, batched):
    ca, cb = {"nn": (1, 0), "nt": (1, 1), "tn": (0, 0)}[form]
    if batched:
        return (((ca + 1,), (cb + 1,)), ((0,), (0,)))
    return (((ca,), (cb,)), ((), ()))


def _mm(a, b, mode, form="nn"):
    dims = _dims(form, a.ndim == 3)
    if mode == "hi":
        return lax.dot_general(a, b, dims, precision=HI, preferred_element_type=F32)
    d = functools.partial(lax.dot_general, dimension_numbers=dims, preferred_element_type=F32)
    if mode == "x1":
        return d(a.astype(BF16), b.astype(BF16))
    ah, al = _split(a)
    bh, bl = _split(b)
    return d(ah, bh) + (d(ah, bl) + d(al, bh))


def _mm_exact_rhs(a, b, terms):
    b = b.astype(BF16)
    acc = None
    for _ in range(terms):
        piece = a.astype(BF16)
        part = _dot(piece, b)
        acc = part if acc is None else acc + part
        a = a - piece.astype(F32)
    return acc


def _mm_exact_lhs(a, b, terms):
    a = a.astype(BF16)
    acc = None
    for _ in range(terms):
        piece = b.astype(BF16)
        part = _dot(a, piece)
        acc = part if acc is None else acc + part
        b = b - piece.astype(F32)
    return acc


def _heads(x, width=HEAD_DIM):
    return jnp.stack([x[:, h * width:(h + 1) * width] for h in range(x.shape[1] // width)])


def _block_diag_const(n, blk, value):
    r = lax.broadcasted_iota(jnp.int32, (n, n), 0) // blk
    c = lax.broadcasted_iota(jnp.int32, (n, n), 1) // blk
    return jnp.where(r == c, value, 0.0).astype(F32)


def _rmsnorm_kernel(x_ref, g_ref, o_ref):
    x = x_ref[...]
    ms = jnp.mean(x * x, axis=-1, keepdims=True)
    o_ref[...] = (x * lax.rsqrt(ms + NORM_EPS) * g_ref[...]).astype(o_ref.dtype)


def _rmsnorm(x, g, out_dtype, tb=512):
    t, d = x.shape
    return pl.pallas_call(
        _rmsnorm_kernel,
        grid=(t // tb,),
        in_specs=[pl.BlockSpec((tb, d), lambda i: (i, 0)), pl.BlockSpec((1, d), lambda i: (0, 0))],
        out_specs=pl.BlockSpec((tb, d), lambda i: (i, 0)),
        out_shape=jax.ShapeDtypeStruct((t, d), out_dtype),
        compiler_params=_cparams(("parallel",)),
        name="rmsnorm",
    )(x, g.reshape(1, d))


def _mm_nt_kernel(a_ref, b_ref, o_ref):
    o_ref[...] = _mm(a_ref[...], b_ref[...], "x1", "nt").astype(o_ref.dtype)


def _matmul_nt(a, b_t, row0, n, tm, tn, out_dtype, name):
    m, k = a.shape
    return pl.pallas_call(
        _mm_nt_kernel,
        grid=(m // tm, n // tn),
        in_specs=[pl.BlockSpec((tm, k), lambda i, j: (i, 0)), pl.BlockSpec((tn, k), lambda i, j: (row0 // tn + j, 0))],
        out_specs=pl.BlockSpec((tm, tn), lambda i, j: (i, j)),
        out_shape=jax.ShapeDtypeStruct((m, n), out_dtype),
        compiler_params=_cparams(("parallel", "parallel")),
        name=name,
    )(a, b_t)


def _mm_res_kernel(a_ref, b_ref, r_ref, o_ref, b_bf):
    @pl.when(pl.program_id(1) == 0)
    def _():
        b_bf[...] = b_ref[...].astype(BF16)

    o_ref[...] = r_ref[...] + _dot(a_ref[...], b_bf[...])


def _matmul_residual(a, b, layer, res, tm, tn):
    m, k = a.shape
    n = b.shape[2]
    return pl.pallas_call(
        _mm_res_kernel,
        grid=(n // tn, m // tm),
        in_specs=[pl.BlockSpec((tm, k), lambda j, i: (i, 0)), pl.BlockSpec((None, k, tn), lambda j, i: (layer, 0, j)),
                  pl.BlockSpec((tm, tn), lambda j, i: (i, j))],
        out_specs=pl.BlockSpec((tm, tn), lambda j, i: (i, j)),
        out_shape=jax.ShapeDtypeStruct((m, n), F32),
        scratch_shapes=[pltpu.VMEM((k, tn), BF16)],
        compiler_params=_cparams(("parallel", "arbitrary")),
        name="wout_residual",
    )(a, b, res)


def _shifted(x, tail_ref, sc_ref, width, first):
    tb = x.shape[0]
    sc_ref[0:8, 0:width] = jnp.where(first, 0.0, tail_ref[...])
    sc_ref[8:8 + tb, 0:width] = x
    return sc_ref[7:7 + tb, 0:width]


def _rwkv_prep_kernel(r_ref, k_ref, v_ref, l_ref, rt_ref, kt_ref, vt_ref, lt_ref,
                      mur_ref, muk_ref, muv_ref, mul_ref, wl_ref, b0_ref, kk_ref, ka_ref,
                      ro_ref, lw_ref, ko_ref, vo_ref, kkn_ref, kka_ref, g_ref, sc_ref):
    first = pl.program_id(0) == 0

    def mix(x_ref, t_ref, mu_ref, width):
        x = x_ref[...]
        prev = _shifted(x, t_ref, sc_ref, width, first)
        return x + (prev - x) * mu_ref[...]

    r = mix(r_ref, rt_ref, mur_ref, MIX_W)
    k = mix(k_ref, kt_ref, muk_ref, MIX_W)
    v = mix(v_ref, vt_ref, muv_ref, MIX_W)
    zl = mix(l_ref, lt_ref, mul_ref, 256)
    lane = lax.broadcasted_iota(jnp.int32, zl.shape, 1)
    act = jnp.where(lane < RWKV_W_LORA, jnp.tanh(zl),
                    jnp.where(lane < RWKV_W_LORA + RWKV_A_LORA, zl, _sigmoid(zl)))
    lo = _mm(act, wl_ref[...], "x3") + b0_ref[...]
    lw = -RWKV_DECAY_SCALE * _sigmoid(lo[:, 0:MIX_W])
    a = _sigmoid(lo[:, MIX_W:2 * MIX_W])
    g = lo[:, 2 * MIX_W:3 * MIX_W]
    kk = k * kk_ref[...]
    ss = _mm_exact_rhs(kk * kk, _block_diag_const(MIX_W, HEAD_DIM, 1.0), 2)
    kkn = kk / jnp.maximum(jnp.sqrt(ss), 1e-12)
    ro_ref[...] = r
    lw_ref[...] = lw
    ko_ref[...] = k * (1.0 + (a - 1.0) * ka_ref[...])
    vo_ref[...] = v
    kkn_ref[...] = kkn
    kka_ref[...] = kkn * a
    g_ref[...] = g


def _rwkv_prep(p, mu, w0, w_up, a0, a_up, g_up, k_k, k_a, tb=256):
    t = p.shape[0]
    nb8 = tb // 8

    def blk(width, off):
        return pl.BlockSpec((tb, width), lambda i: (i, off // width))

    def tail(width, off):
        return pl.BlockSpec((8, width), lambda i: (jnp.maximum(i * nb8 - 1, 0), off // width))

    def vec(width):
        return pl.BlockSpec((1, width), lambda i: (0, 0))

    w_lora = jnp.zeros((256, 3 * MIX_W), F32)
    w_lora = w_lora.at[0:32, 0:MIX_W].set(w_up)
    w_lora = w_lora.at[32:64, MIX_W:2 * MIX_W].set(a_up)
    w_lora = w_lora.at[64:160, 2 * MIX_W:].set(g_up)
    b0 = jnp.concatenate([w0, a0, jnp.zeros((MIX_W,), F32)]).reshape(1, 3 * MIX_W)
    mu_l = jnp.concatenate([mu[3 * MIX_W:], jnp.zeros((256 - RWKV_LORA,), F32)]).reshape(1, 256)
    out = jax.ShapeDtypeStruct((t, MIX_W), F32)
    return pl.pallas_call(
        _rwkv_prep_kernel,
        grid=(t // tb,),
        in_specs=[blk(MIX_W, OFF_AR), blk(MIX_W, OFF_AK), blk(MIX_W, OFF_AV), blk(256, OFF_AL),
                  tail(MIX_W, OFF_AR), tail(MIX_W, OFF_AK), tail(MIX_W, OFF_AV), tail(256, OFF_AL),
                  vec(MIX_W), vec(MIX_W), vec(MIX_W), vec(256),
                  pl.BlockSpec((256, 3 * MIX_W), lambda i: (0, 0)), vec(3 * MIX_W), vec(MIX_W), vec(MIX_W)],
        out_specs=[pl.BlockSpec((tb, MIX_W), lambda i: (i, 0))] * 7,
        out_shape=[out] * 7,
        scratch_shapes=[pltpu.VMEM((tb + 8, MIX_W), F32)],
        compiler_params=_cparams(("parallel",)),
        name="rwkv_prep",
    )(p, p, p, p, p, p, p, p,
      mu[0:MIX_W].reshape(1, -1), mu[MIX_W:2 * MIX_W].reshape(1, -1), mu[2 * MIX_W:3 * MIX_W].reshape(1, -1), mu_l,
      w_lora, b0, k_k.reshape(1, -1), k_a.reshape(1, -1))


RW_SC, RW_INV, RW_APPLY, RW_STATE, RW_SCAN = "x1", "x1", "x1", "x1", "x3"


def _rwkv_local_chunk(r, lw, k, v, kk, kka):
    c = r.shape[0]
    row = lax.broadcasted_iota(jnp.int32, (c, c), 0)
    col = lax.broadcasted_iota(jnp.int32, (c, c), 1)
    incl = row >= col
    strict = row > col
    eye = jnp.where(row == col, 1.0, 0.0)
    cum = _mm_exact_lhs(incl.astype(F32), lw, 3)
    last = cum[c - 1:c, :]
    pinv = jnp.exp(-cum)
    dl = jnp.exp(last - cum)
    at = _heads(-kk * jnp.exp(cum - lw))
    rt = _heads(r * jnp.exp(cum))
    bt = _heads(kka * pinv)
    kt = _heads(k * pinv)
    bl = _heads(kka * dl)
    kl = _heads(k * dl)
    v = _heads(v)
    pc = _heads(jnp.exp(last))
    sc = _mm(jnp.concatenate([at, rt], axis=1), jnp.concatenate([bt, kt], axis=1), RW_SC, "nt")
    a_ab = jnp.where(strict, sc[:, 0:c, 0:c], 0.0)
    a_ak = jnp.where(strict, sc[:, 0:c, c:2 * c], 0.0)
    r_b = jnp.where(incl, sc[:, c:2 * c, 0:c], 0.0)
    r_k = jnp.where(incl, sc[:, c:2 * c, c:2 * c], 0.0)
    x = eye + a_ab
    pw = _mm(a_ab, a_ab, RW_INV)
    levels = int(math.log2(c)) - 1
    for lvl in range(1, levels):
        prod = _mm(jnp.concatenate([x, pw], axis=1), pw, RW_INV)
        x = x + prod[:, 0:c]
        pw = prod[:, c:2 * c]
    x = x + _mm(x, pw, RW_INV)
    wu = _mm(x, jnp.concatenate([at, _mm(a_ak, v, RW_APPLY)], axis=2), RW_APPLY)
    lower = jnp.concatenate([jnp.zeros_like(v), v], axis=2)
    qy = _mm(jnp.concatenate([r_b, r_k], axis=2), jnp.concatenate([wu, lower], axis=1), RW_APPLY)
    wb = _mm(wu, bl, RW_STATE, "tn")
    n = wb[:, HEAD_DIM:] + _mm(v, kl, RW_STATE, "tn")
    m = eye * pc + wb[:, 0:HEAD_DIM]
    return rt + qy[:, :, 0:HEAD_DIM], qy[:, :, HEAD_DIM:], m, n


def _rwkv_local_kernel(r_ref, lw_ref, k_ref, v_ref, kk_ref, kka_ref, q_ref, y0_ref, m_ref, n_ref):
    c = RWKV_CHUNK
    outs = [_rwkv_local_chunk(*(ref[j * c:(j + 1) * c, :] for ref in (r_ref, lw_ref, k_ref, v_ref, kk_ref, kka_ref)))
            for j in range(r_ref.shape[0] // c)]
    for j, res in enumerate(outs):
        for o_ref, val in zip((q_ref, y0_ref, m_ref, n_ref), res):
            for h in range(N_HEADS):
                o_ref[j * c:(j + 1) * c, h * HEAD_DIM:(h + 1) * HEAD_DIM] = val[h]


def _rwkv_scan_kernel(q_ref, y0_ref, m_ref, n_ref, r_ref, k_ref, v_ref, g_ref, lng_ref, lnb_ref, rk_ref,
                      o_ref, s_ref, y_sc):
    @pl.when(pl.program_id(0) == 0)
    def _():
        s_ref[...] = jnp.zeros_like(s_ref)

    c = RWKV_CHUNK
    s = s_ref[...]
    for j in range(q_ref.shape[0] // c):
        rows = slice(j * c, (j + 1) * c)
        y = _mm(_heads(q_ref[rows, :]), s, RW_SCAN, "nt") + _heads(y0_ref[rows, :])
        s = _mm(s, _heads(m_ref[rows, :]), RW_SCAN) + _heads(n_ref[rows, :])
        for h in range(N_HEADS):
            y_sc[rows, h * HEAD_DIM:(h + 1) * HEAD_DIM] = y[h]
    s_ref[...] = s
    y = y_sc[...]
    avg = _block_diag_const(MIX_W, HEAD_DIM, 1.0 / HEAD_DIM)
    mean = _mm_exact_rhs(y, avg, 2)
    yc = y - mean
    var = _mm_exact_rhs(yc * yc, avg, 2)
    yn = yc * lax.rsqrt(var + RWKV_LN_EPS) * lng_ref[...] + lnb_ref[...]
    v = v_ref[...]
    bonus = _mm_exact_rhs(r_ref[...] * k_ref[...] * rk_ref[...], _block_diag_const(MIX_W, HEAD_DIM, 1.0), 2) * v
    o_ref[...] = ((yn + bonus) * g_ref[...]).astype(o_ref.dtype)


def _rwkv_recurrence(r, lw, k, v, kkn, kka, g, ln_g, ln_b, r_k, tb=256):
    t = r.shape[0]
    c = RWKV_CHUNK
    cblk = pl.BlockSpec((RWKV_LOCAL_CHUNKS * c, MIX_W), lambda i: (i, 0))
    f = jax.ShapeDtypeStruct((t, MIX_W), F32)
    qm, y0, m, n = pl.pallas_call(
        _rwkv_local_kernel,
        grid=(t // (RWKV_LOCAL_CHUNKS * c),),
        in_specs=[cblk] * 6,
        out_specs=[cblk] * 4,
        out_shape=[f] * 4,
        compiler_params=_cparams(("parallel",)),
        name="rwkv_local",
    )(r, lw, k, v, kkn, kka)
    blk = pl.BlockSpec((tb, MIX_W), lambda i: (i, 0))
    vec = pl.BlockSpec((1, MIX_W), lambda i: (0, 0))
    return pl.pallas_call(
        _rwkv_scan_kernel,
        grid=(t // tb,),
        in_specs=[blk] * 8 + [vec] * 3,
        out_specs=blk,
        out_shape=jax.ShapeDtypeStruct((t, MIX_W), BF16),
        scratch_shapes=[pltpu.VMEM((N_HEADS, HEAD_DIM, HEAD_DIM), F32), pltpu.VMEM((tb, MIX_W), F32)],
        compiler_params=_cparams(("arbitrary",)),
        name="rwkv_scan",
    )(qm, y0, m, n, r, k, v, g, ln_g.reshape(1, -1), ln_b.reshape(1, -1), r_k.reshape(1, -1))


def _rwkv_branch(p, mu, w0, w_up, a0, a_up, g_up, k_k, k_a, r_k, ln_g, ln_b):
    prep = _rwkv_prep(p, mu, w0, w_up, a0, a_up, g_up, k_k, k_a)
    return _rwkv_recurrence(*prep, ln_g, ln_b, r_k.reshape(-1))


def _rope_kernel(pos_ref, inv_ref, cos_ref, sin_ref):
    ang = pos_ref[...].astype(F32) * inv_ref[...]
    lane = lax.broadcasted_iota(jnp.int32, ang.shape, 1)
    cos_ref[...] = jnp.cos(ang)
    sin_ref[...] = jnp.where(lane % HEAD_DIM < HEAD_DIM // 2, -jnp.sin(ang), jnp.sin(ang))


def _rope_table(pos_b, tb=512):
    t = pos_b.shape[0]
    d = HEAD_DIM
    inv = 1.0 / (ROPE_BASE ** (jnp.arange(0, d, 2, dtype=F32) / d))
    inv_b = jnp.tile(inv, 4).reshape(1, 128)
    blk = pl.BlockSpec((tb, 128), lambda i: (i, 0))
    return pl.pallas_call(
        _rope_kernel,
        grid=(t // tb,),
        in_specs=[blk, pl.BlockSpec((1, 128), lambda i: (0, 0))],
        out_specs=[blk, blk],
        out_shape=[jax.ShapeDtypeStruct((t, 128), F32)] * 2,
        compiler_params=_cparams(("parallel",)),
        name="rope_table",
    )(pos_b, inv_b)


def _ret_kernel(q_ref, k_ref, v_ref, g_ref, cos_ref, sin_ref, dec_ref, zeta_ref, xi_ref, cd_ref, ng_ref,
                o_ref, s_ref, o_sc):
    @pl.when(pl.program_id(0) == 0)
    def _():
        s_ref[...] = jnp.zeros_like(s_ref)

    cos = jnp.concatenate([cos_ref[...]] * 4, axis=1)
    sin = jnp.concatenate([sin_ref[...]] * 4, axis=1)
    lane = lax.broadcasted_iota(jnp.int32, cos.shape, 1)
    lower_half = lane % HEAD_DIM < HEAD_DIM // 2

    def rope(x):
        swapped = jnp.where(lower_half, pltpu.roll(x, MIX_W - HEAD_DIM // 2, 1), pltpu.roll(x, HEAD_DIM // 2, 1))
        return x * cos + swapped * sin

    q = _heads(rope(q_ref[...]))
    k = _heads(rope(k_ref[...]) * (HEAD_DIM ** -0.5))
    v = _heads(v_ref[...])
    s = s_ref[...]
    scores = _mm(q, k, "x1", "nt") * dec_ref[...]
    o = _mm(scores, v, "x1") + _mm(q * xi_ref[...], s, "x1")
    s_ref[...] = s * cd_ref[...] + _mm(k * zeta_ref[...], v, "x1", "tn")
    on = o * lax.rsqrt(jnp.mean(o * o, axis=-1, keepdims=True) + NORM_EPS)
    for h in range(N_HEADS):
        o_sc[:, h * HEAD_DIM:(h + 1) * HEAD_DIM] = on[h]
    gt = g_ref[...]
    o_ref[...] = (o_sc[...] * ng_ref[...] * (gt * _sigmoid(gt))).astype(o_ref.dtype)


def _retention(p, cos_t, sin_t, norm_g):
    t = p.shape[0]
    c = RET_CHUNK
    h = N_HEADS
    log_g = jnp.log(1.0 - jnp.exp(jnp.linspace(math.log(1.0 / 32), math.log(1.0 / 512), h, dtype=F32)))
    idx = jnp.arange(c, dtype=F32)
    diff = idx[:, None] - idx[None, :]
    decay = jnp.where(diff >= 0, jnp.exp(log_g[:, None, None] * jnp.maximum(diff, 0.0)[None]), 0.0)
    zeta = jnp.exp(log_g[:, None] * (c - 1 - idx)[None])
    xi = jnp.exp(log_g[:, None] * (idx + 1)[None])
    cdec = jnp.exp(log_g * c)
    zeta_b = jnp.broadcast_to(zeta[:, :, None], (h, c, HEAD_DIM))
    xi_b = jnp.broadcast_to(xi[:, :, None], (h, c, HEAD_DIM))
    cd_b = jnp.broadcast_to(cdec[:, None, None], (h, HEAD_DIM, HEAD_DIM))

    def blk(off):
        return pl.BlockSpec((c, MIX_W), lambda i: (i, off // MIX_W))

    tab = pl.BlockSpec((c, 128), lambda i: (i, 0))

    def const(shape):
        return pl.BlockSpec(shape, lambda i: (0,) * len(shape))

    return pl.pallas_call(
        _ret_kernel,
        grid=(t // c,),
        in_specs=[blk(OFF_BQ), blk(OFF_BK), blk(OFF_BV), blk(OFF_BG), tab, tab,
                  const((h, c, c)), const((h, c, HEAD_DIM)), const((h, c, HEAD_DIM)),
                  const((h, HEAD_DIM, HEAD_DIM)), const((1, MIX_W))],
        out_specs=pl.BlockSpec((c, MIX_W), lambda i: (i, 0)),
        out_shape=jax.ShapeDtypeStruct((t, MIX_W), BF16),
        scratch_shapes=[pltpu.VMEM((h, HEAD_DIM, HEAD_DIM), F32), pltpu.VMEM((c, MIX_W), F32)],
        compiler_params=_cparams(("arbitrary",)),
        name="retention",
    )(p, p, p, p, cos_t, sin_t, decay, zeta_b, xi_b, cd_b, norm_g.reshape(1, -1))


def _lru_kernel(x_ref, xt_ref, gb_ref, pos_ref, cw_ref, cb_ref, wa_ref, ba_ref, wx_ref, bx_ref, lam_ref,
                o_ref, sc_ref, a_sc, b_sc, h_ref):
    tb = x_ref.shape[0]
    first = pl.program_id(0) == 0

    @pl.when(first)
    def _():
        h_ref[...] = jnp.zeros_like(h_ref)

    x = x_ref[...]
    sc_ref[0:8, :] = jnp.where(first, 0.0, xt_ref[...])
    sc_ref[8:8 + tb, :] = x
    cw = cw_ref[...]
    xc = cb_ref[...] + x * cw[CONV_W - 1:CONV_W, :]
    for j in range(1, CONV_W):
        xc = xc + sc_ref[8 - j:8 - j + tb, :] * cw[CONV_W - 1 - j:CONV_W - j, :]
    r = _sigmoid(_mm(xc, wa_ref[...], "x3") + ba_ref[...])
    ig = _sigmoid(_mm(xc, wx_ref[...], "x3") + bx_ref[...])
    nl = -lam_ref[...]
    softplus = jnp.maximum(nl, 0.0) + jnp.log1p(jnp.exp(-jnp.abs(nl)))
    log_a = -LRU_C * r * softplus
    pos = pos_ref[...]
    reset = jnp.concatenate([pos, pos, pos, pos], axis=1) == 0
    a_sc[...] = jnp.where(reset, 0.0, jnp.exp(log_a))
    th = jnp.tanh(log_a)
    b_sc[...] = jnp.where(reset, 1.0, jnp.sqrt(-2.0 * th / (1.0 - th))) * ig * xc

    row = lax.broadcasted_iota(jnp.int32, (8, MIX_W), 0)

    def group(gi, h):
        off = pl.multiple_of(gi * 8, 8)
        a = a_sc[pl.ds(off, 8), :]
        b = b_sc[pl.ds(off, 8), :]
        for d in (1, 2, 4):
            keep = row >= d
            b = jnp.where(keep, a * pltpu.roll(b, d, 0) + b, b)
            a = jnp.where(keep, a * pltpu.roll(a, d, 0), a)
        hs = a * h + b
        b_sc[pl.ds(off, 8), :] = hs
        return hs[7:8, :]

    h_ref[...] = lax.fori_loop(0, tb // 8, group, h_ref[...])
    gb = gb_ref[...]
    gelu = gb * (0.5 * (1.0 + jnp.tanh(math.sqrt(2.0 / math.pi) * (gb + 0.044715 * (gb * gb * gb)))))
    o_ref[...] = (b_sc[...] * gelu).astype(o_ref.dtype)


def _block_diag_weight(w):
    g, n, _ = w.shape
    eye = jnp.eye(g, dtype=w.dtype)
    return (eye[:, None, :, None] * w[:, :, None, :]).reshape(g * n, g * n)


def _rglru(p, pos_b, conv_w, conv_b, w_a, b_a, w_x, b_x, lam, tb=256):
    t = p.shape[0]
    nb8 = tb // 8
    vec = pl.BlockSpec((1, MIX_W), lambda i: (0, 0))
    mat = pl.BlockSpec((MIX_W, MIX_W), lambda i: (0, 0))
    cw8 = jnp.concatenate([conv_w, jnp.zeros((8 - CONV_W, MIX_W), F32)], axis=0)
    return pl.pallas_call(
        _lru_kernel,
        grid=(t // tb,),
        in_specs=[pl.BlockSpec((tb, MIX_W), lambda i: (i, OFF_CX // MIX_W)),
                  pl.BlockSpec((8, MIX_W), lambda i: (jnp.maximum(i * nb8 - 1, 0), OFF_CX // MIX_W)),
                  pl.BlockSpec((tb, MIX_W), lambda i: (i, OFF_CG // MIX_W)),
                  pl.BlockSpec((tb, 128), lambda i: (i, 0)),
                  pl.BlockSpec((8, MIX_W), lambda i: (0, 0)), vec, mat, vec, mat, vec, vec],
        out_specs=pl.BlockSpec((tb, MIX_W), lambda i: (i, 0)),
        out_shape=jax.ShapeDtypeStruct((t, MIX_W), BF16),
        scratch_shapes=[pltpu.VMEM((tb + 8, MIX_W), F32), pltpu.VMEM((tb, MIX_W), F32),
                        pltpu.VMEM((tb, MIX_W), F32), pltpu.VMEM((1, MIX_W), F32)],
        compiler_params=_cparams(("arbitrary",)),
        name="rglru",
    )(p, p, p, pos_b, cw8, conv_b.reshape(1, -1), _block_diag_weight(w_a), b_a.reshape(1, -1),
      _block_diag_weight(w_x), b_x.reshape(1, -1), lam.reshape(1, -1))


def _gla_kernel(v_ref, gt_ref, q_ref, k_ref, al_ref, aup_ref, ab_ref, ng_ref, o_ref, s_ref):
    @pl.when(pl.program_id(0) == 0)
    def _():
        s_ref[...] = jnp.zeros_like(s_ref)

    c = GLA_CHUNK
    row = lax.broadcasted_iota(jnp.int32, (c, c), 0)
    col = lax.broadcasted_iota(jnp.int32, (c, c), 1)
    causal = row >= col
    pre = _mm(al_ref[...], aup_ref[...], "x3") + ab_ref[...]
    log_a = (jnp.minimum(pre, 0.0) - jnp.log1p(jnp.exp(-jnp.abs(pre)))) / GLA_GATE_NORM
    s = s_ref[...]
    outs = []
    for j in range(q_ref.shape[0] // c):
        rows = slice(j * c, (j + 1) * c)
        bcum = _mm_exact_lhs(causal.astype(F32), log_a[rows], 3)
        blast = bcum[c - 1:c, :]
        k = k_ref[rows, :]
        q_e = _heads(q_ref[rows, :] * (GLA_DK ** -0.5) * jnp.exp(bcum))
        k_e = _heads(k * jnp.exp(-bcum))
        k_l = _heads(k * jnp.exp(blast - bcum))
        dec = _heads(jnp.exp(blast))
        v = _heads(v_ref[rows, :], GLA_DV)
        scores = jnp.where(causal, _mm(q_e, k_e, "x1", "nt"), 0.0)
        o = _mm(scores, v, "x1") + _mm(q_e, s, "x1", "nt")
        s = s * dec + _mm(v, k_l, "x1", "tn")
        on = o * lax.rsqrt(jnp.mean(o * o, axis=-1, keepdims=True) + NORM_EPS)
        outs.append(jnp.concatenate([on[h] for h in range(GLA_HEADS)], axis=1))
    s_ref[...] = s
    on = jnp.concatenate(outs, axis=0) * ng_ref[...]
    gt = gt_ref[...]
    o_ref[...] = (on * (gt * _sigmoid(gt))).astype(o_ref.dtype)


def _gla(p, alpha_up, alpha_b, norm_g):
    t = p.shape[0]
    c = GLA_STEP_CHUNKS * GLA_CHUNK
    hk = MIX_W // 2
    aup = jnp.concatenate([alpha_up, jnp.zeros((128 - GLA_LORA, hk), F32)], axis=0)

    def blk(width, off):
        return pl.BlockSpec((c, width), lambda i: (i, off // width))

    return pl.pallas_call(
        _gla_kernel,
        grid=(t // c,),
        in_specs=[blk(MIX_W, OFF_DV), blk(MIX_W, OFF_DGT), blk(hk, OFF_DQ), blk(hk, OFF_DK), blk(128, OFF_DAL),
                  pl.BlockSpec((128, hk), lambda i: (0, 0)), pl.BlockSpec((1, hk), lambda i: (0, 0)),
                  pl.BlockSpec((1, MIX_W), lambda i: (0, 0))],
        out_specs=pl.BlockSpec((c, MIX_W), lambda i: (i, 0)),
        out_shape=jax.ShapeDtypeStruct((t, MIX_W), BF16),
        scratch_shapes=[pltpu.VMEM((GLA_HEADS, GLA_DV, GLA_DK), F32)],
        compiler_params=_cparams(("arbitrary",)),
        name="gla",
    )(p, p, p, p, p, aup, alpha_b.reshape(1, -1), norm_g.reshape(1, -1))


def _merge_kernel(ya_ref, yb_ref, yc_ref, yd_ref, wa_ref, wb_ref, wc_ref, wd_ref,
                  ga_ref, gb_ref, gc_ref, gd_ref, o_ref, w_bf):
    @pl.when(pl.program_id(1) == 0)
    def _():
        for kk, w_ref in enumerate((wa_ref, wb_ref, wc_ref, wd_ref)):
            w_bf[kk] = w_ref[0].astype(BF16)

    acc = _sigmoid(ga_ref[...].astype(F32)) * _dot(ya_ref[...], w_bf[0])
    acc += _sigmoid(gb_ref[...].astype(F32)) * _dot(yb_ref[...], w_bf[1])
    acc += _sigmoid(gc_ref[...].astype(F32)) * _dot(yc_ref[...], w_bf[2])
    acc += _sigmoid(gd_ref[...].astype(F32)) * _dot(yd_ref[...], w_bf[3])
    o_ref[...] = acc.astype(o_ref.dtype)


def _merge(ys, ws, layer, gates, tm=512, tn=512):
    t = gates.shape[0]
    yspec = pl.BlockSpec((tm, MIX_W), lambda j, i: (i, 0))
    wspec = pl.BlockSpec((1, MIX_W, tn), lambda j, i: (layer, 0, j))

    def gspec(kk):
        return pl.BlockSpec((tm, tn), lambda j, i: (i, kk * D_MODEL // tn + j))

    return pl.pallas_call(
        _merge_kernel,
        grid=(D_MODEL // tn, t // tm),
        in_specs=[yspec] * 4 + [wspec] * 4 + [gspec(kk) for kk in range(4)],
        out_specs=pl.BlockSpec((tm, tn), lambda j, i: (i, j)),
        out_shape=jax.ShapeDtypeStruct((t, D_MODEL), BF16),
        scratch_shapes=[pltpu.VMEM((4, MIX_W, tn), BF16)],
        compiler_params=_cparams(("parallel", "arbitrary")),
        name="merge",
    )(*ys, *ws, gates, gates, gates, gates)


def _ffn_norm_kernel(h_ref, g_ref, wr_ref, br_ref, hn_ref, rt_ref):
    x = h_ref[...]
    ms = jnp.mean(x * x, axis=-1, keepdims=True)
    hn = x * lax.rsqrt(ms + NORM_EPS) * g_ref[...]
    hn_ref[...] = hn
    z = _mm(hn, wr_ref[...], "x3") + br_ref[...]
    lane = lax.broadcasted_iota(jnp.int32, z.shape, 1)
    neg = jnp.float32(-1e30)

    def first_argmax(v):
        m = jnp.max(v, axis=-1, keepdims=True)
        return m, jnp.min(jnp.where(v == m, lane, 128), axis=-1, keepdims=True)

    is_group = lane < N_GROUPS
    zg = jnp.where(is_group, z, neg)
    mg, g_idx = first_argmax(zg)
    pg_top = 1.0 / jnp.sum(jnp.where(is_group, jnp.exp(zg - mg), 0.0), axis=-1, keepdims=True)
    lo = N_GROUPS + g_idx * EXP_PER_GROUP
    in_group = jnp.logical_and(lane >= lo, lane < lo + EXP_PER_GROUP)
    ze = jnp.where(in_group, z, neg)
    m1, i1 = first_argmax(ze)
    se = jnp.sum(jnp.where(in_group, jnp.exp(ze - m1), 0.0), axis=-1, keepdims=True)
    m2, i2 = first_argmax(jnp.where(lane == i1, neg, ze))
    p1 = 1.0 / se
    p2 = jnp.exp(m2 - m1) / se
    tot = p1 + p2
    rt_ref[...] = jnp.where(lane == 0, pg_top * (p1 / tot),
                            jnp.where(lane == 1, pg_top * (p2 / tot),
                                      jnp.where(lane == 2, (i1 - N_GROUPS).astype(F32),
                                                jnp.where(lane == 3, (i2 - N_GROUPS).astype(F32), 0.0))))


def _ffn_norm_router(h, g, w_r, b_r, tb=256):
    t, d = h.shape
    return pl.pallas_call(
        _ffn_norm_kernel,
        grid=(t // tb,),
        in_specs=[pl.BlockSpec((tb, d), lambda i: (i, 0)), pl.BlockSpec((1, d), lambda i: (0, 0)),
                  pl.BlockSpec((d, 128), lambda i: (0, 0)), pl.BlockSpec((1, 128), lambda i: (0, 0))],
        out_specs=[pl.BlockSpec((tb, d), lambda i: (i, 0)), pl.BlockSpec((tb, 128), lambda i: (i, 0))],
        out_shape=[jax.ShapeDtypeStruct((t, d), F32), jax.ShapeDtypeStruct((t, 128), F32)],
        compiler_params=_cparams(("parallel",)),
        name="ffn_norm_router",
    )(h, g.reshape(1, d), w_r, b_r)


def _row_copy(src_hbm, idx, dst, r, sem):
    return pltpu.make_async_copy(src_hbm.at[pl.ds(idx, 1), :], dst.at[pl.ds(r, 1), :], sem)


def _expert_changed(i, exp_ref):
    return jnp.logical_or(i == 0, exp_ref[i] != exp_ref[jnp.maximum(i - 1, 0)])


def _gather_rows(src_hbm, row_of, n, dst, sem):
    def start(r, carry):
        _row_copy(src_hbm, row_of(r), dst, r, sem).start()
        return carry

    lax.fori_loop(0, n, start, 0, unroll=8)


def _gather_wait(src_hbm, dst, sem):
    pltpu.make_async_copy(src_hbm.at[pl.ds(0, dst.shape[0]), :], dst, sem).wait()


def _expert_kernel(st_ref, exp_ref, base_ref, nxt_ref, nblk_ref, x_hbm, wg_hbm, wu_hbm, wd_hbm, o_ref,
                   xbuf, stage_g, stage_u, stage_d, wg_bf, wu_bf, wd_bf, xsem, wsem, *, layer):
    i = pl.program_id(0)
    n = nblk_ref[0]
    slot = i % 2

    stages = (stage_g, stage_u, stage_d)
    caches = (wg_bf, wu_bf, wd_bf)

    def chunk_rows(m, c):
        rows = stages[m].shape[0] // WEIGHT_CHUNKS
        return pl.ds(c * rows, rows)

    def weight_copy(e, m, c):
        src = (wg_hbm, wu_hbm, wd_hbm)[m]
        return pltpu.make_async_copy(src.at[layer, e, chunk_rows(m, c), :], stages[m].at[chunk_rows(m, c), :],
                                     wsem.at[m, c])

    def start_weights(e):
        for c in range(WEIGHT_CHUNKS):
            for m in range(3):
                weight_copy(e, m, c).start(priority=1)

    last = st_ref.shape[0] - 1

    def fetch(blk, sl):
        base = base_ref[blk]
        _gather_rows(x_hbm, lambda r: st_ref[jnp.minimum(base + r, last)], MOE_BLOCK, xbuf.at[sl], xsem.at[sl])

    @pl.when(jnp.logical_and(i == 0, n > 0))
    def _():
        fetch(0, 0)
        start_weights(exp_ref[0])

    @pl.when(i + 1 < n)
    def _():
        fetch(i + 1, 1 - slot)

    @pl.when(i < n)
    def _():
        @pl.when(_expert_changed(i, exp_ref))
        def _():
            nxt = nxt_ref[i]
            for c in range(WEIGHT_CHUNKS):
                for m in range(3):
                    weight_copy(exp_ref[i], m, c).wait()
                    caches[m][chunk_rows(m, c), :] = stages[m][chunk_rows(m, c), :].astype(BF16)

                @pl.when(nxt >= 0)
                def _():
                    for m in range(3):
                        weight_copy(nxt, m, c).start(priority=1)

        _gather_wait(x_hbm, xbuf.at[slot], xsem.at[slot])
        x = xbuf[slot].astype(BF16)
        gate = _dot(x, wg_bf[...])
        up = _dot(x, wu_bf[...])
        hmid = (gate * _sigmoid(gate) * up).astype(BF16)
        o_ref[...] = _dot(hmid, wd_bf[...])

    @pl.when(i >= n)
    def _():
        o_ref[...] = jnp.zeros_like(o_ref)


def _experts(hn, st, plan, w_gate, w_up, w_down, layer):
    block_exp, src_base, next_exp, n_used, _ = plan
    n_blocks = block_exp.shape[0]
    d = hn.shape[1]
    hbm = pl.BlockSpec(memory_space=pl.ANY)
    return pl.pallas_call(
        functools.partial(_expert_kernel, layer=layer),
        grid_spec=pltpu.PrefetchScalarGridSpec(
            num_scalar_prefetch=5,
            grid=(n_blocks,),
            in_specs=[hbm, hbm, hbm, hbm],
            out_specs=pl.BlockSpec((MOE_BLOCK, d), lambda i, *_: (i, 0)),
            scratch_shapes=[pltpu.VMEM((2, MOE_BLOCK, d), F32),
                            pltpu.VMEM((d, D_EXPERT), F32), pltpu.VMEM((d, D_EXPERT), F32),
                            pltpu.VMEM((D_EXPERT, d), F32),
                            pltpu.VMEM((d, D_EXPERT), BF16), pltpu.VMEM((d, D_EXPERT), BF16),
                            pltpu.VMEM((D_EXPERT, d), BF16),
                            pltpu.SemaphoreType.DMA((2,)), pltpu.SemaphoreType.DMA((3, WEIGHT_CHUNKS))],
        ),
        out_shape=jax.ShapeDtypeStruct((n_blocks * MOE_BLOCK, d), F32),
        compiler_params=_cparams(("arbitrary",), VMEM_LIMIT_EXPERT),
        name="experts",
    )(st, block_exp, src_base, next_exp, n_used, hn, w_gate, w_up, w_down)


def _combine_kernel(pos_ref, y_hbm, h_ref, rt_ref, g_ref, o_ref, buf, sem, *, final_norm):
    tb = h_ref.shape[0]
    i = pl.program_id(0)
    slot = i % 2

    def fetch(blk, sl):
        for s in range(TOP_K):
            base = (s * pl.num_programs(0) + blk) * tb
            _gather_rows(y_hbm, lambda r: pos_ref[base + r], tb, buf.at[sl, s], sem.at[sl])

    @pl.when(i == 0)
    def _():
        fetch(0, 0)

    @pl.when(i + 1 < pl.num_programs(0))
    def _():
        fetch(i + 1, 1 - slot)

    for s in range(TOP_K):
        _gather_wait(y_hbm, buf.at[slot, s], sem.at[slot])
    rt = rt_ref[...]
    out = h_ref[...] + (rt[:, 0:1] * buf[slot, 0] + rt[:, 1:2] * buf[slot, 1])
    if final_norm:
        ms = jnp.mean(out * out, axis=-1, keepdims=True)
        out = out * lax.rsqrt(ms + NORM_EPS) * g_ref[...]
    o_ref[...] = out


def _combine(h, y_rows, pos, route, g, final_norm, tb=128):
    t, d = h.shape
    grid_spec = pltpu.PrefetchScalarGridSpec(
        num_scalar_prefetch=1,
        grid=(t // tb,),
        in_specs=[pl.BlockSpec(memory_space=pl.ANY),
                  pl.BlockSpec((tb, d), lambda i, *_: (i, 0)),
                  pl.BlockSpec((tb, 128), lambda i, *_: (i, 0)),
                  pl.BlockSpec((1, d), lambda i, *_: (0, 0))],
        out_specs=pl.BlockSpec((tb, d), lambda i, *_: (i, 0)),
        scratch_shapes=[pltpu.VMEM((2, TOP_K, tb, d), F32), pltpu.SemaphoreType.DMA((2,))],
    )
    return pl.pallas_call(
        functools.partial(_combine_kernel, final_norm=final_norm),
        grid_spec=grid_spec,
        out_shape=jax.ShapeDtypeStruct((t, d), F32),
        compiler_params=_cparams(("arbitrary",)),
        name="moe_combine",
    )(pos, y_rows, h, route, g.reshape(1, d))


def _plan_kernel(counts_ref, rank_ref, eid_ref, bexp_ref, base_ref, nxt_ref, nused_ref, pos_ref, shift_ref):
    n_blocks = bexp_ref.shape[0]

    def per_expert(e, carry):
        start, pad_start = carry
        nb = (counts_ref[e] + MOE_BLOCK - 1) // MOE_BLOCK
        shift_ref[e] = pad_start - start

        def fill(j, c):
            blk = pad_start // MOE_BLOCK + j
            bexp_ref[blk] = e
            base_ref[blk] = start + j * MOE_BLOCK
            return c

        lax.fori_loop(0, nb, fill, 0)
        return start + counts_ref[e], pad_start + nb * MOE_BLOCK

    _, pad_end = lax.fori_loop(0, N_EXPERTS, per_expert, (jnp.int32(0), jnp.int32(0)))
    n_used = pad_end // MOE_BLOCK
    nused_ref[0] = n_used

    def tail(blk, c):
        bexp_ref[blk] = N_EXPERTS - 1
        base_ref[blk] = 0
        nxt_ref[blk] = -1
        return c

    lax.fori_loop(n_used, n_blocks, tail, 0)

    def backward(k, carry):
        nxt, cur = carry
        blk = n_used - 1 - k
        e = bexp_ref[blk]
        nxt = jnp.where(e != cur, cur, nxt)
        nxt_ref[blk] = nxt
        return nxt, e

    lax.fori_loop(0, n_used, backward, (jnp.int32(-1), jnp.int32(-1)))

    eid = eid_ref[...]
    pos = rank_ref[...]
    for e in range(N_EXPERTS):
        pos = pos + jnp.where(eid == e, shift_ref[e], 0)
    pos_ref[...] = pos


def _plan(counts, rank, eid, n_blocks):
    smem = pl.BlockSpec(memory_space=pltpu.SMEM)
    vmem = pl.BlockSpec(memory_space=pltpu.VMEM)
    i32 = lambda n: jax.ShapeDtypeStruct((n,), jnp.int32)
    return pl.pallas_call(
        _plan_kernel,
        in_specs=[smem, vmem, vmem],
        out_specs=[smem, smem, smem, smem, vmem],
        out_shape=[i32(n_blocks), i32(n_blocks), i32(n_blocks), i32(1), jax.ShapeDtypeStruct(rank.shape, jnp.int32)],
        scratch_shapes=[pltpu.SMEM((N_EXPERTS,), jnp.int32)],
        name="moe_plan",
    )(counts, rank, eid)


def _moe_layer(h, norm_g, wg_r, bg_r, we_r, be_r, w_gate, w_up, w_down, layer, out_g, final_norm):
    t, d = h.shape
    w_r = jnp.concatenate([wg_r, we_r, jnp.zeros((d, 128 - N_GROUPS - N_EXPERTS), F32)], axis=1)
    b_r = jnp.concatenate([bg_r, be_r, jnp.zeros((128 - N_GROUPS - N_EXPERTS,), F32)]).reshape(1, 128)
    hn, route = _ffn_norm_router(h, norm_g, w_r, b_r)
    expert_id = route[:, TOP_K:2 * TOP_K].astype(jnp.int32)
    flat = expert_id.reshape(-1)
    order = jnp.argsort(flat).astype(jnp.int32)
    st = order // TOP_K
    rank_sm = jnp.argsort(order).astype(jnp.int32).reshape(t, TOP_K).T.reshape(-1, 128)
    eid_sm = expert_id.T.reshape(-1, 128)
    counts = jnp.sum((flat[:, None] == jnp.arange(N_EXPERTS, dtype=jnp.int32)[None, :]).astype(jnp.int32), axis=0)
    plan = _plan(counts, rank_sm, eid_sm, (t * TOP_K + MOE_BLOCK - 1) // MOE_BLOCK + N_EXPERTS)
    y_rows = _experts(hn, st, plan, w_gate, w_up, w_down, layer)
    return _combine(h, y_rows, plan[4].reshape(-1), route, out_g, final_norm)


_D0 = A_COLS + B_COLS + C_COLS
_W_IN_PIECES = (
    (OFF_BQ, A_COLS, B_COLS), (OFF_CX, A_COLS + B_COLS, C_COLS),
    (OFF_DV, _D0 + 512, 512), (OFF_DGT, _D0 + 1040, 512),
    (OFF_AR, 0, 3 * MIX_W), (OFF_DQ, _D0, 512),
    (OFF_AL, 3 * MIX_W, RWKV_LORA), (OFF_DAL, _D0 + 1024, GLA_LORA),
    (OFF_G, _D0 + D_COLS, 4 * D_MODEL),
)


RELAYOUT_ROWS = 512
_SPECIAL_BLOCK = OFF_AL // RELAYOUT_ROWS


def _w_in_source_rows():
    rows = np.zeros((NP_COLS // RELAYOUT_ROWS,), np.int32)
    for dst, src, width in _W_IN_PIECES:
        if width % RELAYOUT_ROWS == 0:
            for k in range(width // RELAYOUT_ROWS):
                rows[dst // RELAYOUT_ROWS + k] = src + k * RELAYOUT_ROWS
    return rows


def _relayout_kernel(src_ref, w_hbm, extra_hbm, o_ref, stage, sem, *, layer):
    j = pl.program_id(0)
    slot = j % 2

    def start(blk, sl):
        @pl.when(blk == _SPECIAL_BLOCK)
        def _():
            pltpu.make_async_copy(extra_hbm, stage.at[sl], sem.at[sl]).start()

        @pl.when(blk != _SPECIAL_BLOCK)
        def _():
            rows = pl.ds(pl.multiple_of(src_ref[blk], 16), RELAYOUT_ROWS)
            pltpu.make_async_copy(w_hbm.at[layer, rows, :], stage.at[sl], sem.at[sl]).start()

    @pl.when(j == 0)
    def _():
        start(0, 0)

    @pl.when(j + 1 < pl.num_programs(0))
    def _():
        start(j + 1, 1 - slot)

    pltpu.make_async_copy(extra_hbm, stage.at[slot], sem.at[slot]).wait()
    o_ref[...] = stage[slot].astype(o_ref.dtype)


def _permute_w_in(w_in, layer):
    w_t = jnp.swapaxes(w_in, 1, 2)
    d = w_t.shape[2]
    zeros = lambda n: jnp.zeros((n, d), F32)
    extra = jnp.concatenate([w_t[layer, 3 * MIX_W:A_COLS], zeros(OFF_DAL - OFF_AL - RWKV_LORA),
                             w_t[layer, _D0 + 1024:_D0 + 1024 + GLA_LORA],
                             zeros(OFF_G - OFF_DAL - GLA_LORA)], axis=0)
    hbm = pl.BlockSpec(memory_space=pl.ANY)
    return pl.pallas_call(
        functools.partial(_relayout_kernel, layer=layer),
        grid_spec=pltpu.PrefetchScalarGridSpec(
            num_scalar_prefetch=1,
            grid=(NP_COLS // RELAYOUT_ROWS,),
            in_specs=[hbm, hbm],
            out_specs=pl.BlockSpec((RELAYOUT_ROWS, d), lambda j, *_: (j, 0)),
            scratch_shapes=[pltpu.VMEM((2, RELAYOUT_ROWS, d), F32), pltpu.SemaphoreType.DMA((2,))],
        ),
        out_shape=jax.ShapeDtypeStruct((NP_COLS, d), BF16),
        compiler_params=_cparams(("arbitrary",)),
        name="w_in_relayout",
    )(jnp.asarray(_w_in_source_rows()), w_t, extra)


def kernel(x, positions, norm_mix_g, w_in, rwkv_mu, rwkv_w0, rwkv_w_up, rwkv_a0, rwkv_a_up, rwkv_g_up, rwkv_k_k, rwkv_k_a, rwkv_r_k, rwkv_ln_g, rwkv_ln_b, rwkv_w_o, ret_norm_g, ret_w_o, lru_conv_w, lru_conv_b, lru_w_a, lru_b_a, lru_w_x, lru_b_x, lru_lambda, lru_w_o, gla_alpha_up, gla_alpha_b, gla_norm_g, gla_w_o, w_out, norm_ffn_g, router_group_w, router_group_b, router_expert_w, router_expert_b, moe_w_gate, moe_w_up, moe_w_down, final_norm_g):
    b_, s_, d = x.shape
    assert b_ == 1 and d == D_MODEL
    depth = w_in.shape[0]
    h = x.reshape(s_, d)
    pos_b = jnp.broadcast_to(positions.reshape(s_, 1), (s_, 128)).astype(jnp.int32)
    cos_t, sin_t = _rope_table(pos_b)
    for l in range(depth):
        xn = _rmsnorm(h, norm_mix_g[l], BF16)
        w_t = _permute_w_in(w_in, l)
        p = _matmul_nt(xn, w_t, 0, OFF_G, 2048, 512, F32, "in_proj")
        gates = _matmul_nt(xn, w_t, OFF_G, 4 * D_MODEL, 2048, 512, BF16, "in_proj_gates")
        y_a = _rwkv_branch(p, rwkv_mu[l], rwkv_w0[l], rwkv_w_up[l], rwkv_a0[l], rwkv_a_up[l], rwkv_g_up[l],
                           rwkv_k_k[l], rwkv_k_a[l], rwkv_r_k[l], rwkv_ln_g[l], rwkv_ln_b[l])
        y_b = _retention(p, cos_t, sin_t, ret_norm_g[l])
        y_c = _rglru(p, pos_b, lru_conv_w[l], lru_conv_b[l], lru_w_a[l], lru_b_a[l], lru_w_x[l], lru_b_x[l],
                     lru_lambda[l])
        y_d = _gla(p, gla_alpha_up[l], gla_alpha_b[l], gla_norm_g[l])
        merged = _merge((y_a, y_b, y_c, y_d), (rwkv_w_o, ret_w_o, lru_w_o, gla_w_o), l, gates)
        h = _matmul_residual(merged, w_out, l, h, 1024, 512)
        last = l == depth - 1
        h = _moe_layer(h, norm_ffn_g[l], router_group_w[l], router_group_b[l], router_expert_w[l],
                       router_expert_b[l], moe_w_gate, moe_w_up, moe_w_down, l,
                       final_norm_g if last else norm_ffn_g[l], last)
    return h.reshape(b_, s_, d)
```

```python
import functools
import math

import jax
import jax.numpy as jnp
import numpy as np
from jax import lax
from jax.experimental import pallas as pl
from jax.experimental.pallas import tpu as pltpu

F32 = jnp.float32
BF16 = jnp.bfloat16
HI = lax.Precision.HIGHEST

D_MODEL = 2048
MIX_W = 512
NORM_EPS = 1e-6
HEAD_DIM = 64
N_HEADS = MIX_W // HEAD_DIM

RWKV_W_LORA, RWKV_A_LORA, RWKV_G_LORA = 32, 32, 96
RWKV_LORA = RWKV_W_LORA + RWKV_A_LORA + RWKV_G_LORA
RWKV_DECAY_SCALE = 0.6065306597126334
RWKV_LN_EPS = 64e-5
RWKV_CHUNK = 64
RWKV_LOCAL_CHUNKS = 4

RET_CHUNK = 128
ROPE_BASE = 10000.0

LRU_BLOCKS = 8
CONV_W = 4
LRU_C = 8.0

GLA_HEADS = 4
GLA_DK = 64
GLA_DV = 128
GLA_LORA = 16
GLA_GATE_NORM = 16.0
GLA_CHUNK = 64
GLA_STEP_CHUNKS = 2

N_GROUPS = 4
EXP_PER_GROUP = 8
N_EXPERTS = N_GROUPS * EXP_PER_GROUP
TOP_K = 2
D_EXPERT = 1024
MOE_BLOCK = 128
WEIGHT_CHUNKS = 4

A_COLS = 3 * MIX_W + RWKV_LORA
B_COLS = 4 * MIX_W
C_COLS = 2 * MIX_W
D_COLS = 2 * (MIX_W // 2) + MIX_W + GLA_LORA + MIX_W

OFF_BQ, OFF_BK, OFF_BV, OFF_BG = 0, 512, 1024, 1536
OFF_CX, OFF_CG = 2048, 2560
OFF_DV, OFF_DGT = 3072, 3584
OFF_AR, OFF_AK, OFF_AV = 4096, 4608, 5120
OFF_DQ, OFF_DK = 5632, 5888
OFF_AL = 6144
OFF_DAL = 6400
OFF_G = 6656
NP_COLS = OFF_G + 4 * D_MODEL

VMEM_LIMIT = 48 * 1024 * 1024
VMEM_LIMIT_EXPERT = 56 * 1024 * 1024


def _cparams(sem, vmem=VMEM_LIMIT):
    return pltpu.CompilerParams(dimension_semantics=sem, vmem_limit_bytes=vmem)


def _dot(a, b, prec=None):
    return jnp.dot(a, b, precision=prec, preferred_element_type=F32)


def _sigmoid(x):
    return 1.0 / (1.0 + jnp.exp(-x))


def _split(x):
    hi = x.astype(BF16)
    return hi, (x - hi.astype(F32)).astype(BF16)


def _dims(form, batched):
    ca, cb = {"nn": (1, 0), "nt": (1, 1), "tn": (0, 0)}[form]
    if batched:
        return (((ca + 1,), (cb + 1,)), ((0,), (0,)))
    return (((ca,), (cb,)), ((), ()))


def _mm(a, b, mode, form="nn"):
    dims = _dims(form, a.ndim == 3)
    if mode == "hi":
        return lax.dot_general(a, b, dims, precision=HI, preferred_element_type=F32)
    d = functools.partial(lax.dot_general, dimension_numbers=dims, preferred_element_type=F32)
    if mode == "x1":
        return d(a.astype(BF16), b.astype(BF16))
    ah, al = _split(a)
    bh, bl = _split(b)
    return d(ah, bh) + (d(ah, bl) + d(al, bh))


def _mm_exact_rhs(a, b, terms):
    b = b.astype(BF16)
    acc = None
    for _ in range(terms):
        piece = a.astype(BF16)
        part = _dot(piece, b)
        acc = part if acc is None else acc + part
        a = a - piece.astype(F32)
    return acc


def _mm_exact_lhs(a, b, terms):
    a = a.astype(BF16)
    acc = None
    for _ in range(terms):
        piece = b.astype(BF16)
        part = _dot(a, piece)
        acc = part if acc is None else acc + part
        b = b - piece.astype(F32)
    return acc


def _heads(x, width=HEAD_DIM):
    return jnp.stack([x[:, h * width:(h + 1) * width] for h in range(x.shape[1] // width)])


def _block_diag_const(n, blk, value):
    r = lax.broadcasted_iota(jnp.int32, (n, n), 0) // blk
    c = lax.broadcasted_iota(jnp.int32, (n, n), 1) // blk
    return jnp.where(r == c, value, 0.0).astype(F32)


def _rmsnorm_kernel(x_ref, g_ref, o_ref):
    x = x_ref[...]
    ms = jnp.mean(x * x, axis=-1, keepdims=True)
    o_ref[...] = (x * lax.rsqrt(ms + NORM_EPS) * g_ref[...]).astype(o_ref.dtype)


def _rmsnorm(x, g, out_dtype, tb=512):
    t, d = x.shape
    return pl.pallas_call(
        _rmsnorm_kernel,
        grid=(t // tb,),
        in_specs=[pl.BlockSpec((tb, d), lambda i: (i, 0)), pl.BlockSpec((1, d), lambda i: (0, 0))],
        out_specs=pl.BlockSpec((tb, d), lambda i: (i, 0)),
        out_shape=jax.ShapeDtypeStruct((t, d), out_dtype),
        compiler_params=_cparams(("parallel",)),
        name="rmsnorm",
    )(x, g.reshape(1, d))


def _mm_nt_kernel(a_ref, b_ref, o_ref):
    o_ref[...] = _mm(a_ref[...], b_ref[...], "x1", "nt").astype(o_ref.dtype)


def _matmul_nt(a, b_t, row0, n, tm, tn, out_dtype, name):
    m, k = a.shape
    assert row0 % tn == 0 and n % tn == 0 and m % tm == 0
    return pl.pallas_call(
        _mm_nt_kernel,
        grid=(m // tm, n // tn),
        in_specs=[pl.BlockSpec((tm, k), lambda i, j: (i, 0)), pl.BlockSpec((tn, k), lambda i, j: (row0 // tn + j, 0))],
        out_specs=pl.BlockSpec((tm, tn), lambda i, j: (i, j)),
        out_shape=jax.ShapeDtypeStruct((m, n), out_dtype),
        compiler_params=_cparams(("parallel", "parallel")),
        name=name,
    )(a, b_t)


def _mm_res_kernel(a_ref, b_ref, r_ref, o_ref, b_bf):
    @pl.when(pl.program_id(1) == 0)
    def _():
        b_bf[...] = b_ref[...].astype(BF16)

    o_ref[...] = r_ref[...] + _dot(a_ref[...], b_bf[...])


def _matmul_residual(a, b, layer, res, tm, tn):
    m, k = a.shape
    n = b.shape[2]
    return pl.pallas_call(
        _mm_res_kernel,
        grid=(n // tn, m // tm),
        in_specs=[pl.BlockSpec((tm, k), lambda j, i: (i, 0)), pl.BlockSpec((None, k, tn), lambda j, i: (layer, 0, j)),
                  pl.BlockSpec((tm, tn), lambda j, i: (i, j))],
        out_specs=pl.BlockSpec((tm, tn), lambda j, i: (i, j)),
        out_shape=jax.ShapeDtypeStruct((m, n), F32),
        scratch_shapes=[pltpu.VMEM((k, tn), BF16)],
        compiler_params=_cparams(("parallel", "arbitrary")),
        name="wout_residual",
    )(a, b, res)


def _shifted(x, tail_ref, sc_ref, width, first):
    tb = x.shape[0]
    sc_ref[0:8, 0:width] = jnp.where(first, 0.0, tail_ref[...])
    sc_ref[8:8 + tb, 0:width] = x
    return sc_ref[7:7 + tb, 0:width]


def _rwkv_prep_values(r_ref, k_ref, v_ref, l_ref, rt_ref, kt_ref, vt_ref, lt_ref,
                      mur_ref, muk_ref, muv_ref, mul_ref, wl_ref, b0_ref, kk_ref, ka_ref, sc_ref):
    first = pl.program_id(0) == 0

    def mix(x_ref, t_ref, mu_ref, width):
        x = x_ref[...]
        prev = _shifted(x, t_ref, sc_ref, width, first)
        return x + (prev - x) * mu_ref[...]

    r = mix(r_ref, rt_ref, mur_ref, MIX_W)
    k = mix(k_ref, kt_ref, muk_ref, MIX_W)
    v = mix(v_ref, vt_ref, muv_ref, MIX_W)
    zl = mix(l_ref, lt_ref, mul_ref, 256)
    lane = lax.broadcasted_iota(jnp.int32, zl.shape, 1)
    act = jnp.where(lane < RWKV_W_LORA, jnp.tanh(zl),
                    jnp.where(lane < RWKV_W_LORA + RWKV_A_LORA, zl, _sigmoid(zl)))
    lo = _mm(act, wl_ref[...], "x3") + b0_ref[...]
    lw = -RWKV_DECAY_SCALE * _sigmoid(lo[:, 0:MIX_W])
    a = _sigmoid(lo[:, MIX_W:2 * MIX_W])
    g = lo[:, 2 * MIX_W:3 * MIX_W]
    kk = k * kk_ref[...]
    ss = _mm_exact_rhs(kk * kk, _block_diag_const(MIX_W, HEAD_DIM, 1.0), 2)
    kkn = kk / jnp.maximum(jnp.sqrt(ss), 1e-12)
    return r, lw, k * (1.0 + (a - 1.0) * ka_ref[...]), v, kkn, kkn * a, g


RW_SC, RW_INV, RW_APPLY, RW_STATE, RW_SCAN = "x1", "x1", "x1", "x1", "x1"


def _rwkv_local_chunk(r, lw, k, v, kk, kka):
    c = r.shape[0]
    row = lax.broadcasted_iota(jnp.int32, (c, c), 0)
    col = lax.broadcasted_iota(jnp.int32, (c, c), 1)
    incl = row >= col
    strict = row > col
    eye = jnp.where(row == col, 1.0, 0.0)
    cum = _mm_exact_lhs(incl.astype(F32), lw, 3)
    last = cum[c - 1:c, :]
    pinv = jnp.exp(-cum)
    dl = jnp.exp(last - cum)
    at = _heads(-kk * jnp.exp(cum - lw))
    rt = _heads(r * jnp.exp(cum))
    bt = _heads(kka * pinv)
    kt = _heads(k * pinv)
    bl = _heads(kka * dl)
    kl = _heads(k * dl)
    v = _heads(v)
    pc = _heads(jnp.exp(last))
    sc = _mm(jnp.concatenate([at, rt], axis=1), jnp.concatenate([bt, kt], axis=1), RW_SC, "nt")
    a_ab = jnp.where(strict, sc[:, 0:c, 0:c], 0.0)
    a_ak = jnp.where(strict, sc[:, 0:c, c:2 * c], 0.0)
    r_b = jnp.where(incl, sc[:, c:2 * c, 0:c], 0.0)
    r_k = jnp.where(incl, sc[:, c:2 * c, c:2 * c], 0.0)
    x = eye + a_ab
    pw = _mm(a_ab, a_ab, RW_INV)
    levels = int(math.log2(c)) - 1
    for lvl in range(1, levels):
        prod = _mm(jnp.concatenate([x, pw], axis=1), pw, RW_INV)
        x = x + prod[:, 0:c]
        pw = prod[:, c:2 * c]
    x = x + _mm(x, pw, RW_INV)
    wu = _mm(x, jnp.concatenate([at, _mm(a_ak, v, RW_APPLY)], axis=2), RW_APPLY)
    lower = jnp.concatenate([jnp.zeros_like(v), v], axis=2)
    qy = _mm(jnp.concatenate([r_b, r_k], axis=2), jnp.concatenate([wu, lower], axis=1), RW_APPLY)
    wb = _mm(wu, bl, RW_STATE, "tn")
    n = wb[:, HEAD_DIM:] + _mm(v, kl, RW_STATE, "tn")
    m = eye * pc + wb[:, 0:HEAD_DIM]
    return rt + qy[:, :, 0:HEAD_DIM], qy[:, :, HEAD_DIM:], m, n


def _rwkv_local_kernel(*refs):
    prep_refs, (q_ref, y0_ref, m_ref, n_ref, ro_ref, ko_ref, vo_ref, g_ref, sc_ref) = refs[:16], refs[16:]
    r, lw, k, v, kkn, kka, g = _rwkv_prep_values(*prep_refs, sc_ref)
    ro_ref[...] = r
    ko_ref[...] = k
    vo_ref[...] = v
    g_ref[...] = g
    c = RWKV_CHUNK
    outs = [_rwkv_local_chunk(*(x[j * c:(j + 1) * c, :] for x in (r, lw, k, v, kkn, kka)))
            for j in range(r.shape[0] // c)]
    for j, res in enumerate(outs):
        for o_ref, val in zip((q_ref, y0_ref, m_ref, n_ref), res):
            for h in range(N_HEADS):
                o_ref[j * c:(j + 1) * c, h * HEAD_DIM:(h + 1) * HEAD_DIM] = val[h]


def _rwkv_scan_kernel(q_ref, y0_ref, m_ref, n_ref, r_ref, k_ref, v_ref, g_ref, lng_ref, lnb_ref, rk_ref,
                      o_ref, s_ref, y_sc):
    @pl.when(pl.program_id(0) == 0)
    def _():
        s_ref[...] = jnp.zeros_like(s_ref)

    c = RWKV_CHUNK
    s = s_ref[...]
    for j in range(q_ref.shape[0] // c):
        rows = slice(j * c, (j + 1) * c)
        y = _mm(_heads(q_ref[rows, :]), s, RW_SCAN, "nt") + _heads(y0_ref[rows, :])
        s = _mm(s, _heads(m_ref[rows, :]), RW_SCAN) + _heads(n_ref[rows, :])
        for h in range(N_HEADS):
            y_sc[rows, h * HEAD_DIM:(h + 1) * HEAD_DIM] = y[h]
    s_ref[...] = s
    y = y_sc[...]
    avg = _block_diag_const(MIX_W, HEAD_DIM, 1.0 / HEAD_DIM)
    mean = _mm_exact_rhs(y, avg, 2)
    yc = y - mean
    var = _mm_exact_rhs(yc * yc, avg, 2)
    yn = yc * lax.rsqrt(var + RWKV_LN_EPS) * lng_ref[...] + lnb_ref[...]
    v = v_ref[...]
    bonus = _mm_exact_rhs(r_ref[...] * k_ref[...] * rk_ref[...], _block_diag_const(MIX_W, HEAD_DIM, 1.0), 2) * v
    o_ref[...] = ((yn + bonus) * g_ref[...]).astype(o_ref.dtype)


def _rwkv_branch(p, mu, w0, w_up, a0, a_up, g_up, k_k, k_a, r_k, ln_g, ln_b, tb_scan=256):
    t = p.shape[0]
    tb = RWKV_LOCAL_CHUNKS * RWKV_CHUNK
    nb8 = tb // 8

    def blk(width, off):
        return pl.BlockSpec((tb, width), lambda i: (i, off // width))

    def tail(width, off):
        return pl.BlockSpec((8, width), lambda i: (jnp.maximum(i * nb8 - 1, 0), off // width))

    def vec(width):
        return pl.BlockSpec((1, width), lambda i: (0, 0))

    w_lora = jnp.zeros((256, 3 * MIX_W), F32)
    w_lora = w_lora.at[0:32, 0:MIX_W].set(w_up)
    w_lora = w_lora.at[32:64, MIX_W:2 * MIX_W].set(a_up)
    w_lora = w_lora.at[64:160, 2 * MIX_W:].set(g_up)
    b0 = jnp.concatenate([w0, a0, jnp.zeros((MIX_W,), F32)]).reshape(1, 3 * MIX_W)
    mu_l = jnp.concatenate([mu[3 * MIX_W:], jnp.zeros((256 - RWKV_LORA,), F32)]).reshape(1, 256)
    f = jax.ShapeDtypeStruct((t, MIX_W), F32)
    oblk = pl.BlockSpec((tb, MIX_W), lambda i: (i, 0))
    qm, y0, m, n, r, k, v, g = pl.pallas_call(
        _rwkv_local_kernel,
        grid=(t // tb,),
        in_specs=[blk(MIX_W, OFF_AR), blk(MIX_W, OFF_AK), blk(MIX_W, OFF_AV), blk(256, OFF_AL),
                  tail(MIX_W, OFF_AR), tail(MIX_W, OFF_AK), tail(MIX_W, OFF_AV), tail(256, OFF_AL),
                  vec(MIX_W), vec(MIX_W), vec(MIX_W), vec(256),
                  pl.BlockSpec((256, 3 * MIX_W), lambda i: (0, 0)), vec(3 * MIX_W), vec(MIX_W), vec(MIX_W)],
        out_specs=[oblk] * 8,
        out_shape=[f] * 8,
        scratch_shapes=[pltpu.VMEM((tb + 8, MIX_W), F32)],
        compiler_params=_cparams(("parallel",)),
        name="rwkv_local",
    )(p, p, p, p, p, p, p, p,
      mu[0:MIX_W].reshape(1, -1), mu[MIX_W:2 * MIX_W].reshape(1, -1), mu[2 * MIX_W:3 * MIX_W].reshape(1, -1), mu_l,
      w_lora, b0, k_k.reshape(1, -1), k_a.reshape(1, -1))
    sblk = pl.BlockSpec((tb_scan, MIX_W), lambda i: (i, 0))
    svec = pl.BlockSpec((1, MIX_W), lambda i: (0, 0))
    return pl.pallas_call(
        _rwkv_scan_kernel,
        grid=(t // tb_scan,),
        in_specs=[sblk] * 8 + [svec] * 3,
        out_specs=sblk,
        out_shape=jax.ShapeDtypeStruct((t, MIX_W), BF16),
        scratch_shapes=[pltpu.VMEM((N_HEADS, HEAD_DIM, HEAD_DIM), F32), pltpu.VMEM((tb_scan, MIX_W), F32)],
        compiler_params=_cparams(("arbitrary",)),
        name="rwkv_scan",
    )(qm, y0, m, n, r, k, v, g, ln_g.reshape(1, -1), ln_b.reshape(1, -1), r_k.reshape(1, -1))


def _rope_kernel(pos_ref, inv_ref, cos_ref, sin_ref):
    ang = pos_ref[...].astype(F32) * inv_ref[...]
    lane = lax.broadcasted_iota(jnp.int32, ang.shape, 1)
    cos_ref[...] = jnp.cos(ang)
    sin_ref[...] = jnp.where(lane % HEAD_DIM < HEAD_DIM // 2, -jnp.sin(ang), jnp.sin(ang))


def _rope_table(pos_b, tb=512):
    t = pos_b.shape[0]
    d = HEAD_DIM
    inv = 1.0 / (ROPE_BASE ** (jnp.arange(0, d, 2, dtype=F32) / d))
    inv_b = jnp.tile(inv, 4).reshape(1, 128)
    blk = pl.BlockSpec((tb, 128), lambda i: (i, 0))
    return pl.pallas_call(
        _rope_kernel,
        grid=(t // tb,),
        in_specs=[blk, pl.BlockSpec((1, 128), lambda i: (0, 0))],
        out_specs=[blk, blk],
        out_shape=[jax.ShapeDtypeStruct((t, 128), F32)] * 2,
        compiler_params=_cparams(("parallel",)),
        name="rope_table",
    )(pos_b, inv_b)


def _ret_kernel(q_ref, k_ref, v_ref, g_ref, cos_ref, sin_ref, dec_ref, zeta_ref, xi_ref, cd_ref, ng_ref,
                o_ref, s_ref, o_sc):
    @pl.when(pl.program_id(0) == 0)
    def _():
        s_ref[...] = jnp.zeros_like(s_ref)

    cos = jnp.concatenate([cos_ref[...]] * 4, axis=1)
    sin = jnp.concatenate([sin_ref[...]] * 4, axis=1)
    lane = lax.broadcasted_iota(jnp.int32, cos.shape, 1)
    lower_half = lane % HEAD_DIM < HEAD_DIM // 2

    def rope(x):
        swapped = jnp.where(lower_half, pltpu.roll(x, MIX_W - HEAD_DIM // 2, 1), pltpu.roll(x, HEAD_DIM // 2, 1))
        return x * cos + swapped * sin

    q = _heads(rope(q_ref[...]))
    k = _heads(rope(k_ref[...]) * (HEAD_DIM ** -0.5))
    v = _heads(v_ref[...])
    s = s_ref[...]
    scores = _mm(q, k, "x1", "nt") * dec_ref[...]
    o = _mm(scores, v, "x1") + _mm(q * xi_ref[...], s, "x1")
    s_ref[...] = s * cd_ref[...] + _mm(k * zeta_ref[...], v, "x1", "tn")
    on = o * lax.rsqrt(jnp.mean(o * o, axis=-1, keepdims=True) + NORM_EPS)
    for h in range(N_HEADS):
        o_sc[:, h * HEAD_DIM:(h + 1) * HEAD_DIM] = on[h]
    gt = g_ref[...]
    o_ref[...] = (o_sc[...] * ng_ref[...] * (gt * _sigmoid(gt))).astype(o_ref.dtype)


def _retention(p, cos_t, sin_t, norm_g):
    t = p.shape[0]
    c = RET_CHUNK
    h = N_HEADS
    log_g = jnp.log(1.0 - jnp.exp(jnp.linspace(math.log(1.0 / 32), math.log(1.0 / 512), h, dtype=F32)))
    idx = jnp.arange(c, dtype=F32)
    diff = idx[:, None] - idx[None, :]
    decay = jnp.where(diff >= 0, jnp.exp(log_g[:, None, None] * jnp.maximum(diff, 0.0)[None]), 0.0)
    zeta = jnp.exp(log_g[:, None] * (c - 1 - idx)[None])
    xi = jnp.exp(log_g[:, None] * (idx + 1)[None])
    cdec = jnp.exp(log_g * c)
    zeta_b = jnp.broadcast_to(zeta[:, :, None], (h, c, HEAD_DIM))
    xi_b = jnp.broadcast_to(xi[:, :, None], (h, c, HEAD_DIM))
    cd_b = jnp.broadcast_to(cdec[:, None, None], (h, HEAD_DIM, HEAD_DIM))

    def blk(off):
        return pl.BlockSpec((c, MIX_W), lambda i: (i, off // MIX_W))

    tab = pl.BlockSpec((c, 128), lambda i: (i, 0))

    def const(shape):
        return pl.BlockSpec(shape, lambda i: (0,) * len(shape))

    return pl.pallas_call(
        _ret_kernel,
        grid=(t // c,),
        in_specs=[blk(OFF_BQ), blk(OFF_BK), blk(OFF_BV), blk(OFF_BG), tab, tab,
                  const((h, c, c)), const((h, c, HEAD_DIM)), const((h, c, HEAD_DIM)),
                  const((h, HEAD_DIM, HEAD_DIM)), const((1, MIX_W))],
        out_specs=pl.BlockSpec((c, MIX_W), lambda i: (i, 0)),
        out_shape=jax.ShapeDtypeStruct((t, MIX_W), BF16),
        scratch_shapes=[pltpu.VMEM((h, HEAD_DIM, HEAD_DIM), F32), pltpu.VMEM((c, MIX_W), F32)],
        compiler_params=_cparams(("arbitrary",)),
        name="retention",
    )(p, p, p, p, cos_t, sin_t, decay, zeta_b, xi_b, cd_b, norm_g.reshape(1, -1))


def _lru_kernel(x_ref, xt_ref, gb_ref, pos_ref, cw_ref, cb_ref, wa_ref, ba_ref, wx_ref, bx_ref, lam_ref,
                o_ref, sc_ref, a_sc, b_sc, h_ref):
    tb = x_ref.shape[0]
    first = pl.program_id(0) == 0

    @pl.when(first)
    def _():
        h_ref[...] = jnp.zeros_like(h_ref)

    x = x_ref[...]
    sc_ref[0:8, :] = jnp.where(first, 0.0, xt_ref[...])
    sc_ref[8:8 + tb, :] = x
    cw = cw_ref[...]
    xc = cb_ref[...] + x * cw[CONV_W - 1:CONV_W, :]
    for j in range(1, CONV_W):
        xc = xc + sc_ref[8 - j:8 - j + tb, :] * cw[CONV_W - 1 - j:CONV_W - j, :]
    r = _sigmoid(_mm(xc, wa_ref[...], "x3") + ba_ref[...])
    ig = _sigmoid(_mm(xc, wx_ref[...], "x3") + bx_ref[...])
    nl = -lam_ref[...]
    softplus = jnp.maximum(nl, 0.0) + jnp.log1p(jnp.exp(-jnp.abs(nl)))
    log_a = -LRU_C * r * softplus
    pos = pos_ref[...]
    reset = jnp.concatenate([pos, pos, pos, pos], axis=1) == 0
    a_sc[...] = jnp.where(reset, 0.0, jnp.exp(log_a))
    th = jnp.tanh(log_a)
    b_sc[...] = jnp.where(reset, 1.0, jnp.sqrt(-2.0 * th / (1.0 - th))) * ig * xc

    row = lax.broadcasted_iota(jnp.int32, (8, MIX_W), 0)

    def group(gi, h):
        off = pl.multiple_of(gi * 8, 8)
        a = a_sc[pl.ds(off, 8), :]
        b = b_sc[pl.ds(off, 8), :]
        for d in (1, 2, 4):
            keep = row >= d
            b = jnp.where(keep, a * pltpu.roll(b, d, 0) + b, b)
            a = jnp.where(keep, a * pltpu.roll(a, d, 0), a)
        hs = a * h + b
        b_sc[pl.ds(off, 8), :] = hs
        return hs[7:8, :]

    h_ref[...] = lax.fori_loop(0, tb // 8, group, h_ref[...])
    gb = gb_ref[...]
    gelu = gb * (0.5 * (1.0 + jnp.tanh(math.sqrt(2.0 / math.pi) * (gb + 0.044715 * (gb * gb * gb)))))
    o_ref[...] = (b_sc[...] * gelu).astype(o_ref.dtype)


def _block_diag_weight(w):
    g, n, _ = w.shape
    eye = jnp.eye(g, dtype=w.dtype)
    return (eye[:, None, :, None] * w[:, :, None, :]).reshape(g * n, g * n)


def _rglru(p, pos_b, conv_w, conv_b, w_a, b_a, w_x, b_x, lam, tb=256):
    t = p.shape[0]
    nb8 = tb // 8
    vec = pl.BlockSpec((1, MIX_W), lambda i: (0, 0))
    mat = pl.BlockSpec((MIX_W, MIX_W), lambda i: (0, 0))
    cw8 = jnp.concatenate([conv_w, jnp.zeros((8 - CONV_W, MIX_W), F32)], axis=0)
    return pl.pallas_call(
        _lru_kernel,
        grid=(t // tb,),
        in_specs=[pl.BlockSpec((tb, MIX_W), lambda i: (i, OFF_CX // MIX_W)),
                  pl.BlockSpec((8, MIX_W), lambda i: (jnp.maximum(i * nb8 - 1, 0), OFF_CX // MIX_W)),
                  pl.BlockSpec((tb, MIX_W), lambda i: (i, OFF_CG // MIX_W)),
                  pl.BlockSpec((tb, 128), lambda i: (i, 0)),
                  pl.BlockSpec((8, MIX_W), lambda i: (0, 0)), vec, mat, vec, mat, vec, vec],
        out_specs=pl.BlockSpec((tb, MIX_W), lambda i: (i, 0)),
        out_shape=jax.ShapeDtypeStruct((t, MIX_W), BF16),
        scratch_shapes=[pltpu.VMEM((tb + 8, MIX_W), F32), pltpu.VMEM((tb, MIX_W), F32),
                        pltpu.VMEM((tb, MIX_W), F32), pltpu.VMEM((1, MIX_W), F32)],
        compiler_params=_cparams(("arbitrary",)),
        name="rglru",
    )(p, p, p, pos_b, cw8, conv_b.reshape(1, -1), _block_diag_weight(w_a), b_a.reshape(1, -1),
      _block_diag_weight(w_x), b_x.reshape(1, -1), lam.reshape(1, -1))


def _gla_kernel(v_ref, gt_ref, q_ref, k_ref, al_ref, aup_ref, ab_ref, ng_ref, o_ref, s_ref):
    @pl.when(pl.program_id(0) == 0)
    def _():
        s_ref[...] = jnp.zeros_like(s_ref)

    c = GLA_CHUNK
    row = lax.broadcasted_iota(jnp.int32, (c, c), 0)
    col = lax.broadcasted_iota(jnp.int32, (c, c), 1)
    causal = row >= col
    pre = _mm(al_ref[...], aup_ref[...], "x3") + ab_ref[...]
    log_a = (jnp.minimum(pre, 0.0) - jnp.log1p(jnp.exp(-jnp.abs(pre)))) / GLA_GATE_NORM
    s = s_ref[...]
    outs = []
    for j in range(q_ref.shape[0] // c):
        rows = slice(j * c, (j + 1) * c)
        bcum = _mm_exact_lhs(causal.astype(F32), log_a[rows], 3)
        blast = bcum[c - 1:c, :]
        k = k_ref[rows, :]
        q_e = _heads(q_ref[rows, :] * (GLA_DK ** -0.5) * jnp.exp(bcum))
        k_e = _heads(k * jnp.exp(-bcum))
        k_l = _heads(k * jnp.exp(blast - bcum))
        dec = _heads(jnp.exp(blast))
        v = _heads(v_ref[rows, :], GLA_DV)
        scores = jnp.where(causal, _mm(q_e, k_e, "x1", "nt"), 0.0)
        o = _mm(scores, v, "x1") + _mm(q_e, s, "x1", "nt")
        s = s * dec + _mm(v, k_l, "x1", "tn")
        on = o * lax.rsqrt(jnp.mean(o * o, axis=-1, keepdims=True) + NORM_EPS)
        outs.append(jnp.concatenate([on[h] for h in range(GLA_HEADS)], axis=1))
    s_ref[...] = s
    on = jnp.concatenate(outs, axis=0) * ng_ref[...]
    gt = gt_ref[...]
    o_ref[...] = (on * (gt * _sigmoid(gt))).astype(o_ref.dtype)


def _gla(p, alpha_up, alpha_b, norm_g):
    t = p.shape[0]
    c = GLA_STEP_CHUNKS * GLA_CHUNK
    hk = MIX_W // 2
    aup = jnp.concatenate([alpha_up, jnp.zeros((128 - GLA_LORA, hk), F32)], axis=0)

    def blk(width, off):
        return pl.BlockSpec((c, width), lambda i: (i, off // width))

    return pl.pallas_call(
        _gla_kernel,
        grid=(t // c,),
        in_specs=[blk(MIX_W, OFF_DV), blk(MIX_W, OFF_DGT), blk(hk, OFF_DQ), blk(hk, OFF_DK), blk(128, OFF_DAL),
                  pl.BlockSpec((128, hk), lambda i: (0, 0)), pl.BlockSpec((1, hk), lambda i: (0, 0)),
                  pl.BlockSpec((1, MIX_W), lambda i: (0, 0))],
        out_specs=pl.BlockSpec((c, MIX_W), lambda i: (i, 0)),
        out_shape=jax.ShapeDtypeStruct((t, MIX_W), BF16),
        scratch_shapes=[pltpu.VMEM((GLA_HEADS, GLA_DV, GLA_DK), F32)],
        compiler_params=_cparams(("arbitrary",)),
        name="gla",
    )(p, p, p, p, p, aup, alpha_b.reshape(1, -1), norm_g.reshape(1, -1))


def _merge_kernel(ya_ref, yb_ref, yc_ref, yd_ref, wa_ref, wb_ref, wc_ref, wd_ref,
                  ga_ref, gb_ref, gc_ref, gd_ref, o_ref, w_bf):
    @pl.when(pl.program_id(1) == 0)
    def _():
        for kk, w_ref in enumerate((wa_ref, wb_ref, wc_ref, wd_ref)):
            w_bf[kk] = w_ref[0].astype(BF16)

    acc = _sigmoid(ga_ref[...].astype(F32)) * _dot(ya_ref[...], w_bf[0])
    acc += _sigmoid(gb_ref[...].astype(F32)) * _dot(yb_ref[...], w_bf[1])
    acc += _sigmoid(gc_ref[...].astype(F32)) * _dot(yc_ref[...], w_bf[2])
    acc += _sigmoid(gd_ref[...].astype(F32)) * _dot(yd_ref[...], w_bf[3])
    o_ref[...] = acc.astype(o_ref.dtype)


def _merge(ys, ws, layer, gates, tm=512, tn=512):
    t = gates.shape[0]
    yspec = pl.BlockSpec((tm, MIX_W), lambda j, i: (i, 0))
    wspec = pl.BlockSpec((1, MIX_W, tn), lambda j, i: (layer, 0, j))

    def gspec(kk):
        return pl.BlockSpec((tm, tn), lambda j, i: (i, kk * D_MODEL // tn + j))

    return pl.pallas_call(
        _merge_kernel,
        grid=(D_MODEL // tn, t // tm),
        in_specs=[yspec] * 4 + [wspec] * 4 + [gspec(kk) for kk in range(4)],
        out_specs=pl.BlockSpec((tm, tn), lambda j, i: (i, j)),
        out_shape=jax.ShapeDtypeStruct((t, D_MODEL), BF16),
        scratch_shapes=[pltpu.VMEM((4, MIX_W, tn), BF16)],
        compiler_params=_cparams(("parallel", "arbitrary")),
        name="merge",
    )(*ys, *ws, gates, gates, gates, gates)


def _ffn_norm_kernel(h_ref, g_ref, wr_ref, br_ref, hn_ref, rt_ref):
    x = h_ref[...]
    ms = jnp.mean(x * x, axis=-1, keepdims=True)
    hn = x * lax.rsqrt(ms + NORM_EPS) * g_ref[...]
    hn_ref[...] = hn
    z = _mm(hn, wr_ref[...], "x3") + br_ref[...]
    lane = lax.broadcasted_iota(jnp.int32, z.shape, 1)
    neg = jnp.float32(-1e30)

    def first_argmax(v):
        m = jnp.max(v, axis=-1, keepdims=True)
        return m, jnp.min(jnp.where(v == m, lane, 128), axis=-1, keepdims=True)

    is_group = lane < N_GROUPS
    zg = jnp.where(is_group, z, neg)
    mg, g_idx = first_argmax(zg)
    pg_top = 1.0 / jnp.sum(jnp.where(is_group, jnp.exp(zg - mg), 0.0), axis=-1, keepdims=True)
    lo = N_GROUPS + g_idx * EXP_PER_GROUP
    in_group = jnp.logical_and(lane >= lo, lane < lo + EXP_PER_GROUP)
    ze = jnp.where(in_group, z, neg)
    m1, i1 = first_argmax(ze)
    se = jnp.sum(jnp.where(in_group, jnp.exp(ze - m1), 0.0), axis=-1, keepdims=True)
    m2, i2 = first_argmax(jnp.where(lane == i1, neg, ze))
    p1 = 1.0 / se
    p2 = jnp.exp(m2 - m1) / se
    tot = p1 + p2
    rt_ref[...] = jnp.where(lane == 0, pg_top * (p1 / tot),
                            jnp.where(lane == 1, pg_top * (p2 / tot),
                                      jnp.where(lane == 2, (i1 - N_GROUPS).astype(F32),
                                                jnp.where(lane == 3, (i2 - N_GROUPS).astype(F32), 0.0))))


def _ffn_norm_router(h, g, w_r, b_r, tb=256):
    t, d = h.shape
    return pl.pallas_call(
        _ffn_norm_kernel,
        grid=(t // tb,),
        in_specs=[pl.BlockSpec((tb, d), lambda i: (i, 0)), pl.BlockSpec((1, d), lambda i: (0, 0)),
                  pl.BlockSpec((d, 128), lambda i: (0, 0)), pl.BlockSpec((1, 128), lambda i: (0, 0))],
        out_specs=[pl.BlockSpec((tb, d), lambda i: (i, 0)), pl.BlockSpec((tb, 128), lambda i: (i, 0))],
        out_shape=[jax.ShapeDtypeStruct((t, d), F32), jax.ShapeDtypeStruct((t, 128), F32)],
        compiler_params=_cparams(("parallel",)),
        name="ffn_norm_router",
    )(h, g.reshape(1, d), w_r, b_r)


def _row_copy(src_hbm, idx, dst, r, sem):
    return pltpu.make_async_copy(src_hbm.at[pl.ds(idx, 1), :], dst.at[pl.ds(r, 1), :], sem)


def _expert_changed(i, exp_ref):
    return jnp.logical_or(i == 0, exp_ref[i] != exp_ref[jnp.maximum(i - 1, 0)])


def _gather_rows(src_hbm, row_of, n, dst, sem):
    def start(r, carry):
        _row_copy(src_hbm, row_of(r), dst, r, sem).start()
        return carry

    lax.fori_loop(0, n, start, 0, unroll=8)


def _gather_wait(src_hbm, dst, sem):
    pltpu.make_async_copy(src_hbm.at[pl.ds(0, dst.shape[0]), :], dst, sem).wait()


def _expert_kernel(st_ref, exp_ref, base_ref, nxt_ref, nblk_ref, x_hbm, wg_hbm, wu_hbm, wd_hbm, o_ref,
                   xbuf, stage_g, stage_u, stage_d, wg_bf, wu_bf, wd_bf, xsem, wsem, *, layer):
    i = pl.program_id(0)
    n = nblk_ref[0]
    slot = i % 2

    stages = (stage_g, stage_u, stage_d)
    caches = (wg_bf, wu_bf, wd_bf)

    def chunk_rows(m, c):
        rows = stages[m].shape[0] // WEIGHT_CHUNKS
        return pl.ds(c * rows, rows)

    def weight_copy(e, m, c):
        src = (wg_hbm, wu_hbm, wd_hbm)[m]
        return pltpu.make_async_copy(src.at[layer, e, chunk_rows(m, c), :], stages[m].at[chunk_rows(m, c), :],
                                     wsem.at[m, c])

    def start_weights(e):
        for c in range(WEIGHT_CHUNKS):
            for m in range(3):
                weight_copy(e, m, c).start(priority=1)

    last = st_ref.shape[0] - 1

    def fetch(blk, sl):
        base = base_ref[blk]
        _gather_rows(x_hbm, lambda r: st_ref[jnp.minimum(base + r, last)], MOE_BLOCK, xbuf.at[sl], xsem.at[sl])

    @pl.when(jnp.logical_and(i == 0, n > 0))
    def _():
        fetch(0, 0)
        start_weights(exp_ref[0])

    @pl.when(i + 1 < n)
    def _():
        fetch(i + 1, 1 - slot)

    @pl.when(i < n)
    def _():
        @pl.when(_expert_changed(i, exp_ref))
        def _():
            nxt = nxt_ref[i]
            for c in range(WEIGHT_CHUNKS):
                for m in range(3):
                    weight_copy(exp_ref[i], m, c).wait()
                    caches[m][chunk_rows(m, c), :] = stages[m][chunk_rows(m, c), :].astype(BF16)

                @pl.when(nxt >= 0)
                def _():
                    for m in range(3):
                        weight_copy(nxt, m, c).start(priority=1)

        _gather_wait(x_hbm, xbuf.at[slot], xsem.at[slot])
        x = xbuf[slot].astype(BF16)
        gate = _dot(x, wg_bf[...])
        up = _dot(x, wu_bf[...])
        hmid = (gate * _sigmoid(gate) * up).astype(BF16)
        o_ref[...] = _dot(hmid, wd_bf[...])

    @pl.when(i >= n)
    def _():
        o_ref[...] = jnp.zeros_like(o_ref)


def _experts(hn, st, plan, w_gate, w_up, w_down, layer):
    block_exp, src_base, next_exp, n_used, _ = plan
    n_blocks = block_exp.shape[0]
    d = hn.shape[1]
    hbm = pl.BlockSpec(memory_space=pl.ANY)
    return pl.pallas_call(
        functools.partial(_expert_kernel, layer=layer),
        grid_spec=pltpu.PrefetchScalarGridSpec(
            num_scalar_prefetch=5,
            grid=(n_blocks,),
            in_specs=[hbm, hbm, hbm, hbm],
            out_specs=pl.BlockSpec((MOE_BLOCK, d), lambda i, *_: (i, 0)),
            scratch_shapes=[pltpu.VMEM((2, MOE_BLOCK, d), F32),
                            pltpu.VMEM((d, D_EXPERT), F32), pltpu.VMEM((d, D_EXPERT), F32),
                            pltpu.VMEM((D_EXPERT, d), F32),
                            pltpu.VMEM((d, D_EXPERT), BF16), pltpu.VMEM((d, D_EXPERT), BF16),
                            pltpu.VMEM((D_EXPERT, d), BF16),
                            pltpu.SemaphoreType.DMA((2,)), pltpu.SemaphoreType.DMA((3, WEIGHT_CHUNKS))],
        ),
        out_shape=jax.ShapeDtypeStruct((n_blocks * MOE_BLOCK, d), F32),
        compiler_params=_cparams(("arbitrary",), VMEM_LIMIT_EXPERT),
        name="experts",
    )(st, block_exp, src_base, next_exp, n_used, hn, w_gate, w_up, w_down)


def _combine_kernel(pos_ref, y_hbm, h_ref, rt_ref, g_ref, o_ref, buf, sem, *, final_norm):
    tb = h_ref.shape[0]
    i = pl.program_id(0)
    slot = i % 2

    def fetch(blk, sl):
        for s in range(TOP_K):
            base = (s * pl.num_programs(0) + blk) * tb
            _gather_rows(y_hbm, lambda r: pos_ref[base + r], tb, buf.at[sl, s], sem.at[sl])

    @pl.when(i == 0)
    def _():
        fetch(0, 0)

    @pl.when(i + 1 < pl.num_programs(0))
    def _():
        fetch(i + 1, 1 - slot)

    for s in range(TOP_K):
        _gather_wait(y_hbm, buf.at[slot, s], sem.at[slot])
    rt = rt_ref[...]
    out = h_ref[...] + (rt[:, 0:1] * buf[slot, 0] + rt[:, 1:2] * buf[slot, 1])
    if final_norm:
        ms = jnp.mean(out * out, axis=-1, keepdims=True)
        out = out * lax.rsqrt(ms + NORM_EPS) * g_ref[...]
    o_ref[...] = out


def _combine(h, y_rows, pos, route, g, final_norm, tb=256):
    t, d = h.shape
    grid_spec = pltpu.PrefetchScalarGridSpec(
        num_scalar_prefetch=1,
        grid=(t // tb,),
        in_specs=[pl.BlockSpec(memory_space=pl.ANY),
                  pl.BlockSpec((tb, d), lambda i, *_: (i, 0)),
                  pl.BlockSpec((tb, 128), lambda i, *_: (i, 0)),
                  pl.BlockSpec((1, d), lambda i, *_: (0, 0))],
        out_specs=pl.BlockSpec((tb, d), lambda i, *_: (i, 0)),
        scratch_shapes=[pltpu.VMEM((2, TOP_K, tb, d), F32), pltpu.SemaphoreType.DMA((2,))],
    )
    return pl.pallas_call(
        functools.partial(_combine_kernel, final_norm=final_norm),
        grid_spec=grid_spec,
        out_shape=jax.ShapeDtypeStruct((t, d), F32),
        compiler_params=_cparams(("arbitrary",)),
        name="moe_combine",
    )(pos, y_rows, h, route, g.reshape(1, d))


def _plan_kernel(counts_ref, rank_ref, eid_ref, bexp_ref, base_ref, nxt_ref, nused_ref, pos_ref, shift_ref):
    n_blocks = bexp_ref.shape[0]

    def per_expert(e, carry):
        start, pad_start = carry
        nb = (counts_ref[e] + MOE_BLOCK - 1) // MOE_BLOCK
        shift_ref[e] = pad_start - start

        def fill(j, c):
            blk = pad_start // MOE_BLOCK + j
            bexp_ref[blk] = e
            base_ref[blk] = start + j * MOE_BLOCK
            return c

        lax.fori_loop(0, nb, fill, 0)
        return start + counts_ref[e], pad_start + nb * MOE_BLOCK

    _, pad_end = lax.fori_loop(0, N_EXPERTS, per_expert, (jnp.int32(0), jnp.int32(0)))
    n_used = pad_end // MOE_BLOCK
    nused_ref[0] = n_used

    def tail(blk, c):
        bexp_ref[blk] = N_EXPERTS - 1
        base_ref[blk] = 0
        nxt_ref[blk] = -1
        return c

    lax.fori_loop(n_used, n_blocks, tail, 0)

    def backward(k, carry):
        nxt, cur = carry
        blk = n_used - 1 - k
        e = bexp_ref[blk]
        nxt = jnp.where(e != cur, cur, nxt)
        nxt_ref[blk] = nxt
        return nxt, e

    lax.fori_loop(0, n_used, backward, (jnp.int32(-1), jnp.int32(-1)))

    eid = eid_ref[...]
    pos = rank_ref[...]
    for e in range(N_EXPERTS):
        pos = pos + jnp.where(eid == e, shift_ref[e], 0)
    pos_ref[...] = pos


def _plan(counts, rank, eid, n_blocks):
    smem = pl.BlockSpec(memory_space=pltpu.SMEM)
    vmem = pl.BlockSpec(memory_space=pltpu.VMEM)
    i32 = lambda n: jax.ShapeDtypeStruct((n,), jnp.int32)
    return pl.pallas_call(
        _plan_kernel,
        in_specs=[smem, vmem, vmem],
        out_specs=[smem, smem, smem, smem, vmem],
        out_shape=[i32(n_blocks), i32(n_blocks), i32(n_blocks), i32(1), jax.ShapeDtypeStruct(rank.shape, jnp.int32)],
        scratch_shapes=[pltpu.SMEM((N_EXPERTS,), jnp.int32)],
        name="moe_plan",
    )(counts, rank, eid)


def _moe_layer(h, norm_g, wg_r, bg_r, we_r, be_r, w_gate, w_up, w_down, layer, out_g, final_norm):
    t, d = h.shape
    w_r = jnp.concatenate([wg_r, we_r, jnp.zeros((d, 128 - N_GROUPS - N_EXPERTS), F32)], axis=1)
    b_r = jnp.concatenate([bg_r, be_r, jnp.zeros((128 - N_GROUPS - N_EXPERTS,), F32)]).reshape(1, 128)
    hn, route = _ffn_norm_router(h, norm_g, w_r, b_r)
    expert_id = route[:, TOP_K:2 * TOP_K].astype(jnp.int32)
    flat = expert_id.reshape(-1)
    order = jnp.argsort(flat).astype(jnp.int32)
    st = order // TOP_K
    rank_sm = jnp.argsort(order).astype(jnp.int32).reshape(t, TOP_K).T.reshape(-1, 128)
    eid_sm = expert_id.T.reshape(-1, 128)
    counts = jnp.sum((flat[:, None] == jnp.arange(N_EXPERTS, dtype=jnp.int32)[None, :]).astype(jnp.int32), axis=0)
    plan = _plan(counts, rank_sm, eid_sm, (t * TOP_K + MOE_BLOCK - 1) // MOE_BLOCK + N_EXPERTS)
    y_rows = _experts(hn, st, plan, w_gate, w_up, w_down, layer)
    return _combine(h, y_rows, plan[4].reshape(-1), route, out_g, final_norm)


_D0 = A_COLS + B_COLS + C_COLS
_W_IN_PIECES = (
    (OFF_BQ, A_COLS, B_COLS), (OFF_CX, A_COLS + B_COLS, C_COLS),
    (OFF_DV, _D0 + 512, 512), (OFF_DGT, _D0 + 1040, 512),
    (OFF_AR, 0, 3 * MIX_W), (OFF_DQ, _D0, 512),
    (OFF_AL, 3 * MIX_W, RWKV_LORA), (OFF_DAL, _D0 + 1024, GLA_LORA),
    (OFF_G, _D0 + D_COLS, 4 * D_MODEL),
)


RELAYOUT_ROWS = 512
_SPECIAL_BLOCK = OFF_AL // RELAYOUT_ROWS


def _w_in_source_rows():
    rows = np.zeros((NP_COLS // RELAYOUT_ROWS,), np.int32)
    for dst, src, width in _W_IN_PIECES:
        if width % RELAYOUT_ROWS == 0:
            for k in range(width // RELAYOUT_ROWS):
                rows[dst // RELAYOUT_ROWS + k] = src + k * RELAYOUT_ROWS
    return rows


def _relayout_kernel(src_ref, w_hbm, extra_hbm, o_ref, stage, sem, *, layer):
    j = pl.program_id(0)
    slot = j % 2

    def start(blk, sl):
        @pl.when(blk == _SPECIAL_BLOCK)
        def _():
            pltpu.make_async_copy(extra_hbm, stage.at[sl], sem.at[sl]).start()

        @pl.when(blk != _SPECIAL_BLOCK)
        def _():
            rows = pl.ds(pl.multiple_of(src_ref[blk], 16), RELAYOUT_ROWS)
            pltpu.make_async_copy(w_hbm.at[layer, rows, :], stage.at[sl], sem.at[sl]).start()

    @pl.when(j == 0)
    def _():
        start(0, 0)

    @pl.when(j + 1 < pl.num_programs(0))
    def _():
        start(j + 1, 1 - slot)

    pltpu.make_async_copy(extra_hbm, stage.at[slot], sem.at[slot]).wait()
    o_ref[...] = stage[slot].astype(o_ref.dtype)


def _permute_w_in(w_in, layer):
    w_t = jnp.swapaxes(w_in, 1, 2)
    d = w_t.shape[2]
    zeros = lambda n: jnp.zeros((n, d), F32)
    extra = jnp.concatenate([w_t[layer, 3 * MIX_W:A_COLS], zeros(OFF_DAL - OFF_AL - RWKV_LORA),
                             w_t[layer, _D0 + 1024:_D0 + 1024 + GLA_LORA],
                             zeros(OFF_G - OFF_DAL - GLA_LORA)], axis=0)
    hbm = pl.BlockSpec(memory_space=pl.ANY)
    return pl.pallas_call(
        functools.partial(_relayout_kernel, layer=layer),
        grid_spec=pltpu.PrefetchScalarGridSpec(
            num_scalar_prefetch=1,
            grid=(NP_COLS // RELAYOUT_ROWS,),
            in_specs=[hbm, hbm],
            out_specs=pl.BlockSpec((RELAYOUT_ROWS, d), lambda j, *_: (j, 0)),
            scratch_shapes=[pltpu.VMEM((2, RELAYOUT_ROWS, d), F32), pltpu.SemaphoreType.DMA((2,))],
        ),
        out_shape=jax.ShapeDtypeStruct((NP_COLS, d), BF16),
        compiler_params=_cparams(("arbitrary",)),
        name="w_in_relayout",
    )(jnp.asarray(_w_in_source_rows()), w_t, extra)


def kernel(x, positions, norm_mix_g, w_in, rwkv_mu, rwkv_w0, rwkv_w_up, rwkv_a0, rwkv_a_up, rwkv_g_up, rwkv_k_k, rwkv_k_a, rwkv_r_k, rwkv_ln_g, rwkv_ln_b, rwkv_w_o, ret_norm_g, ret_w_o, lru_conv_w, lru_conv_b, lru_w_a, lru_b_a, lru_w_x, lru_b_x, lru_lambda, lru_w_o, gla_alpha_up, gla_alpha_b, gla_norm_g, gla_w_o, w_out, norm_ffn_g, router_group_w, router_group_b, router_expert_w, router_expert_b, moe_w_gate, moe_w_up, moe_w_down, final_norm_g):
    b_, s_, d = x.shape
    assert b_ == 1 and d == D_MODEL
    depth = w_in.shape[0]
    h = x.reshape(s_, d)
    pos_b = jnp.broadcast_to(positions.reshape(s_, 1), (s_, 128)).astype(jnp.int32)
    cos_t, sin_t = _rope_table(pos_b)
    for l in range(depth):
        xn = _rmsnorm(h, norm_mix_g[l], BF16)
        w_t = _permute_w_in(w_in, l)
        p = _matmul_nt(xn, w_t, 0, OFF_G, 2048, 512, F32, "in_proj")
        gates = _matmul_nt(xn, w_t, OFF_G, 4 * D_MODEL, 2048, 512, BF16, "in_proj_gates")
        y_a = _rwkv_branch(p, rwkv_mu[l], rwkv_w0[l], rwkv_w_up[l], rwkv_a0[l], rwkv_a_up[l], rwkv_g_up[l],
                           rwkv_k_k[l], rwkv_k_a[l], rwkv_r_k[l], rwkv_ln_g[l], rwkv_ln_b[l])
        y_b = _retention(p, cos_t, sin_t, ret_norm_g[l])
        y_c = _rglru(p, pos_b, lru_conv_w[l], lru_conv_b[l], lru_w_a[l], lru_b_a[l], lru_w_x[l], lru_b_x[l],
                     lru_lambda[l])
        y_d = _gla(p, gla_alpha_up[l], gla_alpha_b[l], gla_norm_g[l])
        merged = _merge((y_a, y_b, y_c, y_d), (rwkv_w_o, ret_w_o, lru_w_o, gla_w_o), l, gates)
        h = _matmul_residual(merged, w_out, l, h, 1024, 512)
        last = l == depth - 1
        h = _moe_layer(h, norm_ffn_g[l], router_group_w[l], router_group_b[l], router_expert_w[l],
                       router_expert_b[l], moe_w_gate, moe_w_up, moe_w_down, l,
                       final_norm_g if last else norm_ffn_g[l], last)
    return h.reshape(b_, s_, d)
```

```python
import functools
import math

import jax
import jax.numpy as jnp
import numpy as np
from jax import lax
from jax.experimental import pallas as pl
from jax.experimental.pallas import tpu as pltpu

F32 = jnp.float32
BF16 = jnp.bfloat16
HI = lax.Precision.HIGHEST

D_MODEL = 2048
MIX_W = 512
NORM_EPS = 1e-6
HEAD_DIM = 64
N_HEADS = MIX_W // HEAD_DIM

RWKV_W_LORA, RWKV_A_LORA, RWKV_G_LORA = 32, 32, 96
RWKV_LORA = RWKV_W_LORA + RWKV_A_LORA + RWKV_G_LORA
RWKV_DECAY_SCALE = 0.6065306597126334
RWKV_LN_EPS = 64e-5
RWKV_CHUNK = 64
RWKV_LOCAL_CHUNKS = 4

RET_CHUNK = 128
ROPE_BASE = 10000.0

LRU_BLOCKS = 8
CONV_W = 4
LRU_C = 8.0

GLA_HEADS = 4
GLA_DK = 64
GLA_DV = 128
GLA_LORA = 16
GLA_GATE_NORM = 16.0
GLA_CHUNK = 64
GLA_STEP_CHUNKS = 2

N_GROUPS = 4
EXP_PER_GROUP = 8
N_EXPERTS = N_GROUPS * EXP_PER_GROUP
TOP_K = 2
D_EXPERT = 1024
MOE_BLOCK = 128
WEIGHT_CHUNKS = 4

A_COLS = 3 * MIX_W + RWKV_LORA
B_COLS = 4 * MIX_W
C_COLS = 2 * MIX_W
D_COLS = 2 * (MIX_W // 2) + MIX_W + GLA_LORA + MIX_W

OFF_BQ, OFF_BK, OFF_BV, OFF_BG = 0, 512, 1024, 1536
OFF_CX, OFF_CG = 2048, 2560
OFF_DV, OFF_DGT = 3072, 3584
OFF_AR, OFF_AK, OFF_AV = 4096, 4608, 5120
OFF_DQ, OFF_DK = 5632, 5888
OFF_AL = 6144
OFF_DAL = 6400
OFF_G = 6656
NP_COLS = OFF_G + 4 * D_MODEL

VMEM_LIMIT = 48 * 1024 * 1024
VMEM_LIMIT_EXPERT = 56 * 1024 * 1024


def _cparams(sem, vmem=VMEM_LIMIT):
    return pltpu.CompilerParams(dimension_semantics=sem, vmem_limit_bytes=vmem)


def _dot(a, b, prec=None):
    return jnp.dot(a, b, precision=prec, preferred_element_type=F32)


def _sigmoid(x):
    return 1.0 / (1.0 + jnp.exp(-x))


def _split(x):
    hi = x.astype(BF16)
    return hi, (x - hi.astype(F32)).astype(BF16)


def _dims(form, batched):
    ca, cb = {"nn": (1, 0), "nt": (1, 1), "tn": (0, 0)}[form]
    if batched:
        return (((ca + 1,), (cb + 1,)), ((0,), (0,)))
    return (((ca,), (cb,)), ((), ()))


def _mm(a, b, mode, form="nn"):
    dims = _dims(form, a.ndim == 3)
    if mode == "hi":
        return lax.dot_general(a, b, dims, precision=HI, preferred_element_type=F32)
    d = functools.partial(lax.dot_general, dimension_numbers=dims, preferred_element_type=F32)
    if mode == "x1":
        return d(a.astype(BF16), b.astype(BF16))
    ah, al = _split(a)
    bh, bl = _split(b)
    return d(ah, bh) + (d(ah, bl) + d(al, bh))


def _mm_exact_rhs(a, b, terms):
    b = b.astype(BF16)
    acc = None
    for _ in range(terms):
        piece = a.astype(BF16)
        part = _dot(piece, b)
        acc = part if acc is None else acc + part
        a = a - piece.astype(F32)
    return acc


def _mm_exact_lhs(a, b, terms):
    a = a.astype(BF16)
    acc = None
    for _ in range(terms):
        piece = b.astype(BF16)
        part = _dot(a, piece)
        acc = part if acc is None else acc + part
        b = b - piece.astype(F32)
    return acc


def _heads(x, width=HEAD_DIM):
    return jnp.stack([x[:, h * width:(h + 1) * width] for h in range(x.shape[1] // width)])


def _block_diag_const(n, blk, value):
    r = lax.broadcasted_iota(jnp.int32, (n, n), 0) // blk
    c = lax.broadcasted_iota(jnp.int32, (n, n), 1) // blk
    return jnp.where(r == c, value, 0.0).astype(F32)


def _rmsnorm_kernel(x_ref, g_ref, o_ref):
    x = x_ref[...]
    ms = jnp.mean(x * x, axis=-1, keepdims=True)
    o_ref[...] = (x * lax.rsqrt(ms + NORM_EPS) * g_ref[...]).astype(o_ref.dtype)


def _rmsnorm(x, g, out_dtype, tb=512):
    t, d = x.shape
    return pl.pallas_call(
        _rmsnorm_kernel,
        grid=(t // tb,),
        in_specs=[pl.BlockSpec((tb, d), lambda i: (i, 0)), pl.BlockSpec((1, d), lambda i: (0, 0))],
        out_specs=pl.BlockSpec((tb, d), lambda i: (i, 0)),
        out_shape=jax.ShapeDtypeStruct((t, d), out_dtype),
        compiler_params=_cparams(("parallel",)),
        name="rmsnorm",
    )(x, g.reshape(1, d))


def _mm_nt_kernel(a_ref, b_ref, o_ref):
    o_ref[...] = _mm(a_ref[...], b_ref[...], "x1", "nt").astype(o_ref.dtype)


def _matmul_nt(a, b_t, row0, n, tm, tn, out_dtype, name):
    m, k = a.shape
    assert row0 % tn == 0 and n % tn == 0 and m % tm == 0
    return pl.pallas_call(
        _mm_nt_kernel,
        grid=(m // tm, n // tn),
        in_specs=[pl.BlockSpec((tm, k), lambda i, j: (i, 0)), pl.BlockSpec((tn, k), lambda i, j: (row0 // tn + j, 0))],
        out_specs=pl.BlockSpec((tm, tn), lambda i, j: (i, j)),
        out_shape=jax.ShapeDtypeStruct((m, n), out_dtype),
        compiler_params=_cparams(("parallel", "parallel")),
        name=name,
    )(a, b_t)


def _mm_res_kernel(a_ref, b_ref, r_ref, o_ref, b_bf):
    @pl.when(pl.program_id(1) == 0)
    def _():
        b_bf[...] = b_ref[...].astype(BF16)

    o_ref[...] = r_ref[...] + _dot(a_ref[...], b_bf[...])


def _matmul_residual(a, b, layer, res, tm, tn):
    m, k = a.shape
    n = b.shape[2]
    return pl.pallas_call(
        _mm_res_kernel,
        grid=(n // tn, m // tm),
        in_specs=[pl.BlockSpec((tm, k), lambda j, i: (i, 0)), pl.BlockSpec((None, k, tn), lambda j, i: (layer, 0, j)),
                  pl.BlockSpec((tm, tn), lambda j, i: (i, j))],
        out_specs=pl.BlockSpec((tm, tn), lambda j, i: (i, j)),
        out_shape=jax.ShapeDtypeStruct((m, n), F32),
        scratch_shapes=[pltpu.VMEM((k, tn), BF16)],
        compiler_params=_cparams(("parallel", "arbitrary")),
        name="wout_residual",
    )(a, b, res)


def _shifted(x, tail_ref, sc_ref, width, first):
    tb = x.shape[0]
    sc_ref[0:8, 0:width] = jnp.where(first, 0.0, tail_ref[...])
    sc_ref[8:8 + tb, 0:width] = x
    return sc_ref[7:7 + tb, 0:width]


def _rwkv_prep_values(r_ref, k_ref, v_ref, l_ref, rt_ref, kt_ref, vt_ref, lt_ref,
                      mur_ref, muk_ref, muv_ref, mul_ref, wl_ref, b0_ref, kk_ref, ka_ref, sc_ref):
    first = pl.program_id(0) == 0

    def mix(x_ref, t_ref, mu_ref, width):
        x = x_ref[...]
        prev = _shifted(x, t_ref, sc_ref, width, first)
        return x + (prev - x) * mu_ref[...]

    r = mix(r_ref, rt_ref, mur_ref, MIX_W)
    k = mix(k_ref, kt_ref, muk_ref, MIX_W)
    v = mix(v_ref, vt_ref, muv_ref, MIX_W)
    zl = mix(l_ref, lt_ref, mul_ref, 256)
    lane = lax.broadcasted_iota(jnp.int32, zl.shape, 1)
    act = jnp.where(lane < RWKV_W_LORA, jnp.tanh(zl),
                    jnp.where(lane < RWKV_W_LORA + RWKV_A_LORA, zl, _sigmoid(zl)))
    lo = _mm(act, wl_ref[...], "x3") + b0_ref[...]
    lw = -RWKV_DECAY_SCALE * _sigmoid(lo[:, 0:MIX_W])
    a = _sigmoid(lo[:, MIX_W:2 * MIX_W])
    g = lo[:, 2 * MIX_W:3 * MIX_W]
    kk = k * kk_ref[...]
    ss = _mm_exact_rhs(kk * kk, _block_diag_const(MIX_W, HEAD_DIM, 1.0), 2)
    kkn = kk / jnp.maximum(jnp.sqrt(ss), 1e-12)
    return r, lw, k * (1.0 + (a - 1.0) * ka_ref[...]), v, kkn, kkn * a, g


RW_SC, RW_INV, RW_APPLY, RW_STATE, RW_SCAN = "x1", "x1", "x1", "x1", "x1"


def _rwkv_local_chunk(r, lw, k, v, kk, kka):
    c = r.shape[0]
    row = lax.broadcasted_iota(jnp.int32, (c, c), 0)
    col = lax.broadcasted_iota(jnp.int32, (c, c), 1)
    incl = row >= col
    strict = row > col
    eye = jnp.where(row == col, 1.0, 0.0)
    cum = _mm_exact_lhs(incl.astype(F32), lw, 3)
    last = cum[c - 1:c, :]
    pinv = jnp.exp(-cum)
    dl = jnp.exp(last - cum)
    at = _heads(-kk * jnp.exp(cum - lw))
    rt = _heads(r * jnp.exp(cum))
    bt = _heads(kka * pinv)
    kt = _heads(k * pinv)
    bl = _heads(kka * dl)
    kl = _heads(k * dl)
    v = _heads(v)
    pc = _heads(jnp.exp(last))
    sc = _mm(jnp.concatenate([at, rt], axis=1), jnp.concatenate([bt, kt], axis=1), RW_SC, "nt")
    a_ab = jnp.where(strict, sc[:, 0:c, 0:c], 0.0)
    a_ak = jnp.where(strict, sc[:, 0:c, c:2 * c], 0.0)
    r_b = jnp.where(incl, sc[:, c:2 * c, 0:c], 0.0)
    r_k = jnp.where(incl, sc[:, c:2 * c, c:2 * c], 0.0)
    x = eye + a_ab
    pw = _mm(a_ab, a_ab, RW_INV)
    levels = int(math.log2(c)) - 1
    for lvl in range(1, levels):
        prod = _mm(jnp.concatenate([x, pw], axis=1), pw, RW_INV)
        x = x + prod[:, 0:c]
        pw = prod[:, c:2 * c]
    x = x + _mm(x, pw, RW_INV)
    wu = _mm(x, jnp.concatenate([at, _mm(a_ak, v, RW_APPLY)], axis=2), RW_APPLY)
    lower = jnp.concatenate([jnp.zeros_like(v), v], axis=2)
    qy = _mm(jnp.concatenate([r_b, r_k], axis=2), jnp.concatenate([wu, lower], axis=1), RW_APPLY)
    wb = _mm(wu, bl, RW_STATE, "tn")
    n = wb[:, HEAD_DIM:] + _mm(v, kl, RW_STATE, "tn")
    m = eye * pc + wb[:, 0:HEAD_DIM]
    return rt + qy[:, :, 0:HEAD_DIM], qy[:, :, HEAD_DIM:], m, n


def _rwkv_local_kernel(*refs):
    prep_refs, (q_ref, y0_ref, m_ref, n_ref, ro_ref, ko_ref, vo_ref, g_ref, sc_ref) = refs[:16], refs[16:]
    r, lw, k, v, kkn, kka, g = _rwkv_prep_values(*prep_refs, sc_ref)
    ro_ref[...] = r
    ko_ref[...] = k
    vo_ref[...] = v
    g_ref[...] = g
    c = RWKV_CHUNK
    outs = [_rwkv_local_chunk(*(x[j * c:(j + 1) * c, :] for x in (r, lw, k, v, kkn, kka)))
            for j in range(r.shape[0] // c)]
    for j, res in enumerate(outs):
        for o_ref, val in zip((q_ref, y0_ref, m_ref, n_ref), res):
            for h in range(N_HEADS):
                o_ref[j * c:(j + 1) * c, h * HEAD_DIM:(h + 1) * HEAD_DIM] = val[h]


def _rwkv_scan_kernel(q_ref, y0_ref, m_ref, n_ref, r_ref, k_ref, v_ref, g_ref, lng_ref, lnb_ref, rk_ref,
                      o_ref, s_ref, y_sc):
    @pl.when(pl.program_id(0) == 0)
    def _():
        s_ref[...] = jnp.zeros_like(s_ref)

    c = RWKV_CHUNK
    s = s_ref[...]
    for j in range(q_ref.shape[0] // c):
        rows = slice(j * c, (j + 1) * c)
        y = _mm(_heads(q_ref[rows, :]), s, RW_SCAN, "nt") + _heads(y0_ref[rows, :])
        s = _mm(s, _heads(m_ref[rows, :]), RW_SCAN) + _heads(n_ref[rows, :])
        for h in range(N_HEADS):
            y_sc[rows, h * HEAD_DIM:(h + 1) * HEAD_DIM] = y[h]
    s_ref[...] = s
    y = y_sc[...]
    avg = _block_diag_const(MIX_W, HEAD_DIM, 1.0 / HEAD_DIM)
    mean = _mm_exact_rhs(y, avg, 2)
    yc = y - mean
    var = _mm_exact_rhs(yc * yc, avg, 2)
    yn = yc * lax.rsqrt(var + RWKV_LN_EPS) * lng_ref[...] + lnb_ref[...]
    v = v_ref[...]
    bonus = _mm_exact_rhs(r_ref[...] * k_ref[...] * rk_ref[...], _block_diag_const(MIX_W, HEAD_DIM, 1.0), 2) * v
    o_ref[...] = ((yn + bonus) * g_ref[...]).astype(o_ref.dtype)


def _rwkv_branch(p, mu, w0, w_up, a0, a_up, g_up, k_k, k_a, r_k, ln_g, ln_b, tb_scan=256):
    t = p.shape[0]
    tb = RWKV_LOCAL_CHUNKS * RWKV_CHUNK
    nb8 = tb // 8

    def blk(width, off):
        return pl.BlockSpec((tb, width), lambda i: (i, off // width))

    def tail(width, off):
        return pl.BlockSpec((8, width), lambda i: (jnp.maximum(i * nb8 - 1, 0), off // width))

    def vec(width):
        return pl.BlockSpec((1, width), lambda i: (0, 0))

    w_lora = jnp.zeros((256, 3 * MIX_W), F32)
    w_lora = w_lora.at[0:32, 0:MIX_W].set(w_up)
    w_lora = w_lora.at[32:64, MIX_W:2 * MIX_W].set(a_up)
    w_lora = w_lora.at[64:160, 2 * MIX_W:].set(g_up)
    b0 = jnp.concatenate([w0, a0, jnp.zeros((MIX_W,), F32)]).reshape(1, 3 * MIX_W)
    mu_l = jnp.concatenate([mu[3 * MIX_W:], jnp.zeros((256 - RWKV_LORA,), F32)]).reshape(1, 256)
    f = jax.ShapeDtypeStruct((t, MIX_W), F32)
    oblk = pl.BlockSpec((tb, MIX_W), lambda i: (i, 0))
    qm, y0, m, n, r, k, v, g = pl.pallas_call(
        _rwkv_local_kernel,
        grid=(t // tb,),
        in_specs=[blk(MIX_W, OFF_AR), blk(MIX_W, OFF_AK), blk(MIX_W, OFF_AV), blk(256, OFF_AL),
                  tail(MIX_W, OFF_AR), tail(MIX_W, OFF_AK), tail(MIX_W, OFF_AV), tail(256, OFF_AL),
                  vec(MIX_W), vec(MIX_W), vec(MIX_W), vec(256),
                  pl.BlockSpec((256, 3 * MIX_W), lambda i: (0, 0)), vec(3 * MIX_W), vec(MIX_W), vec(MIX_W)],
        out_specs=[oblk] * 8,
        out_shape=[f] * 8,
        scratch_shapes=[pltpu.VMEM((tb + 8, MIX_W), F32)],
        compiler_params=_cparams(("parallel",)),
        name="rwkv_local",
    )(p, p, p, p, p, p, p, p,
      mu[0:MIX_W].reshape(1, -1), mu[MIX_W:2 * MIX_W].reshape(1, -1), mu[2 * MIX_W:3 * MIX_W].reshape(1, -1), mu_l,
      w_lora, b0, k_k.reshape(1, -1), k_a.reshape(1, -1))
    sblk = pl.BlockSpec((tb_scan, MIX_W), lambda i: (i, 0))
    svec = pl.BlockSpec((1, MIX_W), lambda i: (0, 0))
    return pl.pallas_call(
        _rwkv_scan_kernel,
        grid=(t // tb_scan,),
        in_specs=[sblk] * 8 + [svec] * 3,
        out_specs=sblk,
        out_shape=jax.ShapeDtypeStruct((t, MIX_W), BF16),
        scratch_shapes=[pltpu.VMEM((N_HEADS, HEAD_DIM, HEAD_DIM), F32), pltpu.VMEM((tb_scan, MIX_W), F32)],
        compiler_params=_cparams(("arbitrary",)),
        name="rwkv_scan",
    )(qm, y0, m, n, r, k, v, g, ln_g.reshape(1, -1), ln_b.reshape(1, -1), r_k.reshape(1, -1))


def _rope_kernel(pos_ref, inv_ref, cos_ref, sin_ref):
    ang = pos_ref[...].astype(F32) * inv_ref[...]
    lane = lax.broadcasted_iota(jnp.int32, ang.shape, 1)
    cos_ref[...] = jnp.cos(ang)
    sin_ref[...] = jnp.where(lane % HEAD_DIM < HEAD_DIM // 2, -jnp.sin(ang), jnp.sin(ang))


def _rope_table(pos_b, tb=512):
    t = pos_b.shape[0]
    d = HEAD_DIM
    inv = 1.0 / (ROPE_BASE ** (jnp.arange(0, d, 2, dtype=F32) / d))
    inv_b = jnp.tile(inv, 4).reshape(1, 128)
    blk = pl.BlockSpec((tb, 128), lambda i: (i, 0))
    return pl.pallas_call(
        _rope_kernel,
        grid=(t // tb,),
        in_specs=[blk, pl.BlockSpec((1, 128), lambda i: (0, 0))],
        out_specs=[blk, blk],
        out_shape=[jax.ShapeDtypeStruct((t, 128), F32)] * 2,
        compiler_params=_cparams(("parallel",)),
        name="rope_table",
    )(pos_b, inv_b)


def _ret_kernel(q_ref, k_ref, v_ref, g_ref, cos_ref, sin_ref, dec_ref, zeta_ref, xi_ref, cd_ref, ng_ref,
                o_ref, s_ref, o_sc):
    @pl.when(pl.program_id(0) == 0)
    def _():
        s_ref[...] = jnp.zeros_like(s_ref)

    cos = jnp.concatenate([cos_ref[...]] * 4, axis=1)
    sin = jnp.concatenate([sin_ref[...]] * 4, axis=1)
    lane = lax.broadcasted_iota(jnp.int32, cos.shape, 1)
    lower_half = lane % HEAD_DIM < HEAD_DIM // 2

    def rope(x):
        swapped = jnp.where(lower_half, pltpu.roll(x, MIX_W - HEAD_DIM // 2, 1), pltpu.roll(x, HEAD_DIM // 2, 1))
        return x * cos + swapped * sin

    q = _heads(rope(q_ref[...]))
    k = _heads(rope(k_ref[...]) * (HEAD_DIM ** -0.5))
    v = _heads(v_ref[...])
    s = s_ref[...]
    scores = _mm(q, k, "x1", "nt") * dec_ref[...]
    o = _mm(scores, v, "x1") + _mm(q * xi_ref[...], s, "x1")
    s_ref[...] = s * cd_ref[...] + _mm(k * zeta_ref[...], v, "x1", "tn")
    on = o * lax.rsqrt(jnp.mean(o * o, axis=-1, keepdims=True) + NORM_EPS)
    for h in range(N_HEADS):
        o_sc[:, h * HEAD_DIM:(h + 1) * HEAD_DIM] = on[h]
    gt = g_ref[...]
    o_ref[...] = (o_sc[...] * ng_ref[...] * (gt * _sigmoid(gt))).astype(o_ref.dtype)


def _retention(p, cos_t, sin_t, norm_g):
    t = p.shape[0]
    c = RET_CHUNK
    h = N_HEADS
    log_g = jnp.log(1.0 - jnp.exp(jnp.linspace(math.log(1.0 / 32), math.log(1.0 / 512), h, dtype=F32)))
    idx = jnp.arange(c, dtype=F32)
    diff = idx[:, None] - idx[None, :]
    decay = jnp.where(diff >= 0, jnp.exp(log_g[:, None, None] * jnp.maximum(diff, 0.0)[None]), 0.0)
    zeta = jnp.exp(log_g[:, None] * (c - 1 - idx)[None])
    xi = jnp.exp(log_g[:, None] * (idx + 1)[None])
    cdec = jnp.exp(log_g * c)
    zeta_b = jnp.broadcast_to(zeta[:, :, None], (h, c, HEAD_DIM))
    xi_b = jnp.broadcast_to(xi[:, :, None], (h, c, HEAD_DIM))
    cd_b = jnp.broadcast_to(cdec[:, None, None], (h, HEAD_DIM, HEAD_DIM))

    def blk(off):
        return pl.BlockSpec((c, MIX_W), lambda i: (i, off // MIX_W))

    tab = pl.BlockSpec((c, 128), lambda i: (i, 0))

    def const(shape):
        return pl.BlockSpec(shape, lambda i: (0,) * len(shape))

    return pl.pallas_call(
        _ret_kernel,
        grid=(t // c,),
        in_specs=[blk(OFF_BQ), blk(OFF_BK), blk(OFF_BV), blk(OFF_BG), tab, tab,
                  const((h, c, c)), const((h, c, HEAD_DIM)), const((h, c, HEAD_DIM)),
                  const((h, HEAD_DIM, HEAD_DIM)), const((1, MIX_W))],
        out_specs=pl.BlockSpec((c, MIX_W), lambda i: (i, 0)),
        out_shape=jax.ShapeDtypeStruct((t, MIX_W), BF16),
        scratch_shapes=[pltpu.VMEM((h, HEAD_DIM, HEAD_DIM), F32), pltpu.VMEM((c, MIX_W), F32)],
        compiler_params=_cparams(("arbitrary",)),
        name="retention",
    )(p, p, p, p, cos_t, sin_t, decay, zeta_b, xi_b, cd_b, norm_g.reshape(1, -1))


def _lru_kernel(x_ref, xt_ref, gb_ref, pos_ref, cw_ref, cb_ref, wa_ref, ba_ref, wx_ref, bx_ref, lam_ref,
                o_ref, sc_ref, a_sc, b_sc, h_ref):
    tb = x_ref.shape[0]
    first = pl.program_id(0) == 0

    @pl.when(first)
    def _():
        h_ref[...] = jnp.zeros_like(h_ref)

    x = x_ref[...]
    sc_ref[0:8, :] = jnp.where(first, 0.0, xt_ref[...])
    sc_ref[8:8 + tb, :] = x
    cw = cw_ref[...]
    xc = cb_ref[...] + x * cw[CONV_W - 1:CONV_W, :]
    for j in range(1, CONV_W):
        xc = xc + sc_ref[8 - j:8 - j + tb, :] * cw[CONV_W - 1 - j:CONV_W - j, :]
    r = _sigmoid(_mm(xc, wa_ref[...], "x3") + ba_ref[...])
    ig = _sigmoid(_mm(xc, wx_ref[...], "x3") + bx_ref[...])
    nl = -lam_ref[...]
    softplus = jnp.maximum(nl, 0.0) + jnp.log1p(jnp.exp(-jnp.abs(nl)))
    log_a = -LRU_C * r * softplus
    pos = pos_ref[...]
    reset = jnp.concatenate([pos, pos, pos, pos], axis=1) == 0
    a_sc[...] = jnp.where(reset, 0.0, jnp.exp(log_a))
    th = jnp.tanh(log_a)
    b_sc[...] = jnp.where(reset, 1.0, jnp.sqrt(-2.0 * th / (1.0 - th))) * ig * xc

    row = lax.broadcasted_iota(jnp.int32, (8, MIX_W), 0)

    def group(gi, h):
        off = pl.multiple_of(gi * 8, 8)
        a = a_sc[pl.ds(off, 8), :]
        b = b_sc[pl.ds(off, 8), :]
        for d in (1, 2, 4):
            keep = row >= d
            b = jnp.where(keep, a * pltpu.roll(b, d, 0) + b, b)
            a = jnp.where(keep, a * pltpu.roll(a, d, 0), a)
        hs = a * h + b
        b_sc[pl.ds(off, 8), :] = hs
        return hs[7:8, :]

    h_ref[...] = lax.fori_loop(0, tb // 8, group, h_ref[...])
    gb = gb_ref[...]
    gelu = gb * (0.5 * (1.0 + jnp.tanh(math.sqrt(2.0 / math.pi) * (gb + 0.044715 * (gb * gb * gb)))))
    o_ref[...] = (b_sc[...] * gelu).astype(o_ref.dtype)


def _block_diag_weight(w):
    g, n, _ = w.shape
    eye = jnp.eye(g, dtype=w.dtype)
    return (eye[:, None, :, None] * w[:, :, None, :]).reshape(g * n, g * n)


def _rglru(p, pos_b, conv_w, conv_b, w_a, b_a, w_x, b_x, lam, tb=256):
    t = p.shape[0]
    nb8 = tb // 8
    vec = pl.BlockSpec((1, MIX_W), lambda i: (0, 0))
    mat = pl.BlockSpec((MIX_W, MIX_W), lambda i: (0, 0))
    cw8 = jnp.concatenate([conv_w, jnp.zeros((8 - CONV_W, MIX_W), F32)], axis=0)
    return pl.pallas_call(
        _lru_kernel,
        grid=(t // tb,),
        in_specs=[pl.BlockSpec((tb, MIX_W), lambda i: (i, OFF_CX // MIX_W)),
                  pl.BlockSpec((8, MIX_W), lambda i: (jnp.maximum(i * nb8 - 1, 0), OFF_CX // MIX_W)),
                  pl.BlockSpec((tb, MIX_W), lambda i: (i, OFF_CG // MIX_W)),
                  pl.BlockSpec((tb, 128), lambda i: (i, 0)),
                  pl.BlockSpec((8, MIX_W), lambda i: (0, 0)), vec, mat, vec, mat, vec, vec],
        out_specs=pl.BlockSpec((tb, MIX_W), lambda i: (i, 0)),
        out_shape=jax.ShapeDtypeStruct((t, MIX_W), BF16),
        scratch_shapes=[pltpu.VMEM((tb + 8, MIX_W), F32), pltpu.VMEM((tb, MIX_W), F32),
                        pltpu.VMEM((tb, MIX_W), F32), pltpu.VMEM((1, MIX_W), F32)],
        compiler_params=_cparams(("arbitrary",)),
        name="rglru",
    )(p, p, p, pos_b, cw8, conv_b.reshape(1, -1), _block_diag_weight(w_a), b_a.reshape(1, -1),
      _block_diag_weight(w_x), b_x.reshape(1, -1), lam.reshape(1, -1))


def _gla_kernel(v_ref, gt_ref, q_ref, k_ref, al_ref, aup_ref, ab_ref, ng_ref, o_ref, s_ref):
    @pl.when(pl.program_id(0) == 0)
    def _():
        s_ref[...] = jnp.zeros_like(s_ref)

    c = GLA_CHUNK
    row = lax.broadcasted_iota(jnp.int32, (c, c), 0)
    col = lax.broadcasted_iota(jnp.int32, (c, c), 1)
    causal = row >= col
    pre = _mm(al_ref[...], aup_ref[...], "x3") + ab_ref[...]
    log_a = (jnp.minimum(pre, 0.0) - jnp.log1p(jnp.exp(-jnp.abs(pre)))) / GLA_GATE_NORM
    s = s_ref[...]
    outs = []
    for j in range(q_ref.shape[0] // c):
        rows = slice(j * c, (j + 1) * c)
        bcum = _mm_exact_lhs(causal.astype(F32), log_a[rows], 3)
        blast = bcum[c - 1:c, :]
        k = k_ref[rows, :]
        q_e = _heads(q_ref[rows, :] * (GLA_DK ** -0.5) * jnp.exp(bcum))
        k_e = _heads(k * jnp.exp(-bcum))
        k_l = _heads(k * jnp.exp(blast - bcum))
        dec = _heads(jnp.exp(blast))
        v = _heads(v_ref[rows, :], GLA_DV)
        scores = jnp.where(causal, _mm(q_e, k_e, "x1", "nt"), 0.0)
        o = _mm(scores, v, "x1") + _mm(q_e, s, "x1", "nt")
        s = s * dec + _mm(v, k_l, "x1", "tn")
        on = o * lax.rsqrt(jnp.mean(o * o, axis=-1, keepdims=True) + NORM_EPS)
        outs.append(jnp.concatenate([on[h] for h in range(GLA_HEADS)], axis=1))
    s_ref[...] = s
    on = jnp.concatenate(outs, axis=0) * ng_ref[...]
    gt = gt_ref[...]
    o_ref[...] = (on * (gt * _sigmoid(gt))).astype(o_ref.dtype)


def _gla(p, alpha_up, alpha_b, norm_g):
    t = p.shape[0]
    c = GLA_STEP_CHUNKS * GLA_CHUNK
    hk = MIX_W // 2
    aup = jnp.concatenate([alpha_up, jnp.zeros((128 - GLA_LORA, hk), F32)], axis=0)

    def blk(width, off):
        return pl.BlockSpec((c, width), lambda i: (i, off // width))

    return pl.pallas_call(
        _gla_kernel,
        grid=(t // c,),
        in_specs=[blk(MIX_W, OFF_DV), blk(MIX_W, OFF_DGT), blk(hk, OFF_DQ), blk(hk, OFF_DK), blk(128, OFF_DAL),
                  pl.BlockSpec((128, hk), lambda i: (0, 0)), pl.BlockSpec((1, hk), lambda i: (0, 0)),
                  pl.BlockSpec((1, MIX_W), lambda i: (0, 0))],
        out_specs=pl.BlockSpec((c, MIX_W), lambda i: (i, 0)),
        out_shape=jax.ShapeDtypeStruct((t, MIX_W), BF16),
        scratch_shapes=[pltpu.VMEM((GLA_HEADS, GLA_DV, GLA_DK), F32)],
        compiler_params=_cparams(("arbitrary",)),
        name="gla",
    )(p, p, p, p, p, aup, alpha_b.reshape(1, -1), norm_g.reshape(1, -1))


def _merge_kernel(xn_ref, ga_ref, gb_ref, gc_ref, gd_ref, ya_ref, yb_ref, yc_ref, yd_ref,
                  wa_ref, wb_ref, wc_ref, wd_ref, o_ref, w_bf):
    @pl.when(pl.program_id(1) == 0)
    def _():
        for kk, w_ref in enumerate((wa_ref, wb_ref, wc_ref, wd_ref)):
            w_bf[kk] = w_ref[0].astype(BF16)

    xn = xn_ref[...]
    acc = None
    for kk, (g_ref, y_ref) in enumerate(((ga_ref, ya_ref), (gb_ref, yb_ref), (gc_ref, yc_ref), (gd_ref, yd_ref))):
        term = _sigmoid(_mm(xn, g_ref[...], "x1", "nt")) * _dot(y_ref[...], w_bf[kk])
        acc = term if acc is None else acc + term
    o_ref[...] = acc.astype(o_ref.dtype)


def _merge(xn, w_t, ys, ws, layer, tm=1024, tn=512):
    t, d = xn.shape
    xspec = pl.BlockSpec((tm, d), lambda j, i: (i, 0))
    yspec = pl.BlockSpec((tm, MIX_W), lambda j, i: (i, 0))
    wspec = pl.BlockSpec((1, MIX_W, tn), lambda j, i: (layer, 0, j))

    def gspec(kk):
        return pl.BlockSpec((tn, d), lambda j, i: ((OFF_G + kk * D_MODEL) // tn + j, 0))

    return pl.pallas_call(
        _merge_kernel,
        grid=(D_MODEL // tn, t // tm),
        in_specs=[xspec] + [gspec(kk) for kk in range(4)] + [yspec] * 4 + [wspec] * 4,
        out_specs=pl.BlockSpec((tm, tn), lambda j, i: (i, j)),
        out_shape=jax.ShapeDtypeStruct((t, D_MODEL), BF16),
        scratch_shapes=[pltpu.VMEM((4, MIX_W, tn), BF16)],
        compiler_params=_cparams(("parallel", "arbitrary"), VMEM_LIMIT_EXPERT),
        name="merge",
    )(xn, w_t, w_t, w_t, w_t, *ys, *ws)


def _ffn_norm_kernel(h_ref, g_ref, wr_ref, br_ref, rt_ref):
    x = h_ref[...]
    ms = jnp.mean(x * x, axis=-1, keepdims=True)
    hn = x * lax.rsqrt(ms + NORM_EPS) * g_ref[...]
    z = _mm(hn, wr_ref[...], "x3") + br_ref[...]
    lane = lax.broadcasted_iota(jnp.int32, z.shape, 1)
    neg = jnp.float32(-1e30)

    def first_argmax(v):
        m = jnp.max(v, axis=-1, keepdims=True)
        return m, jnp.min(jnp.where(v == m, lane, 128), axis=-1, keepdims=True)

    is_group = lane < N_GROUPS
    zg = jnp.where(is_group, z, neg)
    mg, g_idx = first_argmax(zg)
    pg_top = 1.0 / jnp.sum(jnp.where(is_group, jnp.exp(zg - mg), 0.0), axis=-1, keepdims=True)
    lo = N_GROUPS + g_idx * EXP_PER_GROUP
    in_group = jnp.logical_and(lane >= lo, lane < lo + EXP_PER_GROUP)
    ze = jnp.where(in_group, z, neg)
    m1, i1 = first_argmax(ze)
    se = jnp.sum(jnp.where(in_group, jnp.exp(ze - m1), 0.0), axis=-1, keepdims=True)
    m2, i2 = first_argmax(jnp.where(lane == i1, neg, ze))
    p1 = 1.0 / se
    p2 = jnp.exp(m2 - m1) / se
    tot = p1 + p2
    rt_ref[...] = jnp.where(lane == 0, pg_top * (p1 / tot),
                            jnp.where(lane == 1, pg_top * (p2 / tot),
                                      jnp.where(lane == 2, (i1 - N_GROUPS).astype(F32),
                                                jnp.where(lane == 3, (i2 - N_GROUPS).astype(F32), 0.0))))


def _ffn_norm_router(h, g, w_r, b_r, tb=256):
    t, d = h.shape
    return pl.pallas_call(
        _ffn_norm_kernel,
        grid=(t // tb,),
        in_specs=[pl.BlockSpec((tb, d), lambda i: (i, 0)), pl.BlockSpec((1, d), lambda i: (0, 0)),
                  pl.BlockSpec((d, 128), lambda i: (0, 0)), pl.BlockSpec((1, 128), lambda i: (0, 0))],
        out_specs=pl.BlockSpec((tb, 128), lambda i: (i, 0)),
        out_shape=jax.ShapeDtypeStruct((t, 128), F32),
        compiler_params=_cparams(("parallel",)),
        name="ffn_norm_router",
    )(h, g.reshape(1, d), w_r, b_r)


def _row_copy(src_hbm, idx, dst, r, sem):
    return pltpu.make_async_copy(src_hbm.at[pl.ds(idx, 1), :], dst.at[pl.ds(r, 1), :], sem)


def _expert_changed(i, exp_ref):
    return jnp.logical_or(i == 0, exp_ref[i] != exp_ref[jnp.maximum(i - 1, 0)])


def _gather_rows(src_hbm, row_of, n, dst, sem):
    def start(r, carry):
        _row_copy(src_hbm, row_of(r), dst, r, sem).start()
        return carry

    lax.fori_loop(0, n, start, 0, unroll=8)


def _gather_wait(src_hbm, dst, sem):
    pltpu.make_async_copy(src_hbm.at[pl.ds(0, dst.shape[0]), :], dst, sem).wait()


def _expert_kernel(st_ref, exp_ref, base_ref, nxt_ref, nblk_ref, x_hbm, wg_hbm, wu_hbm, wd_hbm, g_ref, o_ref,
                   xbuf, stage_g, stage_u, stage_d, wg_bf, wu_bf, wd_bf, xsem, wsem, *, layer):
    i = pl.program_id(0)
    n = nblk_ref[0]
    slot = i % 2

    stages = (stage_g, stage_u, stage_d)
    caches = (wg_bf, wu_bf, wd_bf)

    def chunk_rows(m, c):
        rows = stages[m].shape[0] // WEIGHT_CHUNKS
        return pl.ds(c * rows, rows)

    def weight_copy(e, m, c):
        src = (wg_hbm, wu_hbm, wd_hbm)[m]
        return pltpu.make_async_copy(src.at[layer, e, chunk_rows(m, c), :], stages[m].at[chunk_rows(m, c), :],
                                     wsem.at[m, c])

    def start_weights(e):
        for c in range(WEIGHT_CHUNKS):
            for m in range(3):
                weight_copy(e, m, c).start(priority=1)

    last = st_ref.shape[0] - 1

    def fetch(blk, sl):
        base = base_ref[blk]
        _gather_rows(x_hbm, lambda r: st_ref[jnp.minimum(base + r, last)], MOE_BLOCK, xbuf.at[sl], xsem.at[sl])

    @pl.when(jnp.logical_and(i == 0, n > 0))
    def _():
        fetch(0, 0)
        start_weights(exp_ref[0])

    @pl.when(i + 1 < n)
    def _():
        fetch(i + 1, 1 - slot)

    @pl.when(i < n)
    def _():
        @pl.when(_expert_changed(i, exp_ref))
        def _():
            nxt = nxt_ref[i]
            for c in range(WEIGHT_CHUNKS):
                for m in range(3):
                    weight_copy(exp_ref[i], m, c).wait()
                    caches[m][chunk_rows(m, c), :] = stages[m][chunk_rows(m, c), :].astype(BF16)

                @pl.when(nxt >= 0)
                def _():
                    for m in range(3):
                        weight_copy(nxt, m, c).start(priority=1)

        _gather_wait(x_hbm, xbuf.at[slot], xsem.at[slot])
        x = xbuf[slot]
        x = (x * lax.rsqrt(jnp.mean(x * x, axis=-1, keepdims=True) + NORM_EPS) * g_ref[...]).astype(BF16)
        gate = _dot(x, wg_bf[...])
        up = _dot(x, wu_bf[...])
        hmid = (gate * _sigmoid(gate) * up).astype(BF16)
        o_ref[...] = _dot(hmid, wd_bf[...])

    @pl.when(i >= n)
    def _():
        o_ref[...] = jnp.zeros_like(o_ref)


def _experts(h, norm_g, st, plan, w_gate, w_up, w_down, layer):
    block_exp, src_base, next_exp, n_used, _ = plan
    n_blocks = block_exp.shape[0]
    d = h.shape[1]
    hbm = pl.BlockSpec(memory_space=pl.ANY)
    return pl.pallas_call(
        functools.partial(_expert_kernel, layer=layer),
        grid_spec=pltpu.PrefetchScalarGridSpec(
            num_scalar_prefetch=5,
            grid=(n_blocks,),
            in_specs=[hbm, hbm, hbm, hbm, pl.BlockSpec((1, d), lambda i, *_: (0, 0))],
            out_specs=pl.BlockSpec((MOE_BLOCK, d), lambda i, *_: (i, 0)),
            scratch_shapes=[pltpu.VMEM((2, MOE_BLOCK, d), F32),
                            pltpu.VMEM((d, D_EXPERT), F32), pltpu.VMEM((d, D_EXPERT), F32),
                            pltpu.VMEM((D_EXPERT, d), F32),
                            pltpu.VMEM((d, D_EXPERT), BF16), pltpu.VMEM((d, D_EXPERT), BF16),
                            pltpu.VMEM((D_EXPERT, d), BF16),
                            pltpu.SemaphoreType.DMA((2,)), pltpu.SemaphoreType.DMA((3, WEIGHT_CHUNKS))],
        ),
        out_shape=jax.ShapeDtypeStruct((n_blocks * MOE_BLOCK, d), F32),
        compiler_params=_cparams(("arbitrary",), VMEM_LIMIT_EXPERT),
        name="experts",
    )(st, block_exp, src_base, next_exp, n_used, h, w_gate, w_up, w_down, norm_g.reshape(1, d))


def _combine_kernel(pos_ref, y_hbm, h_ref, rt_ref, g_ref, o_ref, buf, sem, *, final_norm):
    tb = h_ref.shape[0]
    i = pl.program_id(0)
    slot = i % 2

    def fetch(blk, sl):
        for s in range(TOP_K):
            base = (s * pl.num_programs(0) + blk) * tb
            _gather_rows(y_hbm, lambda r: pos_ref[base + r], tb, buf.at[sl, s], sem.at[sl])

    @pl.when(i == 0)
    def _():
        fetch(0, 0)

    @pl.when(i + 1 < pl.num_programs(0))
    def _():
        fetch(i + 1, 1 - slot)

    for s in range(TOP_K):
        _gather_wait(y_hbm, buf.at[slot, s], sem.at[slot])
    rt = rt_ref[...]
    out = h_ref[...] + (rt[:, 0:1] * buf[slot, 0] + rt[:, 1:2] * buf[slot, 1])
    if final_norm:
        ms = jnp.mean(out * out, axis=-1, keepdims=True)
        out = out * lax.rsqrt(ms + NORM_EPS) * g_ref[...]
    o_ref[...] = out


def _combine(h, y_rows, pos, route, g, final_norm, tb=256):
    t, d = h.shape
    grid_spec = pltpu.PrefetchScalarGridSpec(
        num_scalar_prefetch=1,
        grid=(t // tb,),
        in_specs=[pl.BlockSpec(memory_space=pl.ANY),
                  pl.BlockSpec((tb, d), lambda i, *_: (i, 0)),
                  pl.BlockSpec((tb, 128), lambda i, *_: (i, 0)),
                  pl.BlockSpec((1, d), lambda i, *_: (0, 0))],
        out_specs=pl.BlockSpec((tb, d), lambda i, *_: (i, 0)),
        scratch_shapes=[pltpu.VMEM((2, TOP_K, tb, d), F32), pltpu.SemaphoreType.DMA((2,))],
    )
    return pl.pallas_call(
        functools.partial(_combine_kernel, final_norm=final_norm),
        grid_spec=grid_spec,
        out_shape=jax.ShapeDtypeStruct((t, d), F32),
        compiler_params=_cparams(("arbitrary",)),
        name="moe_combine",
    )(pos, y_rows, h, route, g.reshape(1, d))


def _plan_kernel(counts_ref, rank_ref, eid_ref, bexp_ref, base_ref, nxt_ref, nused_ref, pos_ref, shift_ref):
    n_blocks = bexp_ref.shape[0]

    def per_expert(e, carry):
        start, pad_start = carry
        nb = (counts_ref[e] + MOE_BLOCK - 1) // MOE_BLOCK
        shift_ref[e] = pad_start - start

        def fill(j, c):
            blk = pad_start // MOE_BLOCK + j
            bexp_ref[blk] = e
            base_ref[blk] = start + j * MOE_BLOCK
            return c

        lax.fori_loop(0, nb, fill, 0)
        return start + counts_ref[e], pad_start + nb * MOE_BLOCK

    _, pad_end = lax.fori_loop(0, N_EXPERTS, per_expert, (jnp.int32(0), jnp.int32(0)))
    n_used = pad_end // MOE_BLOCK
    nused_ref[0] = n_used

    def tail(blk, c):
        bexp_ref[blk] = N_EXPERTS - 1
        base_ref[blk] = 0
        nxt_ref[blk] = -1
        return c

    lax.fori_loop(n_used, n_blocks, tail, 0)

    def backward(k, carry):
        nxt, cur = carry
        blk = n_used - 1 - k
        e = bexp_ref[blk]
        nxt = jnp.where(e != cur, cur, nxt)
        nxt_ref[blk] = nxt
        return nxt, e

    lax.fori_loop(0, n_used, backward, (jnp.int32(-1), jnp.int32(-1)))

    eid = eid_ref[...]
    pos = rank_ref[...]
    for e in range(N_EXPERTS):
        pos = pos + jnp.where(eid == e, shift_ref[e], 0)
    pos_ref[...] = pos


def _plan(counts, rank, eid, n_blocks):
    smem = pl.BlockSpec(memory_space=pltpu.SMEM)
    vmem = pl.BlockSpec(memory_space=pltpu.VMEM)
    i32 = lambda n: jax.ShapeDtypeStruct((n,), jnp.int32)
    return pl.pallas_call(
        _plan_kernel,
        in_specs=[smem, vmem, vmem],
        out_specs=[smem, smem, smem, smem, vmem],
        out_shape=[i32(n_blocks), i32(n_blocks), i32(n_blocks), i32(1), jax.ShapeDtypeStruct(rank.shape, jnp.int32)],
        scratch_shapes=[pltpu.SMEM((N_EXPERTS,), jnp.int32)],
        name="moe_plan",
    )(counts, rank, eid)


def _moe_layer(h, norm_g, wg_r, bg_r, we_r, be_r, w_gate, w_up, w_down, layer, out_g, final_norm):
    t, d = h.shape
    w_r = jnp.concatenate([wg_r, we_r, jnp.zeros((d, 128 - N_GROUPS - N_EXPERTS), F32)], axis=1)
    b_r = jnp.concatenate([bg_r, be_r, jnp.zeros((128 - N_GROUPS - N_EXPERTS,), F32)]).reshape(1, 128)
    route = _ffn_norm_router(h, norm_g, w_r, b_r)
    expert_id = route[:, TOP_K:2 * TOP_K].astype(jnp.int32)
    flat = expert_id.reshape(-1)
    order = jnp.argsort(flat).astype(jnp.int32)
    st = order // TOP_K
    rank_sm = jnp.argsort(order).astype(jnp.int32).reshape(t, TOP_K).T.reshape(-1, 128)
    eid_sm = expert_id.T.reshape(-1, 128)
    counts = jnp.sum((flat[:, None] == jnp.arange(N_EXPERTS, dtype=jnp.int32)[None, :]).astype(jnp.int32), axis=0)
    plan = _plan(counts, rank_sm, eid_sm, (t * TOP_K + MOE_BLOCK - 1) // MOE_BLOCK + N_EXPERTS)
    y_rows = _experts(h, norm_g, st, plan, w_gate, w_up, w_down, layer)
    return _combine(h, y_rows, plan[4].reshape(-1), route, out_g, final_norm)


_D0 = A_COLS + B_COLS + C_COLS
_W_IN_PIECES = (
    (OFF_BQ, A_COLS, B_COLS), (OFF_CX, A_COLS + B_COLS, C_COLS),
    (OFF_DV, _D0 + 512, 512), (OFF_DGT, _D0 + 1040, 512),
    (OFF_AR, 0, 3 * MIX_W), (OFF_DQ, _D0, 512),
    (OFF_AL, 3 * MIX_W, RWKV_LORA), (OFF_DAL, _D0 + 1024, GLA_LORA),
    (OFF_G, _D0 + D_COLS, 4 * D_MODEL),
)


RELAYOUT_ROWS = 512
_SPECIAL_BLOCK = OFF_AL // RELAYOUT_ROWS


def _w_in_source_rows():
    rows = np.zeros((NP_COLS // RELAYOUT_ROWS,), np.int32)
    for dst, src, width in _W_IN_PIECES:
        if width % RELAYOUT_ROWS == 0:
            for k in range(width // RELAYOUT_ROWS):
                rows[dst // RELAYOUT_ROWS + k] = src + k * RELAYOUT_ROWS
    return rows


def _relayout_kernel(src_ref, w_hbm, extra_hbm, o_ref, stage, sem, *, layer):
    j = pl.program_id(0)
    slot = j % 2

    def start(blk, sl):
        @pl.when(blk == _SPECIAL_BLOCK)
        def _():
            pltpu.make_async_copy(extra_hbm, stage.at[sl], sem.at[sl]).start()

        @pl.when(blk != _SPECIAL_BLOCK)
        def _():
            rows = pl.ds(pl.multiple_of(src_ref[blk], 16), RELAYOUT_ROWS)
            pltpu.make_async_copy(w_hbm.at[layer, rows, :], stage.at[sl], sem.at[sl]).start()

    @pl.when(j == 0)
    def _():
        start(0, 0)

    @pl.when(j + 1 < pl.num_programs(0))
    def _():
        start(j + 1, 1 - slot)

    pltpu.make_async_copy(extra_hbm, stage.at[slot], sem.at[slot]).wait()
    o_ref[...] = stage[slot].astype(o_ref.dtype)


def _permute_w_in(w_in, layer):
    w_t = jnp.swapaxes(w_in, 1, 2)
    d = w_t.shape[2]
    zeros = lambda n: jnp.zeros((n, d), F32)
    extra = jnp.concatenate([w_t[layer, 3 * MIX_W:A_COLS], zeros(OFF_DAL - OFF_AL - RWKV_LORA),
                             w_t[layer, _D0 + 1024:_D0 + 1024 + GLA_LORA],
                             zeros(OFF_G - OFF_DAL - GLA_LORA)], axis=0)
    hbm = pl.BlockSpec(memory_space=pl.ANY)
    return pl.pallas_call(
        functools.partial(_relayout_kernel, layer=layer),
        grid_spec=pltpu.PrefetchScalarGridSpec(
            num_scalar_prefetch=1,
            grid=(NP_COLS // RELAYOUT_ROWS,),
            in_specs=[hbm, hbm],
            out_specs=pl.BlockSpec((RELAYOUT_ROWS, d), lambda j, *_: (j, 0)),
            scratch_shapes=[pltpu.VMEM((2, RELAYOUT_ROWS, d), F32), pltpu.SemaphoreType.DMA((2,))],
        ),
        out_shape=jax.ShapeDtypeStruct((NP_COLS, d), BF16),
        compiler_params=_cparams(("arbitrary",)),
        name="w_in_relayout",
    )(jnp.asarray(_w_in_source_rows()), w_t, extra)


def kernel(x, positions, norm_mix_g, w_in, rwkv_mu, rwkv_w0, rwkv_w_up, rwkv_a0, rwkv_a_up, rwkv_g_up, rwkv_k_k, rwkv_k_a, rwkv_r_k, rwkv_ln_g, rwkv_ln_b, rwkv_w_o, ret_norm_g, ret_w_o, lru_conv_w, lru_conv_b, lru_w_a, lru_b_a, lru_w_x, lru_b_x, lru_lambda, lru_w_o, gla_alpha_up, gla_alpha_b, gla_norm_g, gla_w_o, w_out, norm_ffn_g, router_group_w, router_group_b, router_expert_w, router_expert_b, moe_w_gate, moe_w_up, moe_w_down, final_norm_g):
    b_, s_, d = x.shape
    assert b_ == 1 and d == D_MODEL
    depth = w_in.shape[0]
    h = x.reshape(s_, d)
    pos_b = jnp.broadcast_to(positions.reshape(s_, 1), (s_, 128)).astype(jnp.int32)
    cos_t, sin_t = _rope_table(pos_b)
    for l in range(depth):
        xn = _rmsnorm(h, norm_mix_g[l], BF16)
        w_t = _permute_w_in(w_in, l)
        p = _matmul_nt(xn, w_t, 0, OFF_G, 2048, 512, F32, "in_proj")
        y_a = _rwkv_branch(p, rwkv_mu[l], rwkv_w0[l], rwkv_w_up[l], rwkv_a0[l], rwkv_a_up[l], rwkv_g_up[l],
                           rwkv_k_k[l], rwkv_k_a[l], rwkv_r_k[l], rwkv_ln_g[l], rwkv_ln_b[l])
        y_b = _retention(p, cos_t, sin_t, ret_norm_g[l])
        y_c = _rglru(p, pos_b, lru_conv_w[l], lru_conv_b[l], lru_w_a[l], lru_b_a[l], lru_w_x[l], lru_b_x[l],
                     lru_lambda[l])
        y_d = _gla(p, gla_alpha_up[l], gla_alpha_b[l], gla_norm_g[l])
        merged = _merge(xn, w_t, (y_a, y_b, y_c, y_d), (rwkv_w_o, ret_w_o, lru_w_o, gla_w_o), l)
        h = _matmul_residual(merged, w_out, l, h, 1024, 512)
        last = l == depth - 1
        h = _moe_layer(h, norm_ffn_g[l], router_group_w[l], router_group_b[l], router_expert_w[l],
                       router_expert_b[l], moe_w_gate, moe_w_up, moe_w_down, l,
                       final_norm_g if last else norm_ffn_g[l], last)
    return h.reshape(b_, s_, d)
```

```python
import functools
import math

import jax
import jax.numpy as jnp
import numpy as np
from jax import lax
from jax.experimental import pallas as pl
from jax.experimental.pallas import tpu as pltpu

F32 = jnp.float32
BF16 = jnp.bfloat16
HI = lax.Precision.HIGHEST

D_MODEL = 2048
MIX_W = 512
NORM_EPS = 1e-6
HEAD_DIM = 64
N_HEADS = MIX_W // HEAD_DIM

RWKV_W_LORA, RWKV_A_LORA, RWKV_G_LORA = 32, 32, 96
RWKV_LORA = RWKV_W_LORA + RWKV_A_LORA + RWKV_G_LORA
RWKV_DECAY_SCALE = 0.6065306597126334
RWKV_LN_EPS = 64e-5
RWKV_CHUNK = 64
RWKV_LOCAL_CHUNKS = 4

RET_CHUNK = 128
RET_STEP_CHUNKS = 2
ROPE_BASE = 10000.0

LRU_BLOCKS = 8
CONV_W = 4
LRU_C = 8.0

GLA_HEADS = 4
GLA_DK = 64
GLA_DV = 128
GLA_LORA = 16
GLA_GATE_NORM = 16.0
GLA_CHUNK = 64
GLA_STEP_CHUNKS = 2

N_GROUPS = 4
EXP_PER_GROUP = 8
N_EXPERTS = N_GROUPS * EXP_PER_GROUP
TOP_K = 2
D_EXPERT = 1024
MOE_BLOCK = 128
WEIGHT_CHUNKS = 4

A_COLS = 3 * MIX_W + RWKV_LORA
B_COLS = 4 * MIX_W
C_COLS = 2 * MIX_W
D_COLS = 2 * (MIX_W // 2) + MIX_W + GLA_LORA + MIX_W

OFF_BQ, OFF_BK, OFF_BV, OFF_BG = 0, 512, 1024, 1536
OFF_CX, OFF_CG = 2048, 2560
OFF_DV, OFF_DGT = 3072, 3584
OFF_AR, OFF_AK, OFF_AV = 4096, 4608, 5120
OFF_DQ, OFF_DK = 5632, 5888
OFF_AL = 6144
OFF_DAL = 6400
OFF_G = 6656
NP_COLS = OFF_G + 4 * D_MODEL

VMEM_LIMIT = 48 * 1024 * 1024
VMEM_LIMIT_EXPERT = 56 * 1024 * 1024


def _cparams(sem, vmem=VMEM_LIMIT):
    return pltpu.CompilerParams(dimension_semantics=sem, vmem_limit_bytes=vmem)


def _dot(a, b, prec=None):
    return jnp.dot(a, b, precision=prec, preferred_element_type=F32)


def _sigmoid(x):
    return 1.0 / (1.0 + jnp.exp(-x))


def _split(x):
    hi = x.astype(BF16)
    return hi, (x - hi.astype(F32)).astype(BF16)


def _dims(form, batched):
    ca, cb = {"nn": (1, 0), "nt": (1, 1), "tn": (0, 0)}[form]
    if batched:
        return (((ca + 1,), (cb + 1,)), ((0,), (0,)))
    return (((ca,), (cb,)), ((), ()))


def _mm(a, b, mode, form="nn"):
    dims = _dims(form, a.ndim == 3)
    if mode == "hi":
        return lax.dot_general(a, b, dims, precision=HI, preferred_element_type=F32)
    d = functools.partial(lax.dot_general, dimension_numbers=dims, preferred_element_type=F32)
    if mode == "x1":
        return d(a.astype(BF16), b.astype(BF16))
    ah, al = _split(a)
    bh, bl = _split(b)
    return d(ah, bh) + (d(ah, bl) + d(al, bh))


def _mm_exact_rhs(a, b, terms):
    b = b.astype(BF16)
    acc = None
    for _ in range(terms):
        piece = a.astype(BF16)
        part = _dot(piece, b)
        acc = part if acc is None else acc + part
        a = a - piece.astype(F32)
    return acc


def _mm_exact_lhs(a, b, terms):
    a = a.astype(BF16)
    acc = None
    for _ in range(terms):
        piece = b.astype(BF16)
        part = _dot(a, piece)
        acc = part if acc is None else acc + part
        b = b - piece.astype(F32)
    return acc


def _heads(x, width=HEAD_DIM):
    return jnp.stack([x[:, h * width:(h + 1) * width] for h in range(x.shape[1] // width)])


def _block_diag_const(n, blk, value):
    r = lax.broadcasted_iota(jnp.int32, (n, n), 0) // blk
    c = lax.broadcasted_iota(jnp.int32, (n, n), 1) // blk
    return jnp.where(r == c, value, 0.0).astype(F32)


def _rmsnorm_kernel(x_ref, g_ref, o_ref):
    x = x_ref[...]
    ms = jnp.mean(x * x, axis=-1, keepdims=True)
    o_ref[...] = (x * lax.rsqrt(ms + NORM_EPS) * g_ref[...]).astype(o_ref.dtype)


def _rmsnorm(x, g, out_dtype, tb=1024):
    t, d = x.shape
    return pl.pallas_call(
        _rmsnorm_kernel,
        grid=(t // tb,),
        in_specs=[pl.BlockSpec((tb, d), lambda i: (i, 0)), pl.BlockSpec((1, d), lambda i: (0, 0))],
        out_specs=pl.BlockSpec((tb, d), lambda i: (i, 0)),
        out_shape=jax.ShapeDtypeStruct((t, d), out_dtype),
        compiler_params=_cparams(("parallel",)),
        name="rmsnorm",
    )(x, g.reshape(1, d))


def _mm_nt_kernel(a_ref, b_ref, o_ref):
    o_ref[...] = _mm(a_ref[...], b_ref[...], "x1", "nt").astype(o_ref.dtype)


def _matmul_nt(a, b_t, row0, n, tm, tn, out_dtype, name):
    m, k = a.shape
    assert row0 % tn == 0 and n % tn == 0 and m % tm == 0
    return pl.pallas_call(
        _mm_nt_kernel,
        grid=(m // tm, n // tn),
        in_specs=[pl.BlockSpec((tm, k), lambda i, j: (i, 0)), pl.BlockSpec((tn, k), lambda i, j: (row0 // tn + j, 0))],
        out_specs=pl.BlockSpec((tm, tn), lambda i, j: (i, j)),
        out_shape=jax.ShapeDtypeStruct((m, n), out_dtype),
        compiler_params=_cparams(("parallel", "parallel")),
        name=name,
    )(a, b_t)


def _mm_res_kernel(a_ref, b_ref, r_ref, o_ref, b_bf):
    @pl.when(pl.program_id(1) == 0)
    def _():
        b_bf[...] = b_ref[...].astype(BF16)

    o_ref[...] = r_ref[...] + _dot(a_ref[...], b_bf[...])


def _matmul_residual(a, b, layer, res, tm, tn):
    m, k = a.shape
    n = b.shape[2]
    return pl.pallas_call(
        _mm_res_kernel,
        grid=(n // tn, m // tm),
        in_specs=[pl.BlockSpec((tm, k), lambda j, i: (i, 0)), pl.BlockSpec((None, k, tn), lambda j, i: (layer, 0, j)),
                  pl.BlockSpec((tm, tn), lambda j, i: (i, j))],
        out_specs=pl.BlockSpec((tm, tn), lambda j, i: (i, j)),
        out_shape=jax.ShapeDtypeStruct((m, n), F32),
        scratch_shapes=[pltpu.VMEM((k, tn), BF16)],
        compiler_params=_cparams(("parallel", "arbitrary"), VMEM_LIMIT_EXPERT),
        name="wout_residual",
    )(a, b, res)


def _shifted(x, tail_ref, sc_ref, width, first):
    tb = x.shape[0]
    sc_ref[0:8, 0:width] = jnp.where(first, 0.0, tail_ref[...])
    sc_ref[8:8 + tb, 0:width] = x
    return sc_ref[7:7 + tb, 0:width]


def _rwkv_prep_values(r_ref, k_ref, v_ref, l_ref, rt_ref, kt_ref, vt_ref, lt_ref,
                      mur_ref, muk_ref, muv_ref, mul_ref, wl_ref, b0_ref, kk_ref, ka_ref, sc_ref):
    first = pl.program_id(0) == 0

    def mix(x_ref, t_ref, mu_ref, width):
        x = x_ref[...]
        prev = _shifted(x, t_ref, sc_ref, width, first)
        return x + (prev - x) * mu_ref[...]

    r = mix(r_ref, rt_ref, mur_ref, MIX_W)
    k = mix(k_ref, kt_ref, muk_ref, MIX_W)
    v = mix(v_ref, vt_ref, muv_ref, MIX_W)
    zl = mix(l_ref, lt_ref, mul_ref, 256)
    lane = lax.broadcasted_iota(jnp.int32, zl.shape, 1)
    act = jnp.where(lane < RWKV_W_LORA, jnp.tanh(zl),
                    jnp.where(lane < RWKV_W_LORA + RWKV_A_LORA, zl, _sigmoid(zl)))
    lo = _mm(act, wl_ref[...], "x3") + b0_ref[...]
    lw = -RWKV_DECAY_SCALE * _sigmoid(lo[:, 0:MIX_W])
    a = _sigmoid(lo[:, MIX_W:2 * MIX_W])
    g = lo[:, 2 * MIX_W:3 * MIX_W]
    kk = k * kk_ref[...]
    ss = _mm_exact_rhs(kk * kk, _block_diag_const(MIX_W, HEAD_DIM, 1.0), 2)
    kkn = kk / jnp.maximum(jnp.sqrt(ss), 1e-12)
    return r, lw, k * (1.0 + (a - 1.0) * ka_ref[...]), v, kkn, kkn * a, g


RW_SC, RW_INV, RW_APPLY, RW_STATE, RW_SCAN = "x1", "x1", "x1", "x1", "x1"


def _rwkv_local_chunk(r, lw, k, v, kk, kka):
    c = r.shape[0]
    row = lax.broadcasted_iota(jnp.int32, (c, c), 0)
    col = lax.broadcasted_iota(jnp.int32, (c, c), 1)
    incl = row >= col
    strict = row > col
    eye = jnp.where(row == col, 1.0, 0.0)
    cum = _mm_exact_lhs(incl.astype(F32), lw, 3)
    last = cum[c - 1:c, :]
    pinv = jnp.exp(-cum)
    dl = jnp.exp(last - cum)
    at = _heads(-kk * jnp.exp(cum - lw))
    rt = _heads(r * jnp.exp(cum))
    bt = _heads(kka * pinv)
    kt = _heads(k * pinv)
    bl = _heads(kka * dl)
    kl = _heads(k * dl)
    v = _heads(v)
    pc = _heads(jnp.exp(last))
    sc = _mm(jnp.concatenate([at, rt], axis=1), jnp.concatenate([bt, kt], axis=1), RW_SC, "nt")
    a_ab = jnp.where(strict, sc[:, 0:c, 0:c], 0.0)
    a_ak = jnp.where(strict, sc[:, 0:c, c:2 * c], 0.0)
    r_b = jnp.where(incl, sc[:, c:2 * c, 0:c], 0.0)
    r_k = jnp.where(incl, sc[:, c:2 * c, c:2 * c], 0.0)
    x = eye + a_ab
    pw = _mm(a_ab, a_ab, RW_INV)
    levels = int(math.log2(c)) - 1
    for lvl in range(1, levels):
        prod = _mm(jnp.concatenate([x, pw], axis=1), pw, RW_INV)
        x = x + prod[:, 0:c]
        pw = prod[:, c:2 * c]
    x = x + _mm(x, pw, RW_INV)
    wu = _mm(x, jnp.concatenate([at, _mm(a_ak, v, RW_APPLY)], axis=2), RW_APPLY)
    lower = jnp.concatenate([jnp.zeros_like(v), v], axis=2)
    qy = _mm(jnp.concatenate([r_b, r_k], axis=2), jnp.concatenate([wu, lower], axis=1), RW_APPLY)
    wb = _mm(wu, bl, RW_STATE, "tn")
    n = wb[:, HEAD_DIM:] + _mm(v, kl, RW_STATE, "tn")
    m = eye * pc + wb[:, 0:HEAD_DIM]
    return rt + qy[:, :, 0:HEAD_DIM], qy[:, :, HEAD_DIM:], m, n


def _rwkv_local_kernel(*refs):
    prep_refs, (q_ref, y0_ref, m_ref, n_ref, ro_ref, ko_ref, vo_ref, g_ref, sc_ref) = refs[:16], refs[16:]
    r, lw, k, v, kkn, kka, g = _rwkv_prep_values(*prep_refs, sc_ref)
    ro_ref[...] = r
    ko_ref[...] = k
    vo_ref[...] = v
    g_ref[...] = g
    c = RWKV_CHUNK
    outs = [_rwkv_local_chunk(*(x[j * c:(j + 1) * c, :] for x in (r, lw, k, v, kkn, kka)))
            for j in range(r.shape[0] // c)]
    for j, res in enumerate(outs):
        for o_ref, val in zip((q_ref, y0_ref, m_ref, n_ref), res):
            for h in range(N_HEADS):
                o_ref[j * c:(j + 1) * c, h * HEAD_DIM:(h + 1) * HEAD_DIM] = val[h]


def _rwkv_scan_kernel(q_ref, y0_ref, m_ref, n_ref, r_ref, k_ref, v_ref, g_ref, lng_ref, lnb_ref, rk_ref,
                      o_ref, s_ref, y_sc):
    @pl.when(pl.program_id(0) == 0)
    def _():
        s_ref[...] = jnp.zeros_like(s_ref)

    c = RWKV_CHUNK
    s = s_ref[...]
    for j in range(q_ref.shape[0] // c):
        rows = slice(j * c, (j + 1) * c)
        y = _mm(_heads(q_ref[rows, :]), s, RW_SCAN, "nt") + _heads(y0_ref[rows, :])
        s = _mm(s, _heads(m_ref[rows, :]), RW_SCAN) + _heads(n_ref[rows, :])
        for h in range(N_HEADS):
            y_sc[rows, h * HEAD_DIM:(h + 1) * HEAD_DIM] = y[h]
    s_ref[...] = s
    y = y_sc[...]
    avg = _block_diag_const(MIX_W, HEAD_DIM, 1.0 / HEAD_DIM)
    mean = _mm_exact_rhs(y, avg, 2)
    yc = y - mean
    var = _mm_exact_rhs(yc * yc, avg, 2)
    yn = yc * lax.rsqrt(var + RWKV_LN_EPS) * lng_ref[...] + lnb_ref[...]
    v = v_ref[...]
    bonus = _mm_exact_rhs(r_ref[...] * k_ref[...] * rk_ref[...], _block_diag_const(MIX_W, HEAD_DIM, 1.0), 2) * v
    o_ref[...] = ((yn + bonus) * g_ref[...]).astype(o_ref.dtype)


def _rwkv_branch(p, mu, w0, w_up, a0, a_up, g_up, k_k, k_a, r_k, ln_g, ln_b, tb_scan=256):
    t = p.shape[0]
    tb = RWKV_LOCAL_CHUNKS * RWKV_CHUNK
    nb8 = tb // 8

    def blk(width, off):
        return pl.BlockSpec((tb, width), lambda i: (i, off // width))

    def tail(width, off):
        return pl.BlockSpec((8, width), lambda i: (jnp.maximum(i * nb8 - 1, 0), off // width))

    def vec(width):
        return pl.BlockSpec((1, width), lambda i: (0, 0))

    w_lora = jnp.zeros((256, 3 * MIX_W), F32)
    w_lora = w_lora.at[0:32, 0:MIX_W].set(w_up)
    w_lora = w_lora.at[32:64, MIX_W:2 * MIX_W].set(a_up)
    w_lora = w_lora.at[64:160, 2 * MIX_W:].set(g_up)
    b0 = jnp.concatenate([w0, a0, jnp.zeros((MIX_W,), F32)]).reshape(1, 3 * MIX_W)
    mu_l = jnp.concatenate([mu[3 * MIX_W:], jnp.zeros((256 - RWKV_LORA,), F32)]).reshape(1, 256)
    f = jax.ShapeDtypeStruct((t, MIX_W), F32)
    oblk = pl.BlockSpec((tb, MIX_W), lambda i: (i, 0))
    qm, y0, m, n, r, k, v, g = pl.pallas_call(
        _rwkv_local_kernel,
        grid=(t // tb,),
        in_specs=[blk(MIX_W, OFF_AR), blk(MIX_W, OFF_AK), blk(MIX_W, OFF_AV), blk(256, OFF_AL),
                  tail(MIX_W, OFF_AR), tail(MIX_W, OFF_AK), tail(MIX_W, OFF_AV), tail(256, OFF_AL),
                  vec(MIX_W), vec(MIX_W), vec(MIX_W), vec(256),
                  pl.BlockSpec((256, 3 * MIX_W), lambda i: (0, 0)), vec(3 * MIX_W), vec(MIX_W), vec(MIX_W)],
        out_specs=[oblk] * 8,
        out_shape=[f] * 8,
        scratch_shapes=[pltpu.VMEM((tb + 8, MIX_W), F32)],
        compiler_params=_cparams(("parallel",)),
        name="rwkv_local",
    )(p, p, p, p, p, p, p, p,
      mu[0:MIX_W].reshape(1, -1), mu[MIX_W:2 * MIX_W].reshape(1, -1), mu[2 * MIX_W:3 * MIX_W].reshape(1, -1), mu_l,
      w_lora, b0, k_k.reshape(1, -1), k_a.reshape(1, -1))
    sblk = pl.BlockSpec((tb_scan, MIX_W), lambda i: (i, 0))
    svec = pl.BlockSpec((1, MIX_W), lambda i: (0, 0))
    return pl.pallas_call(
        _rwkv_scan_kernel,
        grid=(t // tb_scan,),
        in_specs=[sblk] * 8 + [svec] * 3,
        out_specs=sblk,
        out_shape=jax.ShapeDtypeStruct((t, MIX_W), BF16),
        scratch_shapes=[pltpu.VMEM((N_HEADS, HEAD_DIM, HEAD_DIM), F32), pltpu.VMEM((tb_scan, MIX_W), F32)],
        compiler_params=_cparams(("arbitrary",)),
        name="rwkv_scan",
    )(qm, y0, m, n, r, k, v, g, ln_g.reshape(1, -1), ln_b.reshape(1, -1), r_k.reshape(1, -1))


def _rope_kernel(pos_ref, inv_ref, cos_ref, sin_ref):
    ang = pos_ref[...].astype(F32) * inv_ref[...]
    lane = lax.broadcasted_iota(jnp.int32, ang.shape, 1)
    cos_ref[...] = jnp.cos(ang)
    sin_ref[...] = jnp.where(lane % HEAD_DIM < HEAD_DIM // 2, -jnp.sin(ang), jnp.sin(ang))


def _rope_table(pos_b, tb=512):
    t = pos_b.shape[0]
    d = HEAD_DIM
    inv = 1.0 / (ROPE_BASE ** (jnp.arange(0, d, 2, dtype=F32) / d))
    inv_b = jnp.tile(inv, 4).reshape(1, 128)
    blk = pl.BlockSpec((tb, 128), lambda i: (i, 0))
    return pl.pallas_call(
        _rope_kernel,
        grid=(t // tb,),
        in_specs=[blk, pl.BlockSpec((1, 128), lambda i: (0, 0))],
        out_specs=[blk, blk],
        out_shape=[jax.ShapeDtypeStruct((t, 128), F32)] * 2,
        compiler_params=_cparams(("parallel",)),
        name="rope_table",
    )(pos_b, inv_b)


def _ret_kernel(q_ref, k_ref, v_ref, g_ref, cos_ref, sin_ref, dec_ref, zeta_ref, xi_ref, cd_ref, ng_ref,
                o_ref, s_ref, o_sc):
    @pl.when(pl.program_id(0) == 0)
    def _():
        s_ref[...] = jnp.zeros_like(s_ref)

    cos = jnp.concatenate([cos_ref[...]] * 4, axis=1)
    sin = jnp.concatenate([sin_ref[...]] * 4, axis=1)
    lane = lax.broadcasted_iota(jnp.int32, cos.shape, 1)
    lower_half = lane % HEAD_DIM < HEAD_DIM // 2

    def rope(x):
        swapped = jnp.where(lower_half, pltpu.roll(x, MIX_W - HEAD_DIM // 2, 1), pltpu.roll(x, HEAD_DIM // 2, 1))
        return x * cos + swapped * sin

    c = RET_CHUNK
    qr = rope(q_ref[...])
    kr = rope(k_ref[...]) * (HEAD_DIM ** -0.5)
    s = s_ref[...]
    for j in range(q_ref.shape[0] // c):
        rows = slice(j * c, (j + 1) * c)
        q = _heads(qr[rows])
        k = _heads(kr[rows])
        v = _heads(v_ref[rows, :])
        scores = _mm(q, k, "x1", "nt") * dec_ref[...]
        o = _mm(scores, v, "x1") + _mm(q * xi_ref[...], s, "x1")
        s = s * cd_ref[...] + _mm(k * zeta_ref[...], v, "x1", "tn")
        on = o * lax.rsqrt(jnp.mean(o * o, axis=-1, keepdims=True) + NORM_EPS)
        for h in range(N_HEADS):
            o_sc[rows, h * HEAD_DIM:(h + 1) * HEAD_DIM] = on[h]
    s_ref[...] = s
    gt = g_ref[...]
    o_ref[...] = (o_sc[...] * ng_ref[...] * (gt * _sigmoid(gt))).astype(o_ref.dtype)


def _retention(p, cos_t, sin_t, norm_g):
    t = p.shape[0]
    c = RET_CHUNK
    h = N_HEADS
    log_g = jnp.log(1.0 - jnp.exp(jnp.linspace(math.log(1.0 / 32), math.log(1.0 / 512), h, dtype=F32)))
    idx = jnp.arange(c, dtype=F32)
    diff = idx[:, None] - idx[None, :]
    decay = jnp.where(diff >= 0, jnp.exp(log_g[:, None, None] * jnp.maximum(diff, 0.0)[None]), 0.0)
    zeta = jnp.exp(log_g[:, None] * (c - 1 - idx)[None])
    xi = jnp.exp(log_g[:, None] * (idx + 1)[None])
    cdec = jnp.exp(log_g * c)
    zeta_b = jnp.broadcast_to(zeta[:, :, None], (h, c, HEAD_DIM))
    xi_b = jnp.broadcast_to(xi[:, :, None], (h, c, HEAD_DIM))
    cd_b = jnp.broadcast_to(cdec[:, None, None], (h, HEAD_DIM, HEAD_DIM))

    tb = RET_STEP_CHUNKS * c

    def blk(off):
        return pl.BlockSpec((tb, MIX_W), lambda i: (i, off // MIX_W))

    tab = pl.BlockSpec((tb, 128), lambda i: (i, 0))

    def const(shape):
        return pl.BlockSpec(shape, lambda i: (0,) * len(shape))

    return pl.pallas_call(
        _ret_kernel,
        grid=(t // tb,),
        in_specs=[blk(OFF_BQ), blk(OFF_BK), blk(OFF_BV), blk(OFF_BG), tab, tab,
                  const((h, c, c)), const((h, c, HEAD_DIM)), const((h, c, HEAD_DIM)),
                  const((h, HEAD_DIM, HEAD_DIM)), const((1, MIX_W))],
        out_specs=pl.BlockSpec((tb, MIX_W), lambda i: (i, 0)),
        out_shape=jax.ShapeDtypeStruct((t, MIX_W), BF16),
        scratch_shapes=[pltpu.VMEM((h, HEAD_DIM, HEAD_DIM), F32), pltpu.VMEM((tb, MIX_W), F32)],
        compiler_params=_cparams(("arbitrary",)),
        name="retention",
    )(p, p, p, p, cos_t, sin_t, decay, zeta_b, xi_b, cd_b, norm_g.reshape(1, -1))


def _lru_kernel(x_ref, xt_ref, gb_ref, pos_ref, cw_ref, cb_ref, wa_ref, ba_ref, wx_ref, bx_ref, lam_ref,
                o_ref, sc_ref, a_sc, b_sc, h_ref):
    tb = x_ref.shape[0]
    first = pl.program_id(0) == 0

    @pl.when(first)
    def _():
        h_ref[...] = jnp.zeros_like(h_ref)

    x = x_ref[...]
    sc_ref[0:8, :] = jnp.where(first, 0.0, xt_ref[...])
    sc_ref[8:8 + tb, :] = x
    cw = cw_ref[...]
    xc = cb_ref[...] + x * cw[CONV_W - 1:CONV_W, :]
    for j in range(1, CONV_W):
        xc = xc + sc_ref[8 - j:8 - j + tb, :] * cw[CONV_W - 1 - j:CONV_W - j, :]
    r = _sigmoid(_mm(xc, wa_ref[...], "x3") + ba_ref[...])
    ig = _sigmoid(_mm(xc, wx_ref[...], "x3") + bx_ref[...])
    nl = -lam_ref[...]
    softplus = jnp.maximum(nl, 0.0) + jnp.log1p(jnp.exp(-jnp.abs(nl)))
    log_a = -LRU_C * r * softplus
    pos = pos_ref[...]
    reset = jnp.concatenate([pos, pos, pos, pos], axis=1) == 0
    a_sc[...] = jnp.where(reset, 0.0, jnp.exp(log_a))
    th = jnp.tanh(log_a)
    b_sc[...] = jnp.where(reset, 1.0, jnp.sqrt(-2.0 * th / (1.0 - th))) * ig * xc

    row = lax.broadcasted_iota(jnp.int32, (8, MIX_W), 0)

    def group(gi, h):
        off = pl.multiple_of(gi * 8, 8)
        a = a_sc[pl.ds(off, 8), :]
        b = b_sc[pl.ds(off, 8), :]
        for d in (1, 2, 4):
            keep = row >= d
            b = jnp.where(keep, a * pltpu.roll(b, d, 0) + b, b)
            a = jnp.where(keep, a * pltpu.roll(a, d, 0), a)
        hs = a * h + b
        b_sc[pl.ds(off, 8), :] = hs
        return hs[7:8, :]

    h_ref[...] = lax.fori_loop(0, tb // 8, group, h_ref[...])
    gb = gb_ref[...]
    gelu = gb * (0.5 * (1.0 + jnp.tanh(math.sqrt(2.0 / math.pi) * (gb + 0.044715 * (gb * gb * gb)))))
    o_ref[...] = (b_sc[...] * gelu).astype(o_ref.dtype)


def _block_diag_weight(w):
    g, n, _ = w.shape
    eye = jnp.eye(g, dtype=w.dtype)
    return (eye[:, None, :, None] * w[:, :, None, :]).reshape(g * n, g * n)


def _rglru(p, pos_b, conv_w, conv_b, w_a, b_a, w_x, b_x, lam, tb=512):
    t = p.shape[0]
    nb8 = tb // 8
    vec = pl.BlockSpec((1, MIX_W), lambda i: (0, 0))
    mat = pl.BlockSpec((MIX_W, MIX_W), lambda i: (0, 0))
    cw8 = jnp.concatenate([conv_w, jnp.zeros((8 - CONV_W, MIX_W), F32)], axis=0)
    return pl.pallas_call(
        _lru_kernel,
        grid=(t // tb,),
        in_specs=[pl.BlockSpec((tb, MIX_W), lambda i: (i, OFF_CX // MIX_W)),
                  pl.BlockSpec((8, MIX_W), lambda i: (jnp.maximum(i * nb8 - 1, 0), OFF_CX // MIX_W)),
                  pl.BlockSpec((tb, MIX_W), lambda i: (i, OFF_CG // MIX_W)),
                  pl.BlockSpec((tb, 128), lambda i: (i, 0)),
                  pl.BlockSpec((8, MIX_W), lambda i: (0, 0)), vec, mat, vec, mat, vec, vec],
        out_specs=pl.BlockSpec((tb, MIX_W), lambda i: (i, 0)),
        out_shape=jax.ShapeDtypeStruct((t, MIX_W), BF16),
        scratch_shapes=[pltpu.VMEM((tb + 8, MIX_W), F32), pltpu.VMEM((tb, MIX_W), F32),
                        pltpu.VMEM((tb, MIX_W), F32), pltpu.VMEM((1, MIX_W), F32)],
        compiler_params=_cparams(("arbitrary",)),
        name="rglru",
    )(p, p, p, pos_b, cw8, conv_b.reshape(1, -1), _block_diag_weight(w_a), b_a.reshape(1, -1),
      _block_diag_weight(w_x), b_x.reshape(1, -1), lam.reshape(1, -1))


def _gla_kernel(v_ref, gt_ref, q_ref, k_ref, al_ref, aup_ref, ab_ref, ng_ref, o_ref, s_ref):
    @pl.when(pl.program_id(0) == 0)
    def _():
        s_ref[...] = jnp.zeros_like(s_ref)

    c = GLA_CHUNK
    row = lax.broadcasted_iota(jnp.int32, (c, c), 0)
    col = lax.broadcasted_iota(jnp.int32, (c, c), 1)
    causal = row >= col
    pre = _mm(al_ref[...], aup_ref[...], "x3") + ab_ref[...]
    log_a = (jnp.minimum(pre, 0.0) - jnp.log1p(jnp.exp(-jnp.abs(pre)))) / GLA_GATE_NORM
    s = s_ref[...]
    outs = []
    for j in range(q_ref.shape[0] // c):
        rows = slice(j * c, (j + 1) * c)
        bcum = _mm_exact_lhs(causal.astype(F32), log_a[rows], 3)
        blast = bcum[c - 1:c, :]
        k = k_ref[rows, :]
        q_e = _heads(q_ref[rows, :] * (GLA_DK ** -0.5) * jnp.exp(bcum))
        k_e = _heads(k * jnp.exp(-bcum))
        k_l = _heads(k * jnp.exp(blast - bcum))
        dec = _heads(jnp.exp(blast))
        v = _heads(v_ref[rows, :], GLA_DV)
        scores = jnp.where(causal, _mm(q_e, k_e, "x1", "nt"), 0.0)
        o = _mm(scores, v, "x1") + _mm(q_e, s, "x1", "nt")
        s = s * dec + _mm(v, k_l, "x1", "tn")
        on = o * lax.rsqrt(jnp.mean(o * o, axis=-1, keepdims=True) + NORM_EPS)
        outs.append(jnp.concatenate([on[h] for h in range(GLA_HEADS)], axis=1))
    s_ref[...] = s
    on = jnp.concatenate(outs, axis=0) * ng_ref[...]
    gt = gt_ref[...]
    o_ref[...] = (on * (gt * _sigmoid(gt))).astype(o_ref.dtype)


def _gla(p, alpha_up, alpha_b, norm_g):
    t = p.shape[0]
    c = GLA_STEP_CHUNKS * GLA_CHUNK
    hk = MIX_W // 2
    aup = jnp.concatenate([alpha_up, jnp.zeros((128 - GLA_LORA, hk), F32)], axis=0)

    def blk(width, off):
        return pl.BlockSpec((c, width), lambda i: (i, off // width))

    return pl.pallas_call(
        _gla_kernel,
        grid=(t // c,),
        in_specs=[blk(MIX_W, OFF_DV), blk(MIX_W, OFF_DGT), blk(hk, OFF_DQ), blk(hk, OFF_DK), blk(128, OFF_DAL),
                  pl.BlockSpec((128, hk), lambda i: (0, 0)), pl.BlockSpec((1, hk), lambda i: (0, 0)),
                  pl.BlockSpec((1, MIX_W), lambda i: (0, 0))],
        out_specs=pl.BlockSpec((c, MIX_W), lambda i: (i, 0)),
        out_shape=jax.ShapeDtypeStruct((t, MIX_W), BF16),
        scratch_shapes=[pltpu.VMEM((GLA_HEADS, GLA_DV, GLA_DK), F32)],
        compiler_params=_cparams(("arbitrary",)),
        name="gla",
    )(p, p, p, p, p, aup, alpha_b.reshape(1, -1), norm_g.reshape(1, -1))


def _merge_kernel(xn_ref, ga_ref, gb_ref, gc_ref, gd_ref, ya_ref, yb_ref, yc_ref, yd_ref,
                  wa_ref, wb_ref, wc_ref, wd_ref, o_ref, w_bf):
    @pl.when(pl.program_id(1) == 0)
    def _():
        for kk, w_ref in enumerate((wa_ref, wb_ref, wc_ref, wd_ref)):
            w_bf[kk] = w_ref[0].astype(BF16)

    xn = xn_ref[...]
    acc = None
    for kk, (g_ref, y_ref) in enumerate(((ga_ref, ya_ref), (gb_ref, yb_ref), (gc_ref, yc_ref), (gd_ref, yd_ref))):
        term = _sigmoid(_mm(xn, g_ref[...], "x1", "nt")) * _dot(y_ref[...], w_bf[kk])
        acc = term if acc is None else acc + term
    o_ref[...] = acc.astype(o_ref.dtype)


def _merge(xn, w_t, ys, ws, layer, tm=1024, tn=512):
    t, d = xn.shape
    xspec = pl.BlockSpec((tm, d), lambda j, i: (i, 0))
    yspec = pl.BlockSpec((tm, MIX_W), lambda j, i: (i, 0))
    wspec = pl.BlockSpec((1, MIX_W, tn), lambda j, i: (layer, 0, j))

    def gspec(kk):
        return pl.BlockSpec((tn, d), lambda j, i: ((OFF_G + kk * D_MODEL) // tn + j, 0))

    return pl.pallas_call(
        _merge_kernel,
        grid=(D_MODEL // tn, t // tm),
        in_specs=[xspec] + [gspec(kk) for kk in range(4)] + [yspec] * 4 + [wspec] * 4,
        out_specs=pl.BlockSpec((tm, tn), lambda j, i: (i, j)),
        out_shape=jax.ShapeDtypeStruct((t, D_MODEL), BF16),
        scratch_shapes=[pltpu.VMEM((4, MIX_W, tn), BF16)],
        compiler_params=_cparams(("parallel", "arbitrary"), VMEM_LIMIT_EXPERT),
        name="merge",
    )(xn, w_t, w_t, w_t, w_t, *ys, *ws)


def _ffn_norm_kernel(h_ref, g_ref, wr_ref, br_ref, rt_ref):
    x = h_ref[...]
    ms = jnp.mean(x * x, axis=-1, keepdims=True)
    hn = x * lax.rsqrt(ms + NORM_EPS) * g_ref[...]
    z = _mm(hn, wr_ref[...], "x3") + br_ref[...]
    lane = lax.broadcasted_iota(jnp.int32, z.shape, 1)
    neg = jnp.float32(-1e30)

    def first_argmax(v):
        m = jnp.max(v, axis=-1, keepdims=True)
        return m, jnp.min(jnp.where(v == m, lane, 128), axis=-1, keepdims=True)

    is_group = lane < N_GROUPS
    zg = jnp.where(is_group, z, neg)
    mg, g_idx = first_argmax(zg)
    pg_top = 1.0 / jnp.sum(jnp.where(is_group, jnp.exp(zg - mg), 0.0), axis=-1, keepdims=True)
    lo = N_GROUPS + g_idx * EXP_PER_GROUP
    in_group = jnp.logical_and(lane >= lo, lane < lo + EXP_PER_GROUP)
    ze = jnp.where(in_group, z, neg)
    m1, i1 = first_argmax(ze)
    se = jnp.sum(jnp.where(in_group, jnp.exp(ze - m1), 0.0), axis=-1, keepdims=True)
    m2, i2 = first_argmax(jnp.where(lane == i1, neg, ze))
    p1 = 1.0 / se
    p2 = jnp.exp(m2 - m1) / se
    tot = p1 + p2
    rt_ref[...] = jnp.where(lane == 0, pg_top * (p1 / tot),
                            jnp.where(lane == 1, pg_top * (p2 / tot),
                                      jnp.where(lane == 2, (i1 - N_GROUPS).astype(F32),
                                                jnp.where(lane == 3, (i2 - N_GROUPS).astype(F32), 0.0))))


def _ffn_norm_router(h, g, w_r, b_r, tb=512):
    t, d = h.shape
    return pl.pallas_call(
        _ffn_norm_kernel,
        grid=(t // tb,),
        in_specs=[pl.BlockSpec((tb, d), lambda i: (i, 0)), pl.BlockSpec((1, d), lambda i: (0, 0)),
                  pl.BlockSpec((d, 128), lambda i: (0, 0)), pl.BlockSpec((1, 128), lambda i: (0, 0))],
        out_specs=pl.BlockSpec((tb, 128), lambda i: (i, 0)),
        out_shape=jax.ShapeDtypeStruct((t, 128), F32),
        compiler_params=_cparams(("parallel",)),
        name="ffn_norm_router",
    )(h, g.reshape(1, d), w_r, b_r)


def _row_copy(src_hbm, idx, dst, r, sem):
    return pltpu.make_async_copy(src_hbm.at[pl.ds(idx, 1), :], dst.at[pl.ds(r, 1), :], sem)


def _expert_changed(i, exp_ref):
    return jnp.logical_or(i == 0, exp_ref[i] != exp_ref[jnp.maximum(i - 1, 0)])


def _gather_rows(src_hbm, row_of, n, dst, sem):
    def start(r, carry):
        _row_copy(src_hbm, row_of(r), dst, r, sem).start()
        return carry

    lax.fori_loop(0, n, start, 0, unroll=8)


def _gather_wait(src_hbm, dst, sem):
    pltpu.make_async_copy(src_hbm.at[pl.ds(0, dst.shape[0]), :], dst, sem).wait()


def _expert_kernel(st_ref, exp_ref, base_ref, nxt_ref, nblk_ref, x_hbm, wg_hbm, wu_hbm, wd_hbm, g_ref, o_ref,
                   xbuf, stage_g, stage_u, stage_d, wg_bf, wu_bf, wd_bf, xsem, wsem, *, layer):
    i = pl.program_id(0)
    n = nblk_ref[0]
    slot = i % 2

    stages = (stage_g, stage_u, stage_d)
    caches = (wg_bf, wu_bf, wd_bf)

    def chunk_rows(m, c):
        rows = stages[m].shape[0] // WEIGHT_CHUNKS
        return pl.ds(c * rows, rows)

    def weight_copy(e, m, c):
        src = (wg_hbm, wu_hbm, wd_hbm)[m]
        return pltpu.make_async_copy(src.at[layer, e, chunk_rows(m, c), :], stages[m].at[chunk_rows(m, c), :],
                                     wsem.at[m, c])

    def start_weights(e):
        for c in range(WEIGHT_CHUNKS):
            for m in range(3):
                weight_copy(e, m, c).start(priority=1)

    last = st_ref.shape[0] - 1

    def fetch(blk, sl):
        base = base_ref[blk]
        _gather_rows(x_hbm, lambda r: st_ref[jnp.minimum(base + r, last)], MOE_BLOCK, xbuf.at[sl], xsem.at[sl])

    @pl.when(jnp.logical_and(i == 0, n > 0))
    def _():
        fetch(0, 0)
        start_weights(exp_ref[0])

    @pl.when(i + 1 < n)
    def _():
        fetch(i + 1, 1 - slot)

    @pl.when(i < n)
    def _():
        @pl.when(_expert_changed(i, exp_ref))
        def _():
            nxt = nxt_ref[i]
            for c in range(WEIGHT_CHUNKS):
                for m in range(3):
                    weight_copy(exp_ref[i], m, c).wait()
                    caches[m][chunk_rows(m, c), :] = stages[m][chunk_rows(m, c), :].astype(BF16)

                @pl.when(nxt >= 0)
                def _():
                    for m in range(3):
                        weight_copy(nxt, m, c).start(priority=1)

        _gather_wait(x_hbm, xbuf.at[slot], xsem.at[slot])
        x = xbuf[slot]
        x = (x * lax.rsqrt(jnp.mean(x * x, axis=-1, keepdims=True) + NORM_EPS) * g_ref[...]).astype(BF16)
        gate = _dot(x, wg_bf[...])
        up = _dot(x, wu_bf[...])
        hmid = (gate * _sigmoid(gate) * up).astype(BF16)
        o_ref[...] = _dot(hmid, wd_bf[...])

    @pl.when(i >= n)
    def _():
        o_ref[...] = jnp.zeros_like(o_ref)


def _experts(h, norm_g, st, plan, w_gate, w_up, w_down, layer):
    block_exp, src_base, next_exp, n_used, _ = plan
    n_blocks = block_exp.shape[0]
    d = h.shape[1]
    hbm = pl.BlockSpec(memory_space=pl.ANY)
    return pl.pallas_call(
        functools.partial(_expert_kernel, layer=layer),
        grid_spec=pltpu.PrefetchScalarGridSpec(
            num_scalar_prefetch=5,
            grid=(n_blocks,),
            in_specs=[hbm, hbm, hbm, hbm, pl.BlockSpec((1, d), lambda i, *_: (0, 0))],
            out_specs=pl.BlockSpec((MOE_BLOCK, d), lambda i, *_: (i, 0)),
            scratch_shapes=[pltpu.VMEM((2, MOE_BLOCK, d), F32),
                            pltpu.VMEM((d, D_EXPERT), F32), pltpu.VMEM((d, D_EXPERT), F32),
                            pltpu.VMEM((D_EXPERT, d), F32),
                            pltpu.VMEM((d, D_EXPERT), BF16), pltpu.VMEM((d, D_EXPERT), BF16),
                            pltpu.VMEM((D_EXPERT, d), BF16),
                            pltpu.SemaphoreType.DMA((2,)), pltpu.SemaphoreType.DMA((3, WEIGHT_CHUNKS))],
        ),
        out_shape=jax.ShapeDtypeStruct((n_blocks * MOE_BLOCK, d), F32),
        compiler_params=_cparams(("arbitrary",), VMEM_LIMIT_EXPERT),
        name="experts",
    )(st, block_exp, src_base, next_exp, n_used, h, w_gate, w_up, w_down, norm_g.reshape(1, d))


def _combine_kernel(pos_ref, y_hbm, h_ref, rt_ref, g_ref, o_ref, buf, sem, *, final_norm):
    tb = h_ref.shape[0]
    i = pl.program_id(0)
    slot = i % 2

    def fetch(blk, sl):
        for s in range(TOP_K):
            base = (s * pl.num_programs(0) + blk) * tb
            _gather_rows(y_hbm, lambda r: pos_ref[base + r], tb, buf.at[sl, s], sem.at[sl])

    @pl.when(i == 0)
    def _():
        fetch(0, 0)

    @pl.when(i + 1 < pl.num_programs(0))
    def _():
        fetch(i + 1, 1 - slot)

    for s in range(TOP_K):
        _gather_wait(y_hbm, buf.at[slot, s], sem.at[slot])
    rt = rt_ref[...]
    out = h_ref[...] + (rt[:, 0:1] * buf[slot, 0] + rt[:, 1:2] * buf[slot, 1])
    if final_norm:
        ms = jnp.mean(out * out, axis=-1, keepdims=True)
        out = out * lax.rsqrt(ms + NORM_EPS) * g_ref[...]
    o_ref[...] = out


def _combine(h, y_rows, pos, route, g, final_norm, tb=256):
    t, d = h.shape
    grid_spec = pltpu.PrefetchScalarGridSpec(
        num_scalar_prefetch=1,
        grid=(t // tb,),
        in_specs=[pl.BlockSpec(memory_space=pl.ANY),
                  pl.BlockSpec((tb, d), lambda i, *_: (i, 0)),
                  pl.BlockSpec((tb, 128), lambda i, *_: (i, 0)),
                  pl.BlockSpec((1, d), lambda i, *_: (0, 0))],
        out_specs=pl.BlockSpec((tb, d), lambda i, *_: (i, 0)),
        scratch_shapes=[pltpu.VMEM((2, TOP_K, tb, d), F32), pltpu.SemaphoreType.DMA((2,))],
    )
    return pl.pallas_call(
        functools.partial(_combine_kernel, final_norm=final_norm),
        grid_spec=grid_spec,
        out_shape=jax.ShapeDtypeStruct((t, d), F32),
        compiler_params=_cparams(("arbitrary",)),
        name="moe_combine",
    )(pos, y_rows, h, route, g.reshape(1, d))


def _plan_kernel(counts_ref, rank_ref, eid_ref, bexp_ref, base_ref, nxt_ref, nused_ref, pos_ref, shift_ref):
    n_blocks = bexp_ref.shape[0]

    def per_expert(e, carry):
        start, pad_start = carry
        nb = (counts_ref[e] + MOE_BLOCK - 1) // MOE_BLOCK
        shift_ref[e] = pad_start - start

        def fill(j, c):
            blk = pad_start // MOE_BLOCK + j
            bexp_ref[blk] = e
            base_ref[blk] = start + j * MOE_BLOCK
            return c

        lax.fori_loop(0, nb, fill, 0)
        return start + counts_ref[e], pad_start + nb * MOE_BLOCK

    _, pad_end = lax.fori_loop(0, N_EXPERTS, per_expert, (jnp.int32(0), jnp.int32(0)))
    n_used = pad_end // MOE_BLOCK
    nused_ref[0] = n_used

    def tail(blk, c):
        bexp_ref[blk] = N_EXPERTS - 1
        base_ref[blk] = 0
        nxt_ref[blk] = -1
        return c

    lax.fori_loop(n_used, n_blocks, tail, 0)

    def backward(k, carry):
        nxt, cur = carry
        blk = n_used - 1 - k
        e = bexp_ref[blk]
        nxt = jnp.where(e != cur, cur, nxt)
        nxt_ref[blk] = nxt
        return nxt, e

    lax.fori_loop(0, n_used, backward, (jnp.int32(-1), jnp.int32(-1)))

    eid = eid_ref[...]
    pos = rank_ref[...]
    for e in range(N_EXPERTS):
        pos = pos + jnp.where(eid == e, shift_ref[e], 0)
    pos_ref[...] = pos


def _plan(counts, rank, eid, n_blocks):
    smem = pl.BlockSpec(memory_space=pltpu.SMEM)
    vmem = pl.BlockSpec(memory_space=pltpu.VMEM)
    i32 = lambda n: jax.ShapeDtypeStruct((n,), jnp.int32)
    return pl.pallas_call(
        _plan_kernel,
        in_specs=[smem, vmem, vmem],
        out_specs=[smem, smem, smem, smem, vmem],
        out_shape=[i32(n_blocks), i32(n_blocks), i32(n_blocks), i32(1), jax.ShapeDtypeStruct(rank.shape, jnp.int32)],
        scratch_shapes=[pltpu.SMEM((N_EXPERTS,), jnp.int32)],
        name="moe_plan",
    )(counts, rank, eid)


def _moe_layer(h, norm_g, wg_r, bg_r, we_r, be_r, w_gate, w_up, w_down, layer, out_g, final_norm):
    t, d = h.shape
    w_r = jnp.concatenate([wg_r, we_r, jnp.zeros((d, 128 - N_GROUPS - N_EXPERTS), F32)], axis=1)
    b_r = jnp.concatenate([bg_r, be_r, jnp.zeros((128 - N_GROUPS - N_EXPERTS,), F32)]).reshape(1, 128)
    route = _ffn_norm_router(h, norm_g, w_r, b_r)
    expert_id = route[:, TOP_K:2 * TOP_K].astype(jnp.int32)
    flat = expert_id.reshape(-1)
    order = jnp.argsort(flat).astype(jnp.int32)
    st = order // TOP_K
    rank_sm = jnp.argsort(order).astype(jnp.int32).reshape(t, TOP_K).T.reshape(-1, 128)
    eid_sm = expert_id.T.reshape(-1, 128)
    counts = jnp.sum((flat[:, None] == jnp.arange(N_EXPERTS, dtype=jnp.int32)[None, :]).astype(jnp.int32), axis=0)
    plan = _plan(counts, rank_sm, eid_sm, (t * TOP_K + MOE_BLOCK - 1) // MOE_BLOCK + N_EXPERTS)
    y_rows = _experts(h, norm_g, st, plan, w_gate, w_up, w_down, layer)
    return _combine(h, y_rows, plan[4].reshape(-1), route, out_g, final_norm)


_D0 = A_COLS + B_COLS + C_COLS
_W_IN_PIECES = (
    (OFF_BQ, A_COLS, B_COLS), (OFF_CX, A_COLS + B_COLS, C_COLS),
    (OFF_DV, _D0 + 512, 512), (OFF_DGT, _D0 + 1040, 512),
    (OFF_AR, 0, 3 * MIX_W), (OFF_DQ, _D0, 512),
    (OFF_AL, 3 * MIX_W, RWKV_LORA), (OFF_DAL, _D0 + 1024, GLA_LORA),
    (OFF_G, _D0 + D_COLS, 4 * D_MODEL),
)


RELAYOUT_ROWS = 512
_SPECIAL_BLOCK = OFF_AL // RELAYOUT_ROWS


def _w_in_source_rows():
    rows = np.zeros((NP_COLS // RELAYOUT_ROWS,), np.int32)
    for dst, src, width in _W_IN_PIECES:
        if width % RELAYOUT_ROWS == 0:
            for k in range(width // RELAYOUT_ROWS):
                rows[dst // RELAYOUT_ROWS + k] = src + k * RELAYOUT_ROWS
    return rows


def _relayout_kernel(src_ref, w_hbm, extra_hbm, o_ref, stage, sem, *, layer):
    j = pl.program_id(0)
    slot = j % 2

    def start(blk, sl):
        @pl.when(blk == _SPECIAL_BLOCK)
        def _():
            pltpu.make_async_copy(extra_hbm, stage.at[sl], sem.at[sl]).start()

        @pl.when(blk != _SPECIAL_BLOCK)
        def _():
            rows = pl.ds(pl.multiple_of(src_ref[blk], 16), RELAYOUT_ROWS)
            pltpu.make_async_copy(w_hbm.at[layer, rows, :], stage.at[sl], sem.at[sl]).start()

    @pl.when(j == 0)
    def _():
        start(0, 0)

    @pl.when(j + 1 < pl.num_programs(0))
    def _():
        start(j + 1, 1 - slot)

    pltpu.make_async_copy(extra_hbm, stage.at[slot], sem.at[slot]).wait()
    o_ref[...] = stage[slot].astype(o_ref.dtype)


def _permute_w_in(w_in, layer):
    w_t = jnp.swapaxes(w_in, 1, 2)
    d = w_t.shape[2]
    zeros = lambda n: jnp.zeros((n, d), F32)
    extra = jnp.concatenate([w_t[layer, 3 * MIX_W:A_COLS], zeros(OFF_DAL - OFF_AL - RWKV_LORA),
                             w_t[layer, _D0 + 1024:_D0 + 1024 + GLA_LORA],
                             zeros(OFF_G - OFF_DAL - GLA_LORA)], axis=0)
    hbm = pl.BlockSpec(memory_space=pl.ANY)
    return pl.pallas_call(
        functools.partial(_relayout_kernel, layer=layer),
        grid_spec=pltpu.PrefetchScalarGridSpec(
            num_scalar_prefetch=1,
            grid=(NP_COLS // RELAYOUT_ROWS,),
            in_specs=[hbm, hbm],
            out_specs=pl.BlockSpec((RELAYOUT_ROWS, d), lambda j, *_: (j, 0)),
            scratch_shapes=[pltpu.VMEM((2, RELAYOUT_ROWS, d), F32), pltpu.SemaphoreType.DMA((2,))],
        ),
        out_shape=jax.ShapeDtypeStruct((NP_COLS, d), BF16),
        compiler_params=_cparams(("arbitrary",)),
        name="w_in_relayout",
    )(jnp.asarray(_w_in_source_rows()), w_t, extra)


def kernel(x, positions, norm_mix_g, w_in, rwkv_mu, rwkv_w0, rwkv_w_up, rwkv_a0, rwkv_a_up, rwkv_g_up, rwkv_k_k, rwkv_k_a, rwkv_r_k, rwkv_ln_g, rwkv_ln_b, rwkv_w_o, ret_norm_g, ret_w_o, lru_conv_w, lru_conv_b, lru_w_a, lru_b_a, lru_w_x, lru_b_x, lru_lambda, lru_w_o, gla_alpha_up, gla_alpha_b, gla_norm_g, gla_w_o, w_out, norm_ffn_g, router_group_w, router_group_b, router_expert_w, router_expert_b, moe_w_gate, moe_w_up, moe_w_down, final_norm_g):
    b_, s_, d = x.shape
    assert b_ == 1 and d == D_MODEL
    depth = w_in.shape[0]
    h = x.reshape(s_, d)
    pos_b = jnp.broadcast_to(positions.reshape(s_, 1), (s_, 128)).astype(jnp.int32)
    cos_t, sin_t = _rope_table(pos_b)
    for l in range(depth):
        xn = _rmsnorm(h, norm_mix_g[l], BF16)
        w_t = _permute_w_in(w_in, l)
        p = _matmul_nt(xn, w_t, 0, OFF_G, 2048, 512, F32, "in_proj")
        y_a = _rwkv_branch(p, rwkv_mu[l], rwkv_w0[l], rwkv_w_up[l], rwkv_a0[l], rwkv_a_up[l], rwkv_g_up[l],
                           rwkv_k_k[l], rwkv_k_a[l], rwkv_r_k[l], rwkv_ln_g[l], rwkv_ln_b[l])
        y_b = _retention(p, cos_t, sin_t, ret_norm_g[l])
        y_c = _rglru(p, pos_b, lru_conv_w[l], lru_conv_b[l], lru_w_a[l], lru_b_a[l], lru_w_x[l], lru_b_x[l],
                     lru_lambda[l])
        y_d = _gla(p, gla_alpha_up[l], gla_alpha_b[l], gla_norm_g[l])
        merged = _merge(xn, w_t, (y_a, y_b, y_c, y_d), (rwkv_w_o, ret_w_o, lru_w_o, gla_w_o), l)
        h = _matmul_residual(merged, w_out, l, h, 2048, 512)
        last = l == depth - 1
        h = _moe_layer(h, norm_ffn_g[l], router_group_w[l], router_group_b[l], router_expert_w[l],
                       router_expert_b[l], moe_w_gate, moe_w_up, moe_w_down, l,
                       final_norm_g if last else norm_ffn_g[l], last)
    return h.reshape(b_, s_, d)
```

```python
import functools
import math

import jax
import jax.numpy as jnp
import numpy as np
from jax import lax
from jax.experimental import pallas as pl
from jax.experimental.pallas import tpu as pltpu

F32 = jnp.float32
BF16 = jnp.bfloat16
HI = lax.Precision.HIGHEST

D_MODEL = 2048
MIX_W = 512
NORM_EPS = 1e-6
HEAD_DIM = 64
N_HEADS = MIX_W // HEAD_DIM

RWKV_W_LORA, RWKV_A_LORA, RWKV_G_LORA = 32, 32, 96
RWKV_LORA = RWKV_W_LORA + RWKV_A_LORA + RWKV_G_LORA
RWKV_DECAY_SCALE = 0.6065306597126334
RWKV_LN_EPS = 64e-5
RWKV_CHUNK = 64
RWKV_LOCAL_CHUNKS = 8

RET_CHUNK = 128
RET_STEP_CHUNKS = 4
ROPE_BASE = 10000.0

LRU_BLOCKS = 8
CONV_W = 4
LRU_C = 8.0

GLA_HEADS = 4
GLA_DK = 64
GLA_DV = 128
GLA_LORA = 16
GLA_GATE_NORM = 16.0
GLA_CHUNK = 64
GLA_STEP_CHUNKS = 4

N_GROUPS = 4
EXP_PER_GROUP = 8
N_EXPERTS = N_GROUPS * EXP_PER_GROUP
TOP_K = 2
D_EXPERT = 1024
MOE_BLOCK = 128
WEIGHT_CHUNKS = 4

A_COLS = 3 * MIX_W + RWKV_LORA
B_COLS = 4 * MIX_W
C_COLS = 2 * MIX_W
D_COLS = 2 * (MIX_W // 2) + MIX_W + GLA_LORA + MIX_W

OFF_BQ, OFF_BK, OFF_BV, OFF_BG = 0, 512, 1024, 1536
OFF_CX, OFF_CG = 2048, 2560
OFF_DV, OFF_DGT = 3072, 3584
OFF_AR, OFF_AK, OFF_AV = 4096, 4608, 5120
OFF_DQ, OFF_DK = 5632, 5888
OFF_AL = 6144
OFF_DAL = 6400
OFF_G = 6656
NP_COLS = OFF_G + 4 * D_MODEL

VMEM_LIMIT = 48 * 1024 * 1024
VMEM_LIMIT_EXPERT = 56 * 1024 * 1024


def _cparams(sem, vmem=VMEM_LIMIT):
    return pltpu.CompilerParams(dimension_semantics=sem, vmem_limit_bytes=vmem)


def _dot(a, b, prec=None):
    return jnp.dot(a, b, precision=prec, preferred_element_type=F32)


def _sigmoid(x):
    return 1.0 / (1.0 + jnp.exp(-x))


def _split(x):
    hi = x.astype(BF16)
    return hi, (x - hi.astype(F32)).astype(BF16)


def _dims(form, batched):
    ca, cb = {"nn": (1, 0), "nt": (1, 1), "tn": (0, 0)}[form]
    if batched:
        return (((ca + 1,), (cb + 1,)), ((0,), (0,)))
    return (((ca,), (cb,)), ((), ()))


def _mm(a, b, mode, form="nn"):
    dims = _dims(form, a.ndim == 3)
    if mode == "hi":
        return lax.dot_general(a, b, dims, precision=HI, preferred_element_type=F32)
    d = functools.partial(lax.dot_general, dimension_numbers=dims, preferred_element_type=F32)
    if mode == "x1":
        return d(a.astype(BF16), b.astype(BF16))
    ah, al = _split(a)
    bh, bl = _split(b)
    return d(ah, bh) + (d(ah, bl) + d(al, bh))


def _mm_exact_rhs(a, b, terms):
    b = b.astype(BF16)
    acc = None
    for _ in range(terms):
        piece = a.astype(BF16)
        part = _dot(piece, b)
        acc = part if acc is None else acc + part
        a = a - piece.astype(F32)
    return acc


def _mm_exact_lhs(a, b, terms):
    a = a.astype(BF16)
    acc = None
    for _ in range(terms):
        piece = b.astype(BF16)
        part = _dot(a, piece)
        acc = part if acc is None else acc + part
        b = b - piece.astype(F32)
    return acc


def _heads(x, width=HEAD_DIM):
    return jnp.stack([x[:, h * width:(h + 1) * width] for h in range(x.shape[1] // width)])


def _block_diag_const(n, blk, value):
    r = lax.broadcasted_iota(jnp.int32, (n, n), 0) // blk
    c = lax.broadcasted_iota(jnp.int32, (n, n), 1) // blk
    return jnp.where(r == c, value, 0.0).astype(F32)


def _rmsnorm_kernel(x_ref, g_ref, o_ref):
    x = x_ref[...]
    ms = jnp.mean(x * x, axis=-1, keepdims=True)
    o_ref[...] = (x * lax.rsqrt(ms + NORM_EPS) * g_ref[...]).astype(o_ref.dtype)


def _rmsnorm(x, g, out_dtype, tb=1024):
    t, d = x.shape
    return pl.pallas_call(
        _rmsnorm_kernel,
        grid=(t // tb,),
        in_specs=[pl.BlockSpec((tb, d), lambda i: (i, 0)), pl.BlockSpec((1, d), lambda i: (0, 0))],
        out_specs=pl.BlockSpec((tb, d), lambda i: (i, 0)),
        out_shape=jax.ShapeDtypeStruct((t, d), out_dtype),
        compiler_params=_cparams(("parallel",)),
        name="rmsnorm",
    )(x, g.reshape(1, d))


def _mm_nt_kernel(a_ref, b_ref, o_ref):
    o_ref[...] = _mm(a_ref[...], b_ref[...], "x1", "nt").astype(o_ref.dtype)


def _matmul_nt(a, b_t, row0, n, tm, tn, out_dtype, name):
    m, k = a.shape
    assert row0 % tn == 0 and n % tn == 0 and m % tm == 0
    return pl.pallas_call(
        _mm_nt_kernel,
        grid=(m // tm, n // tn),
        in_specs=[pl.BlockSpec((tm, k), lambda i, j: (i, 0)), pl.BlockSpec((tn, k), lambda i, j: (row0 // tn + j, 0))],
        out_specs=pl.BlockSpec((tm, tn), lambda i, j: (i, j)),
        out_shape=jax.ShapeDtypeStruct((m, n), out_dtype),
        compiler_params=_cparams(("parallel", "parallel")),
        name=name,
    )(a, b_t)


def _mm_res_kernel(a_ref, b_ref, r_ref, o_ref, b_bf):
    @pl.when(pl.program_id(1) == 0)
    def _():
        b_bf[...] = b_ref[...].astype(BF16)

    o_ref[...] = r_ref[...] + _dot(a_ref[...], b_bf[...])


def _matmul_residual(a, b, layer, res, tm, tn):
    m, k = a.shape
    n = b.shape[2]
    return pl.pallas_call(
        _mm_res_kernel,
        grid=(n // tn, m // tm),
        in_specs=[pl.BlockSpec((tm, k), lambda j, i: (i, 0)), pl.BlockSpec((None, k, tn), lambda j, i: (layer, 0, j)),
                  pl.BlockSpec((tm, tn), lambda j, i: (i, j))],
        out_specs=pl.BlockSpec((tm, tn), lambda j, i: (i, j)),
        out_shape=jax.ShapeDtypeStruct((m, n), F32),
        scratch_shapes=[pltpu.VMEM((k, tn), BF16)],
        compiler_params=_cparams(("parallel", "arbitrary"), VMEM_LIMIT_EXPERT),
        name="wout_residual",
    )(a, b, res)


def _shifted(x, tail_ref, sc_ref, width, first):
    tb = x.shape[0]
    sc_ref[0:8, 0:width] = jnp.where(first, 0.0, tail_ref[...])
    sc_ref[8:8 + tb, 0:width] = x
    return sc_ref[7:7 + tb, 0:width]


def _rwkv_prep_values(r_ref, k_ref, v_ref, l_ref, rt_ref, kt_ref, vt_ref, lt_ref,
                      mur_ref, muk_ref, muv_ref, mul_ref, wl_ref, b0_ref, kk_ref, ka_ref, sc_ref):
    first = pl.program_id(0) == 0

    def mix(x_ref, t_ref, mu_ref, width):
        x = x_ref[...]
        prev = _shifted(x, t_ref, sc_ref, width, first)
        return x + (prev - x) * mu_ref[...]

    r = mix(r_ref, rt_ref, mur_ref, MIX_W)
    k = mix(k_ref, kt_ref, muk_ref, MIX_W)
    v = mix(v_ref, vt_ref, muv_ref, MIX_W)
    zl = mix(l_ref, lt_ref, mul_ref, 256)
    lane = lax.broadcasted_iota(jnp.int32, zl.shape, 1)
    act = jnp.where(lane < RWKV_W_LORA, jnp.tanh(zl),
                    jnp.where(lane < RWKV_W_LORA + RWKV_A_LORA, zl, _sigmoid(zl)))
    lo = _mm(act, wl_ref[...], "x3") + b0_ref[...]
    lw = -RWKV_DECAY_SCALE * _sigmoid(lo[:, 0:MIX_W])
    a = _sigmoid(lo[:, MIX_W:2 * MIX_W])
    g = lo[:, 2 * MIX_W:3 * MIX_W]
    kk = k * kk_ref[...]
    ss = _mm_exact_rhs(kk * kk, _block_diag_const(MIX_W, HEAD_DIM, 1.0), 2)
    kkn = kk / jnp.maximum(jnp.sqrt(ss), 1e-12)
    return r, lw, k * (1.0 + (a - 1.0) * ka_ref[...]), v, kkn, kkn * a, g


RW_SC, RW_INV, RW_APPLY, RW_STATE, RW_SCAN = "x1", "x1", "x1", "x1", "x1"


def _rwkv_local_chunk(r, lw, k, v, kk, kka):
    c = r.shape[0]
    row = lax.broadcasted_iota(jnp.int32, (c, c), 0)
    col = lax.broadcasted_iota(jnp.int32, (c, c), 1)
    incl = row >= col
    strict = row > col
    eye = jnp.where(row == col, 1.0, 0.0)
    cum = _mm_exact_lhs(incl.astype(F32), lw, 3)
    last = cum[c - 1:c, :]
    pinv = jnp.exp(-cum)
    dl = jnp.exp(last - cum)
    at = _heads(-kk * jnp.exp(cum - lw))
    rt = _heads(r * jnp.exp(cum))
    bt = _heads(kka * pinv)
    kt = _heads(k * pinv)
    bl = _heads(kka * dl)
    kl = _heads(k * dl)
    v = _heads(v)
    pc = _heads(jnp.exp(last))
    sc = _mm(jnp.concatenate([at, rt], axis=1), jnp.concatenate([bt, kt], axis=1), RW_SC, "nt")
    a_ab = jnp.where(strict, sc[:, 0:c, 0:c], 0.0)
    a_ak = jnp.where(strict, sc[:, 0:c, c:2 * c], 0.0)
    r_b = jnp.where(incl, sc[:, c:2 * c, 0:c], 0.0)
    r_k = jnp.where(incl, sc[:, c:2 * c, c:2 * c], 0.0)
    x = eye + a_ab
    pw = _mm(a_ab, a_ab, RW_INV)
    levels = int(math.log2(c)) - 1
    for lvl in range(1, levels):
        prod = _mm(jnp.concatenate([x, pw], axis=1), pw, RW_INV)
        x = x + prod[:, 0:c]
        pw = prod[:, c:2 * c]
    x = x + _mm(x, pw, RW_INV)
    wu = _mm(x, jnp.concatenate([at, _mm(a_ak, v, RW_APPLY)], axis=2), RW_APPLY)
    lower = jnp.concatenate([jnp.zeros_like(v), v], axis=2)
    qy = _mm(jnp.concatenate([r_b, r_k], axis=2), jnp.concatenate([wu, lower], axis=1), RW_APPLY)
    wb = _mm(wu, bl, RW_STATE, "tn")
    n = wb[:, HEAD_DIM:] + _mm(v, kl, RW_STATE, "tn")
    m = eye * pc + wb[:, 0:HEAD_DIM]
    return rt + qy[:, :, 0:HEAD_DIM], qy[:, :, HEAD_DIM:], m, n


def _rwkv_local_kernel(*refs):
    prep_refs, (q_ref, y0_ref, m_ref, n_ref, ro_ref, ko_ref, vo_ref, g_ref, sc_ref) = refs[:16], refs[16:]
    r, lw, k, v, kkn, kka, g = _rwkv_prep_values(*prep_refs, sc_ref)
    ro_ref[...] = r
    ko_ref[...] = k
    vo_ref[...] = v
    g_ref[...] = g
    c = RWKV_CHUNK
    outs = [_rwkv_local_chunk(*(x[j * c:(j + 1) * c, :] for x in (r, lw, k, v, kkn, kka)))
            for j in range(r.shape[0] // c)]
    for j, res in enumerate(outs):
        for o_ref, val in zip((q_ref, y0_ref, m_ref, n_ref), res):
            for h in range(N_HEADS):
                o_ref[j * c:(j + 1) * c, h * HEAD_DIM:(h + 1) * HEAD_DIM] = val[h]


def _rwkv_scan_kernel(q_ref, y0_ref, m_ref, n_ref, r_ref, k_ref, v_ref, g_ref, lng_ref, lnb_ref, rk_ref,
                      o_ref, s_ref, y_sc):
    @pl.when(pl.program_id(0) == 0)
    def _():
        s_ref[...] = jnp.zeros_like(s_ref)

    c = RWKV_CHUNK
    s = s_ref[...]
    for j in range(q_ref.shape[0] // c):
        rows = slice(j * c, (j + 1) * c)
        y = _mm(_heads(q_ref[rows, :]), s, RW_SCAN, "nt") + _heads(y0_ref[rows, :])
        s = _mm(s, _heads(m_ref[rows, :]), RW_SCAN) + _heads(n_ref[rows, :])
        for h in range(N_HEADS):
            y_sc[rows, h * HEAD_DIM:(h + 1) * HEAD_DIM] = y[h]
    s_ref[...] = s
    y = y_sc[...]
    avg = _block_diag_const(MIX_W, HEAD_DIM, 1.0 / HEAD_DIM)
    mean = _mm_exact_rhs(y, avg, 2)
    yc = y - mean
    var = _mm_exact_rhs(yc * yc, avg, 2)
    yn = yc * lax.rsqrt(var + RWKV_LN_EPS) * lng_ref[...] + lnb_ref[...]
    v = v_ref[...]
    bonus = _mm_exact_rhs(r_ref[...] * k_ref[...] * rk_ref[...], _block_diag_const(MIX_W, HEAD_DIM, 1.0), 2) * v
    o_ref[...] = ((yn + bonus) * g_ref[...]).astype(o_ref.dtype)


def _rwkv_branch(p, mu, w0, w_up, a0, a_up, g_up, k_k, k_a, r_k, ln_g, ln_b, tb_scan=512):
    t = p.shape[0]
    tb = RWKV_LOCAL_CHUNKS * RWKV_CHUNK
    nb8 = tb // 8

    def blk(width, off):
        return pl.BlockSpec((tb, width), lambda i: (i, off // width))

    def tail(width, off):
        return pl.BlockSpec((8, width), lambda i: (jnp.maximum(i * nb8 - 1, 0), off // width))

    def vec(width):
        return pl.BlockSpec((1, width), lambda i: (0, 0))

    w_lora = jnp.zeros((256, 3 * MIX_W), F32)
    w_lora = w_lora.at[0:32, 0:MIX_W].set(w_up)
    w_lora = w_lora.at[32:64, MIX_W:2 * MIX_W].set(a_up)
    w_lora = w_lora.at[64:160, 2 * MIX_W:].set(g_up)
    b0 = jnp.concatenate([w0, a0, jnp.zeros((MIX_W,), F32)]).reshape(1, 3 * MIX_W)
    mu_l = jnp.concatenate([mu[3 * MIX_W:], jnp.zeros((256 - RWKV_LORA,), F32)]).reshape(1, 256)
    f = jax.ShapeDtypeStruct((t, MIX_W), F32)
    oblk = pl.BlockSpec((tb, MIX_W), lambda i: (i, 0))
    qm, y0, m, n, r, k, v, g = pl.pallas_call(
        _rwkv_local_kernel,
        grid=(t // tb,),
        in_specs=[blk(MIX_W, OFF_AR), blk(MIX_W, OFF_AK), blk(MIX_W, OFF_AV), blk(256, OFF_AL),
                  tail(MIX_W, OFF_AR), tail(MIX_W, OFF_AK), tail(MIX_W, OFF_AV), tail(256, OFF_AL),
                  vec(MIX_W), vec(MIX_W), vec(MIX_W), vec(256),
                  pl.BlockSpec((256, 3 * MIX_W), lambda i: (0, 0)), vec(3 * MIX_W), vec(MIX_W), vec(MIX_W)],
        out_specs=[oblk] * 8,
        out_shape=[f] * 8,
        scratch_shapes=[pltpu.VMEM((tb + 8, MIX_W), F32)],
        compiler_params=_cparams(("parallel",)),
        name="rwkv_local",
    )(p, p, p, p, p, p, p, p,
      mu[0:MIX_W].reshape(1, -1), mu[MIX_W:2 * MIX_W].reshape(1, -1), mu[2 * MIX_W:3 * MIX_W].reshape(1, -1), mu_l,
      w_lora, b0, k_k.reshape(1, -1), k_a.reshape(1, -1))
    sblk = pl.BlockSpec((tb_scan, MIX_W), lambda i: (i, 0))
    svec = pl.BlockSpec((1, MIX_W), lambda i: (0, 0))
    return pl.pallas_call(
        _rwkv_scan_kernel,
        grid=(t // tb_scan,),
        in_specs=[sblk] * 8 + [svec] * 3,
        out_specs=sblk,
        out_shape=jax.ShapeDtypeStruct((t, MIX_W), BF16),
        scratch_shapes=[pltpu.VMEM((N_HEADS, HEAD_DIM, HEAD_DIM), F32), pltpu.VMEM((tb_scan, MIX_W), F32)],
        compiler_params=_cparams(("arbitrary",)),
        name="rwkv_scan",
    )(qm, y0, m, n, r, k, v, g, ln_g.reshape(1, -1), ln_b.reshape(1, -1), r_k.reshape(1, -1))


def _rope_kernel(pos_ref, inv_ref, cos_ref, sin_ref):
    ang = pos_ref[...].astype(F32) * inv_ref[...]
    lane = lax.broadcasted_iota(jnp.int32, ang.shape, 1)
    cos_ref[...] = jnp.cos(ang)
    sin_ref[...] = jnp.where(lane % HEAD_DIM < HEAD_DIM // 2, -jnp.sin(ang), jnp.sin(ang))


def _rope_table(pos_b, tb=512):
    t = pos_b.shape[0]
    d = HEAD_DIM
    inv = 1.0 / (ROPE_BASE ** (jnp.arange(0, d, 2, dtype=F32) / d))
    inv_b = jnp.tile(inv, 4).reshape(1, 128)
    blk = pl.BlockSpec((tb, 128), lambda i: (i, 0))
    return pl.pallas_call(
        _rope_kernel,
        grid=(t // tb,),
        in_specs=[blk, pl.BlockSpec((1, 128), lambda i: (0, 0))],
        out_specs=[blk, blk],
        out_shape=[jax.ShapeDtypeStruct((t, 128), F32)] * 2,
        compiler_params=_cparams(("parallel",)),
        name="rope_table",
    )(pos_b, inv_b)


def _ret_kernel(q_ref, k_ref, v_ref, g_ref, cos_ref, sin_ref, dec_ref, zeta_ref, xi_ref, cd_ref, ng_ref,
                o_ref, s_ref, o_sc):
    @pl.when(pl.program_id(0) == 0)
    def _():
        s_ref[...] = jnp.zeros_like(s_ref)

    cos = jnp.concatenate([cos_ref[...]] * 4, axis=1)
    sin = jnp.concatenate([sin_ref[...]] * 4, axis=1)
    lane = lax.broadcasted_iota(jnp.int32, cos.shape, 1)
    lower_half = lane % HEAD_DIM < HEAD_DIM // 2

    def rope(x):
        swapped = jnp.where(lower_half, pltpu.roll(x, MIX_W - HEAD_DIM // 2, 1), pltpu.roll(x, HEAD_DIM // 2, 1))
        return x * cos + swapped * sin

    c = RET_CHUNK
    qr = rope(q_ref[...])
    kr = rope(k_ref[...]) * (HEAD_DIM ** -0.5)
    s = s_ref[...]
    for j in range(q_ref.shape[0] // c):
        rows = slice(j * c, (j + 1) * c)
        q = _heads(qr[rows])
        k = _heads(kr[rows])
        v = _heads(v_ref[rows, :])
        scores = _mm(q, k, "x1", "nt") * dec_ref[...]
        o = _mm(scores, v, "x1") + _mm(q * xi_ref[...], s, "x1")
        s = s * cd_ref[...] + _mm(k * zeta_ref[...], v, "x1", "tn")
        on = o * lax.rsqrt(jnp.mean(o * o, axis=-1, keepdims=True) + NORM_EPS)
        for h in range(N_HEADS):
            o_sc[rows, h * HEAD_DIM:(h + 1) * HEAD_DIM] = on[h]
    s_ref[...] = s
    gt = g_ref[...]
    o_ref[...] = (o_sc[...] * ng_ref[...] * (gt * _sigmoid(gt))).astype(o_ref.dtype)


def _retention(p, cos_t, sin_t, norm_g):
    t = p.shape[0]
    c = RET_CHUNK
    h = N_HEADS
    log_g = jnp.log(1.0 - jnp.exp(jnp.linspace(math.log(1.0 / 32), math.log(1.0 / 512), h, dtype=F32)))
    idx = jnp.arange(c, dtype=F32)
    diff = idx[:, None] - idx[None, :]
    decay = jnp.where(diff >= 0, jnp.exp(log_g[:, None, None] * jnp.maximum(diff, 0.0)[None]), 0.0)
    zeta = jnp.exp(log_g[:, None] * (c - 1 - idx)[None])
    xi = jnp.exp(log_g[:, None] * (idx + 1)[None])
    cdec = jnp.exp(log_g * c)
    zeta_b = jnp.broadcast_to(zeta[:, :, None], (h, c, HEAD_DIM))
    xi_b = jnp.broadcast_to(xi[:, :, None], (h, c, HEAD_DIM))
    cd_b = jnp.broadcast_to(cdec[:, None, None], (h, HEAD_DIM, HEAD_DIM))

    tb = RET_STEP_CHUNKS * c

    def blk(off):
        return pl.BlockSpec((tb, MIX_W), lambda i: (i, off // MIX_W))

    tab = pl.BlockSpec((tb, 128), lambda i: (i, 0))

    def const(shape):
        return pl.BlockSpec(shape, lambda i: (0,) * len(shape))

    return pl.pallas_call(
        _ret_kernel,
        grid=(t // tb,),
        in_specs=[blk(OFF_BQ), blk(OFF_BK), blk(OFF_BV), blk(OFF_BG), tab, tab,
                  const((h, c, c)), const((h, c, HEAD_DIM)), const((h, c, HEAD_DIM)),
                  const((h, HEAD_DIM, HEAD_DIM)), const((1, MIX_W))],
        out_specs=pl.BlockSpec((tb, MIX_W), lambda i: (i, 0)),
        out_shape=jax.ShapeDtypeStruct((t, MIX_W), BF16),
        scratch_shapes=[pltpu.VMEM((h, HEAD_DIM, HEAD_DIM), F32), pltpu.VMEM((tb, MIX_W), F32)],
        compiler_params=_cparams(("arbitrary",)),
        name="retention",
    )(p, p, p, p, cos_t, sin_t, decay, zeta_b, xi_b, cd_b, norm_g.reshape(1, -1))


def _lru_kernel(x_ref, xt_ref, gb_ref, pos_ref, cw_ref, cb_ref, wa_ref, ba_ref, wx_ref, bx_ref, lam_ref,
                o_ref, sc_ref, a_sc, b_sc, h_ref):
    tb = x_ref.shape[0]
    first = pl.program_id(0) == 0

    @pl.when(first)
    def _():
        h_ref[...] = jnp.zeros_like(h_ref)

    x = x_ref[...]
    sc_ref[0:8, :] = jnp.where(first, 0.0, xt_ref[...])
    sc_ref[8:8 + tb, :] = x
    cw = cw_ref[...]
    xc = cb_ref[...] + x * cw[CONV_W - 1:CONV_W, :]
    for j in range(1, CONV_W):
        xc = xc + sc_ref[8 - j:8 - j + tb, :] * cw[CONV_W - 1 - j:CONV_W - j, :]
    r = _sigmoid(_mm(xc, wa_ref[...], "x3") + ba_ref[...])
    ig = _sigmoid(_mm(xc, wx_ref[...], "x3") + bx_ref[...])
    nl = -lam_ref[...]
    softplus = jnp.maximum(nl, 0.0) + jnp.log1p(jnp.exp(-jnp.abs(nl)))
    log_a = -LRU_C * r * softplus
    pos = pos_ref[...]
    reset = jnp.concatenate([pos, pos, pos, pos], axis=1) == 0
    a_sc[...] = jnp.where(reset, 0.0, jnp.exp(log_a))
    th = jnp.tanh(log_a)
    b_sc[...] = jnp.where(reset, 1.0, jnp.sqrt(-2.0 * th / (1.0 - th))) * ig * xc

    row = lax.broadcasted_iota(jnp.int32, (8, MIX_W), 0)

    def group(gi, h):
        off = pl.multiple_of(gi * 8, 8)
        a = a_sc[pl.ds(off, 8), :]
        b = b_sc[pl.ds(off, 8), :]
        for d in (1, 2, 4):
            keep = row >= d
            b = jnp.where(keep, a * pltpu.roll(b, d, 0) + b, b)
            a = jnp.where(keep, a * pltpu.roll(a, d, 0), a)
        hs = a * h + b
        b_sc[pl.ds(off, 8), :] = hs
        return hs[7:8, :]

    h_ref[...] = lax.fori_loop(0, tb // 8, group, h_ref[...])
    gb = gb_ref[...]
    gelu = gb * (0.5 * (1.0 + jnp.tanh(math.sqrt(2.0 / math.pi) * (gb + 0.044715 * (gb * gb * gb)))))
    o_ref[...] = (b_sc[...] * gelu).astype(o_ref.dtype)


def _block_diag_weight(w):
    g, n, _ = w.shape
    eye = jnp.eye(g, dtype=w.dtype)
    return (eye[:, None, :, None] * w[:, :, None, :]).reshape(g * n, g * n)


def _rglru(p, pos_b, conv_w, conv_b, w_a, b_a, w_x, b_x, lam, tb=512):
    t = p.shape[0]
    nb8 = tb // 8
    vec = pl.BlockSpec((1, MIX_W), lambda i: (0, 0))
    mat = pl.BlockSpec((MIX_W, MIX_W), lambda i: (0, 0))
    cw8 = jnp.concatenate([conv_w, jnp.zeros((8 - CONV_W, MIX_W), F32)], axis=0)
    return pl.pallas_call(
        _lru_kernel,
        grid=(t // tb,),
        in_specs=[pl.BlockSpec((tb, MIX_W), lambda i: (i, OFF_CX // MIX_W)),
                  pl.BlockSpec((8, MIX_W), lambda i: (jnp.maximum(i * nb8 - 1, 0), OFF_CX // MIX_W)),
                  pl.BlockSpec((tb, MIX_W), lambda i: (i, OFF_CG // MIX_W)),
                  pl.BlockSpec((tb, 128), lambda i: (i, 0)),
                  pl.BlockSpec((8, MIX_W), lambda i: (0, 0)), vec, mat, vec, mat, vec, vec],
        out_specs=pl.BlockSpec((tb, MIX_W), lambda i: (i, 0)),
        out_shape=jax.ShapeDtypeStruct((t, MIX_W), BF16),
        scratch_shapes=[pltpu.VMEM((tb + 8, MIX_W), F32), pltpu.VMEM((tb, MIX_W), F32),
                        pltpu.VMEM((tb, MIX_W), F32), pltpu.VMEM((1, MIX_W), F32)],
        compiler_params=_cparams(("arbitrary",)),
        name="rglru",
    )(p, p, p, pos_b, cw8, conv_b.reshape(1, -1), _block_diag_weight(w_a), b_a.reshape(1, -1),
      _block_diag_weight(w_x), b_x.reshape(1, -1), lam.reshape(1, -1))


def _gla_kernel(v_ref, gt_ref, q_ref, k_ref, al_ref, aup_ref, ab_ref, ng_ref, o_ref, s_ref):
    @pl.when(pl.program_id(0) == 0)
    def _():
        s_ref[...] = jnp.zeros_like(s_ref)

    c = GLA_CHUNK
    row = lax.broadcasted_iota(jnp.int32, (c, c), 0)
    col = lax.broadcasted_iota(jnp.int32, (c, c), 1)
    causal = row >= col
    pre = _mm(al_ref[...], aup_ref[...], "x3") + ab_ref[...]
    log_a = (jnp.minimum(pre, 0.0) - jnp.log1p(jnp.exp(-jnp.abs(pre)))) / GLA_GATE_NORM
    s = s_ref[...]
    outs = []
    for j in range(q_ref.shape[0] // c):
        rows = slice(j * c, (j + 1) * c)
        bcum = _mm_exact_lhs(causal.astype(F32), log_a[rows], 3)
        blast = bcum[c - 1:c, :]
        k = k_ref[rows, :]
        q_e = _heads(q_ref[rows, :] * (GLA_DK ** -0.5) * jnp.exp(bcum))
        k_e = _heads(k * jnp.exp(-bcum))
        k_l = _heads(k * jnp.exp(blast - bcum))
        dec = _heads(jnp.exp(blast))
        v = _heads(v_ref[rows, :], GLA_DV)
        scores = jnp.where(causal, _mm(q_e, k_e, "x1", "nt"), 0.0)
        o = _mm(scores, v, "x1") + _mm(q_e, s, "x1", "nt")
        s = s * dec + _mm(v, k_l, "x1", "tn")
        on = o * lax.rsqrt(jnp.mean(o * o, axis=-1, keepdims=True) + NORM_EPS)
        outs.append(jnp.concatenate([on[h] for h in range(GLA_HEADS)], axis=1))
    s_ref[...] = s
    on = jnp.concatenate(outs, axis=0) * ng_ref[...]
    gt = gt_ref[...]
    o_ref[...] = (on * (gt * _sigmoid(gt))).astype(o_ref.dtype)


def _gla(p, alpha_up, alpha_b, norm_g):
    t = p.shape[0]
    c = GLA_STEP_CHUNKS * GLA_CHUNK
    hk = MIX_W // 2
    aup = jnp.concatenate([alpha_up, jnp.zeros((128 - GLA_LORA, hk), F32)], axis=0)

    def blk(width, off):
        return pl.BlockSpec((c, width), lambda i: (i, off // width))

    return pl.pallas_call(
        _gla_kernel,
        grid=(t // c,),
        in_specs=[blk(MIX_W, OFF_DV), blk(MIX_W, OFF_DGT), blk(hk, OFF_DQ), blk(hk, OFF_DK), blk(128, OFF_DAL),
                  pl.BlockSpec((128, hk), lambda i: (0, 0)), pl.BlockSpec((1, hk), lambda i: (0, 0)),
                  pl.BlockSpec((1, MIX_W), lambda i: (0, 0))],
        out_specs=pl.BlockSpec((c, MIX_W), lambda i: (i, 0)),
        out_shape=jax.ShapeDtypeStruct((t, MIX_W), BF16),
        scratch_shapes=[pltpu.VMEM((GLA_HEADS, GLA_DV, GLA_DK), F32)],
        compiler_params=_cparams(("arbitrary",)),
        name="gla",
    )(p, p, p, p, p, aup, alpha_b.reshape(1, -1), norm_g.reshape(1, -1))


def _merge_kernel(xn_ref, ga_ref, gb_ref, gc_ref, gd_ref, ya_ref, yb_ref, yc_ref, yd_ref,
                  wa_ref, wb_ref, wc_ref, wd_ref, o_ref, w_bf):
    @pl.when(pl.program_id(1) == 0)
    def _():
        for kk, w_ref in enumerate((wa_ref, wb_ref, wc_ref, wd_ref)):
            w_bf[kk] = w_ref[0].astype(BF16)

    xn = xn_ref[...]
    acc = None
    for kk, (g_ref, y_ref) in enumerate(((ga_ref, ya_ref), (gb_ref, yb_ref), (gc_ref, yc_ref), (gd_ref, yd_ref))):
        term = _sigmoid(_mm(xn, g_ref[...], "x1", "nt")) * _dot(y_ref[...], w_bf[kk])
        acc = term if acc is None else acc + term
    o_ref[...] = acc.astype(o_ref.dtype)


def _merge(xn, w_t, ys, ws, layer, tm=1024, tn=512):
    t, d = xn.shape
    xspec = pl.BlockSpec((tm, d), lambda j, i: (i, 0))
    yspec = pl.BlockSpec((tm, MIX_W), lambda j, i: (i, 0))
    wspec = pl.BlockSpec((1, MIX_W, tn), lambda j, i: (layer, 0, j))

    def gspec(kk):
        return pl.BlockSpec((tn, d), lambda j, i: ((OFF_G + kk * D_MODEL) // tn + j, 0))

    return pl.pallas_call(
        _merge_kernel,
        grid=(D_MODEL // tn, t // tm),
        in_specs=[xspec] + [gspec(kk) for kk in range(4)] + [yspec] * 4 + [wspec] * 4,
        out_specs=pl.BlockSpec((tm, tn), lambda j, i: (i, j)),
        out_shape=jax.ShapeDtypeStruct((t, D_MODEL), BF16),
        scratch_shapes=[pltpu.VMEM((4, MIX_W, tn), BF16)],
        compiler_params=_cparams(("parallel", "arbitrary"), VMEM_LIMIT_EXPERT),
        name="merge",
    )(xn, w_t, w_t, w_t, w_t, *ys, *ws)


def _ffn_norm_kernel(h_ref, g_ref, wr_ref, br_ref, rt_ref):
    x = h_ref[...]
    ms = jnp.mean(x * x, axis=-1, keepdims=True)
    hn = x * lax.rsqrt(ms + NORM_EPS) * g_ref[...]
    z = _mm(hn, wr_ref[...], "x3") + br_ref[...]
    lane = lax.broadcasted_iota(jnp.int32, z.shape, 1)
    neg = jnp.float32(-1e30)

    def first_argmax(v):
        m = jnp.max(v, axis=-1, keepdims=True)
        return m, jnp.min(jnp.where(v == m, lane, 128), axis=-1, keepdims=True)

    is_group = lane < N_GROUPS
    zg = jnp.where(is_group, z, neg)
    mg, g_idx = first_argmax(zg)
    pg_top = 1.0 / jnp.sum(jnp.where(is_group, jnp.exp(zg - mg), 0.0), axis=-1, keepdims=True)
    lo = N_GROUPS + g_idx * EXP_PER_GROUP
    in_group = jnp.logical_and(lane >= lo, lane < lo + EXP_PER_GROUP)
    ze = jnp.where(in_group, z, neg)
    m1, i1 = first_argmax(ze)
    se = jnp.sum(jnp.where(in_group, jnp.exp(ze - m1), 0.0), axis=-1, keepdims=True)
    m2, i2 = first_argmax(jnp.where(lane == i1, neg, ze))
    p1 = 1.0 / se
    p2 = jnp.exp(m2 - m1) / se
    tot = p1 + p2
    rt_ref[...] = jnp.where(lane == 0, pg_top * (p1 / tot),
                            jnp.where(lane == 1, pg_top * (p2 / tot),
                                      jnp.where(lane == 2, (i1 - N_GROUPS).astype(F32),
                                                jnp.where(lane == 3, (i2 - N_GROUPS).astype(F32), 0.0))))


def _ffn_norm_router(h, g, w_r, b_r, tb=512):
    t, d = h.shape
    return pl.pallas_call(
        _ffn_norm_kernel,
        grid=(t // tb,),
        in_specs=[pl.BlockSpec((tb, d), lambda i: (i, 0)), pl.BlockSpec((1, d), lambda i: (0, 0)),
                  pl.BlockSpec((d, 128), lambda i: (0, 0)), pl.BlockSpec((1, 128), lambda i: (0, 0))],
        out_specs=pl.BlockSpec((tb, 128), lambda i: (i, 0)),
        out_shape=jax.ShapeDtypeStruct((t, 128), F32),
        compiler_params=_cparams(("parallel",)),
        name="ffn_norm_router",
    )(h, g.reshape(1, d), w_r, b_r)


def _row_copy(src_hbm, idx, dst, r, sem):
    return pltpu.make_async_copy(src_hbm.at[pl.ds(idx, 1), :], dst.at[pl.ds(r, 1), :], sem)


def _expert_changed(i, exp_ref):
    return jnp.logical_or(i == 0, exp_ref[i] != exp_ref[jnp.maximum(i - 1, 0)])


def _gather_rows(src_hbm, row_of, n, dst, sem):
    def start(r, carry):
        _row_copy(src_hbm, row_of(r), dst, r, sem).start()
        return carry

    lax.fori_loop(0, n, start, 0, unroll=8)


def _gather_wait(src_hbm, dst, sem):
    pltpu.make_async_copy(src_hbm.at[pl.ds(0, dst.shape[0]), :], dst, sem).wait()


def _expert_kernel(st_ref, exp_ref, base_ref, nxt_ref, nblk_ref, x_hbm, wg_hbm, wu_hbm, wd_hbm, g_ref, o_ref,
                   xbuf, stage_g, stage_u, stage_d, wg_bf, wu_bf, wd_bf, xsem, wsem, *, layer):
    i = pl.program_id(0)
    n = nblk_ref[0]
    slot = i % 2

    stages = (stage_g, stage_u, stage_d)
    caches = (wg_bf, wu_bf, wd_bf)

    def chunk_rows(m, c):
        rows = stages[m].shape[0] // WEIGHT_CHUNKS
        return pl.ds(c * rows, rows)

    def weight_copy(e, m, c):
        src = (wg_hbm, wu_hbm, wd_hbm)[m]
        return pltpu.make_async_copy(src.at[layer, e, chunk_rows(m, c), :], stages[m].at[chunk_rows(m, c), :],
                                     wsem.at[m, c])

    def start_weights(e):
        for c in range(WEIGHT_CHUNKS):
            for m in range(3):
                weight_copy(e, m, c).start(priority=1)

    last = st_ref.shape[0] - 1

    def fetch(blk, sl):
        base = base_ref[blk]
        _gather_rows(x_hbm, lambda r: st_ref[jnp.minimum(base + r, last)], MOE_BLOCK, xbuf.at[sl], xsem.at[sl])

    @pl.when(jnp.logical_and(i == 0, n > 0))
    def _():
        fetch(0, 0)
        start_weights(exp_ref[0])

    @pl.when(i + 1 < n)
    def _():
        fetch(i + 1, 1 - slot)

    @pl.when(i < n)
    def _():
        @pl.when(_expert_changed(i, exp_ref))
        def _():
            nxt = nxt_ref[i]
            for c in range(WEIGHT_CHUNKS):
                for m in range(3):
                    weight_copy(exp_ref[i], m, c).wait()
                    caches[m][chunk_rows(m, c), :] = stages[m][chunk_rows(m, c), :].astype(BF16)

                @pl.when(nxt >= 0)
                def _():
                    for m in range(3):
                        weight_copy(nxt, m, c).start(priority=1)

        _gather_wait(x_hbm, xbuf.at[slot], xsem.at[slot])
        x = xbuf[slot]
        x = (x * lax.rsqrt(jnp.mean(x * x, axis=-1, keepdims=True) + NORM_EPS) * g_ref[...]).astype(BF16)
        gate = _dot(x, wg_bf[...])
        up = _dot(x, wu_bf[...])
        hmid = (gate * _sigmoid(gate) * up).astype(BF16)
        o_ref[...] = _dot(hmid, wd_bf[...])

    @pl.when(i >= n)
    def _():
        o_ref[...] = jnp.zeros_like(o_ref)


def _experts(h, norm_g, st, plan, w_gate, w_up, w_down, layer):
    block_exp, src_base, next_exp, n_used, _ = plan
    n_blocks = block_exp.shape[0]
    d = h.shape[1]
    hbm = pl.BlockSpec(memory_space=pl.ANY)
    return pl.pallas_call(
        functools.partial(_expert_kernel, layer=layer),
        grid_spec=pltpu.PrefetchScalarGridSpec(
            num_scalar_prefetch=5,
            grid=(n_blocks,),
            in_specs=[hbm, hbm, hbm, hbm, pl.BlockSpec((1, d), lambda i, *_: (0, 0))],
            out_specs=pl.BlockSpec((MOE_BLOCK, d), lambda i, *_: (i, 0)),
            scratch_shapes=[pltpu.VMEM((2, MOE_BLOCK, d), F32),
                            pltpu.VMEM((d, D_EXPERT), F32), pltpu.VMEM((d, D_EXPERT), F32),
                            pltpu.VMEM((D_EXPERT, d), F32),
                            pltpu.VMEM((d, D_EXPERT), BF16), pltpu.VMEM((d, D_EXPERT), BF16),
                            pltpu.VMEM((D_EXPERT, d), BF16),
                            pltpu.SemaphoreType.DMA((2,)), pltpu.SemaphoreType.DMA((3, WEIGHT_CHUNKS))],
        ),
        out_shape=jax.ShapeDtypeStruct((n_blocks * MOE_BLOCK, d), F32),
        compiler_params=_cparams(("arbitrary",), VMEM_LIMIT_EXPERT),
        name="experts",
    )(st, block_exp, src_base, next_exp, n_used, h, w_gate, w_up, w_down, norm_g.reshape(1, d))


def _combine_kernel(pos_ref, y_hbm, h_ref, rt_ref, g_ref, o_ref, buf, sem, *, final_norm):
    tb = h_ref.shape[0]
    i = pl.program_id(0)
    slot = i % 2

    def fetch(blk, sl):
        for s in range(TOP_K):
            base = (s * pl.num_programs(0) + blk) * tb
            _gather_rows(y_hbm, lambda r: pos_ref[base + r], tb, buf.at[sl, s], sem.at[sl])

    @pl.when(i == 0)
    def _():
        fetch(0, 0)

    @pl.when(i + 1 < pl.num_programs(0))
    def _():
        fetch(i + 1, 1 - slot)

    for s in range(TOP_K):
        _gather_wait(y_hbm, buf.at[slot, s], sem.at[slot])
    rt = rt_ref[...]
    out = h_ref[...] + (rt[:, 0:1] * buf[slot, 0] + rt[:, 1:2] * buf[slot, 1])
    if final_norm:
        ms = jnp.mean(out * out, axis=-1, keepdims=True)
        out = out * lax.rsqrt(ms + NORM_EPS) * g_ref[...]
    o_ref[...] = out


def _combine(h, y_rows, pos, route, g, final_norm, tb=512):
    t, d = h.shape
    grid_spec = pltpu.PrefetchScalarGridSpec(
        num_scalar_prefetch=1,
        grid=(t // tb,),
        in_specs=[pl.BlockSpec(memory_space=pl.ANY),
                  pl.BlockSpec((tb, d), lambda i, *_: (i, 0)),
                  pl.BlockSpec((tb, 128), lambda i, *_: (i, 0)),
                  pl.BlockSpec((1, d), lambda i, *_: (0, 0))],
        out_specs=pl.BlockSpec((tb, d), lambda i, *_: (i, 0)),
        scratch_shapes=[pltpu.VMEM((2, TOP_K, tb, d), F32), pltpu.SemaphoreType.DMA((2,))],
    )
    return pl.pallas_call(
        functools.partial(_combine_kernel, final_norm=final_norm),
        grid_spec=grid_spec,
        out_shape=jax.ShapeDtypeStruct((t, d), F32),
        compiler_params=_cparams(("arbitrary",)),
        name="moe_combine",
    )(pos, y_rows, h, route, g.reshape(1, d))


def _plan_kernel(counts_ref, rank_ref, eid_ref, bexp_ref, base_ref, nxt_ref, nused_ref, pos_ref, shift_ref):
    n_blocks = bexp_ref.shape[0]

    def per_expert(e, carry):
        start, pad_start = carry
        nb = (counts_ref[e] + MOE_BLOCK - 1) // MOE_BLOCK
        shift_ref[e] = pad_start - start

        def fill(j, c):
            blk = pad_start // MOE_BLOCK + j
            bexp_ref[blk] = e
            base_ref[blk] = start + j * MOE_BLOCK
            return c

        lax.fori_loop(0, nb, fill, 0)
        return start + counts_ref[e], pad_start + nb * MOE_BLOCK

    _, pad_end = lax.fori_loop(0, N_EXPERTS, per_expert, (jnp.int32(0), jnp.int32(0)))
    n_used = pad_end // MOE_BLOCK
    nused_ref[0] = n_used

    def tail(blk, c):
        bexp_ref[blk] = N_EXPERTS - 1
        base_ref[blk] = 0
        nxt_ref[blk] = -1
        return c

    lax.fori_loop(n_used, n_blocks, tail, 0)

    def backward(k, carry):
        nxt, cur = carry
        blk = n_used - 1 - k
        e = bexp_ref[blk]
        nxt = jnp.where(e != cur, cur, nxt)
        nxt_ref[blk] = nxt
        return nxt, e

    lax.fori_loop(0, n_used, backward, (jnp.int32(-1), jnp.int32(-1)))

    eid = eid_ref[...]
    pos = rank_ref[...]
    for e in range(N_EXPERTS):
        pos = pos + jnp.where(eid == e, shift_ref[e], 0)
    pos_ref[...] = pos


def _plan(counts, rank, eid, n_blocks):
    smem = pl.BlockSpec(memory_space=pltpu.SMEM)
    vmem = pl.BlockSpec(memory_space=pltpu.VMEM)
    i32 = lambda n: jax.ShapeDtypeStruct((n,), jnp.int32)
    return pl.pallas_call(
        _plan_kernel,
        in_specs=[smem, vmem, vmem],
        out_specs=[smem, smem, smem, smem, vmem],
        out_shape=[i32(n_blocks), i32(n_blocks), i32(n_blocks), i32(1), jax.ShapeDtypeStruct(rank.shape, jnp.int32)],
        scratch_shapes=[pltpu.SMEM((N_EXPERTS,), jnp.int32)],
        name="moe_plan",
    )(counts, rank, eid)


def _moe_layer(h, norm_g, wg_r, bg_r, we_r, be_r, w_gate, w_up, w_down, layer, out_g, final_norm):
    t, d = h.shape
    w_r = jnp.concatenate([wg_r, we_r, jnp.zeros((d, 128 - N_GROUPS - N_EXPERTS), F32)], axis=1)
    b_r = jnp.concatenate([bg_r, be_r, jnp.zeros((128 - N_GROUPS - N_EXPERTS,), F32)]).reshape(1, 128)
    route = _ffn_norm_router(h, norm_g, w_r, b_r)
    expert_id = route[:, TOP_K:2 * TOP_K].astype(jnp.int32)
    flat = expert_id.reshape(-1)
    order = jnp.argsort(flat).astype(jnp.int32)
    st = order // TOP_K
    rank_sm = jnp.argsort(order).astype(jnp.int32).reshape(t, TOP_K).T.reshape(-1, 128)
    eid_sm = expert_id.T.reshape(-1, 128)
    counts = jnp.sum((flat[:, None] == jnp.arange(N_EXPERTS, dtype=jnp.int32)[None, :]).astype(jnp.int32), axis=0)
    plan = _plan(counts, rank_sm, eid_sm, (t * TOP_K + MOE_BLOCK - 1) // MOE_BLOCK + N_EXPERTS)
    y_rows = _experts(h, norm_g, st, plan, w_gate, w_up, w_down, layer)
    return _combine(h, y_rows, plan[4].reshape(-1), route, out_g, final_norm)


_D0 = A_COLS + B_COLS + C_COLS
_W_IN_PIECES = (
    (OFF_BQ, A_COLS, B_COLS), (OFF_CX, A_COLS + B_COLS, C_COLS),
    (OFF_DV, _D0 + 512, 512), (OFF_DGT, _D0 + 1040, 512),
    (OFF_AR, 0, 3 * MIX_W), (OFF_DQ, _D0, 512),
    (OFF_AL, 3 * MIX_W, RWKV_LORA), (OFF_DAL, _D0 + 1024, GLA_LORA),
    (OFF_G, _D0 + D_COLS, 4 * D_MODEL),
)


RELAYOUT_ROWS = 512
_SPECIAL_BLOCK = OFF_AL // RELAYOUT_ROWS


def _w_in_source_rows():
    rows = np.zeros((NP_COLS // RELAYOUT_ROWS,), np.int32)
    for dst, src, width in _W_IN_PIECES:
        if width % RELAYOUT_ROWS == 0:
            for k in range(width // RELAYOUT_ROWS):
                rows[dst // RELAYOUT_ROWS + k] = src + k * RELAYOUT_ROWS
    return rows


def _relayout_kernel(src_ref, w_hbm, extra_hbm, o_ref, stage, sem, *, layer):
    j = pl.program_id(0)
    slot = j % 2

    def start(blk, sl):
        @pl.when(blk == _SPECIAL_BLOCK)
        def _():
            pltpu.make_async_copy(extra_hbm, stage.at[sl], sem.at[sl]).start()

        @pl.when(blk != _SPECIAL_BLOCK)
        def _():
            rows = pl.ds(pl.multiple_of(src_ref[blk], 16), RELAYOUT_ROWS)
            pltpu.make_async_copy(w_hbm.at[layer, rows, :], stage.at[sl], sem.at[sl]).start()

    @pl.when(j == 0)
    def _():
        start(0, 0)

    @pl.when(j + 1 < pl.num_programs(0))
    def _():
        start(j + 1, 1 - slot)

    pltpu.make_async_copy(extra_hbm, stage.at[slot], sem.at[slot]).wait()
    o_ref[...] = stage[slot].astype(o_ref.dtype)


def _permute_w_in(w_in, layer):
    w_t = jnp.swapaxes(w_in, 1, 2)
    d = w_t.shape[2]
    zeros = lambda n: jnp.zeros((n, d), F32)
    extra = jnp.concatenate([w_t[layer, 3 * MIX_W:A_COLS], zeros(OFF_DAL - OFF_AL - RWKV_LORA),
                             w_t[layer, _D0 + 1024:_D0 + 1024 + GLA_LORA],
                             zeros(OFF_G - OFF_DAL - GLA_LORA)], axis=0)
    hbm = pl.BlockSpec(memory_space=pl.ANY)
    return pl.pallas_call(
        functools.partial(_relayout_kernel, layer=layer),
        grid_spec=pltpu.PrefetchScalarGridSpec(
            num_scalar_prefetch=1,
            grid=(NP_COLS // RELAYOUT_ROWS,),
            in_specs=[hbm, hbm],
            out_specs=pl.BlockSpec((RELAYOUT_ROWS, d), lambda j, *_: (j, 0)),
            scratch_shapes=[pltpu.VMEM((2, RELAYOUT_ROWS, d), F32), pltpu.SemaphoreType.DMA((2,))],
        ),
        out_shape=jax.ShapeDtypeStruct((NP_COLS, d), BF16),
        compiler_params=_cparams(("arbitrary",)),
        name="w_in_relayout",
    )(jnp.asarray(_w_in_source_rows()), w_t, extra)


def kernel(x, positions, norm_mix_g, w_in, rwkv_mu, rwkv_w0, rwkv_w_up, rwkv_a0, rwkv_a_up, rwkv_g_up, rwkv_k_k, rwkv_k_a, rwkv_r_k, rwkv_ln_g, rwkv_ln_b, rwkv_w_o, ret_norm_g, ret_w_o, lru_conv_w, lru_conv_b, lru_w_a, lru_b_a, lru_w_x, lru_b_x, lru_lambda, lru_w_o, gla_alpha_up, gla_alpha_b, gla_norm_g, gla_w_o, w_out, norm_ffn_g, router_group_w, router_group_b, router_expert_w, router_expert_b, moe_w_gate, moe_w_up, moe_w_down, final_norm_g):
    b_, s_, d = x.shape
    assert b_ == 1 and d == D_MODEL
    depth = w_in.shape[0]
    h = x.reshape(s_, d)
    pos_b = jnp.broadcast_to(positions.reshape(s_, 1), (s_, 128)).astype(jnp.int32)
    cos_t, sin_t = _rope_table(pos_b)
    for l in range(depth):
        xn = _rmsnorm(h, norm_mix_g[l], BF16)
        w_t = _permute_w_in(w_in, l)
        p = _matmul_nt(xn, w_t, 0, OFF_G, 2048, 512, F32, "in_proj")
        y_a = _rwkv_branch(p, rwkv_mu[l], rwkv_w0[l], rwkv_w_up[l], rwkv_a0[l], rwkv_a_up[l], rwkv_g_up[l],
                           rwkv_k_k[l], rwkv_k_a[l], rwkv_r_k[l], rwkv_ln_g[l], rwkv_ln_b[l])
        y_b = _retention(p, cos_t, sin_t, ret_norm_g[l])
        y_c = _rglru(p, pos_b, lru_conv_w[l], lru_conv_b[l], lru_w_a[l], lru_b_a[l], lru_w_x[l], lru_b_x[l],
                     lru_lambda[l])
        y_d = _gla(p, gla_alpha_up[l], gla_alpha_b[l], gla_norm_g[l])
        merged = _merge(xn, w_t, (y_a, y_b, y_c, y_d), (rwkv_w_o, ret_w_o, lru_w_o, gla_w_o), l)
        h = _matmul_residual(merged, w_out, l, h, 2048, 512)
        last = l == depth - 1
        h = _moe_layer(h, norm_ffn_g[l], router_group_w[l], router_group_b[l], router_expert_w[l],
                       router_expert_b[l], moe_w_gate, moe_w_up, moe_w_down, l,
                       final_norm_g if last else norm_ffn_g[l], last)
    return h.reshape(b_, s_, d)
```

```python
import functools
import math

import jax
import jax.numpy as jnp
import numpy as np
from jax import lax
from jax.experimental import pallas as pl
from jax.experimental.pallas import tpu as pltpu

F32 = jnp.float32
BF16 = jnp.bfloat16
HI = lax.Precision.HIGHEST

D_MODEL = 2048
MIX_W = 512
NORM_EPS = 1e-6
HEAD_DIM = 64
N_HEADS = MIX_W // HEAD_DIM

RWKV_W_LORA, RWKV_A_LORA, RWKV_G_LORA = 32, 32, 96
RWKV_LORA = RWKV_W_LORA + RWKV_A_LORA + RWKV_G_LORA
RWKV_DECAY_SCALE = 0.6065306597126334
RWKV_LN_EPS = 64e-5
RWKV_CHUNK = 64
RWKV_LOCAL_CHUNKS = 4

RET_CHUNK = 128
RET_STEP_CHUNKS = 4
ROPE_BASE = 10000.0

LRU_BLOCKS = 8
CONV_W = 4
LRU_C = 8.0

GLA_HEADS = 4
GLA_DK = 64
GLA_DV = 128
GLA_LORA = 16
GLA_GATE_NORM = 16.0
GLA_CHUNK = 64
GLA_STEP_CHUNKS = 4

N_GROUPS = 4
EXP_PER_GROUP = 8
N_EXPERTS = N_GROUPS * EXP_PER_GROUP
TOP_K = 2
D_EXPERT = 1024
MOE_BLOCK = 128
WEIGHT_CHUNKS = 4

A_COLS = 3 * MIX_W + RWKV_LORA
B_COLS = 4 * MIX_W
C_COLS = 2 * MIX_W
D_COLS = 2 * (MIX_W // 2) + MIX_W + GLA_LORA + MIX_W

OFF_BQ, OFF_BK, OFF_BV, OFF_BG = 0, 512, 1024, 1536
OFF_CX, OFF_CG = 2048, 2560
OFF_DV, OFF_DGT = 3072, 3584
OFF_AR, OFF_AK, OFF_AV = 4096, 4608, 5120
OFF_DQ, OFF_DK = 5632, 5888
OFF_AL = 6144
OFF_DAL = 6400
OFF_G = 6656
NP_COLS = OFF_G + 4 * D_MODEL

VMEM_LIMIT = 48 * 1024 * 1024
VMEM_LIMIT_EXPERT = 56 * 1024 * 1024


def _cparams(sem, vmem=VMEM_LIMIT):
    return pltpu.CompilerParams(dimension_semantics=sem, vmem_limit_bytes=vmem)


def _dot(a, b, prec=None):
    return jnp.dot(a, b, precision=prec, preferred_element_type=F32)


def _sigmoid(x):
    return 1.0 / (1.0 + jnp.exp(-x))


def _split(x):
    hi = x.astype(BF16)
    return hi, (x - hi.astype(F32)).astype(BF16)


def _dims(form, batched):
    ca, cb = {"nn": (1, 0), "nt": (1, 1), "tn": (0, 0)}[form]
    if batched:
        return (((ca + 1,), (cb + 1,)), ((0,), (0,)))
    return (((ca,), (cb,)), ((), ()))


def _mm(a, b, mode, form="nn"):
    dims = _dims(form, a.ndim == 3)
    if mode == "hi":
        return lax.dot_general(a, b, dims, precision=HI, preferred_element_type=F32)
    d = functools.partial(lax.dot_general, dimension_numbers=dims, preferred_element_type=F32)
    if mode == "x1":
        return d(a.astype(BF16), b.astype(BF16))
    ah, al = _split(a)
    bh, bl = _split(b)
    return d(ah, bh) + (d(ah, bl) + d(al, bh))


def _mm_exact_rhs(a, b, terms):
    b = b.astype(BF16)
    acc = None
    for _ in range(terms):
        piece = a.astype(BF16)
        part = _dot(piece, b)
        acc = part if acc is None else acc + part
        a = a - piece.astype(F32)
    return acc


def _mm_exact_lhs(a, b, terms):
    a = a.astype(BF16)
    acc = None
    for _ in range(terms):
        piece = b.astype(BF16)
        part = _dot(a, piece)
        acc = part if acc is None else acc + part
        b = b - piece.astype(F32)
    return acc


def _heads(x, width=HEAD_DIM):
    return jnp.stack([x[:, h * width:(h + 1) * width] for h in range(x.shape[1] // width)])


def _block_diag_const(n, blk, value):
    r = lax.broadcasted_iota(jnp.int32, (n, n), 0) // blk
    c = lax.broadcasted_iota(jnp.int32, (n, n), 1) // blk
    return jnp.where(r == c, value, 0.0).astype(F32)


def _rmsnorm_kernel(x_ref, g_ref, o_ref):
    x = x_ref[...]
    ms = jnp.mean(x * x, axis=-1, keepdims=True)
    o_ref[...] = (x * lax.rsqrt(ms + NORM_EPS) * g_ref[...]).astype(o_ref.dtype)


def _rmsnorm(x, g, out_dtype, tb=1024):
    t, d = x.shape
    return pl.pallas_call(
        _rmsnorm_kernel,
        grid=(t // tb,),
        in_specs=[pl.BlockSpec((tb, d), lambda i: (i, 0)), pl.BlockSpec((1, d), lambda i: (0, 0))],
        out_specs=pl.BlockSpec((tb, d), lambda i: (i, 0)),
        out_shape=jax.ShapeDtypeStruct((t, d), out_dtype),
        compiler_params=_cparams(("parallel",)),
        name="rmsnorm",
    )(x, g.reshape(1, d))


def _mm_nt_kernel(a_ref, b_ref, o_ref):
    o_ref[...] = _mm(a_ref[...], b_ref[...], "x1", "nt").astype(o_ref.dtype)


def _matmul_nt(a, b_t, row0, n, tm, tn, out_dtype, name):
    m, k = a.shape
    assert row0 % tn == 0 and n % tn == 0 and m % tm == 0
    return pl.pallas_call(
        _mm_nt_kernel,
        grid=(m // tm, n // tn),
        in_specs=[pl.BlockSpec((tm, k), lambda i, j: (i, 0)), pl.BlockSpec((tn, k), lambda i, j: (row0 // tn + j, 0))],
        out_specs=pl.BlockSpec((tm, tn), lambda i, j: (i, j)),
        out_shape=jax.ShapeDtypeStruct((m, n), out_dtype),
        compiler_params=_cparams(("parallel", "parallel")),
        name=name,
    )(a, b_t)


def _mm_res_kernel(a_ref, b_ref, r_ref, o_ref, b_bf):
    @pl.when(pl.program_id(1) == 0)
    def _():
        b_bf[...] = b_ref[...].astype(BF16)

    o_ref[...] = r_ref[...] + _dot(a_ref[...], b_bf[...])


def _matmul_residual(a, b, layer, res, tm, tn):
    m, k = a.shape
    n = b.shape[2]
    return pl.pallas_call(
        _mm_res_kernel,
        grid=(n // tn, m // tm),
        in_specs=[pl.BlockSpec((tm, k), lambda j, i: (i, 0)), pl.BlockSpec((None, k, tn), lambda j, i: (layer, 0, j)),
                  pl.BlockSpec((tm, tn), lambda j, i: (i, j))],
        out_specs=pl.BlockSpec((tm, tn), lambda j, i: (i, j)),
        out_shape=jax.ShapeDtypeStruct((m, n), F32),
        scratch_shapes=[pltpu.VMEM((k, tn), BF16)],
        compiler_params=_cparams(("parallel", "arbitrary"), VMEM_LIMIT_EXPERT),
        name="wout_residual",
    )(a, b, res)


def _shifted(x, tail_ref, sc_ref, width, first):
    tb = x.shape[0]
    sc_ref[0:8, 0:width] = jnp.where(first, 0.0, tail_ref[...])
    sc_ref[8:8 + tb, 0:width] = x
    return sc_ref[7:7 + tb, 0:width]


def _rwkv_prep_values(r_ref, k_ref, v_ref, l_ref, rt_ref, kt_ref, vt_ref, lt_ref,
                      mur_ref, muk_ref, muv_ref, mul_ref, wl_ref, b0_ref, kk_ref, ka_ref, sc_ref):
    first = pl.program_id(0) == 0

    def mix(x_ref, t_ref, mu_ref, width):
        x = x_ref[...]
        prev = _shifted(x, t_ref, sc_ref, width, first)
        return x + (prev - x) * mu_ref[...]

    r = mix(r_ref, rt_ref, mur_ref, MIX_W)
    k = mix(k_ref, kt_ref, muk_ref, MIX_W)
    v = mix(v_ref, vt_ref, muv_ref, MIX_W)
    zl = mix(l_ref, lt_ref, mul_ref, 256)
    lane = lax.broadcasted_iota(jnp.int32, zl.shape, 1)
    act = jnp.where(lane < RWKV_W_LORA, jnp.tanh(zl),
                    jnp.where(lane < RWKV_W_LORA + RWKV_A_LORA, zl, _sigmoid(zl)))
    lo = _mm(act, wl_ref[...], "x3") + b0_ref[...]
    lw = -RWKV_DECAY_SCALE * _sigmoid(lo[:, 0:MIX_W])
    a = _sigmoid(lo[:, MIX_W:2 * MIX_W])
    g = lo[:, 2 * MIX_W:3 * MIX_W]
    kk = k * kk_ref[...]
    ss = _mm_exact_rhs(kk * kk, _block_diag_const(MIX_W, HEAD_DIM, 1.0), 2)
    kkn = kk / jnp.maximum(jnp.sqrt(ss), 1e-12)
    return r, lw, k * (1.0 + (a - 1.0) * ka_ref[...]), v, kkn, kkn * a, g


RW_SC, RW_INV, RW_APPLY, RW_STATE, RW_SCAN = "x1", "x1", "x1", "x1", "x1"


def _rwkv_local_chunk(r, lw, k, v, kk, kka):
    c = r.shape[0]
    row = lax.broadcasted_iota(jnp.int32, (c, c), 0)
    col = lax.broadcasted_iota(jnp.int32, (c, c), 1)
    incl = row >= col
    strict = row > col
    eye = jnp.where(row == col, 1.0, 0.0)
    cum = _mm_exact_lhs(incl.astype(F32), lw, 3)
    last = cum[c - 1:c, :]
    pinv = jnp.exp(-cum)
    dl = jnp.exp(last - cum)
    at = _heads(-kk * jnp.exp(cum - lw))
    rt = _heads(r * jnp.exp(cum))
    bt = _heads(kka * pinv)
    kt = _heads(k * pinv)
    bl = _heads(kka * dl)
    kl = _heads(k * dl)
    v = _heads(v)
    pc = _heads(jnp.exp(last))
    sc = _mm(jnp.concatenate([at, rt], axis=1), jnp.concatenate([bt, kt], axis=1), RW_SC, "nt")
    a_ab = jnp.where(strict, sc[:, 0:c, 0:c], 0.0)
    a_ak = jnp.where(strict, sc[:, 0:c, c:2 * c], 0.0)
    r_b = jnp.where(incl, sc[:, c:2 * c, 0:c], 0.0)
    r_k = jnp.where(incl, sc[:, c:2 * c, c:2 * c], 0.0)
    x = eye + a_ab
    pw = _mm(a_ab, a_ab, RW_INV)
    levels = int(math.log2(c)) - 1
    for lvl in range(1, levels):
        prod = _mm(jnp.concatenate([x, pw], axis=1), pw, RW_INV)
        x = x + prod[:, 0:c]
        pw = prod[:, c:2 * c]
    x = x + _mm(x, pw, RW_INV)
    wu = _mm(x, jnp.concatenate([at, _mm(a_ak, v, RW_APPLY)], axis=2), RW_APPLY)
    lower = jnp.concatenate([jnp.zeros_like(v), v], axis=2)
    qy = _mm(jnp.concatenate([r_b, r_k], axis=2), jnp.concatenate([wu, lower], axis=1), RW_APPLY)
    wb = _mm(wu, bl, RW_STATE, "tn")
    n = wb[:, HEAD_DIM:] + _mm(v, kl, RW_STATE, "tn")
    m = eye * pc + wb[:, 0:HEAD_DIM]
    return rt + qy[:, :, 0:HEAD_DIM], qy[:, :, HEAD_DIM:], m, n


def _rwkv_local_kernel(*refs):
    prep_refs, (q_ref, y0_ref, m_ref, n_ref, ro_ref, ko_ref, vo_ref, g_ref, sc_ref) = refs[:16], refs[16:]
    r, lw, k, v, kkn, kka, g = _rwkv_prep_values(*prep_refs, sc_ref)
    ro_ref[...] = r
    ko_ref[...] = k
    vo_ref[...] = v
    g_ref[...] = g
    c = RWKV_CHUNK
    outs = [_rwkv_local_chunk(*(x[j * c:(j + 1) * c, :] for x in (r, lw, k, v, kkn, kka)))
            for j in range(r.shape[0] // c)]
    for j, res in enumerate(outs):
        for o_ref, val in zip((q_ref, y0_ref, m_ref, n_ref), res):
            for h in range(N_HEADS):
                o_ref[j * c:(j + 1) * c, h * HEAD_DIM:(h + 1) * HEAD_DIM] = val[h]


def _rwkv_scan_kernel(q_ref, y0_ref, m_ref, n_ref, r_ref, k_ref, v_ref, g_ref, lng_ref, lnb_ref, rk_ref,
                      o_ref, s_ref, y_sc):
    @pl.when(pl.program_id(0) == 0)
    def _():
        s_ref[...] = jnp.zeros_like(s_ref)

    c = RWKV_CHUNK
    s = s_ref[...]
    for j in range(q_ref.shape[0] // c):
        rows = slice(j * c, (j + 1) * c)
        y = _mm(_heads(q_ref[rows, :]), s, RW_SCAN, "nt") + _heads(y0_ref[rows, :])
        s = _mm(s, _heads(m_ref[rows, :]), RW_SCAN) + _heads(n_ref[rows, :])
        for h in range(N_HEADS):
            y_sc[rows, h * HEAD_DIM:(h + 1) * HEAD_DIM] = y[h]
    s_ref[...] = s
    y = y_sc[...]
    avg = _block_diag_const(MIX_W, HEAD_DIM, 1.0 / HEAD_DIM)
    mean = _mm_exact_rhs(y, avg, 2)
    yc = y - mean
    var = _mm_exact_rhs(yc * yc, avg, 2)
    yn = yc * lax.rsqrt(var + RWKV_LN_EPS) * lng_ref[...] + lnb_ref[...]
    v = v_ref[...]
    bonus = _mm_exact_rhs(r_ref[...] * k_ref[...] * rk_ref[...], _block_diag_const(MIX_W, HEAD_DIM, 1.0), 2) * v
    o_ref[...] = ((yn + bonus) * g_ref[...]).astype(o_ref.dtype)


def _rwkv_branch(p, mu, w0, w_up, a0, a_up, g_up, k_k, k_a, r_k, ln_g, ln_b, tb_scan=512):
    t = p.shape[0]
    tb = RWKV_LOCAL_CHUNKS * RWKV_CHUNK
    nb8 = tb // 8

    def blk(width, off):
        return pl.BlockSpec((tb, width), lambda i: (i, off // width))

    def tail(width, off):
        return pl.BlockSpec((8, width), lambda i: (jnp.maximum(i * nb8 - 1, 0), off // width))

    def vec(width):
        return pl.BlockSpec((1, width), lambda i: (0, 0))

    w_lora = jnp.zeros((256, 3 * MIX_W), F32)
    w_lora = w_lora.at[0:32, 0:MIX_W].set(w_up)
    w_lora = w_lora.at[32:64, MIX_W:2 * MIX_W].set(a_up)
    w_lora = w_lora.at[64:160, 2 * MIX_W:].set(g_up)
    b0 = jnp.concatenate([w0, a0, jnp.zeros((MIX_W,), F32)]).reshape(1, 3 * MIX_W)
    mu_l = jnp.concatenate([mu[3 * MIX_W:], jnp.zeros((256 - RWKV_LORA,), F32)]).reshape(1, 256)
    f = jax.ShapeDtypeStruct((t, MIX_W), F32)
    oblk = pl.BlockSpec((tb, MIX_W), lambda i: (i, 0))
    qm, y0, m, n, r, k, v, g = pl.pallas_call(
        _rwkv_local_kernel,
        grid=(t // tb,),
        in_specs=[blk(MIX_W, OFF_AR), blk(MIX_W, OFF_AK), blk(MIX_W, OFF_AV), blk(256, OFF_AL),
                  tail(MIX_W, OFF_AR), tail(MIX_W, OFF_AK), tail(MIX_W, OFF_AV), tail(256, OFF_AL),
                  vec(MIX_W), vec(MIX_W), vec(MIX_W), vec(256),
                  pl.BlockSpec((256, 3 * MIX_W), lambda i: (0, 0)), vec(3 * MIX_W), vec(MIX_W), vec(MIX_W)],
        out_specs=[oblk] * 8,
        out_shape=[f] * 8,
        scratch_shapes=[pltpu.VMEM((tb + 8, MIX_W), F32)],
        compiler_params=_cparams(("parallel",)),
        name="rwkv_local",
    )(p, p, p, p, p, p, p, p,
      mu[0:MIX_W].reshape(1, -1), mu[MIX_W:2 * MIX_W].reshape(1, -1), mu[2 * MIX_W:3 * MIX_W].reshape(1, -1), mu_l,
      w_lora, b0, k_k.reshape(1, -1), k_a.reshape(1, -1))
    sblk = pl.BlockSpec((tb_scan, MIX_W), lambda i: (i, 0))
    svec = pl.BlockSpec((1, MIX_W), lambda i: (0, 0))
    return pl.pallas_call(
        _rwkv_scan_kernel,
        grid=(t // tb_scan,),
        in_specs=[sblk] * 8 + [svec] * 3,
        out_specs=sblk,
        out_shape=jax.ShapeDtypeStruct((t, MIX_W), BF16),
        scratch_shapes=[pltpu.VMEM((N_HEADS, HEAD_DIM, HEAD_DIM), F32), pltpu.VMEM((tb_scan, MIX_W), F32)],
        compiler_params=_cparams(("arbitrary",)),
        name="rwkv_scan",
    )(qm, y0, m, n, r, k, v, g, ln_g.reshape(1, -1), ln_b.reshape(1, -1), r_k.reshape(1, -1))


def _rope_kernel(pos_ref, inv_ref, cos_ref, sin_ref):
    ang = pos_ref[...].astype(F32) * inv_ref[...]
    lane = lax.broadcasted_iota(jnp.int32, ang.shape, 1)
    cos_ref[...] = jnp.cos(ang)
    sin_ref[...] = jnp.where(lane % HEAD_DIM < HEAD_DIM // 2, -jnp.sin(ang), jnp.sin(ang))


def _rope_table(pos_b, tb=512):
    t = pos_b.shape[0]
    d = HEAD_DIM
    inv = 1.0 / (ROPE_BASE ** (jnp.arange(0, d, 2, dtype=F32) / d))
    inv_b = jnp.tile(inv, 4).reshape(1, 128)
    blk = pl.BlockSpec((tb, 128), lambda i: (i, 0))
    return pl.pallas_call(
        _rope_kernel,
        grid=(t // tb,),
        in_specs=[blk, pl.BlockSpec((1, 128), lambda i: (0, 0))],
        out_specs=[blk, blk],
        out_shape=[jax.ShapeDtypeStruct((t, 128), F32)] * 2,
        compiler_params=_cparams(("parallel",)),
        name="rope_table",
    )(pos_b, inv_b)


def _ret_kernel(q_ref, k_ref, v_ref, g_ref, cos_ref, sin_ref, dec_ref, zeta_ref, xi_ref, cd_ref, ng_ref,
                o_ref, s_ref, o_sc):
    @pl.when(pl.program_id(0) == 0)
    def _():
        s_ref[...] = jnp.zeros_like(s_ref)

    cos = jnp.concatenate([cos_ref[...]] * 4, axis=1)
    sin = jnp.concatenate([sin_ref[...]] * 4, axis=1)
    lane = lax.broadcasted_iota(jnp.int32, cos.shape, 1)
    lower_half = lane % HEAD_DIM < HEAD_DIM // 2

    def rope(x):
        swapped = jnp.where(lower_half, pltpu.roll(x, MIX_W - HEAD_DIM // 2, 1), pltpu.roll(x, HEAD_DIM // 2, 1))
        return x * cos + swapped * sin

    c = RET_CHUNK
    qr = rope(q_ref[...])
    kr = rope(k_ref[...]) * (HEAD_DIM ** -0.5)
    s = s_ref[...]
    for j in range(q_ref.shape[0] // c):
        rows = slice(j * c, (j + 1) * c)
        q = _heads(qr[rows])
        k = _heads(kr[rows])
        v = _heads(v_ref[rows, :])
        scores = _mm(q, k, "x1", "nt") * dec_ref[...]
        o = _mm(scores, v, "x1") + _mm(q * xi_ref[...], s, "x1")
        s = s * cd_ref[...] + _mm(k * zeta_ref[...], v, "x1", "tn")
        on = o * lax.rsqrt(jnp.mean(o * o, axis=-1, keepdims=True) + NORM_EPS)
        for h in range(N_HEADS):
            o_sc[rows, h * HEAD_DIM:(h + 1) * HEAD_DIM] = on[h]
    s_ref[...] = s
    gt = g_ref[...]
    o_ref[...] = (o_sc[...] * ng_ref[...] * (gt * _sigmoid(gt))).astype(o_ref.dtype)


def _retention(p, cos_t, sin_t, norm_g):
    t = p.shape[0]
    c = RET_CHUNK
    h = N_HEADS
    log_g = jnp.log(1.0 - jnp.exp(jnp.linspace(math.log(1.0 / 32), math.log(1.0 / 512), h, dtype=F32)))
    idx = jnp.arange(c, dtype=F32)
    diff = idx[:, None] - idx[None, :]
    decay = jnp.where(diff >= 0, jnp.exp(log_g[:, None, None] * jnp.maximum(diff, 0.0)[None]), 0.0)
    zeta = jnp.exp(log_g[:, None] * (c - 1 - idx)[None])
    xi = jnp.exp(log_g[:, None] * (idx + 1)[None])
    cdec = jnp.exp(log_g * c)
    zeta_b = jnp.broadcast_to(zeta[:, :, None], (h, c, HEAD_DIM))
    xi_b = jnp.broadcast_to(xi[:, :, None], (h, c, HEAD_DIM))
    cd_b = jnp.broadcast_to(cdec[:, None, None], (h, HEAD_DIM, HEAD_DIM))

    tb = RET_STEP_CHUNKS * c

    def blk(off):
        return pl.BlockSpec((tb, MIX_W), lambda i: (i, off // MIX_W))

    tab = pl.BlockSpec((tb, 128), lambda i: (i, 0))

    def const(shape):
        return pl.BlockSpec(shape, lambda i: (0,) * len(shape))

    return pl.pallas_call(
        _ret_kernel,
        grid=(t // tb,),
        in_specs=[blk(OFF_BQ), blk(OFF_BK), blk(OFF_BV), blk(OFF_BG), tab, tab,
                  const((h, c, c)), const((h, c, HEAD_DIM)), const((h, c, HEAD_DIM)),
                  const((h, HEAD_DIM, HEAD_DIM)), const((1, MIX_W))],
        out_specs=pl.BlockSpec((tb, MIX_W), lambda i: (i, 0)),
        out_shape=jax.ShapeDtypeStruct((t, MIX_W), BF16),
        scratch_shapes=[pltpu.VMEM((h, HEAD_DIM, HEAD_DIM), F32), pltpu.VMEM((tb, MIX_W), F32)],
        compiler_params=_cparams(("arbitrary",)),
        name="retention",
    )(p, p, p, p, cos_t, sin_t, decay, zeta_b, xi_b, cd_b, norm_g.reshape(1, -1))


def _lru_kernel(x_ref, xt_ref, gb_ref, pos_ref, cw_ref, cb_ref, wa_ref, ba_ref, wx_ref, bx_ref, lam_ref,
                o_ref, sc_ref, a_sc, b_sc, h_ref):
    tb = x_ref.shape[0]
    first = pl.program_id(0) == 0

    @pl.when(first)
    def _():
        h_ref[...] = jnp.zeros_like(h_ref)

    x = x_ref[...]
    sc_ref[0:8, :] = jnp.where(first, 0.0, xt_ref[...])
    sc_ref[8:8 + tb, :] = x
    cw = cw_ref[...]
    xc = cb_ref[...] + x * cw[CONV_W - 1:CONV_W, :]
    for j in range(1, CONV_W):
        xc = xc + sc_ref[8 - j:8 - j + tb, :] * cw[CONV_W - 1 - j:CONV_W - j, :]
    r = _sigmoid(_mm(xc, wa_ref[...], "x3") + ba_ref[...])
    ig = _sigmoid(_mm(xc, wx_ref[...], "x3") + bx_ref[...])
    nl = -lam_ref[...]
    softplus = jnp.maximum(nl, 0.0) + jnp.log1p(jnp.exp(-jnp.abs(nl)))
    log_a = -LRU_C * r * softplus
    pos = pos_ref[...]
    reset = jnp.concatenate([pos, pos, pos, pos], axis=1) == 0
    a_sc[...] = jnp.where(reset, 0.0, jnp.exp(log_a))
    th = jnp.tanh(log_a)
    b_sc[...] = jnp.where(reset, 1.0, jnp.sqrt(-2.0 * th / (1.0 - th))) * ig * xc

    row = lax.broadcasted_iota(jnp.int32, (8, MIX_W), 0)

    def group(gi, h):
        off = pl.multiple_of(gi * 8, 8)
        a = a_sc[pl.ds(off, 8), :]
        b = b_sc[pl.ds(off, 8), :]
        for d in (1, 2, 4):
            keep = row >= d
            b = jnp.where(keep, a * pltpu.roll(b, d, 0) + b, b)
            a = jnp.where(keep, a * pltpu.roll(a, d, 0), a)
        hs = a * h + b
        b_sc[pl.ds(off, 8), :] = hs
        return hs[7:8, :]

    h_ref[...] = lax.fori_loop(0, tb // 8, group, h_ref[...])
    gb = gb_ref[...]
    gelu = gb * (0.5 * (1.0 + jnp.tanh(math.sqrt(2.0 / math.pi) * (gb + 0.044715 * (gb * gb * gb)))))
    o_ref[...] = (b_sc[...] * gelu).astype(o_ref.dtype)


def _block_diag_weight(w):
    g, n, _ = w.shape
    eye = jnp.eye(g, dtype=w.dtype)
    return (eye[:, None, :, None] * w[:, :, None, :]).reshape(g * n, g * n)


def _rglru(p, pos_b, conv_w, conv_b, w_a, b_a, w_x, b_x, lam, tb=512):
    t = p.shape[0]
    nb8 = tb // 8
    vec = pl.BlockSpec((1, MIX_W), lambda i: (0, 0))
    mat = pl.BlockSpec((MIX_W, MIX_W), lambda i: (0, 0))
    cw8 = jnp.concatenate([conv_w, jnp.zeros((8 - CONV_W, MIX_W), F32)], axis=0)
    return pl.pallas_call(
        _lru_kernel,
        grid=(t // tb,),
        in_specs=[pl.BlockSpec((tb, MIX_W), lambda i: (i, OFF_CX // MIX_W)),
                  pl.BlockSpec((8, MIX_W), lambda i: (jnp.maximum(i * nb8 - 1, 0), OFF_CX // MIX_W)),
                  pl.BlockSpec((tb, MIX_W), lambda i: (i, OFF_CG // MIX_W)),
                  pl.BlockSpec((tb, 128), lambda i: (i, 0)),
                  pl.BlockSpec((8, MIX_W), lambda i: (0, 0)), vec, mat, vec, mat, vec, vec],
        out_specs=pl.BlockSpec((tb, MIX_W), lambda i: (i, 0)),
        out_shape=jax.ShapeDtypeStruct((t, MIX_W), BF16),
        scratch_shapes=[pltpu.VMEM((tb + 8, MIX_W), F32), pltpu.VMEM((tb, MIX_W), F32),
                        pltpu.VMEM((tb, MIX_W), F32), pltpu.VMEM((1, MIX_W), F32)],
        compiler_params=_cparams(("arbitrary",)),
        name="rglru",
    )(p, p, p, pos_b, cw8, conv_b.reshape(1, -1), _block_diag_weight(w_a), b_a.reshape(1, -1),
      _block_diag_weight(w_x), b_x.reshape(1, -1), lam.reshape(1, -1))


def _gla_kernel(v_ref, gt_ref, q_ref, k_ref, al_ref, aup_ref, ab_ref, ng_ref, o_ref, s_ref):
    @pl.when(pl.program_id(0) == 0)
    def _():
        s_ref[...] = jnp.zeros_like(s_ref)

    c = GLA_CHUNK
    row = lax.broadcasted_iota(jnp.int32, (c, c), 0)
    col = lax.broadcasted_iota(jnp.int32, (c, c), 1)
    causal = row >= col
    pre = _mm(al_ref[...], aup_ref[...], "x3") + ab_ref[...]
    log_a = (jnp.minimum(pre, 0.0) - jnp.log1p(jnp.exp(-jnp.abs(pre)))) / GLA_GATE_NORM
    s = s_ref[...]
    outs = []
    for j in range(q_ref.shape[0] // c):
        rows = slice(j * c, (j + 1) * c)
        bcum = _mm_exact_lhs(causal.astype(F32), log_a[rows], 3)
        blast = bcum[c - 1:c, :]
        k = k_ref[rows, :]
        q_e = _heads(q_ref[rows, :] * (GLA_DK ** -0.5) * jnp.exp(bcum))
        k_e = _heads(k * jnp.exp(-bcum))
        k_l = _heads(k * jnp.exp(blast - bcum))
        dec = _heads(jnp.exp(blast))
        v = _heads(v_ref[rows, :], GLA_DV)
        scores = jnp.where(causal, _mm(q_e, k_e, "x1", "nt"), 0.0)
        o = _mm(scores, v, "x1") + _mm(q_e, s, "x1", "nt")
        s = s * dec + _mm(v, k_l, "x1", "tn")
        on = o * lax.rsqrt(jnp.mean(o * o, axis=-1, keepdims=True) + NORM_EPS)
        outs.append(jnp.concatenate([on[h] for h in range(GLA_HEADS)], axis=1))
    s_ref[...] = s
    on = jnp.concatenate(outs, axis=0) * ng_ref[...]
    gt = gt_ref[...]
    o_ref[...] = (on * (gt * _sigmoid(gt))).astype(o_ref.dtype)


def _gla(p, alpha_up, alpha_b, norm_g):
    t = p.shape[0]
    c = GLA_STEP_CHUNKS * GLA_CHUNK
    hk = MIX_W // 2
    aup = jnp.concatenate([alpha_up, jnp.zeros((128 - GLA_LORA, hk), F32)], axis=0)

    def blk(width, off):
        return pl.BlockSpec((c, width), lambda i: (i, off // width))

    return pl.pallas_call(
        _gla_kernel,
        grid=(t // c,),
        in_specs=[blk(MIX_W, OFF_DV), blk(MIX_W, OFF_DGT), blk(hk, OFF_DQ), blk(hk, OFF_DK), blk(128, OFF_DAL),
                  pl.BlockSpec((128, hk), lambda i: (0, 0)), pl.BlockSpec((1, hk), lambda i: (0, 0)),
                  pl.BlockSpec((1, MIX_W), lambda i: (0, 0))],
        out_specs=pl.BlockSpec((c, MIX_W), lambda i: (i, 0)),
        out_shape=jax.ShapeDtypeStruct((t, MIX_W), BF16),
        scratch_shapes=[pltpu.VMEM((GLA_HEADS, GLA_DV, GLA_DK), F32)],
        compiler_params=_cparams(("arbitrary",)),
        name="gla",
    )(p, p, p, p, p, aup, alpha_b.reshape(1, -1), norm_g.reshape(1, -1))


def _merge_kernel(xn_ref, ga_ref, gb_ref, gc_ref, gd_ref, ya_ref, yb_ref, yc_ref, yd_ref,
                  wa_ref, wb_ref, wc_ref, wd_ref, o_ref, w_bf):
    @pl.when(pl.program_id(1) == 0)
    def _():
        for kk, w_ref in enumerate((wa_ref, wb_ref, wc_ref, wd_ref)):
            w_bf[kk] = w_ref[0].astype(BF16)

    xn = xn_ref[...]
    acc = None
    for kk, (g_ref, y_ref) in enumerate(((ga_ref, ya_ref), (gb_ref, yb_ref), (gc_ref, yc_ref), (gd_ref, yd_ref))):
        term = _sigmoid(_mm(xn, g_ref[...], "x1", "nt")) * _dot(y_ref[...], w_bf[kk])
        acc = term if acc is None else acc + term
    o_ref[...] = acc.astype(o_ref.dtype)


def _merge(xn, w_t, ys, ws, layer, tm=1024, tn=512):
    t, d = xn.shape
    xspec = pl.BlockSpec((tm, d), lambda j, i: (i, 0))
    yspec = pl.BlockSpec((tm, MIX_W), lambda j, i: (i, 0))
    wspec = pl.BlockSpec((1, MIX_W, tn), lambda j, i: (layer, 0, j))

    def gspec(kk):
        return pl.BlockSpec((tn, d), lambda j, i: ((OFF_G + kk * D_MODEL) // tn + j, 0))

    return pl.pallas_call(
        _merge_kernel,
        grid=(D_MODEL // tn, t // tm),
        in_specs=[xspec] + [gspec(kk) for kk in range(4)] + [yspec] * 4 + [wspec] * 4,
        out_specs=pl.BlockSpec((tm, tn), lambda j, i: (i, j)),
        out_shape=jax.ShapeDtypeStruct((t, D_MODEL), BF16),
        scratch_shapes=[pltpu.VMEM((4, MIX_W, tn), BF16)],
        compiler_params=_cparams(("parallel", "arbitrary"), VMEM_LIMIT_EXPERT),
        name="merge",
    )(xn, w_t, w_t, w_t, w_t, *ys, *ws)


def _ffn_norm_kernel(h_ref, g_ref, wr_ref, br_ref, rt_ref):
    x = h_ref[...]
    ms = jnp.mean(x * x, axis=-1, keepdims=True)
    hn = x * lax.rsqrt(ms + NORM_EPS) * g_ref[...]
    z = _mm(hn, wr_ref[...], "x3") + br_ref[...]
    lane = lax.broadcasted_iota(jnp.int32, z.shape, 1)
    neg = jnp.float32(-1e30)

    def first_argmax(v):
        m = jnp.max(v, axis=-1, keepdims=True)
        return m, jnp.min(jnp.where(v == m, lane, 128), axis=-1, keepdims=True)

    is_group = lane < N_GROUPS
    zg = jnp.where(is_group, z, neg)
    mg, g_idx = first_argmax(zg)
    pg_top = 1.0 / jnp.sum(jnp.where(is_group, jnp.exp(zg - mg), 0.0), axis=-1, keepdims=True)
    lo = N_GROUPS + g_idx * EXP_PER_GROUP
    in_group = jnp.logical_and(lane >= lo, lane < lo + EXP_PER_GROUP)
    ze = jnp.where(in_group, z, neg)
    m1, i1 = first_argmax(ze)
    se = jnp.sum(jnp.where(in_group, jnp.exp(ze - m1), 0.0), axis=-1, keepdims=True)
    m2, i2 = first_argmax(jnp.where(lane == i1, neg, ze))
    p1 = 1.0 / se
    p2 = jnp.exp(m2 - m1) / se
    tot = p1 + p2
    rt_ref[...] = jnp.where(lane == 0, pg_top * (p1 / tot),
                            jnp.where(lane == 1, pg_top * (p2 / tot),
                                      jnp.where(lane == 2, (i1 - N_GROUPS).astype(F32),
                                                jnp.where(lane == 3, (i2 - N_GROUPS).astype(F32), 0.0))))


def _ffn_norm_router(h, g, w_r, b_r, tb=512):
    t, d = h.shape
    return pl.pallas_call(
        _ffn_norm_kernel,
        grid=(t // tb,),
        in_specs=[pl.BlockSpec((tb, d), lambda i: (i, 0)), pl.BlockSpec((1, d), lambda i: (0, 0)),
                  pl.BlockSpec((d, 128), lambda i: (0, 0)), pl.BlockSpec((1, 128), lambda i: (0, 0))],
        out_specs=pl.BlockSpec((tb, 128), lambda i: (i, 0)),
        out_shape=jax.ShapeDtypeStruct((t, 128), F32),
        compiler_params=_cparams(("parallel",)),
        name="ffn_norm_router",
    )(h, g.reshape(1, d), w_r, b_r)


def _row_copy(src_hbm, idx, dst, r, sem):
    return pltpu.make_async_copy(src_hbm.at[pl.ds(idx, 1), :], dst.at[pl.ds(r, 1), :], sem)


def _expert_changed(i, exp_ref):
    return jnp.logical_or(i == 0, exp_ref[i] != exp_ref[jnp.maximum(i - 1, 0)])


def _gather_rows(src_hbm, row_of, n, dst, sem, split=False):
    def start(g, carry):
        for u in range(8):
            r = g * 8 + u
            _row_copy(src_hbm, row_of(r), dst, r, sem).start(priority=u % 2 if split else 0)
        return carry

    lax.fori_loop(0, n // 8, start, 0)


def _gather_wait(src_hbm, dst, sem):
    pltpu.make_async_copy(src_hbm.at[pl.ds(0, dst.shape[0]), :], dst, sem).wait()


def _expert_kernel(st_ref, exp_ref, base_ref, nxt_ref, nblk_ref, x_hbm, wg_hbm, wu_hbm, wd_hbm, g_ref, o_ref,
                   xbuf, stage_g, stage_u, stage_d, wg_bf, wu_bf, wd_bf, xsem, wsem, *, layer):
    i = pl.program_id(0)
    n = nblk_ref[0]
    slot = i % 2

    stages = (stage_g, stage_u, stage_d)
    caches = (wg_bf, wu_bf, wd_bf)

    def chunk_rows(m, c):
        rows = stages[m].shape[0] // WEIGHT_CHUNKS
        return pl.ds(c * rows, rows)

    def weight_copy(e, m, c):
        src = (wg_hbm, wu_hbm, wd_hbm)[m]
        return pltpu.make_async_copy(src.at[layer, e, chunk_rows(m, c), :], stages[m].at[chunk_rows(m, c), :],
                                     wsem.at[m, c])

    def start_weights(e):
        for c in range(WEIGHT_CHUNKS):
            for m in range(3):
                weight_copy(e, m, c).start(priority=1)

    last = st_ref.shape[0] - 1

    def fetch(blk, sl):
        base = base_ref[blk]
        _gather_rows(x_hbm, lambda r: st_ref[jnp.minimum(base + r, last)], MOE_BLOCK, xbuf.at[sl], xsem.at[sl])

    @pl.when(jnp.logical_and(i == 0, n > 0))
    def _():
        fetch(0, 0)
        start_weights(exp_ref[0])

    @pl.when(i + 1 < n)
    def _():
        fetch(i + 1, 1 - slot)

    @pl.when(i < n)
    def _():
        @pl.when(_expert_changed(i, exp_ref))
        def _():
            nxt = nxt_ref[i]
            for c in range(WEIGHT_CHUNKS):
                for m in range(3):
                    weight_copy(exp_ref[i], m, c).wait()
                    caches[m][chunk_rows(m, c), :] = stages[m][chunk_rows(m, c), :].astype(BF16)

                @pl.when(nxt >= 0)
                def _():
                    for m in range(3):
                        weight_copy(nxt, m, c).start(priority=1)

        _gather_wait(x_hbm, xbuf.at[slot], xsem.at[slot])
        x = xbuf[slot]
        x = (x * lax.rsqrt(jnp.mean(x * x, axis=-1, keepdims=True) + NORM_EPS) * g_ref[...]).astype(BF16)
        gate = _dot(x, wg_bf[...])
        up = _dot(x, wu_bf[...])
        hmid = (gate * _sigmoid(gate) * up).astype(BF16)
        o_ref[...] = _dot(hmid, wd_bf[...])

    @pl.when(i >= n)
    def _():
        o_ref[...] = jnp.zeros_like(o_ref)


def _experts(h, norm_g, st, plan, w_gate, w_up, w_down, layer):
    block_exp, src_base, next_exp, n_used, _ = plan
    n_blocks = block_exp.shape[0]
    d = h.shape[1]
    hbm = pl.BlockSpec(memory_space=pl.ANY)
    return pl.pallas_call(
        functools.partial(_expert_kernel, layer=layer),
        grid_spec=pltpu.PrefetchScalarGridSpec(
            num_scalar_prefetch=5,
            grid=(n_blocks,),
            in_specs=[hbm, hbm, hbm, hbm, pl.BlockSpec((1, d), lambda i, *_: (0, 0))],
            out_specs=pl.BlockSpec((MOE_BLOCK, d), lambda i, *_: (i, 0)),
            scratch_shapes=[pltpu.VMEM((2, MOE_BLOCK, d), F32),
                            pltpu.VMEM((d, D_EXPERT), F32), pltpu.VMEM((d, D_EXPERT), F32),
                            pltpu.VMEM((D_EXPERT, d), F32),
                            pltpu.VMEM((d, D_EXPERT), BF16), pltpu.VMEM((d, D_EXPERT), BF16),
                            pltpu.VMEM((D_EXPERT, d), BF16),
                            pltpu.SemaphoreType.DMA((2,)), pltpu.SemaphoreType.DMA((3, WEIGHT_CHUNKS))],
        ),
        out_shape=jax.ShapeDtypeStruct((n_blocks * MOE_BLOCK, d), F32),
        compiler_params=_cparams(("arbitrary",), VMEM_LIMIT_EXPERT),
        name="experts",
    )(st, block_exp, src_base, next_exp, n_used, h, w_gate, w_up, w_down, norm_g.reshape(1, d))


def _combine_kernel(pos_ref, y_hbm, h_ref, rt_ref, g_ref, o_ref, buf, sem, *, final_norm):
    tb = h_ref.shape[0]
    i = pl.program_id(0)
    slot = i % 2

    def fetch(blk, sl):
        for s in range(TOP_K):
            base = (s * pl.num_programs(0) + blk) * tb
            _gather_rows(y_hbm, lambda r: pos_ref[base + r], tb, buf.at[sl, s], sem.at[sl], split=True)

    @pl.when(i == 0)
    def _():
        fetch(0, 0)

    @pl.when(i + 1 < pl.num_programs(0))
    def _():
        fetch(i + 1, 1 - slot)

    for s in range(TOP_K):
        _gather_wait(y_hbm, buf.at[slot, s], sem.at[slot])
    rt = rt_ref[...]
    out = h_ref[...] + (rt[:, 0:1] * buf[slot, 0] + rt[:, 1:2] * buf[slot, 1])
    if final_norm:
        ms = jnp.mean(out * out, axis=-1, keepdims=True)
        out = out * lax.rsqrt(ms + NORM_EPS) * g_ref[...]
    o_ref[...] = out


def _combine(h, y_rows, pos, route, g, final_norm, tb=512):
    t, d = h.shape
    grid_spec = pltpu.PrefetchScalarGridSpec(
        num_scalar_prefetch=1,
        grid=(t // tb,),
        in_specs=[pl.BlockSpec(memory_space=pl.ANY),
                  pl.BlockSpec((tb, d), lambda i, *_: (i, 0)),
                  pl.BlockSpec((tb, 128), lambda i, *_: (i, 0)),
                  pl.BlockSpec((1, d), lambda i, *_: (0, 0))],
        out_specs=pl.BlockSpec((tb, d), lambda i, *_: (i, 0)),
        scratch_shapes=[pltpu.VMEM((2, TOP_K, tb, d), F32), pltpu.SemaphoreType.DMA((2,))],
    )
    return pl.pallas_call(
        functools.partial(_combine_kernel, final_norm=final_norm),
        grid_spec=grid_spec,
        out_shape=jax.ShapeDtypeStruct((t, d), F32),
        compiler_params=_cparams(("arbitrary",)),
        name="moe_combine",
    )(pos, y_rows, h, route, g.reshape(1, d))


def _plan_kernel(counts_ref, rank_ref, eid_ref, bexp_ref, base_ref, nxt_ref, nused_ref, pos_ref, shift_ref):
    n_blocks = bexp_ref.shape[0]

    def per_expert(e, carry):
        start, pad_start = carry
        nb = (counts_ref[e] + MOE_BLOCK - 1) // MOE_BLOCK
        shift_ref[e] = pad_start - start

        def fill(j, c):
            blk = pad_start // MOE_BLOCK + j
            bexp_ref[blk] = e
            base_ref[blk] = start + j * MOE_BLOCK
            return c

        lax.fori_loop(0, nb, fill, 0)
        return start + counts_ref[e], pad_start + nb * MOE_BLOCK

    _, pad_end = lax.fori_loop(0, N_EXPERTS, per_expert, (jnp.int32(0), jnp.int32(0)))
    n_used = pad_end // MOE_BLOCK
    nused_ref[0] = n_used

    def tail(blk, c):
        bexp_ref[blk] = N_EXPERTS - 1
        base_ref[blk] = 0
        nxt_ref[blk] = -1
        return c

    lax.fori_loop(n_used, n_blocks, tail, 0)

    def backward(k, carry):
        nxt, cur = carry
        blk = n_used - 1 - k
        e = bexp_ref[blk]
        nxt = jnp.where(e != cur, cur, nxt)
        nxt_ref[blk] = nxt
        return nxt, e

    lax.fori_loop(0, n_used, backward, (jnp.int32(-1), jnp.int32(-1)))

    eid = eid_ref[...]
    pos = rank_ref[...]
    for e in range(N_EXPERTS):
        pos = pos + jnp.where(eid == e, shift_ref[e], 0)
    pos_ref[...] = pos


def _plan(counts, rank, eid, n_blocks):
    smem = pl.BlockSpec(memory_space=pltpu.SMEM)
    vmem = pl.BlockSpec(memory_space=pltpu.VMEM)
    i32 = lambda n: jax.ShapeDtypeStruct((n,), jnp.int32)
    return pl.pallas_call(
        _plan_kernel,
        in_specs=[smem, vmem, vmem],
        out_specs=[smem, smem, smem, smem, vmem],
        out_shape=[i32(n_blocks), i32(n_blocks), i32(n_blocks), i32(1), jax.ShapeDtypeStruct(rank.shape, jnp.int32)],
        scratch_shapes=[pltpu.SMEM((N_EXPERTS,), jnp.int32)],
        name="moe_plan",
    )(counts, rank, eid)


def _moe_layer(h, norm_g, wg_r, bg_r, we_r, be_r, w_gate, w_up, w_down, layer, out_g, final_norm):
    t, d = h.shape
    w_r = jnp.concatenate([wg_r, we_r, jnp.zeros((d, 128 - N_GROUPS - N_EXPERTS), F32)], axis=1)
    b_r = jnp.concatenate([bg_r, be_r, jnp.zeros((128 - N_GROUPS - N_EXPERTS,), F32)]).reshape(1, 128)
    route = _ffn_norm_router(h, norm_g, w_r, b_r)
    expert_id = route[:, TOP_K:2 * TOP_K].astype(jnp.int32)
    flat = expert_id.reshape(-1)
    order = jnp.argsort(flat).astype(jnp.int32)
    st = order // TOP_K
    rank_sm = jnp.argsort(order).astype(jnp.int32).reshape(t, TOP_K).T.reshape(-1, 128)
    eid_sm = expert_id.T.reshape(-1, 128)
    counts = jnp.sum((flat[:, None] == jnp.arange(N_EXPERTS, dtype=jnp.int32)[None, :]).astype(jnp.int32), axis=0)
    plan = _plan(counts, rank_sm, eid_sm, (t * TOP_K + MOE_BLOCK - 1) // MOE_BLOCK + N_EXPERTS)
    y_rows = _experts(h, norm_g, st, plan, w_gate, w_up, w_down, layer)
    return _combine(h, y_rows, plan[4].reshape(-1), route, out_g, final_norm)


_D0 = A_COLS + B_COLS + C_COLS
_W_IN_PIECES = (
    (OFF_BQ, A_COLS, B_COLS), (OFF_CX, A_COLS + B_COLS, C_COLS),
    (OFF_DV, _D0 + 512, 512), (OFF_DGT, _D0 + 1040, 512),
    (OFF_AR, 0, 3 * MIX_W), (OFF_DQ, _D0, 512),
    (OFF_AL, 3 * MIX_W, RWKV_LORA), (OFF_DAL, _D0 + 1024, GLA_LORA),
    (OFF_G, _D0 + D_COLS, 4 * D_MODEL),
)


RELAYOUT_ROWS = 512
_SPECIAL_BLOCK = OFF_AL // RELAYOUT_ROWS


def _w_in_source_rows():
    rows = np.zeros((NP_COLS // RELAYOUT_ROWS,), np.int32)
    for dst, src, width in _W_IN_PIECES:
        if width % RELAYOUT_ROWS == 0:
            for k in range(width // RELAYOUT_ROWS):
                rows[dst // RELAYOUT_ROWS + k] = src + k * RELAYOUT_ROWS
    return rows


def _relayout_kernel(src_ref, w_hbm, extra_hbm, o_ref, stage, sem, *, layer):
    j = pl.program_id(0)
    slot = j % 2

    def start(blk, sl):
        @pl.when(blk == _SPECIAL_BLOCK)
        def _():
            pltpu.make_async_copy(extra_hbm, stage.at[sl], sem.at[sl]).start()

        @pl.when(blk != _SPECIAL_BLOCK)
        def _():
            rows = pl.ds(pl.multiple_of(src_ref[blk], 16), RELAYOUT_ROWS)
            pltpu.make_async_copy(w_hbm.at[layer, rows, :], stage.at[sl], sem.at[sl]).start()

    @pl.when(j == 0)
    def _():
        start(0, 0)

    @pl.when(j + 1 < pl.num_programs(0))
    def _():
        start(j + 1, 1 - slot)

    pltpu.make_async_copy(extra_hbm, stage.at[slot], sem.at[slot]).wait()
    o_ref[...] = stage[slot].astype(o_ref.dtype)


def _permute_w_in(w_in, layer):
    w_t = jnp.swapaxes(w_in, 1, 2)
    d = w_t.shape[2]
    zeros = lambda n: jnp.zeros((n, d), F32)
    extra = jnp.concatenate([w_t[layer, 3 * MIX_W:A_COLS], zeros(OFF_DAL - OFF_AL - RWKV_LORA),
                             w_t[layer, _D0 + 1024:_D0 + 1024 + GLA_LORA],
                             zeros(OFF_G - OFF_DAL - GLA_LORA)], axis=0)
    hbm = pl.BlockSpec(memory_space=pl.ANY)
    return pl.pallas_call(
        functools.partial(_relayout_kernel, layer=layer),
        grid_spec=pltpu.PrefetchScalarGridSpec(
            num_scalar_prefetch=1,
            grid=(NP_COLS // RELAYOUT_ROWS,),
            in_specs=[hbm, hbm],
            out_specs=pl.BlockSpec((RELAYOUT_ROWS, d), lambda j, *_: (j, 0)),
            scratch_shapes=[pltpu.VMEM((2, RELAYOUT_ROWS, d), F32), pltpu.SemaphoreType.DMA((2,))],
        ),
        out_shape=jax.ShapeDtypeStruct((NP_COLS, d), BF16),
        compiler_params=_cparams(("arbitrary",)),
        name="w_in_relayout",
    )(jnp.asarray(_w_in_source_rows()), w_t, extra)


def kernel(x, positions, norm_mix_g, w_in, rwkv_mu, rwkv_w0, rwkv_w_up, rwkv_a0, rwkv_a_up, rwkv_g_up, rwkv_k_k, rwkv_k_a, rwkv_r_k, rwkv_ln_g, rwkv_ln_b, rwkv_w_o, ret_norm_g, ret_w_o, lru_conv_w, lru_conv_b, lru_w_a, lru_b_a, lru_w_x, lru_b_x, lru_lambda, lru_w_o, gla_alpha_up, gla_alpha_b, gla_norm_g, gla_w_o, w_out, norm_ffn_g, router_group_w, router_group_b, router_expert_w, router_expert_b, moe_w_gate, moe_w_up, moe_w_down, final_norm_g):
    b_, s_, d = x.shape
    assert b_ == 1 and d == D_MODEL
    depth = w_in.shape[0]
    h = x.reshape(s_, d)
    pos_b = jnp.broadcast_to(positions.reshape(s_, 1), (s_, 128)).astype(jnp.int32)
    cos_t, sin_t = _rope_table(pos_b)
    for l in range(depth):
        xn = _rmsnorm(h, norm_mix_g[l], BF16)
        w_t = _permute_w_in(w_in, l)
        p = _matmul_nt(xn, w_t, 0, OFF_G, 2048, 512, F32, "in_proj")
        y_a = _rwkv_branch(p, rwkv_mu[l], rwkv_w0[l], rwkv_w_up[l], rwkv_a0[l], rwkv_a_up[l], rwkv_g_up[l],
                           rwkv_k_k[l], rwkv_k_a[l], rwkv_r_k[l], rwkv_ln_g[l], rwkv_ln_b[l])
        y_b = _retention(p, cos_t, sin_t, ret_norm_g[l])
        y_c = _rglru(p, pos_b, lru_conv_w[l], lru_conv_b[l], lru_w_a[l], lru_b_a[l], lru_w_x[l], lru_b_x[l],
                     lru_lambda[l])
        y_d = _gla(p, gla_alpha_up[l], gla_alpha_b[l], gla_norm_g[l])
        merged = _merge(xn, w_t, (y_a, y_b, y_c, y_d), (rwkv_w_o, ret_w_o, lru_w_o, gla_w_o), l)
        h = _matmul_residual(merged, w_out, l, h, 2048, 512)
        last = l == depth - 1
        h = _moe_layer(h, norm_ffn_g[l], router_group_w[l], router_group_b[l], router_expert_w[l],
                       router_expert_b[l], moe_w_gate, moe_w_up, moe_w_down, l,
                       final_norm_g if last else norm_ffn_g[l], last)
    return h.reshape(b_, s_, d)
```
